```python
import functools
import jax, jax.numpy as jnp
from jax import lax
import numpy as np

D_MODEL = 1024
BATCH = 8
SEQ = 2048
DEPTH = 4
DEC_BATCH = 32
DEC_SEQ = 1
PAST_LEN = 8192
PAGE_SIZE = 128

N_A = DEPTH // 2
N_B = DEPTH - N_A
HEAD_DIM = 128
D_CONV = D_MODEL
CONV_W = 3
DIL_CONFIGS = ((128, 1), (512, 4), (2048, 16))
N_GROUPS = len(DIL_CONFIGS)
DIL_HEADS = 4
D_DIL = DIL_HEADS * HEAD_DIM
MEM_LEN = 256
MEM_HEADS = 4
D_MEM = MEM_HEADS * HEAD_DIM
BAND_BLOCK = 128
EPS = 1e-6
NEG = -1e30
D_IN_A = 4 * D_CONV + 2 * D_MEM
D_IN_B = N_GROUPS * D_DIL + D_DIL + 2 * D_MEM
D_KV = N_GROUPS * 2 * D_DIL

kernel_name = "yoco_shortconv_dilated_swa_memxattn_step"


def rms_norm(x, g):
    xf = x.astype(jnp.float32)
    y = xf * lax.rsqrt(jnp.mean(xf * xf, axis=-1, keepdims=True) + EPS) * g.astype(jnp.float32)
    return y.astype(x.dtype)


def mem_kv_proj(mem, norm_g, w, k_norm_g):
    b, m, _ = mem.shape
    kv = (rms_norm(mem, norm_g) @ w).reshape(b, m, 2, MEM_HEADS, HEAD_DIM)
    k = rms_norm(kv[:, :, 0], k_norm_g)
    return jnp.stack([k, kv[:, :, 1]], axis=2)


def mem_attend(q, mem_kv):
    s = jnp.einsum('bthe,bmhe->bhtm', q, mem_kv[:, :, 0], preferred_element_type=jnp.float32)
    p = jax.nn.softmax(s, axis=-1)
    return jnp.einsum('bhtm,bmhe->bthe', p, mem_kv[:, :, 1].astype(jnp.float32)).astype(q.dtype)


def shared_kv(x, kv_norm_g, w_kv, k_norm_g):
    b, t, _ = x.shape
    kv = (rms_norm(x, kv_norm_g) @ w_kv).reshape(b, t, N_GROUPS, 2, DIL_HEADS, HEAD_DIM)
    k = rms_norm(kv[:, :, :, 0], k_norm_g[:, None, :])
    return k, kv[:, :, :, 1]


def band_attend(q, k, v, d, nw):
    b, s_len, h, e = q.shape
    span = d * BAND_BLOCK
    s_pad = -(-s_len // span) * span
    n = s_pad // d
    nb = n // BAND_BLOCK

    def to_res(a):
        a = jnp.pad(a, ((0, 0), (0, s_pad - s_len), (0, 0), (0, 0)))
        a = a.reshape(b, n, d, h, e).transpose(0, 2, 1, 3, 4)
        return a.reshape(b, d, nb, BAND_BLOCK, h, e)

    def with_prev(a):
        prev = jnp.pad(a, ((0, 0), (0, 0), (1, 0), (0, 0), (0, 0), (0, 0)))[:, :, :-1]
        return jnp.concatenate([prev, a], axis=3)

    qb = to_res(q)
    kk = with_prev(to_res(k))
    vv = with_prev(to_res(v))
    s = jnp.einsum('brnqhe,brnkhe->brnhqk', qb, kk, preferred_element_type=jnp.float32)
    qi = jnp.arange(BAND_BLOCK)[:, None]
    kj = jnp.arange(2 * BAND_BLOCK)[None, :]
    rel = qi + BAND_BLOCK - kj
    valid = (rel >= 0) & (rel <= nw)
    has_prev = (jnp.arange(nb)[:, None, None] > 0) | (kj >= BAND_BLOCK)[None]
    valid = valid[None] & has_prev
    s = jnp.where(valid[None, None, :, None], s, NEG)
    m = jnp.max(s, axis=-1, keepdims=True)
    p = jnp.exp(s - m)
    l = jnp.sum(p, axis=-1, keepdims=True)
    o = jnp.einsum('brnhqk,brnkhe->brnqhe', p / l, vv.astype(jnp.float32)).astype(q.dtype)
    lse = (m + jnp.log(l))[..., 0]
    o = o.reshape(b, d, n, h, e).transpose(0, 2, 1, 3, 4).reshape(b, s_pad, h, e)[:, :s_len]
    lse = lse.transpose(0, 1, 2, 4, 3).reshape(b, d, n, h).transpose(0, 2, 1, 3)
    lse = lse.reshape(b, s_pad, h)[:, :s_len]
    return o, lse


def strided_gather_attend(q, kk, vv, d, nw):
    t_len = q.shape[1]
    l_len = kk.shape[1] - t_len
    rows = l_len + jnp.arange(t_len)[:, None] - d * jnp.arange(nw + 1)[None, :]
    valid = rows >= 0
    idx = jnp.maximum(rows, 0)
    kg = kk[:, idx]
    vg = vv[:, idx]
    s = jnp.einsum('bthe,btshe->bhts', q, kg, preferred_element_type=jnp.float32)
    s = jnp.where(valid[None, None], s, NEG)
    m = jnp.max(s, axis=-1, keepdims=True)
    p = jnp.exp(s - m)
    l = jnp.sum(p, axis=-1, keepdims=True)
    o = jnp.einsum('bhts,btshe->bthe', p / l, vg.astype(jnp.float32)).astype(q.dtype)
    lse = (m + jnp.log(l))[..., 0].transpose(0, 2, 1)
    return o, lse


def combine_groups(outs, lses):
    o = jnp.stack(outs, axis=2).astype(jnp.float32)
    w = jax.nn.softmax(jnp.stack(lses, axis=2), axis=2)
    return jnp.sum(o * w[..., None], axis=2).astype(outs[0].dtype)


def dilated_prompt(q, k, v):
    outs, lses = [], []
    for g, (win, d) in enumerate(DIL_CONFIGS):
        o, lse = band_attend(q[:, :, g], k[:, :, g], v[:, :, g], d, win // d)
        outs.append(o)
        lses.append(lse)
    return combine_groups(outs, lses)


def dilated_sample(q, k_new, v_new, bufs):
    outs, lses = [], []
    for g, (win, d) in enumerate(DIL_CONFIGS):
        kk = jnp.concatenate([bufs[g][:, :, 0], k_new[:, :, g]], axis=1)
        vv = jnp.concatenate([bufs[g][:, :, 1], v_new[:, :, g]], axis=1)
        o, lse = strided_gather_attend(q[:, :, g], kk, vv, d, win // d)
        outs.append(o)
        lses.append(lse)
    return combine_groups(outs, lses)


def conv_layer(x, conv_state, mem_kv, norm_g, w_in, conv_w, mem_q_g, w_out):
    b, t, _ = x.shape
    z = rms_norm(x, norm_g) @ w_in
    hh, c, bg, g, mq, mg = jnp.split(
        z, [D_CONV, 2 * D_CONV, 3 * D_CONV, 4 * D_CONV, 4 * D_CONV + D_MEM], axis=-1)
    u = c * hh
    ext = jnp.concatenate([conv_state, u], axis=1)
    y = sum(conv_w[j] * ext[:, j:j + t] for j in range(CONV_W))
    conv_out = jax.nn.silu(g) * bg * y
    q = rms_norm(mq.reshape(b, t, MEM_HEADS, HEAD_DIM), mem_q_g) * (HEAD_DIM ** -0.5)
    mem_out = mem_attend(q, mem_kv).reshape(b, t, D_MEM) * jax.nn.silu(mg)
    out = jnp.concatenate([conv_out, mem_out], axis=-1) @ w_out
    return x + out, ext[:, -(CONV_W - 1):]


def dil_layer(x, dil_fn, mem_kv, norm_g, w_in, q_norm_g, mem_q_g, w_out):
    b, t, _ = x.shape
    z = rms_norm(x, norm_g) @ w_in
    qd, gd, mq, mg = jnp.split(
        z, [N_GROUPS * D_DIL, (N_GROUPS + 1) * D_DIL, (N_GROUPS + 1) * D_DIL + D_MEM], axis=-1)
    qd = rms_norm(qd.reshape(b, t, N_GROUPS, DIL_HEADS, HEAD_DIM), q_norm_g[:, None, :]) * (HEAD_DIM ** -0.5)
    dil_out = dil_fn(qd).reshape(b, t, D_DIL) * jax.nn.silu(gd)
    q = rms_norm(mq.reshape(b, t, MEM_HEADS, HEAD_DIM), mem_q_g) * (HEAD_DIM ** -0.5)
    mem_out = mem_attend(q, mem_kv).reshape(b, t, D_MEM) * jax.nn.silu(mg)
    out = jnp.concatenate([dil_out, mem_out], axis=-1) @ w_out
    return x + out


def setup_inputs(seed: int = 0) -> dict:
    key = jax.random.key(seed)
    ks = jax.random.split(key, 24)
    f32 = jnp.float32

    def nrm(k, shape, scale=1.0):
        return jax.random.normal(k, shape, f32) * scale

    def gain(k, shape):
        return 1.0 + 0.1 * jax.random.normal(k, shape, f32)

    win = [min(w, PAST_LEN) for w, _ in DIL_CONFIGS]
    return {
        "x_prompt": nrm(ks[0], (BATCH, SEQ, D_MODEL)),
        "x_sample": nrm(ks[1], (DEC_BATCH, DEC_SEQ, D_MODEL)),
        "cache_mem_kv": nrm(ks[2], (DEPTH, DEC_BATCH, MEM_LEN, 2, MEM_HEADS, HEAD_DIM)),
        "state_conv": nrm(ks[3], (N_A, DEC_BATCH, CONV_W - 1, D_CONV)),
        "cache_dil0_kv": nrm(ks[4], (DEC_BATCH, win[0], 2, DIL_HEADS, HEAD_DIM)),
        "cache_dil1_kv": nrm(ks[5], (DEC_BATCH, win[1], 2, DIL_HEADS, HEAD_DIM)),
        "cache_dil2_kv": nrm(ks[6], (DEC_BATCH, win[2], 2, DIL_HEADS, HEAD_DIM)),
        "mem_prompt": nrm(ks[7], (BATCH, MEM_LEN, D_MODEL)),
        "norm_a": gain(ks[8], (N_A, D_MODEL)),
        "w_in_a": nrm(ks[9], (N_A, D_MODEL, D_IN_A), D_MODEL ** -0.5),
        "conv_w_a": nrm(ks[10], (N_A, CONV_W, D_CONV), CONV_W ** -0.5),
        "w_out_a": nrm(ks[11], (N_A, D_CONV + D_MEM, D_MODEL), (D_CONV + D_MEM) ** -0.5),
        "norm_b": gain(ks[12], (N_B, D_MODEL)),
        "w_in_b": nrm(ks[13], (N_B, D_MODEL, D_IN_B), D_MODEL ** -0.5),
        "q_norm_b": gain(ks[14], (N_B, N_GROUPS, HEAD_DIM)),
        "w_out_b": nrm(ks[15], (N_B, D_DIL + D_MEM, D_MODEL), (D_DIL + D_MEM) ** -0.5),
        "kv_norm": gain(ks[16], (D_MODEL,)),
        "w_kv": nrm(ks[17], (D_MODEL, D_KV), D_MODEL ** -0.5),
        "k_norm_dil": gain(ks[18], (N_GROUPS, HEAD_DIM)),
        "mem_norm": gain(ks[19], (DEPTH, D_MODEL)),
        "w_mem_kv": nrm(ks[20], (DEPTH, D_MODEL, 2 * D_MEM), D_MODEL ** -0.5),
        "mem_k_norm": gain(ks[21], (DEPTH, HEAD_DIM)),
        "mem_q_norm": gain(ks[22], (DEPTH, HEAD_DIM)),
    }


def reference(x_prompt, x_sample, cache_mem_kv, state_conv, cache_dil0_kv, cache_dil1_kv,
              cache_dil2_kv, mem_prompt, norm_a, w_in_a, conv_w_a, w_out_a, norm_b, w_in_b,
              q_norm_b, w_out_b, kv_norm, w_kv, k_norm_dil, mem_norm, w_mem_kv, mem_k_norm,
              mem_q_norm):
    mem_kv_p = jnp.stack([mem_kv_proj(mem_prompt, mem_norm[l], w_mem_kv[l], mem_k_norm[l])
                          for l in range(DEPTH)])
    xp, xs = x_prompt, x_sample
    s_len = x_prompt.shape[1]
    conv_p, conv_s = [], []
    dil_p_state, dil_s_state = [], []
    dil_p_fn = dil_s_fn = None
    for l in range(DEPTH):
        if l < N_A:
            a = l
            zero_state = jnp.zeros((xp.shape[0], CONV_W - 1, D_CONV), xp.dtype)
            xp, cp = conv_layer(xp, zero_state, mem_kv_p[l], norm_a[a], w_in_a[a], conv_w_a[a],
                                mem_q_norm[l], w_out_a[a])
            xs, cs = conv_layer(xs, state_conv[a], cache_mem_kv[l], norm_a[a], w_in_a[a],
                                conv_w_a[a], mem_q_norm[l], w_out_a[a])
            conv_p.append(cp)
            conv_s.append(cs)
        else:
            if l == N_A:
                kp, vp = shared_kv(xp, kv_norm, w_kv, k_norm_dil)
                ks_, vs_ = shared_kv(xs, kv_norm, w_kv, k_norm_dil)
                dil_p_fn = functools.partial(dilated_prompt, k=kp, v=vp)
                dil_s_fn = functools.partial(dilated_sample, k_new=ks_, v_new=vs_,
                                             bufs=(cache_dil0_kv, cache_dil1_kv, cache_dil2_kv))
                for g, (win, _) in enumerate(DIL_CONFIGS):
                    keep = min(win, s_len)
                    dil_p_state.append(jnp.stack([kp[:, -keep:, g], vp[:, -keep:, g]], axis=2))
                    dil_s_state.append(jnp.stack([ks_[:, :, g], vs_[:, :, g]], axis=2))
            bl = l - N_A
            xp = dil_layer(xp, dil_p_fn, mem_kv_p[l], norm_b[bl], w_in_b[bl], q_norm_b[bl],
                           mem_q_norm[l], w_out_b[bl])
            xs = dil_layer(xs, dil_s_fn, cache_mem_kv[l], norm_b[bl], w_in_b[bl], q_norm_b[bl],
                           mem_q_norm[l], w_out_b[bl])
    conv_prompt = jnp.stack(conv_p)
    conv_sample = jnp.stack(conv_s)
    return (xp, xs, mem_kv_p, conv_prompt, conv_sample,
            dil_p_state[0], dil_p_state[1], dil_p_state[2],
            dil_s_state[0], dil_s_state[1], dil_s_state[2])
```

```python
import functools

import jax
import jax.numpy as jnp
from jax import lax
from jax.experimental import pallas as pl
from jax.experimental.pallas import tpu as pltpu

F32 = jnp.float32
BF16 = jnp.bfloat16

D_MODEL = 1024
HEAD_DIM = 128
N_HEADS = 4
D_ATT = N_HEADS * HEAD_DIM
D_CONV = D_MODEL
CONV_W = 3
N_GROUPS = 3
DILATIONS = (1, 4, 16)
WINDOWS = (128, 512, 2048)
BAND = 128
MEM_LEN = 256
EPS = 1e-6
NEG = -1e30
SCALE = HEAD_DIM ** -0.5

TM = 512
SAMPLE_GB = 8
VMEM_LIMIT = 56 * 1024 * 1024


def _rms(x, g):
    return x * lax.rsqrt(jnp.mean(x * x, axis=-1, keepdims=True) + EPS) * g


def _silu(x):
    return x * (1.0 / (1.0 + jnp.exp(-x)))


def _dot(a, b):
    return jnp.dot(a, b, preferred_element_type=F32)


def _dot_nt(a, b):
    return lax.dot_general(a, b, (((1,), (1,)), ((), ())), preferred_element_type=F32)


def _hs(h, base=0):
    return slice(base + h * HEAD_DIM, base + (h + 1) * HEAD_DIM)


def _params(sem):
    return pltpu.CompilerParams(dimension_semantics=sem, vmem_limit_bytes=VMEM_LIMIT)


def _const_spec(shape):
    nd = len(shape)
    return pl.BlockSpec(shape, lambda *_: (0,) * nd, pipeline_mode=pl.Buffered(1))


def _memkv_kernel(mem_ref, g_ref, w_ref, kg_ref, of_ref, ob_ref):
    xn = _rms(mem_ref[...], g_ref[...]).astype(BF16)
    kv = _dot(xn, w_ref[...])
    for h in range(N_HEADS):
        kh = _rms(kv[:, _hs(h)], kg_ref[...])
        of_ref[:, _hs(h)] = kh
        ob_ref[:, _hs(h)] = kh.astype(BF16)
    v = kv[:, D_ATT:]
    of_ref[:, D_ATT:] = v
    ob_ref[:, D_ATT:] = v.astype(BF16)


def _memkv_prompt(mem, mem_norm, w_mem_kv, mem_k_norm):
    depth = w_mem_kv.shape[0]
    rows = mem.shape[0] * mem.shape[1]
    rt = 1024
    out_sds = jax.ShapeDtypeStruct((depth, rows, 2 * D_ATT), F32)
    out_b_sds = jax.ShapeDtypeStruct((depth, rows, 2 * D_ATT), BF16)
    return pl.pallas_call(
        _memkv_kernel,
        grid=(depth, rows // rt),
        in_specs=[
            pl.BlockSpec((rt, D_MODEL), lambda l, i: (i, 0)),
            pl.BlockSpec((None, 1, D_MODEL), lambda l, i: (l, 0, 0)),
            pl.BlockSpec((None, D_MODEL, 2 * D_ATT), lambda l, i: (l, 0, 0)),
            pl.BlockSpec((None, 1, HEAD_DIM), lambda l, i: (l, 0, 0)),
        ],
        out_specs=[
            pl.BlockSpec((None, rt, 2 * D_ATT), lambda l, i: (l, i, 0)),
            pl.BlockSpec((None, rt, 2 * D_ATT), lambda l, i: (l, i, 0)),
        ],
        out_shape=[out_sds, out_b_sds],
        compiler_params=_params(("parallel", "parallel")),
        name="memkv_prompt",
    )(mem.reshape(rows, D_MODEL), mem_norm.reshape(depth, 1, D_MODEL), w_mem_kv,
      mem_k_norm.reshape(depth, 1, HEAD_DIM))


def _mem_attn(mq, mg, mkv_ref, gq, store):
    for h in range(N_HEADS):
        q = (_rms(mq[:, _hs(h)], gq) * SCALE).astype(BF16)
        s = _dot_nt(q, mkv_ref[:, _hs(h)])
        m = jnp.max(s, axis=-1, keepdims=True)
        p = jnp.exp(s - m)
        l = jnp.sum(p, axis=-1, keepdims=True)
        o = _dot(p.astype(BF16), mkv_ref[:, _hs(h, D_ATT)]) / l
        store(h, o * _silu(mg[:, _hs(h)]))


def _a_layer_kernel(x_ref, g_ref, win_ref, cw_ref, mkv_ref, gq_ref, wout_ref,
                    y_ref, cs_ref, ext_ref, cat_ref):
    t = pl.program_id(1)

    @pl.when(t == 0)
    def _():
        ext_ref[0:8, :] = jnp.zeros((8, D_CONV), F32)

    x = x_ref[...]
    xn = _rms(x, g_ref[...]).astype(BF16)

    def proj(lo, hi):
        return _dot(xn, win_ref[:, lo:hi])

    u = proj(D_CONV, 2 * D_CONV) * proj(0, D_CONV)
    ext_ref[8:8 + TM, :] = u
    y = (cw_ref[0:1, :] * ext_ref[6:6 + TM, :] + cw_ref[1:2, :] * ext_ref[7:7 + TM, :]
         + cw_ref[2:3, :] * u)
    ext_ref[0:8, :] = ext_ref[TM:TM + 8, :]
    cs_ref[...] = ext_ref[6:8, :]
    bg = proj(2 * D_CONV, 3 * D_CONV)
    g = proj(3 * D_CONV, 4 * D_CONV)
    cat_ref[:, 0:D_CONV] = (_silu(g) * bg * y).astype(BF16)
    mq = proj(4 * D_CONV, 4 * D_CONV + D_ATT)
    mg = proj(4 * D_CONV + D_ATT, 4 * D_CONV + 2 * D_ATT)

    def store(h, val):
        cat_ref[:, _hs(h, D_CONV)] = val.astype(BF16)

    _mem_attn(mq, mg, mkv_ref, gq_ref[...], store)
    y_ref[...] = x + _dot(cat_ref[...], wout_ref[...])


def _a_layer_prompt(x, norm_g, w_in, conv_w, mkv_b, layer, gq, w_out):
    b, s, _ = x.shape
    d_in = w_in.shape[1]
    return pl.pallas_call(
        _a_layer_kernel,
        grid=(b, s // TM),
        in_specs=[
            pl.BlockSpec((None, TM, D_MODEL), lambda i, t: (i, t, 0)),
            _const_spec((1, D_MODEL)),
            _const_spec((D_MODEL, d_in)),
            _const_spec((CONV_W, D_CONV)),
            pl.BlockSpec((None, None, MEM_LEN, 2 * D_ATT), lambda i, t: (layer, i, 0, 0)),
            _const_spec((1, HEAD_DIM)),
            _const_spec((D_CONV + D_ATT, D_MODEL)),
        ],
        out_specs=[
            pl.BlockSpec((None, TM, D_MODEL), lambda i, t: (i, t, 0)),
            pl.BlockSpec((None, CONV_W - 1, D_CONV), lambda i, t: (i, 0, 0)),
        ],
        out_shape=[jax.ShapeDtypeStruct((b, s, D_MODEL), F32),
                   jax.ShapeDtypeStruct((b, CONV_W - 1, D_CONV), F32)],
        scratch_shapes=[pltpu.VMEM((TM + 8, D_CONV), F32),
                        pltpu.VMEM((TM, D_CONV + D_ATT), BF16)],
        compiler_params=_params(("arbitrary", "arbitrary")),
        name=f"a_layer_prompt_{layer}",
    )(x, norm_g.reshape(1, D_MODEL), w_in, conv_w, mkv_b, gq.reshape(1, HEAD_DIM), w_out)


def _kv_kernel(x_ref, g_ref, w_ref, kn_ref, kvb0_ref, kvb1_ref, kvb2_ref,
               st0_ref, st1_ref, st2_ref):
    t = pl.program_id(1)
    last = pl.num_programs(1) - 1
    xn = _rms(x_ref[...], g_ref[...]).astype(BF16)
    kvb_refs = (kvb0_ref, kvb1_ref, kvb2_ref)
    for g in range(N_GROUPS):
        kv = _dot(xn, w_ref[:, g * 2 * D_ATT:(g + 1) * 2 * D_ATT])
        parts = [_rms(kv[:, _hs(h)], kn_ref[g:g + 1, :]) for h in range(N_HEADS)]
        parts.append(kv[:, D_ATT:])
        kv = jnp.concatenate(parts, axis=-1)
        kvb_refs[g][...] = kv.astype(BF16)
        if g == 2:
            st2_ref[...] = kv
        elif g == 1:
            @pl.when(t == last)
            def _():
                st1_ref[...] = kv
        else:
            @pl.when(t == last)
            def _():
                st0_ref[...] = kv[TM - WINDOWS[0]:, :]


def _kv_prompt(x, kv_norm, w_kv, k_norm_dil):
    b, s, _ = x.shape
    assert WINDOWS[1] == TM and WINDOWS[2] == s and WINDOWS[0] <= TM
    tile = lambda i, t: (i, t, 0)
    first = lambda i, t: (i, 0, 0)
    kvb_sds = jax.ShapeDtypeStruct((b, s, 2 * D_ATT), BF16)
    return pl.pallas_call(
        _kv_kernel,
        grid=(b, s // TM),
        in_specs=[
            pl.BlockSpec((None, TM, D_MODEL), tile),
            _const_spec((1, D_MODEL)),
            _const_spec((D_MODEL, N_GROUPS * 2 * D_ATT)),
            _const_spec((N_GROUPS, HEAD_DIM)),
        ],
        out_specs=[
            pl.BlockSpec((None, TM, 2 * D_ATT), tile),
            pl.BlockSpec((None, TM, 2 * D_ATT), tile),
            pl.BlockSpec((None, TM, 2 * D_ATT), tile),
            pl.BlockSpec((None, WINDOWS[0], 2 * D_ATT), first),
            pl.BlockSpec((None, WINDOWS[1], 2 * D_ATT), first),
            pl.BlockSpec((None, TM, 2 * D_ATT), tile),
        ],
        out_shape=[kvb_sds, kvb_sds, kvb_sds,
                   jax.ShapeDtypeStruct((b, WINDOWS[0], 2 * D_ATT), F32),
                   jax.ShapeDtypeStruct((b, WINDOWS[1], 2 * D_ATT), F32),
                   jax.ShapeDtypeStruct((b, s, 2 * D_ATT), F32)],
        compiler_params=_params(("arbitrary", "arbitrary")),
        name="kv_prompt",
    )(x, kv_norm.reshape(1, D_MODEL), w_kv, k_norm_dil)


def _b_pre_kernel(x_ref, g_ref, win_ref, qn_ref, mkv_ref, gq_ref,
                  q0_ref, q1_ref, q2_ref, gate_ref, memo_ref):
    xn = _rms(x_ref[...], g_ref[...]).astype(BF16)

    def proj(lo, hi):
        return _dot(xn, win_ref[:, lo:hi])

    q_refs = (q0_ref, q1_ref, q2_ref)
    for g in range(N_GROUPS):
        qd = proj(g * D_ATT, (g + 1) * D_ATT)
        for h in range(N_HEADS):
            qh = _rms(qd[:, _hs(h)], qn_ref[g:g + 1, :]) * SCALE
            q_refs[g][:, _hs(h)] = qh.astype(BF16)
    base = N_GROUPS * D_ATT
    gate_ref[...] = _silu(proj(base, base + D_ATT))
    mq = proj(base + D_ATT, base + 2 * D_ATT)
    mg = proj(base + 2 * D_ATT, base + 3 * D_ATT)

    def store(h, val):
        memo_ref[:, _hs(h)] = val.astype(BF16)

    _mem_attn(mq, mg, mkv_ref, gq_ref[...], store)


def _b_pre_prompt(x, norm_g, w_in, q_norm, mkv_b, layer, gq):
    b, s, _ = x.shape
    tile = lambda i, t: (i, t, 0)
    q_sds = jax.ShapeDtypeStruct((b, s, D_ATT), BF16)
    att_spec = pl.BlockSpec((None, TM, D_ATT), tile)
    return pl.pallas_call(
        _b_pre_kernel,
        grid=(b, s // TM),
        in_specs=[
            pl.BlockSpec((None, TM, D_MODEL), tile),
            _const_spec((1, D_MODEL)),
            _const_spec((D_MODEL, w_in.shape[1])),
            _const_spec((N_GROUPS, HEAD_DIM)),
            pl.BlockSpec((None, None, MEM_LEN, 2 * D_ATT), lambda i, t: (layer, i, 0, 0)),
            _const_spec((1, HEAD_DIM)),
        ],
        out_specs=[att_spec] * 5,
        out_shape=[q_sds, q_sds, q_sds, jax.ShapeDtypeStruct((b, s, D_ATT), F32), q_sds],
        compiler_params=_params(("parallel", "parallel")),
        name=f"b_pre_prompt_{layer}",
    )(x, norm_g.reshape(1, D_MODEL), w_in, q_norm, mkv_b, gq.reshape(1, HEAD_DIM))


def _band_unit(q, k, v, mask):
    s = jnp.where(mask, _dot_nt(q, k), NEG)
    m = jnp.max(s, axis=-1, keepdims=True)
    p = jnp.exp(s - m)
    l = jnp.sum(p, axis=-1, keepdims=True)
    return _dot(p.astype(BF16), v) / l, m + jnp.log(l)


def _dil_kernel(q_ref, kv_ref, o_ref, lse_ref, *, n_streams, n_blocks):
    qi = lax.broadcasted_iota(jnp.int32, (BAND, BAND), 0)
    kj = lax.broadcasted_iota(jnp.int32, (BAND, BAND), 1)
    mask_first = kj <= qi
    qi2 = lax.broadcasted_iota(jnp.int32, (BAND, 2 * BAND), 0)
    kj2 = lax.broadcasted_iota(jnp.int32, (BAND, 2 * BAND), 1)
    mask_rest = (kj2 >= qi2) & (kj2 <= qi2 + BAND)
    lane = lax.broadcasted_iota(jnp.int32, (BAND, HEAD_DIM), 1)

    def block(p, q_rows, k_rows, mask):
        lse_tile = jnp.zeros((BAND, HEAD_DIM), F32)
        for h in range(N_HEADS):
            o, lse = _band_unit(q_ref[q_rows, _hs(h, p * D_ATT)],
                                kv_ref[k_rows, _hs(h, p * 2 * D_ATT)],
                                kv_ref[k_rows, _hs(h, p * 2 * D_ATT + D_ATT)], mask)
            o_ref[q_rows, _hs(h, p * D_ATT)] = o
            lse_tile = jnp.where(lane == h, lse, lse_tile)
        lse_ref[q_rows, p * HEAD_DIM:(p + 1) * HEAD_DIM] = lse_tile

    for p in range(n_streams):
        block(p, pl.ds(0, BAND), pl.ds(0, BAND), mask_first)
        if n_blocks > 1:
            def body(n, carry, p=p):
                q0 = pl.multiple_of(n * BAND, BAND)
                k0 = pl.multiple_of((n - 1) * BAND, BAND)
                block(p, pl.ds(q0, BAND), pl.ds(k0, 2 * BAND), mask_rest)
                return carry
            lax.fori_loop(1, n_blocks, body, 0)


def _dil_prompt(q, kvb, group):
    b, s, _ = q.shape
    d = DILATIONS[group]
    assert WINDOWS[group] // d == BAND
    r = s // d
    n_streams = min(d, 4)
    steps = d // n_streams
    blk = lambda i, j: (i, 0, j)
    o, lse = pl.pallas_call(
        functools.partial(_dil_kernel, n_streams=n_streams, n_blocks=r // BAND),
        grid=(b, steps),
        in_specs=[
            pl.BlockSpec((None, r, n_streams * D_ATT), blk),
            pl.BlockSpec((None, r, n_streams * 2 * D_ATT), blk),
        ],
        out_specs=[
            pl.BlockSpec((None, r, n_streams * D_ATT), blk),
            pl.BlockSpec((None, r, n_streams * HEAD_DIM), blk),
        ],
        out_shape=[jax.ShapeDtypeStruct((b, r, d * D_ATT), F32),
                   jax.ShapeDtypeStruct((b, r, d * HEAD_DIM), F32)],
        compiler_params=_params(("parallel", "parallel")),
        name=f"dil_prompt_g{group}",
    )(q.reshape(b, r, d * D_ATT), kvb.reshape(b, r, d * 2 * D_ATT))
    return o.reshape(b, s, D_ATT), lse.reshape(b, s, HEAD_DIM)


def _b_post_kernel(x_ref, o0_ref, o1_ref, o2_ref, l0_ref, l1_ref, l2_ref, gate_ref,
                   memo_ref, wout_ref, y_ref, cat_ref):
    l0, l1, l2 = l0_ref[...], l1_ref[...], l2_ref[...]
    m = jnp.maximum(jnp.maximum(l0, l1), l2)
    e0, e1, e2 = jnp.exp(l0 - m), jnp.exp(l1 - m), jnp.exp(l2 - m)
    den = e0 + e1 + e2
    w0, w1, w2 = e0 / den, e1 / den, e2 / den
    for h in range(N_HEADS):
        dil = (o0_ref[:, _hs(h)] * w0[:, h:h + 1] + o1_ref[:, _hs(h)] * w1[:, h:h + 1]
               + o2_ref[:, _hs(h)] * w2[:, h:h + 1])
        cat_ref[:, _hs(h)] = (dil * gate_ref[:, _hs(h)]).astype(BF16)
    cat_ref[:, D_ATT:] = memo_ref[...]
    y_ref[...] = x_ref[...] + _dot(cat_ref[...], wout_ref[...])


def _b_post_prompt(x, outs, lses, gate, memo, w_out, layer):
    b, s, _ = x.shape
    tile = lambda i, t: (i, t, 0)
    att_spec = pl.BlockSpec((None, TM, D_ATT), tile)
    lse_spec = pl.BlockSpec((None, TM, HEAD_DIM), tile)
    return pl.pallas_call(
        _b_post_kernel,
        grid=(b, s // TM),
        in_specs=[pl.BlockSpec((None, TM, D_MODEL), tile)] + [att_spec] * 3 + [lse_spec] * 3
        + [att_spec, att_spec, _const_spec((2 * D_ATT, D_MODEL))],
        out_specs=pl.BlockSpec((None, TM, D_MODEL), tile),
        out_shape=jax.ShapeDtypeStruct((b, s, D_MODEL), F32),
        scratch_shapes=[pltpu.VMEM((TM, 2 * D_ATT), BF16)],
        compiler_params=_params(("parallel", "parallel")),
        name=f"b_post_prompt_{layer}",
    )(x, *outs, *lses, gate, memo, w_out)


def _row_attend(q, k, v):
    prod = k * q
    res = []
    for h in range(N_HEADS):
        s = jnp.sum(prod[:, _hs(h)], axis=-1, keepdims=True)
        res.append((s, v[:, _hs(h)]))
    return res


def _sample_mem_attn(q, mkv, mg_row):
    outs = []
    for h, (s, vh) in enumerate(_row_attend(q, mkv[:, :D_ATT], mkv[:, D_ATT:])):
        m = jnp.max(s, axis=0, keepdims=True)
        p = jnp.exp(s - m)
        l = jnp.sum(p, axis=0, keepdims=True)
        o = jnp.sum(p * vh, axis=0, keepdims=True) / l
        outs.append(o * _silu(mg_row[:, _hs(h)]))
    return jnp.concatenate(outs, axis=-1)


def _norm_heads(row, g):
    return jnp.concatenate([_rms(row[:, _hs(h)], g) * SCALE for h in range(N_HEADS)], axis=-1)


def _a_sample_kernel(x_ref, g_ref, win_ref, cw_ref, st_ref, mkv_ref, gq_ref, wout_ref,
                     y_ref, sto_ref, zm_ref, cat_ref):
    i = pl.program_id(0)

    @pl.when(i == 0)
    def _():
        xn = _rms(x_ref[...], g_ref[...]).astype(BF16)
        z = _dot(xn, win_ref[...])
        u = z[:, D_CONV:2 * D_CONV] * z[:, 0:D_CONV]
        s0 = st_ref[:, 0:D_CONV]
        s1 = st_ref[:, D_CONV:]
        y = cw_ref[0:1, :] * s0 + cw_ref[1:2, :] * s1 + cw_ref[2:3, :] * u
        sto_ref[:, 0:D_CONV] = s1
        sto_ref[:, D_CONV:] = u
        cat_ref[:, 0:D_CONV] = _silu(z[:, 3 * D_CONV:4 * D_CONV]) * z[:, 2 * D_CONV:3 * D_CONV] * y
        zm_ref[...] = z[:, 4 * D_CONV:]

    rows = pl.ds(pl.multiple_of(i * SAMPLE_GB, SAMPLE_GB), SAMPLE_GB)
    zm = zm_ref[rows, :]
    mem_rows = []
    for j in range(SAMPLE_GB):
        row = zm[j:j + 1, :]
        q = _norm_heads(row[:, :D_ATT], gq_ref[...])
        mem_rows.append(_sample_mem_attn(q, mkv_ref[j], row[:, D_ATT:]))
    cat_ref[rows, D_CONV:] = jnp.concatenate(mem_rows, axis=0)

    @pl.when(i == pl.num_programs(0) - 1)
    def _():
        y_ref[...] = x_ref[...] + _dot(cat_ref[...].astype(BF16), wout_ref[...])


def _a_layer_sample(x, norm_g, w_in, conv_w, state, mem_cache, layer, gq, w_out):
    nb = x.shape[0]
    full = lambda shape: pl.BlockSpec(shape, lambda i: (0,) * len(shape))
    return pl.pallas_call(
        _a_sample_kernel,
        grid=(nb // SAMPLE_GB,),
        in_specs=[
            full((nb, D_MODEL)),
            full((1, D_MODEL)),
            _const_spec(w_in.shape),
            full((CONV_W, D_CONV)),
            full((nb, 2 * D_CONV)),
            pl.BlockSpec((None, SAMPLE_GB, MEM_LEN, 2 * D_ATT), lambda i: (layer, i, 0, 0)),
            full((1, HEAD_DIM)),
            _const_spec(w_out.shape),
        ],
        out_specs=[full((nb, D_MODEL)), full((nb, 2 * D_CONV))],
        out_shape=[jax.ShapeDtypeStruct((nb, D_MODEL), F32),
                   jax.ShapeDtypeStruct((nb, 2 * D_CONV), F32)],
        scratch_shapes=[pltpu.VMEM((nb, 2 * D_ATT), F32),
                        pltpu.VMEM((nb, D_CONV + D_ATT), F32)],
        compiler_params=_params(("arbitrary",)),
        name=f"a_layer_sample_{layer}",
    )(x, norm_g.reshape(1, D_MODEL), w_in, conv_w, state, mem_cache,
      gq.reshape(1, HEAD_DIM), w_out)


def _kv_sample_kernel(x_ref, g_ref, w_ref, kn_ref, o_ref):
    xn = _rms(x_ref[...], g_ref[...]).astype(BF16)
    kv = _dot(xn, w_ref[...])
    for g in range(N_GROUPS):
        base = g * 2 * D_ATT
        for h in range(N_HEADS):
            o_ref[:, _hs(h, base)] = _rms(kv[:, _hs(h, base)], kn_ref[g:g + 1, :])
        o_ref[:, base + D_ATT:base + 2 * D_ATT] = kv[:, base + D_ATT:base + 2 * D_ATT]


def _kv_sample(x, kv_norm, w_kv, k_norm_dil):
    nb = x.shape[0]
    return pl.pallas_call(
        _kv_sample_kernel,
        out_shape=jax.ShapeDtypeStruct((nb, N_GROUPS * 2 * D_ATT), F32),
        compiler_params=pltpu.CompilerParams(vmem_limit_bytes=VMEM_LIMIT),
        name="kv_sample",
    )(x, kv_norm.reshape(1, D_MODEL), w_kv, k_norm_dil)


def _b_sample_kernel(x_ref, g_ref, win_ref, qn_ref, kvn_ref, c0_ref, c1_ref, c2_ref,
                     mkv_ref, gq_ref, wout_ref, y_ref, z_ref, cat_ref):
    i = pl.program_id(0)

    @pl.when(i == 0)
    def _():
        xn = _rms(x_ref[...], g_ref[...]).astype(BF16)
        z_ref[...] = _dot(xn, win_ref[...])

    caches = (c0_ref, c1_ref, c2_ref)
    base = N_GROUPS * D_ATT
    rows = pl.ds(pl.multiple_of(i * SAMPLE_GB, SAMPLE_GB), SAMPLE_GB)
    z = z_ref[rows, :]
    kv_new_rows = kvn_ref[rows, :]
    cat_rows = []
    for j in range(SAMPLE_GB):
        row = z[j:j + 1, :]
        kv_new = kv_new_rows[j:j + 1, :]
        outs, lses = [], []
        for g in range(N_GROUPS):
            q = _norm_heads(row[:, g * D_ATT:(g + 1) * D_ATT], qn_ref[g:g + 1, :])
            kv = caches[g][j]
            k_new = kv_new[:, g * 2 * D_ATT:g * 2 * D_ATT + D_ATT]
            v_new = kv_new[:, g * 2 * D_ATT + D_ATT:(g + 1) * 2 * D_ATT]
            s_new_all = q * k_new
            og, lg = [], []
            for h, (s, vh) in enumerate(_row_attend(q, kv[:, :D_ATT], kv[:, D_ATT:])):
                s_new = jnp.sum(s_new_all[:, _hs(h)], axis=-1, keepdims=True)
                m = jnp.maximum(jnp.max(s, axis=0, keepdims=True), s_new)
                p = jnp.exp(s - m)
                p_new = jnp.exp(s_new - m)
                l = jnp.sum(p, axis=0, keepdims=True) + p_new
                o = (jnp.sum(p * vh, axis=0, keepdims=True) + p_new * v_new[:, _hs(h)]) / l
                og.append(o)
                lg.append(m + jnp.log(l))
            outs.append(og)
            lses.append(lg)
        gd = row[:, base:base + D_ATT]
        parts = []
        for h in range(N_HEADS):
            l0, l1, l2 = lses[0][h], lses[1][h], lses[2][h]
            m = jnp.maximum(jnp.maximum(l0, l1), l2)
            e0, e1, e2 = jnp.exp(l0 - m), jnp.exp(l1 - m), jnp.exp(l2 - m)
            den = e0 + e1 + e2
            dil = outs[0][h] * (e0 / den) + outs[1][h] * (e1 / den) + outs[2][h] * (e2 / den)
            parts.append(dil * _silu(gd[:, _hs(h)]))
        q = _norm_heads(row[:, base + D_ATT:base + 2 * D_ATT], gq_ref[...])
        parts.append(_sample_mem_attn(q, mkv_ref[j], row[:, base + 2 * D_ATT:]))
        cat_rows.append(jnp.concatenate(parts, axis=-1))
    cat_ref[rows, :] = jnp.concatenate(cat_rows, axis=0)

    @pl.when(i == pl.num_programs(0) - 1)
    def _():
        y_ref[...] = x_ref[...] + _dot(cat_ref[...].astype(BF16), wout_ref[...])


def _b_layer_sample(x, norm_g, w_in, q_norm, kv_new, caches, mem_cache, layer, gq, w_out):
    nb = x.shape[0]
    full = lambda shape: pl.BlockSpec(shape, lambda i: (0,) * len(shape))
    cache_spec = pl.BlockSpec((SAMPLE_GB, BAND, 2 * D_ATT), lambda i: (i, 0, 0))
    return pl.pallas_call(
        _b_sample_kernel,
        grid=(nb // SAMPLE_GB,),
        in_specs=[
            full((nb, D_MODEL)),
            full((1, D_MODEL)),
            _const_spec(w_in.shape),
            full((N_GROUPS, HEAD_DIM)),
            full(kv_new.shape),
            cache_spec, cache_spec, cache_spec,
            pl.BlockSpec((None, SAMPLE_GB, MEM_LEN, 2 * D_ATT), lambda i: (layer, i, 0, 0)),
            full((1, HEAD_DIM)),
            _const_spec(w_out.shape),
        ],
        out_specs=full((nb, D_MODEL)),
        out_shape=jax.ShapeDtypeStruct((nb, D_MODEL), F32),
        scratch_shapes=[pltpu.VMEM((nb, w_in.shape[1]), F32),
                        pltpu.VMEM((nb, 2 * D_ATT), F32)],
        compiler_params=_params(("arbitrary",)),
        name=f"b_layer_sample_{layer}",
    )(x, norm_g.reshape(1, D_MODEL), w_in, q_norm, kv_new, *caches, mem_cache,
      gq.reshape(1, HEAD_DIM), w_out)


def kernel(x_prompt, x_sample, cache_mem_kv, state_conv, cache_dil0_kv, cache_dil1_kv, cache_dil2_kv, mem_prompt, norm_a, w_in_a, conv_w_a, w_out_a, norm_b, w_in_b, q_norm_b, w_out_b, kv_norm, w_kv, k_norm_dil, mem_norm, w_mem_kv, mem_k_norm, mem_q_norm):
    n_a = w_in_a.shape[0]
    n_b = w_in_b.shape[0]
    depth = n_a + n_b
    b, s, _ = x_prompt.shape
    nb = x_sample.shape[0]
    assert x_sample.shape[1] == 1 and s % TM == 0 and nb % SAMPLE_GB == 0
    kv_dims = (2, N_HEADS, HEAD_DIM)

    w_in_a16, w_out_a16 = w_in_a.astype(BF16), w_out_a.astype(BF16)
    w_in_b16, w_out_b16 = w_in_b.astype(BF16), w_out_b.astype(BF16)
    w_kv16, w_mem16 = w_kv.astype(BF16), w_mem_kv.astype(BF16)

    mem_kv_f, mem_kv_b = _memkv_prompt(mem_prompt, mem_norm, w_mem16, mem_k_norm)
    mem_kv_b = mem_kv_b.reshape(depth, b, MEM_LEN, 2 * D_ATT)
    mem_cache = cache_mem_kv.reshape(depth, nb, MEM_LEN, 2 * D_ATT)
    caches = []
    for g, cache in enumerate((cache_dil0_kv, cache_dil1_kv, cache_dil2_kv)):
        d = DILATIONS[g]
        assert cache.shape[1] == WINDOWS[g]
        caches.append(cache.reshape(nb, BAND, d * 2 * D_ATT))

    xp = x_prompt
    xs = x_sample.reshape(nb, D_MODEL)
    conv_p, conv_s = [], []
    for a in range(n_a):
        xp, cp = _a_layer_prompt(xp, norm_a[a], w_in_a16[a], conv_w_a[a], mem_kv_b, a,
                                 mem_q_norm[a], w_out_a16[a])
        xs, cs = _a_layer_sample(xs, norm_a[a], w_in_a16[a], conv_w_a[a],
                                 state_conv[a].reshape(nb, 2 * D_CONV), mem_cache, a,
                                 mem_q_norm[a], w_out_a16[a])
        conv_p.append(cp)
        conv_s.append(cs.reshape(nb, CONV_W - 1, D_CONV))

    kvb0, kvb1, kvb2, st0, st1, st2 = _kv_prompt(xp, kv_norm, w_kv16, k_norm_dil)
    kvbs = (kvb0, kvb1, kvb2)
    kv_new = _kv_sample(xs, kv_norm, w_kv16, k_norm_dil)

    for bl in range(n_b):
        layer = n_a + bl
        q0, q1, q2, gate, memo = _b_pre_prompt(xp, norm_b[bl], w_in_b16[bl], q_norm_b[bl],
                                               mem_kv_b, layer, mem_q_norm[layer])
        outs, lses = [], []
        for g, q in enumerate((q0, q1, q2)):
            o, lse = _dil_prompt(q, kvbs[g], g)
            outs.append(o)
            lses.append(lse)
        xp = _b_post_prompt(xp, outs, lses, gate, memo, w_out_b16[bl], layer)
        xs = _b_layer_sample(xs, norm_b[bl], w_in_b16[bl], q_norm_b[bl], kv_new, caches,
                             mem_cache, layer, mem_q_norm[layer], w_out_b16[bl])

    dil_s = [kv_new[:, g * 2 * D_ATT:(g + 1) * 2 * D_ATT].reshape(nb, 1, *kv_dims)
             for g in range(N_GROUPS)]
    return (xp, xs.reshape(nb, 1, D_MODEL),
            mem_kv_f.reshape(depth, b, MEM_LEN, *kv_dims),
            jnp.stack(conv_p), jnp.stack(conv_s),
            st0.reshape(b, WINDOWS[0], *kv_dims), st1.reshape(b, WINDOWS[1], *kv_dims),
            st2.reshape(b, s, *kv_dims), *dil_s)
```

```python
import functools

import jax
import jax.numpy as jnp
from jax import lax
from jax.experimental import pallas as pl
from jax.experimental.pallas import tpu as pltpu

F32 = jnp.float32
BF16 = jnp.bfloat16

D_MODEL = 1024
HEAD_DIM = 128
N_HEADS = 4
D_ATT = N_HEADS * HEAD_DIM
KV_CHUNKS = 2 * N_HEADS
D_CONV = D_MODEL
CONV_W = 3
N_GROUPS = 3
DILATIONS = (1, 4, 16)
WINDOWS = (128, 512, 2048)
BAND = 128
MEM_LEN = 256
EPS = 1e-6
NEG = -1e30
SCALE = HEAD_DIM ** -0.5

TM = 512
SAMPLE_GB = 8
MAX_STREAMS = 4
VMEM_LIMIT = 56 * 1024 * 1024


def _rms(x, g):
    return x * lax.rsqrt(jnp.mean(x * x, axis=-1, keepdims=True) + EPS) * g


def _silu(x):
    return x * (1.0 / (1.0 + jnp.exp(-x)))


def _dot(a, b):
    return jnp.dot(a, b, preferred_element_type=F32)


def _dot_nt(a, b):
    return lax.dot_general(a, b, (((1,), (1,)), ((), ())), preferred_element_type=F32)


def _hs(h, base=0):
    return slice(base + h * HEAD_DIM, base + (h + 1) * HEAD_DIM)


def _params(sem):
    return pltpu.CompilerParams(dimension_semantics=sem, vmem_limit_bytes=VMEM_LIMIT)


def _fixed_spec(block, index):
    return pl.BlockSpec(block, lambda *_: index, pipeline_mode=pl.Buffered(1))


def _const_spec(shape):
    return _fixed_spec(shape, (0,) * len(shape))


def _layer_spec(shape, layer):
    return _fixed_spec((None,) + tuple(shape), (layer,) + (0,) * len(shape))


def _strided(start, size, stride):
    return pl.ds(start, size, stride=stride) if stride > 1 else pl.ds(start, size)


def _memkv_kernel(mem_ref, g_ref, w_ref, kg_ref, of_ref, ob_ref):
    rt = mem_ref.shape[0]
    xn = _rms(mem_ref[...], g_ref[...]).astype(BF16)
    kv = _dot(xn, w_ref[...])
    for c in range(KV_CHUNKS):
        chunk = kv[:, _hs(c)]
        if c < N_HEADS:
            chunk = _rms(chunk, kg_ref[...])
        of_ref[pl.ds(c, rt, stride=KV_CHUNKS), :] = chunk
        ob_ref[:, _hs(c)] = chunk.astype(BF16)


def _memkv_prompt(mem, mem_norm, w_mem_kv, mem_k_norm):
    depth = w_mem_kv.shape[0]
    rows = mem.shape[0] * mem.shape[1]
    rt = 1024
    return pl.pallas_call(
        _memkv_kernel,
        grid=(depth, rows // rt),
        in_specs=[
            pl.BlockSpec((rt, D_MODEL), lambda l, i: (i, 0)),
            pl.BlockSpec((None, 1, D_MODEL), lambda l, i: (l, 0, 0)),
            pl.BlockSpec((None, D_MODEL, 2 * D_ATT), lambda l, i: (l, 0, 0)),
            pl.BlockSpec((None, 1, HEAD_DIM), lambda l, i: (l, 0, 0)),
        ],
        out_specs=[
            pl.BlockSpec((None, rt * KV_CHUNKS, HEAD_DIM), lambda l, i: (l, i, 0)),
            pl.BlockSpec((None, rt, 2 * D_ATT), lambda l, i: (l, i, 0)),
        ],
        out_shape=[jax.ShapeDtypeStruct((depth, rows * KV_CHUNKS, HEAD_DIM), F32),
                   jax.ShapeDtypeStruct((depth, rows, 2 * D_ATT), BF16)],
        compiler_params=_params(("parallel", "parallel")),
        name="memkv_prompt",
    )(mem.reshape(rows, D_MODEL), mem_norm.reshape(depth, 1, D_MODEL), w_mem_kv,
      mem_k_norm.reshape(depth, 1, HEAD_DIM))


def _mem_attn(mq, mg, mkv_ref, gq, store):
    for h in range(N_HEADS):
        q = (_rms(mq[:, _hs(h)], gq) * SCALE).astype(BF16)
        s = _dot_nt(q, mkv_ref[:, _hs(h)])
        m = jnp.max(s, axis=-1, keepdims=True)
        p = jnp.exp(s - m)
        l = jnp.sum(p, axis=-1, keepdims=True)
        o = _dot(p.astype(BF16), mkv_ref[:, _hs(h, D_ATT)]) / l
        store(h, o * _silu(mg[:, _hs(h)]))


def _a_layer_kernel(x_ref, g_ref, win_ref, cw_ref, mkv_ref, gq_ref, wout_ref,
                    y_ref, cs_ref, ext_ref, cat_ref):
    t = pl.program_id(1)

    @pl.when(t == 0)
    def _():
        ext_ref[0:8, :] = jnp.zeros((8, D_CONV), F32)

    x = x_ref[...]
    xn = _rms(x, g_ref[...]).astype(BF16)

    def proj(lo, hi):
        return _dot(xn, win_ref[:, lo:hi])

    u = proj(D_CONV, 2 * D_CONV) * proj(0, D_CONV)
    ext_ref[8:8 + TM, :] = u
    y = (cw_ref[0:1, :] * ext_ref[6:6 + TM, :] + cw_ref[1:2, :] * ext_ref[7:7 + TM, :]
         + cw_ref[2:3, :] * u)
    ext_ref[0:8, :] = ext_ref[TM:TM + 8, :]
    cs_ref[...] = ext_ref[6:8, :]
    bg = proj(2 * D_CONV, 3 * D_CONV)
    g = proj(3 * D_CONV, 4 * D_CONV)
    cat_ref[:, 0:D_CONV] = (_silu(g) * bg * y).astype(BF16)
    mq = proj(4 * D_CONV, 4 * D_CONV + D_ATT)
    mg = proj(4 * D_CONV + D_ATT, 4 * D_CONV + 2 * D_ATT)

    def store(h, val):
        cat_ref[:, _hs(h, D_CONV)] = val.astype(BF16)

    _mem_attn(mq, mg, mkv_ref, gq_ref[...], store)
    y_ref[...] = x + _dot(cat_ref[...], wout_ref[...])


def _a_layer_prompt(x, norm_a, w_in, conv_w, mkv_b, a, layer, mem_q_norm, w_out):
    b, s, _ = x.shape
    return pl.pallas_call(
        _a_layer_kernel,
        grid=(b, s // TM),
        in_specs=[
            pl.BlockSpec((None, TM, D_MODEL), lambda i, t: (i, t, 0)),
            _layer_spec((1, D_MODEL), a),
            _layer_spec(w_in.shape[1:], a),
            _layer_spec((CONV_W, D_CONV), a),
            pl.BlockSpec((None, None, MEM_LEN, 2 * D_ATT), lambda i, t: (layer, i, 0, 0)),
            _layer_spec((1, HEAD_DIM), layer),
            _layer_spec(w_out.shape[1:], a),
        ],
        out_specs=[
            pl.BlockSpec((None, TM, D_MODEL), lambda i, t: (i, t, 0)),
            pl.BlockSpec((None, CONV_W - 1, D_CONV), lambda i, t: (i, 0, 0)),
        ],
        out_shape=[jax.ShapeDtypeStruct((b, s, D_MODEL), F32),
                   jax.ShapeDtypeStruct((b, CONV_W - 1, D_CONV), F32)],
        scratch_shapes=[pltpu.VMEM((TM + 8, D_CONV), F32),
                        pltpu.VMEM((TM, D_CONV + D_ATT), BF16)],
        compiler_params=_params(("arbitrary", "arbitrary")),
        name=f"a_layer_prompt_{layer}",
    )(x, norm_a, w_in, conv_w, mkv_b, mem_q_norm, w_out)


def _to_residue_major(dst_ref, slab_ref, n_chunks, d):
    rows = slab_ref.shape[1] // d
    for r in range(d):
        for c in range(n_chunks):
            dst_ref[r, :, _hs(c)] = slab_ref[c, pl.ds(r, rows, stride=d), :].astype(dst_ref.dtype)


def _kv_kernel(x_ref, g_ref, w_ref, kn_ref, kvb0_ref, kvb1_ref, kvb2_ref,
               st0_ref, st1_ref, st2_ref, slab_ref):
    t = pl.program_id(1)
    last = pl.num_programs(1) - 1
    xn = _rms(x_ref[...], g_ref[...]).astype(BF16)
    kvb_refs = (kvb0_ref, kvb1_ref, kvb2_ref)
    for g in range(N_GROUPS):
        kv = _dot(xn, w_ref[:, g * 2 * D_ATT:(g + 1) * 2 * D_ATT])
        chunks = [_rms(kv[:, _hs(h)], kn_ref[g:g + 1, :]) for h in range(N_HEADS)]
        chunks += [kv[:, _hs(h, D_ATT)] for h in range(N_HEADS)]
        d = DILATIONS[g]
        for c, chunk in enumerate(chunks):
            if d == 1:
                kvb_refs[g][0, :, _hs(c)] = chunk.astype(BF16)
            else:
                slab_ref[c] = chunk
            if g == 2:
                st2_ref[pl.ds(c, TM, stride=KV_CHUNKS), :] = chunk
        if d > 1:
            _to_residue_major(kvb_refs[g], slab_ref, KV_CHUNKS, d)
        if g == 1:
            @pl.when(t == last)
            def _():
                for c, chunk in enumerate(chunks):
                    st1_ref[pl.ds(c, TM, stride=KV_CHUNKS), :] = chunk
        if g == 0:
            @pl.when(t == last)
            def _():
                for c, chunk in enumerate(chunks):
                    st0_ref[pl.ds(c, WINDOWS[0], stride=KV_CHUNKS), :] = chunk[TM - WINDOWS[0]:, :]


def _kv_prompt(x, kv_norm, w_kv, k_norm_dil):
    b, s, _ = x.shape
    assert WINDOWS[1] == TM and WINDOWS[2] == s and WINDOWS[0] <= TM
    first = lambda i, t: (i, 0, 0)
    res_spec = lambda d: pl.BlockSpec((None, d, TM // d, 2 * D_ATT), lambda i, t: (i, 0, t, 0))
    res_sds = lambda d: jax.ShapeDtypeStruct((b, d, s // d, 2 * D_ATT), BF16)
    st_sds = lambda rows: jax.ShapeDtypeStruct((b, rows * KV_CHUNKS, HEAD_DIM), F32)
    return pl.pallas_call(
        _kv_kernel,
        grid=(b, s // TM),
        in_specs=[
            pl.BlockSpec((None, TM, D_MODEL), lambda i, t: (i, t, 0)),
            _const_spec((1, D_MODEL)),
            _const_spec((D_MODEL, N_GROUPS * 2 * D_ATT)),
            _const_spec((N_GROUPS, HEAD_DIM)),
        ],
        out_specs=[
            res_spec(DILATIONS[0]), res_spec(DILATIONS[1]), res_spec(DILATIONS[2]),
            pl.BlockSpec((None, WINDOWS[0] * KV_CHUNKS, HEAD_DIM), first),
            pl.BlockSpec((None, WINDOWS[1] * KV_CHUNKS, HEAD_DIM), first),
            pl.BlockSpec((None, TM * KV_CHUNKS, HEAD_DIM), lambda i, t: (i, t, 0)),
        ],
        out_shape=[res_sds(DILATIONS[0]), res_sds(DILATIONS[1]), res_sds(DILATIONS[2]),
                   st_sds(WINDOWS[0]), st_sds(WINDOWS[1]), st_sds(s)],
        scratch_shapes=[pltpu.VMEM((KV_CHUNKS, TM, HEAD_DIM), F32)],
        compiler_params=_params(("arbitrary", "arbitrary")),
        name="kv_prompt",
    )(x, kv_norm.reshape(1, D_MODEL), w_kv, k_norm_dil)


def _b_pre_kernel(x_ref, g_ref, win_ref, qn_ref, mkv_ref, gq_ref,
                  q0_ref, q1_ref, q2_ref, gate_ref, memo_ref, slab_ref):
    xn = _rms(x_ref[...], g_ref[...]).astype(BF16)

    def proj(lo, hi):
        return _dot(xn, win_ref[:, lo:hi])

    q_refs = (q0_ref, q1_ref, q2_ref)
    for g in range(N_GROUPS):
        qd = proj(g * D_ATT, (g + 1) * D_ATT)
        d = DILATIONS[g]
        for h in range(N_HEADS):
            qh = _rms(qd[:, _hs(h)], qn_ref[g:g + 1, :]) * SCALE
            if d == 1:
                q_refs[g][0, :, _hs(h)] = qh.astype(BF16)
            else:
                slab_ref[h] = qh
        if d > 1:
            _to_residue_major(q_refs[g], slab_ref, N_HEADS, d)
    base = N_GROUPS * D_ATT
    gate_ref[...] = _silu(proj(base, base + D_ATT))
    mq = proj(base + D_ATT, base + 2 * D_ATT)
    mg = proj(base + 2 * D_ATT, base + 3 * D_ATT)

    def store(h, val):
        memo_ref[:, _hs(h)] = val.astype(BF16)

    _mem_attn(mq, mg, mkv_ref, gq_ref[...], store)


def _b_pre_prompt(x, norm_b, w_in, q_norm, mkv_b, bl, layer, mem_q_norm):
    b, s, _ = x.shape
    tile = lambda i, t: (i, t, 0)
    res_spec = lambda d: pl.BlockSpec((None, d, TM // d, D_ATT), lambda i, t: (i, 0, t, 0))
    res_sds = lambda d: jax.ShapeDtypeStruct((b, d, s // d, D_ATT), BF16)
    att_spec = pl.BlockSpec((None, TM, D_ATT), tile)
    return pl.pallas_call(
        _b_pre_kernel,
        grid=(b, s // TM),
        in_specs=[
            pl.BlockSpec((None, TM, D_MODEL), tile),
            _layer_spec((1, D_MODEL), bl),
            _layer_spec(w_in.shape[1:], bl),
            _layer_spec((N_GROUPS, HEAD_DIM), bl),
            pl.BlockSpec((None, None, MEM_LEN, 2 * D_ATT), lambda i, t: (layer, i, 0, 0)),
            _layer_spec((1, HEAD_DIM), layer),
        ],
        out_specs=[res_spec(DILATIONS[0]), res_spec(DILATIONS[1]), res_spec(DILATIONS[2]),
                   att_spec, att_spec],
        out_shape=[res_sds(DILATIONS[0]), res_sds(DILATIONS[1]), res_sds(DILATIONS[2]),
                   jax.ShapeDtypeStruct((b, s, D_ATT), F32),
                   jax.ShapeDtypeStruct((b, s, D_ATT), BF16)],
        scratch_shapes=[pltpu.VMEM((N_HEADS, TM, HEAD_DIM), F32)],
        compiler_params=_params(("parallel", "parallel")),
        name=f"b_pre_prompt_{layer}",
    )(x, norm_b, w_in, q_norm, mkv_b, mem_q_norm)


def _band_units(qs, ks, vs, mask):
    s = jnp.concatenate([jnp.where(mask, _dot_nt(q, k), NEG) for q, k in zip(qs, ks)], axis=0)
    m = jnp.max(s, axis=-1, keepdims=True)
    p = jnp.exp(s - m)
    l = jnp.sum(p, axis=-1, keepdims=True)
    pb = p.astype(BF16)
    inv = 1.0 / l
    lse = m + jnp.log(l)
    res = []
    for i, v in enumerate(vs):
        rows = slice(i * BAND, (i + 1) * BAND)
        res.append((_dot(pb[rows], v) * inv[rows], lse[rows]))
    return res


def _dil_kernel(q_ref, kv_ref, o_ref, lse_ref, *, n_streams, n_blocks, unroll):
    qi = lax.broadcasted_iota(jnp.int32, (BAND, BAND), 0)
    kj = lax.broadcasted_iota(jnp.int32, (BAND, BAND), 1)
    mask_first = kj <= qi
    qi2 = lax.broadcasted_iota(jnp.int32, (BAND, 2 * BAND), 0)
    kj2 = lax.broadcasted_iota(jnp.int32, (BAND, 2 * BAND), 1)
    mask_rest = (kj2 >= qi2) & (kj2 <= qi2 + BAND)
    lane = lax.broadcasted_iota(jnp.int32, (BAND, HEAD_DIM), 1)

    def blocks(q_rows, k_rows, mask):
        units = [(p, h) for p in range(n_streams) for h in range(N_HEADS)]
        res = _band_units([q_ref[p, q_rows, _hs(h)] for p, h in units],
                          [kv_ref[p, k_rows, _hs(h)] for p, h in units],
                          [kv_ref[p, k_rows, _hs(h, D_ATT)] for p, h in units], mask)
        for p in range(n_streams):
            lse_tile = jnp.zeros((BAND, HEAD_DIM), F32)
            for h in range(N_HEADS):
                o, lse = res[p * N_HEADS + h]
                o_ref[p, q_rows, _hs(h)] = o
                lse_tile = jnp.where(lane == h, lse, lse_tile)
            lse_ref[p, q_rows, :] = lse_tile

    blocks(pl.ds(0, BAND), pl.ds(0, BAND), mask_first)
    if n_blocks > 1:
        def body(n, carry):
            q0 = pl.multiple_of(n * BAND, BAND)
            k0 = pl.multiple_of((n - 1) * BAND, BAND)
            blocks(pl.ds(q0, BAND), pl.ds(k0, 2 * BAND), mask_rest)
            return carry
        lax.fori_loop(1, n_blocks, body, 0, unroll=unroll)


def _dil_prompt(q, kvb, group):
    b, d, r, _ = q.shape
    assert WINDOWS[group] // d == BAND and r % BAND == 0
    n_streams = min(d, MAX_STREAMS)
    n_blocks = r // BAND
    blk = lambda i, j: (i, j, 0, 0)
    spec = lambda width: pl.BlockSpec((None, n_streams, r, width), blk)
    return pl.pallas_call(
        functools.partial(_dil_kernel, n_streams=n_streams, n_blocks=n_blocks,
                          unroll=3 if n_blocks > 4 else True),
        grid=(b, d // n_streams),
        in_specs=[spec(D_ATT), spec(2 * D_ATT)],
        out_specs=[spec(D_ATT), spec(HEAD_DIM)],
        out_shape=[jax.ShapeDtypeStruct((b, d, r, D_ATT), F32),
                   jax.ShapeDtypeStruct((b, d, r, HEAD_DIM), F32)],
        compiler_params=_params(("parallel", "parallel")),
        name=f"dil_prompt_g{group}",
    )(q, kvb)


def _b_post_kernel(x_ref, o0_ref, o1_ref, o2_ref, l0_ref, l1_ref, l2_ref, gate_ref,
                   memo_ref, wout_ref, y_ref, oslab_ref, lslab_ref, cat_ref):
    for g, (o_ref, l_ref) in enumerate(((o1_ref, l1_ref), (o2_ref, l2_ref))):
        d = DILATIONS[g + 1]
        for r in range(d):
            rows = pl.ds(r, TM // d, stride=d)
            for h in range(N_HEADS):
                oslab_ref[g, h, rows, :] = o_ref[r, :, _hs(h)]
            lslab_ref[g, rows, :] = l_ref[r]
    l0, l1, l2 = l0_ref[0], lslab_ref[0], lslab_ref[1]
    m = jnp.maximum(jnp.maximum(l0, l1), l2)
    e0, e1, e2 = jnp.exp(l0 - m), jnp.exp(l1 - m), jnp.exp(l2 - m)
    den = e0 + e1 + e2
    w0, w1, w2 = e0 / den, e1 / den, e2 / den
    for h in range(N_HEADS):
        dil = (o0_ref[0, :, _hs(h)] * w0[:, h:h + 1] + oslab_ref[0, h] * w1[:, h:h + 1]
               + oslab_ref[1, h] * w2[:, h:h + 1])
        cat_ref[:, _hs(h)] = (dil * gate_ref[:, _hs(h)]).astype(BF16)
    cat_ref[:, D_ATT:] = memo_ref[...]
    y_ref[...] = x_ref[...] + _dot(cat_ref[...], wout_ref[...])


def _b_post_prompt(x, outs, lses, gate, memo, w_out, bl, layer):
    b, s, _ = x.shape
    tile = lambda i, t: (i, t, 0)
    res_spec = lambda d, width: pl.BlockSpec((None, d, TM // d, width), lambda i, t: (i, 0, t, 0))
    att_spec = pl.BlockSpec((None, TM, D_ATT), tile)
    return pl.pallas_call(
        _b_post_kernel,
        grid=(b, s // TM),
        in_specs=[pl.BlockSpec((None, TM, D_MODEL), tile)]
        + [res_spec(d, D_ATT) for d in DILATIONS] + [res_spec(d, HEAD_DIM) for d in DILATIONS]
        + [att_spec, att_spec, _layer_spec(w_out.shape[1:], bl)],
        out_specs=pl.BlockSpec((None, TM, D_MODEL), tile),
        out_shape=jax.ShapeDtypeStruct((b, s, D_MODEL), F32),
        scratch_shapes=[pltpu.VMEM((N_GROUPS - 1, N_HEADS, TM, HEAD_DIM), F32),
                        pltpu.VMEM((N_GROUPS - 1, TM, HEAD_DIM), F32),
                        pltpu.VMEM((TM, 2 * D_ATT), BF16)],
        compiler_params=_params(("parallel", "parallel")),
        name=f"b_post_prompt_{layer}",
    )(x, *outs, *lses, gate, memo, w_out)


def _swap_halves(x, axis):
    return pltpu.roll(x, N_HEADS, axis=axis)


def _head_rows(tile):
    return jnp.concatenate([tile[N_HEADS + h:N_HEADS + h + 1, :] for h in range(N_HEADS)],
                           axis=-1)


def _low_tile(row):
    return jnp.concatenate([jnp.zeros((N_HEADS, HEAD_DIM), F32)]
                           + [row[:, _hs(h)] for h in range(N_HEADS)], axis=0)


def _q_tile(row, g):
    return jnp.concatenate([_rms(row[:, _hs(h)], g) * SCALE for h in range(N_HEADS)]
                           + [jnp.zeros((N_HEADS, HEAD_DIM), F32)], axis=0)


def _tile_attend(qt, kv):
    s = jnp.sum(kv * qt[None], axis=-1, keepdims=True)
    m = jnp.max(s, axis=0)
    p = jnp.exp(s - m[None])
    l = jnp.sum(p, axis=0)
    acc = jnp.sum(_swap_halves(jnp.broadcast_to(p, kv.shape), 1) * kv, axis=0)
    out = acc / _swap_halves(jnp.broadcast_to(l, acc.shape), 0)
    return out, m + jnp.log(l)


def _sample_mem_attn(q_row, gq, mkv, mg_row):
    out, _ = _tile_attend(_q_tile(q_row, gq), mkv)
    return _head_rows(out * _low_tile(_silu(mg_row)))


def _gather_chunks(ref, n_rows, n_chunks, offset=0):
    total = ref.shape[0] // n_rows
    return jnp.concatenate([ref[pl.ds(offset + c, n_rows, stride=total), :]
                            for c in range(n_chunks)], axis=-1)


def _a_sample_kernel(x_ref, g_ref, win_ref, cw_ref, st_ref, mkv_ref, gq_ref, wout_ref,
                     y_ref, sto_ref, zm_ref, cat_ref):
    i = pl.program_id(0)
    nb = x_ref.shape[0]
    n_st = D_CONV // HEAD_DIM

    @pl.when(i == 0)
    def _():
        xn = _rms(x_ref[...], g_ref[...]).astype(BF16)
        z = _dot(xn, win_ref[...])
        u = z[:, D_CONV:2 * D_CONV] * z[:, 0:D_CONV]
        s0 = _gather_chunks(st_ref, nb, n_st)
        s1 = _gather_chunks(st_ref, nb, n_st, n_st)
        y = cw_ref[0:1, :] * s0 + cw_ref[1:2, :] * s1 + cw_ref[2:3, :] * u
        for c in range(n_st):
            sto_ref[pl.ds(c, nb, stride=2 * n_st), :] = s1[:, _hs(c)]
            sto_ref[pl.ds(n_st + c, nb, stride=2 * n_st), :] = u[:, _hs(c)]
        cat_ref[:, 0:D_CONV] = _silu(z[:, 3 * D_CONV:4 * D_CONV]) * z[:, 2 * D_CONV:3 * D_CONV] * y
        zm_ref[...] = z[:, 4 * D_CONV:]

    sub = lax.broadcasted_iota(jnp.int32, (SAMPLE_GB, D_ATT), 0)

    def body(j, tile):
        row = zm_ref[pl.ds(i * SAMPLE_GB + j, 1), :]
        out = _sample_mem_attn(row[:, :D_ATT], gq_ref[...], mkv_ref[j], row[:, D_ATT:])
        return jnp.where(sub == j, out, tile)

    rows = pl.ds(pl.multiple_of(i * SAMPLE_GB, SAMPLE_GB), SAMPLE_GB)
    cat_ref[rows, D_CONV:] = lax.fori_loop(0, SAMPLE_GB, body, jnp.zeros((SAMPLE_GB, D_ATT), F32))

    @pl.when(i == pl.num_programs(0) - 1)
    def _():
        y_ref[...] = x_ref[...] + _dot(cat_ref[...].astype(BF16), wout_ref[...])


def _a_layer_sample(x, norm_a, w_in, conv_w, state, mem_cache, a, layer, mem_q_norm, w_out):
    nb = x.shape[0]
    full = lambda shape: pl.BlockSpec(shape, lambda i: (0,) * len(shape))
    st_rows = nb * (CONV_W - 1) * D_CONV // HEAD_DIM
    return pl.pallas_call(
        _a_sample_kernel,
        grid=(nb // SAMPLE_GB,),
        in_specs=[
            full((nb, D_MODEL)),
            _layer_spec((1, D_MODEL), a),
            _layer_spec(w_in.shape[1:], a),
            _layer_spec((CONV_W, D_CONV), a),
            _layer_spec((st_rows, HEAD_DIM), a),
            pl.BlockSpec((None, SAMPLE_GB, MEM_LEN, KV_CHUNKS, HEAD_DIM),
                         lambda i: (layer, i, 0, 0, 0)),
            _layer_spec((1, HEAD_DIM), layer),
            _layer_spec(w_out.shape[1:], a),
        ],
        out_specs=[full((nb, D_MODEL)), full((st_rows, HEAD_DIM))],
        out_shape=[jax.ShapeDtypeStruct((nb, D_MODEL), F32),
                   jax.ShapeDtypeStruct((st_rows, HEAD_DIM), F32)],
        scratch_shapes=[pltpu.VMEM((nb, 2 * D_ATT), F32),
                        pltpu.VMEM((nb, D_CONV + D_ATT), F32)],
        compiler_params=_params(("arbitrary",)),
        name=f"a_layer_sample_{layer}",
    )(x, norm_a, w_in, conv_w, state, mem_cache, mem_q_norm, w_out)


def _kv_sample_kernel(x_ref, g_ref, w_ref, kn_ref, o_ref):
    nb = x_ref.shape[0]
    xn = _rms(x_ref[...], g_ref[...]).astype(BF16)
    kv = _dot(xn, w_ref[...])
    n_chunks = N_GROUPS * KV_CHUNKS
    for g in range(N_GROUPS):
        for c in range(KV_CHUNKS):
            chunk = kv[:, _hs(c, g * 2 * D_ATT)]
            if c < N_HEADS:
                chunk = _rms(chunk, kn_ref[g:g + 1, :])
            o_ref[pl.ds(g * KV_CHUNKS + c, nb, stride=n_chunks), :] = chunk


def _kv_sample(x, kv_norm, w_kv, k_norm_dil):
    nb = x.shape[0]
    return pl.pallas_call(
        _kv_sample_kernel,
        out_shape=jax.ShapeDtypeStruct((nb * N_GROUPS * KV_CHUNKS, HEAD_DIM), F32),
        compiler_params=pltpu.CompilerParams(vmem_limit_bytes=VMEM_LIMIT),
        name="kv_sample",
    )(x, kv_norm.reshape(1, D_MODEL), w_kv, k_norm_dil)


def _b_sample_kernel(x_ref, g_ref, win_ref, qn_ref, kvn_ref, c0_ref, c1_ref, c2_ref,
                     mkv_ref, gq_ref, wout_ref, y_ref, z_ref, cat_ref):
    i = pl.program_id(0)

    @pl.when(i == 0)
    def _():
        xn = _rms(x_ref[...], g_ref[...]).astype(BF16)
        z_ref[...] = _dot(xn, win_ref[...])

    caches = (c0_ref, c1_ref, c2_ref)
    base = N_GROUPS * D_ATT
    sub = lax.broadcasted_iota(jnp.int32, (SAMPLE_GB, 2 * D_ATT), 0)

    def body(j, tile):
        row = z_ref[pl.ds(i * SAMPLE_GB + j, 1), :]
        outs, lses = [], []
        for g in range(N_GROUPS):
            qt = _q_tile(row[:, g * D_ATT:(g + 1) * D_ATT], qn_ref[g:g + 1, :])
            kv = jnp.concatenate([caches[g][j], kvn_ref[i * SAMPLE_GB + j, g][None]], axis=0)
            out, lse = _tile_attend(qt, kv)
            outs.append(out)
            lses.append(lse)
        m = jnp.maximum(jnp.maximum(lses[0], lses[1]), lses[2])
        es = [jnp.exp(l - m) for l in lses]
        den = es[0] + es[1] + es[2]
        ws = [_swap_halves(jnp.broadcast_to(e / den, (KV_CHUNKS, HEAD_DIM)), 0) for e in es]
        dil = outs[0] * ws[0] + outs[1] * ws[1] + outs[2] * ws[2]
        dil_row = _head_rows(dil * _low_tile(_silu(row[:, base:base + D_ATT])))
        mem_row = _sample_mem_attn(row[:, base + D_ATT:base + 2 * D_ATT], gq_ref[...],
                                   mkv_ref[j], row[:, base + 2 * D_ATT:])
        return jnp.where(sub == j, jnp.concatenate([dil_row, mem_row], axis=-1), tile)

    rows = pl.ds(pl.multiple_of(i * SAMPLE_GB, SAMPLE_GB), SAMPLE_GB)
    cat_ref[rows, :] = lax.fori_loop(0, SAMPLE_GB, body, jnp.zeros((SAMPLE_GB, 2 * D_ATT), F32))

    @pl.when(i == pl.num_programs(0) - 1)
    def _():
        y_ref[...] = x_ref[...] + _dot(cat_ref[...].astype(BF16), wout_ref[...])


def _b_layer_sample(x, norm_b, w_in, q_norm, kv_new, caches, mem_cache, bl, layer,
                    mem_q_norm, w_out):
    nb = x.shape[0]
    full = lambda shape: pl.BlockSpec(shape, lambda i: (0,) * len(shape))
    cache_spec = pl.BlockSpec((SAMPLE_GB, BAND, None, KV_CHUNKS, HEAD_DIM),
                              lambda i: (i, 0, 0, 0, 0))
    return pl.pallas_call(
        _b_sample_kernel,
        grid=(nb // SAMPLE_GB,),
        in_specs=[
            full((nb, D_MODEL)),
            _layer_spec((1, D_MODEL), bl),
            _layer_spec(w_in.shape[1:], bl),
            _layer_spec((N_GROUPS, HEAD_DIM), bl),
            full(kv_new.shape),
            cache_spec, cache_spec, cache_spec,
            pl.BlockSpec((None, SAMPLE_GB, MEM_LEN, KV_CHUNKS, HEAD_DIM),
                         lambda i: (layer, i, 0, 0, 0)),
            _layer_spec((1, HEAD_DIM), layer),
            _layer_spec(w_out.shape[1:], bl),
        ],
        out_specs=full((nb, D_MODEL)),
        out_shape=jax.ShapeDtypeStruct((nb, D_MODEL), F32),
        scratch_shapes=[pltpu.VMEM((nb, w_in.shape[2]), F32),
                        pltpu.VMEM((nb, 2 * D_ATT), F32)],
        compiler_params=_params(("arbitrary",)),
        name=f"b_layer_sample_{layer}",
    )(x, norm_b, w_in, q_norm, kv_new, *caches, mem_cache, mem_q_norm, w_out)


def kernel(x_prompt, x_sample, cache_mem_kv, state_conv, cache_dil0_kv, cache_dil1_kv, cache_dil2_kv, mem_prompt, norm_a, w_in_a, conv_w_a, w_out_a, norm_b, w_in_b, q_norm_b, w_out_b, kv_norm, w_kv, k_norm_dil, mem_norm, w_mem_kv, mem_k_norm, mem_q_norm):
    n_a = w_in_a.shape[0]
    n_b = w_in_b.shape[0]
    depth = n_a + n_b
    b, s, _ = x_prompt.shape
    nb = x_sample.shape[0]
    assert x_sample.shape[1] == 1 and s % TM == 0 and nb % SAMPLE_GB == 0
    kv_dims = (2, N_HEADS, HEAD_DIM)

    w_in_a16, w_out_a16 = w_in_a.astype(BF16), w_out_a.astype(BF16)
    w_in_b16, w_out_b16 = w_in_b.astype(BF16), w_out_b.astype(BF16)
    w_kv16, w_mem16 = w_kv.astype(BF16), w_mem_kv.astype(BF16)
    norm_a3 = norm_a.reshape(n_a, 1, D_MODEL)
    norm_b3 = norm_b.reshape(n_b, 1, D_MODEL)
    mem_q_norm3 = mem_q_norm.reshape(depth, 1, HEAD_DIM)

    mem_kv_f, mem_kv_b = _memkv_prompt(mem_prompt, mem_norm, w_mem16, mem_k_norm)
    mem_kv_b = mem_kv_b.reshape(depth, b, MEM_LEN, 2 * D_ATT)
    mem_cache = cache_mem_kv.reshape(depth, nb, MEM_LEN, KV_CHUNKS, HEAD_DIM)
    caches = []
    for g, cache in enumerate((cache_dil0_kv, cache_dil1_kv, cache_dil2_kv)):
        assert cache.shape[1] == WINDOWS[g]
        caches.append(cache.reshape(nb, BAND, DILATIONS[g], KV_CHUNKS, HEAD_DIM))
    state = state_conv.reshape(n_a, -1, HEAD_DIM)

    xp = x_prompt
    xs = x_sample.reshape(nb, D_MODEL)
    conv_p, conv_s = [], []
    for a in range(n_a):
        xp, cp = _a_layer_prompt(xp, norm_a3, w_in_a16, conv_w_a, mem_kv_b, a, a,
                                 mem_q_norm3, w_out_a16)
        xs, cs = _a_layer_sample(xs, norm_a3, w_in_a16, conv_w_a, state, mem_cache, a, a,
                                 mem_q_norm3, w_out_a16)
        conv_p.append(cp)
        conv_s.append(cs.reshape(nb, CONV_W - 1, D_CONV))

    kvb0, kvb1, kvb2, st0, st1, st2 = _kv_prompt(xp, kv_norm, w_kv16, k_norm_dil)
    kvbs = (kvb0, kvb1, kvb2)
    kv_new = _kv_sample(xs, kv_norm, w_kv16, k_norm_dil).reshape(nb, N_GROUPS, KV_CHUNKS, HEAD_DIM)

    for bl in range(n_b):
        layer = n_a + bl
        q0, q1, q2, gate, memo = _b_pre_prompt(xp, norm_b3, w_in_b16, q_norm_b, mem_kv_b, bl,
                                               layer, mem_q_norm3)
        outs, lses = [], []
        for g, q in enumerate((q0, q1, q2)):
            o, lse = _dil_prompt(q, kvbs[g], g)
            outs.append(o)
            lses.append(lse)
        xp = _b_post_prompt(xp, outs, lses, gate, memo, w_out_b16, bl, layer)
        xs = _b_layer_sample(xs, norm_b3, w_in_b16, q_norm_b, kv_new, caches, mem_cache, bl,
                             layer, mem_q_norm3, w_out_b16)

    dil_s = [kv_new[:, g].reshape(nb, 1, *kv_dims) for g in range(N_GROUPS)]
    return (xp, xs.reshape(nb, 1, D_MODEL),
            mem_kv_f.reshape(depth, b, MEM_LEN, *kv_dims),
            jnp.stack(conv_p), jnp.stack(conv_s),
            st0.reshape(b, WINDOWS[0], *kv_dims), st1.reshape(b, WINDOWS[1], *kv_dims),
            st2.reshape(b, s, *kv_dims), *dil_s)
```

```python
import jax
import jax.numpy as jnp
from jax import lax
from jax.experimental import pallas as pl
from jax.experimental.pallas import tpu as pltpu

F32 = jnp.float32
BF16 = jnp.bfloat16

D_MODEL = 1024
HEAD_DIM = 128
N_HEADS = 4
D_ATT = N_HEADS * HEAD_DIM
KV_CHUNKS = 2 * N_HEADS
D_CONV = D_MODEL
CONV_W = 3
N_GROUPS = 3
DILATIONS = (1, 4, 16)
WINDOWS = (128, 512, 2048)
BAND = 128
MEM_LEN = 256
EPS = 1e-6
NEG = -1e30
SCALE = HEAD_DIM ** -0.5

TM = 512
SAMPLE_GB = 8
VMEM_LIMIT = 56 * 1024 * 1024


def _rms(x, g):
    return x * lax.rsqrt(jnp.mean(x * x, axis=-1, keepdims=True) + EPS) * g


def _silu(x):
    return x * (1.0 / (1.0 + jnp.exp(-x)))


def _dot(a, b):
    return jnp.dot(a, b, preferred_element_type=F32)


def _dot_nt(a, b):
    return lax.dot_general(a, b, (((1,), (1,)), ((), ())), preferred_element_type=F32)


def _hs(h, base=0):
    return slice(base + h * HEAD_DIM, base + (h + 1) * HEAD_DIM)


def _params(sem):
    return pltpu.CompilerParams(dimension_semantics=sem, vmem_limit_bytes=VMEM_LIMIT)


def _fixed_spec(block, index):
    return pl.BlockSpec(block, lambda *_: index, pipeline_mode=pl.Buffered(1))


def _const_spec(shape):
    return _fixed_spec(shape, (0,) * len(shape))


def _layer_spec(shape, layer):
    return _fixed_spec((None,) + tuple(shape), (layer,) + (0,) * len(shape))


def _memkv_kernel(mem_ref, g_ref, w_ref, kg_ref, of_ref, ob_ref):
    rt = mem_ref.shape[0]
    xn = _rms(mem_ref[...], g_ref[...]).astype(BF16)
    kv = _dot(xn, w_ref[...])
    for c in range(KV_CHUNKS):
        chunk = kv[:, _hs(c)]
        if c < N_HEADS:
            chunk = _rms(chunk, kg_ref[...])
        of_ref[pl.ds(c, rt, stride=KV_CHUNKS), :] = chunk
        ob_ref[:, _hs(c)] = chunk.astype(BF16)


def _memkv_prompt(mem, mem_norm, w_mem_kv, mem_k_norm):
    depth = w_mem_kv.shape[0]
    rows = mem.shape[0] * mem.shape[1]
    rt = 1024
    return pl.pallas_call(
        _memkv_kernel,
        grid=(depth, rows // rt),
        in_specs=[
            pl.BlockSpec((rt, D_MODEL), lambda l, i: (i, 0)),
            pl.BlockSpec((None, 1, D_MODEL), lambda l, i: (l, 0, 0)),
            pl.BlockSpec((None, D_MODEL, 2 * D_ATT), lambda l, i: (l, 0, 0)),
            pl.BlockSpec((None, 1, HEAD_DIM), lambda l, i: (l, 0, 0)),
        ],
        out_specs=[
            pl.BlockSpec((None, rt * KV_CHUNKS, HEAD_DIM), lambda l, i: (l, i, 0)),
            pl.BlockSpec((None, rt, 2 * D_ATT), lambda l, i: (l, i, 0)),
        ],
        out_shape=[jax.ShapeDtypeStruct((depth, rows * KV_CHUNKS, HEAD_DIM), F32),
                   jax.ShapeDtypeStruct((depth, rows, 2 * D_ATT), BF16)],
        compiler_params=_params(("parallel", "parallel")),
        name="memkv_prompt",
    )(mem.reshape(rows, D_MODEL), mem_norm.reshape(depth, 1, D_MODEL), w_mem_kv,
      mem_k_norm.reshape(depth, 1, HEAD_DIM))


def _mem_attn(mq, mg, mkv_ref, gq, store):
    for h in range(N_HEADS):
        q = (_rms(mq[:, _hs(h)], gq) * SCALE).astype(BF16)
        s = _dot_nt(q, mkv_ref[:, _hs(h)])
        m = jnp.max(s, axis=-1, keepdims=True)
        p = jnp.exp(s - m)
        l = jnp.sum(p, axis=-1, keepdims=True)
        o = _dot(p.astype(BF16), mkv_ref[:, _hs(h, D_ATT)]) / l
        store(h, o * _silu(mg[:, _hs(h)]))


def _a_layer_kernel(x_ref, g_ref, win_ref, cw_ref, mkv_ref, gq_ref, wout_ref,
                    y_ref, cs_ref, ext_ref, cat_ref):
    t = pl.program_id(1)

    @pl.when(t == 0)
    def _():
        ext_ref[0:8, :] = jnp.zeros((8, D_CONV), F32)

    x = x_ref[...]
    xn = _rms(x, g_ref[...]).astype(BF16)

    def proj(lo, hi):
        return _dot(xn, win_ref[:, lo:hi])

    u = proj(D_CONV, 2 * D_CONV) * proj(0, D_CONV)
    ext_ref[8:8 + TM, :] = u
    y = (cw_ref[0:1, :] * ext_ref[6:6 + TM, :] + cw_ref[1:2, :] * ext_ref[7:7 + TM, :]
         + cw_ref[2:3, :] * u)
    ext_ref[0:8, :] = ext_ref[TM:TM + 8, :]
    cs_ref[...] = ext_ref[6:8, :]
    bg = proj(2 * D_CONV, 3 * D_CONV)
    g = proj(3 * D_CONV, 4 * D_CONV)
    cat_ref[:, 0:D_CONV] = (_silu(g) * bg * y).astype(BF16)
    mq = proj(4 * D_CONV, 4 * D_CONV + D_ATT)
    mg = proj(4 * D_CONV + D_ATT, 4 * D_CONV + 2 * D_ATT)

    def store(h, val):
        cat_ref[:, _hs(h, D_CONV)] = val.astype(BF16)

    _mem_attn(mq, mg, mkv_ref, gq_ref[...], store)
    y_ref[...] = x + _dot(cat_ref[...], wout_ref[...])


def _a_layer_prompt(x, norm_a, w_in, conv_w, mkv_b, a, layer, mem_q_norm, w_out):
    b, s, _ = x.shape
    return pl.pallas_call(
        _a_layer_kernel,
        grid=(b, s // TM),
        in_specs=[
            pl.BlockSpec((None, TM, D_MODEL), lambda i, t: (i, t, 0)),
            _layer_spec((1, D_MODEL), a),
            _layer_spec(w_in.shape[1:], a),
            _layer_spec((CONV_W, D_CONV), a),
            pl.BlockSpec((None, None, MEM_LEN, 2 * D_ATT), lambda i, t: (layer, i, 0, 0)),
            _layer_spec((1, HEAD_DIM), layer),
            _layer_spec(w_out.shape[1:], a),
        ],
        out_specs=[
            pl.BlockSpec((None, TM, D_MODEL), lambda i, t: (i, t, 0)),
            pl.BlockSpec((None, CONV_W - 1, D_CONV), lambda i, t: (i, 0, 0)),
        ],
        out_shape=[jax.ShapeDtypeStruct((b, s, D_MODEL), F32),
                   jax.ShapeDtypeStruct((b, CONV_W - 1, D_CONV), F32)],
        scratch_shapes=[pltpu.VMEM((TM + 8, D_CONV), F32),
                        pltpu.VMEM((TM, D_CONV + D_ATT), BF16)],
        compiler_params=_params(("arbitrary", "arbitrary")),
        name=f"a_layer_prompt_{layer}",
    )(x, norm_a, w_in, conv_w, mkv_b, mem_q_norm, w_out)


def _to_residue_major(dst_ref, slab_ref, n_chunks, d):
    rows = slab_ref.shape[1] // d
    for r in range(d):
        for c in range(n_chunks):
            dst_ref[r, :, _hs(c)] = slab_ref[c, pl.ds(r, rows, stride=d), :].astype(dst_ref.dtype)


def _kv_kernel(x_ref, g_ref, w_ref, kn_ref, kvb0_ref, kvb1_ref, kvb2_ref,
               st0_ref, st1_ref, st2_ref, slab_ref):
    t = pl.program_id(1)
    last = pl.num_programs(1) - 1
    xn = _rms(x_ref[...], g_ref[...]).astype(BF16)
    kvb_refs = (kvb0_ref, kvb1_ref, kvb2_ref)
    for g in range(N_GROUPS):
        kv = _dot(xn, w_ref[:, g * 2 * D_ATT:(g + 1) * 2 * D_ATT])
        chunks = [_rms(kv[:, _hs(h)], kn_ref[g:g + 1, :]) for h in range(N_HEADS)]
        chunks += [kv[:, _hs(h, D_ATT)] for h in range(N_HEADS)]
        d = DILATIONS[g]
        for c, chunk in enumerate(chunks):
            if d == 1:
                kvb_refs[g][0, :, _hs(c)] = chunk.astype(BF16)
            else:
                slab_ref[c] = chunk
            if g == 2:
                st2_ref[pl.ds(c, TM, stride=KV_CHUNKS), :] = chunk
        if d > 1:
            _to_residue_major(kvb_refs[g], slab_ref, KV_CHUNKS, d)
        if g == 1:
            @pl.when(t == last)
            def _():
                for c, chunk in enumerate(chunks):
                    st1_ref[pl.ds(c, TM, stride=KV_CHUNKS), :] = chunk
        if g == 0:
            @pl.when(t == last)
            def _():
                for c, chunk in enumerate(chunks):
                    st0_ref[pl.ds(c, WINDOWS[0], stride=KV_CHUNKS), :] = chunk[TM - WINDOWS[0]:, :]


def _kv_prompt(x, kv_norm, w_kv, k_norm_dil):
    b, s, _ = x.shape
    assert WINDOWS[1] == TM and WINDOWS[2] == s and WINDOWS[0] <= TM
    first = lambda i, t: (i, 0, 0)
    res_spec = lambda d: pl.BlockSpec((None, d, TM // d, 2 * D_ATT), lambda i, t: (i, 0, t, 0))
    res_sds = lambda d: jax.ShapeDtypeStruct((b, d, s // d, 2 * D_ATT), BF16)
    st_sds = lambda rows: jax.ShapeDtypeStruct((b, rows * KV_CHUNKS, HEAD_DIM), F32)
    return pl.pallas_call(
        _kv_kernel,
        grid=(b, s // TM),
        in_specs=[
            pl.BlockSpec((None, TM, D_MODEL), lambda i, t: (i, t, 0)),
            _const_spec((1, D_MODEL)),
            _const_spec((D_MODEL, N_GROUPS * 2 * D_ATT)),
            _const_spec((N_GROUPS, HEAD_DIM)),
        ],
        out_specs=[
            res_spec(DILATIONS[0]), res_spec(DILATIONS[1]), res_spec(DILATIONS[2]),
            pl.BlockSpec((None, WINDOWS[0] * KV_CHUNKS, HEAD_DIM), first),
            pl.BlockSpec((None, WINDOWS[1] * KV_CHUNKS, HEAD_DIM), first),
            pl.BlockSpec((None, TM * KV_CHUNKS, HEAD_DIM), lambda i, t: (i, t, 0)),
        ],
        out_shape=[res_sds(DILATIONS[0]), res_sds(DILATIONS[1]), res_sds(DILATIONS[2]),
                   st_sds(WINDOWS[0]), st_sds(WINDOWS[1]), st_sds(s)],
        scratch_shapes=[pltpu.VMEM((KV_CHUNKS, TM, HEAD_DIM), F32)],
        compiler_params=_params(("arbitrary", "arbitrary")),
        name="kv_prompt",
    )(x, kv_norm.reshape(1, D_MODEL), w_kv, k_norm_dil)


def _band_units(qs, ks, vs, masks):
    nq = qs[0].shape[0]
    s = jnp.concatenate([jnp.where(mk, _dot_nt(q, k), NEG) for q, k, mk in zip(qs, ks, masks)],
                        axis=0)
    m = jnp.max(s, axis=-1, keepdims=True)
    p = jnp.exp(s - m)
    l = jnp.sum(p, axis=-1, keepdims=True)
    pb = p.astype(BF16)
    inv = 1.0 / l
    lse = m + jnp.log(l)
    res = []
    for i, v in enumerate(vs):
        rows = slice(i * nq, (i + 1) * nq)
        res.append((_dot(pb[rows], v) * inv[rows], lse[rows]))
    return res


def _b_layer_kernel(x_ref, g_ref, win_ref, qn_ref, mkv_ref, gq_ref, wout_ref,
                    kv0_ref, kv0p_ref, kv1_ref, kv1p_ref, kv2_ref,
                    y_ref, qslab_ref, oslab_ref, lslab_ref, cat_ref):
    t = pl.program_id(1)
    x = x_ref[...]
    xn = _rms(x, g_ref[...]).astype(BF16)

    def proj(lo, hi):
        return _dot(xn, win_ref[:, lo:hi])

    qi = lax.broadcasted_iota(jnp.int32, (BAND, 2 * BAND), 0)
    kj = lax.broadcasted_iota(jnp.int32, (BAND, 2 * BAND), 1)
    mask_rest = (kj >= qi) & (kj <= qi + BAND)
    mask_edge = (kj >= jnp.maximum(qi, jnp.where(t == 0, BAND, 0))) & (kj <= qi + BAND)

    def finish(g, h, rows, out, lse):
        oslab_ref[g, h, rows, :] = out
        lslab_ref[g, h, rows, :] = jnp.broadcast_to(lse, out.shape)

    def normed_heads(g):
        qd = proj(g * D_ATT, (g + 1) * D_ATT)
        return [_rms(qd[:, _hs(h)], qn_ref[g:g + 1, :]) * SCALE for h in range(N_HEADS)]

    n_sub = TM // BAND
    qh = [q.astype(BF16) for q in normed_heads(0)]
    units = [(i, h) for i in range(n_sub) for h in range(N_HEADS)]

    def keys0(i, cols):
        if i == 0:
            return jnp.concatenate([kv0p_ref[0, :, cols], kv0_ref[0, 0:BAND, cols]], axis=0)
        return kv0_ref[0, (i - 1) * BAND:(i + 1) * BAND, cols]

    res = _band_units([qh[h][i * BAND:(i + 1) * BAND] for i, h in units],
                      [keys0(i, _hs(h)) for i, h in units],
                      [keys0(i, _hs(h, D_ATT)) for i, h in units],
                      [mask_edge if i == 0 else mask_rest for i, h in units])
    for (i, h), (out, lse) in zip(units, res):
        finish(0, h, pl.ds(i * BAND, BAND), out, lse)

    d = DILATIONS[1]
    for h, q in enumerate(normed_heads(1)):
        qslab_ref[h] = q
    units = [(r, h) for r in range(d) for h in range(N_HEADS)]

    def keys1(r, cols):
        return jnp.concatenate([kv1p_ref[r, :, cols], kv1_ref[r, :, cols]], axis=0)

    res = _band_units([qslab_ref[h, pl.ds(r, BAND, stride=d), :].astype(BF16) for r, h in units],
                      [keys1(r, _hs(h)) for r, h in units],
                      [keys1(r, _hs(h, D_ATT)) for r, h in units],
                      [mask_edge] * len(units))
    for (r, h), (out, lse) in zip(units, res):
        finish(1, h, pl.ds(r, BAND, stride=d), out, lse)

    d = DILATIONS[2]
    nq = TM // d
    for h, q in enumerate(normed_heads(2)):
        qslab_ref[h] = q
    units = [(r, h) for r in range(d) for h in range(N_HEADS)]
    qi2 = lax.broadcasted_iota(jnp.int32, (nq, BAND), 0)
    kj2 = lax.broadcasted_iota(jnp.int32, (nq, BAND), 1)
    mask2 = kj2 <= qi2 + t * nq
    res = _band_units([qslab_ref[h, pl.ds(r, nq, stride=d), :].astype(BF16) for r, h in units],
                      [kv2_ref[r, :, _hs(h)] for r, h in units],
                      [kv2_ref[r, :, _hs(h, D_ATT)] for r, h in units],
                      [mask2] * len(units))
    for (r, h), (out, lse) in zip(units, res):
        finish(2, h, pl.ds(r, nq, stride=d), out, lse)

    base = N_GROUPS * D_ATT
    gd = proj(base, base + D_ATT)
    for h in range(N_HEADS):
        l0, l1, l2 = lslab_ref[0, h], lslab_ref[1, h], lslab_ref[2, h]
        m = jnp.maximum(jnp.maximum(l0, l1), l2)
        e0, e1, e2 = jnp.exp(l0 - m), jnp.exp(l1 - m), jnp.exp(l2 - m)
        den = e0 + e1 + e2
        dil = (oslab_ref[0, h] * (e0 / den) + oslab_ref[1, h] * (e1 / den)
               + oslab_ref[2, h] * (e2 / den))
        cat_ref[:, _hs(h)] = (dil * _silu(gd[:, _hs(h)])).astype(BF16)
    mq = proj(base + D_ATT, base + 2 * D_ATT)
    mg = proj(base + 2 * D_ATT, base + 3 * D_ATT)

    def store(h, val):
        cat_ref[:, _hs(h, D_ATT)] = val.astype(BF16)

    _mem_attn(mq, mg, mkv_ref, gq_ref[...], store)
    y_ref[...] = x + _dot(cat_ref[...], wout_ref[...])


def _b_layer_prompt(x, norm_b, w_in, q_norm, mkv_b, kvbs, bl, layer, mem_q_norm, w_out):
    b, s, _ = x.shape
    assert TM // BAND == DILATIONS[1] and s // DILATIONS[2] == BAND
    tile = lambda i, t: (i, t, 0)
    sub = TM // BAND
    kv_w = 2 * D_ATT
    return pl.pallas_call(
        _b_layer_kernel,
        grid=(b, s // TM),
        in_specs=[
            pl.BlockSpec((None, TM, D_MODEL), tile),
            _layer_spec((1, D_MODEL), bl),
            _layer_spec(w_in.shape[1:], bl),
            _layer_spec((N_GROUPS, HEAD_DIM), bl),
            pl.BlockSpec((None, None, MEM_LEN, kv_w), lambda i, t: (layer, i, 0, 0)),
            _layer_spec((1, HEAD_DIM), layer),
            _layer_spec(w_out.shape[1:], bl),
            pl.BlockSpec((None, 1, TM, kv_w), lambda i, t: (i, 0, t, 0)),
            pl.BlockSpec((None, 1, BAND, kv_w),
                         lambda i, t: (i, 0, jnp.maximum(t * sub - 1, 0), 0)),
            pl.BlockSpec((None, DILATIONS[1], BAND, kv_w), lambda i, t: (i, 0, t, 0)),
            pl.BlockSpec((None, DILATIONS[1], BAND, kv_w),
                         lambda i, t: (i, 0, jnp.maximum(t - 1, 0), 0)),
            pl.BlockSpec((None, DILATIONS[2], BAND, kv_w), lambda i, t: (i, 0, 0, 0)),
        ],
        out_specs=pl.BlockSpec((None, TM, D_MODEL), tile),
        out_shape=jax.ShapeDtypeStruct((b, s, D_MODEL), F32),
        scratch_shapes=[pltpu.VMEM((N_HEADS, TM, HEAD_DIM), F32),
                        pltpu.VMEM((N_GROUPS, N_HEADS, TM, HEAD_DIM), F32),
                        pltpu.VMEM((N_GROUPS, N_HEADS, TM, HEAD_DIM), F32),
                        pltpu.VMEM((TM, 2 * D_ATT), BF16)],
        compiler_params=_params(("parallel", "parallel")),
        name=f"b_layer_prompt_{layer}",
    )(x, norm_b, w_in, q_norm, mkv_b, mem_q_norm, w_out,
      kvbs[0], kvbs[0], kvbs[1], kvbs[1], kvbs[2])


def _swap_halves(x, axis):
    return pltpu.roll(x, N_HEADS, axis=axis)


def _head_rows(tile):
    return jnp.concatenate([tile[N_HEADS + h:N_HEADS + h + 1, :] for h in range(N_HEADS)],
                           axis=-1)


def _low_tile(row):
    return jnp.concatenate([jnp.zeros((N_HEADS, HEAD_DIM), F32)]
                           + [row[:, _hs(h)] for h in range(N_HEADS)], axis=0)


def _q_tile(row, g):
    return jnp.concatenate([_rms(row[:, _hs(h)], g) * SCALE for h in range(N_HEADS)]
                           + [jnp.zeros((N_HEADS, HEAD_DIM), F32)], axis=0)


def _tile_attend(qt, kv):
    s = jnp.sum(kv * qt[None], axis=-1, keepdims=True)
    m = jnp.max(s, axis=0)
    p = jnp.exp(s - m[None])
    l = jnp.sum(p, axis=0)
    acc = jnp.sum(_swap_halves(jnp.broadcast_to(p, kv.shape), 1) * kv, axis=0)
    out = acc / _swap_halves(jnp.broadcast_to(l, acc.shape), 0)
    return out, m + jnp.log(l)


def _sample_mem_attn(q_row, gq, mkv, mg_row):
    out, _ = _tile_attend(_q_tile(q_row, gq), mkv)
    return _head_rows(out * _low_tile(_silu(mg_row)))


def _gather_chunks(ref, n_rows, n_chunks, offset=0):
    total = ref.shape[0] // n_rows
    return jnp.concatenate([ref[pl.ds(offset + c, n_rows, stride=total), :]
                            for c in range(n_chunks)], axis=-1)


def _a_sample_kernel(x_ref, g_ref, win_ref, cw_ref, st_ref, mkv_ref, gq_ref, wout_ref,
                     y_ref, sto_ref, zm_ref, cat_ref):
    i = pl.program_id(0)
    nb = x_ref.shape[0]
    n_st = D_CONV // HEAD_DIM

    @pl.when(i == 0)
    def _():
        xn = _rms(x_ref[...], g_ref[...]).astype(BF16)
        z = _dot(xn, win_ref[...])
        u = z[:, D_CONV:2 * D_CONV] * z[:, 0:D_CONV]
        s0 = _gather_chunks(st_ref, nb, n_st)
        s1 = _gather_chunks(st_ref, nb, n_st, n_st)
        y = cw_ref[0:1, :] * s0 + cw_ref[1:2, :] * s1 + cw_ref[2:3, :] * u
        for c in range(n_st):
            sto_ref[pl.ds(c, nb, stride=2 * n_st), :] = s1[:, _hs(c)]
            sto_ref[pl.ds(n_st + c, nb, stride=2 * n_st), :] = u[:, _hs(c)]
        cat_ref[:, 0:D_CONV] = _silu(z[:, 3 * D_CONV:4 * D_CONV]) * z[:, 2 * D_CONV:3 * D_CONV] * y
        zm_ref[...] = z[:, 4 * D_CONV:]

    sub = lax.broadcasted_iota(jnp.int32, (SAMPLE_GB, D_ATT), 0)

    def body(j, tile):
        row = zm_ref[pl.ds(i * SAMPLE_GB + j, 1), :]
        out = _sample_mem_attn(row[:, :D_ATT], gq_ref[...], mkv_ref[j], row[:, D_ATT:])
        return jnp.where(sub == j, out, tile)

    rows = pl.ds(pl.multiple_of(i * SAMPLE_GB, SAMPLE_GB), SAMPLE_GB)
    cat_ref[rows, D_CONV:] = lax.fori_loop(0, SAMPLE_GB, body, jnp.zeros((SAMPLE_GB, D_ATT), F32))

    @pl.when(i == pl.num_programs(0) - 1)
    def _():
        y_ref[...] = x_ref[...] + _dot(cat_ref[...].astype(BF16), wout_ref[...])


def _a_layer_sample(x, norm_a, w_in, conv_w, state, mem_cache, a, layer, mem_q_norm, w_out):
    nb = x.shape[0]
    full = lambda shape: pl.BlockSpec(shape, lambda i: (0,) * len(shape))
    st_rows = nb * (CONV_W - 1) * D_CONV // HEAD_DIM
    return pl.pallas_call(
        _a_sample_kernel,
        grid=(nb // SAMPLE_GB,),
        in_specs=[
            full((nb, D_MODEL)),
            _layer_spec((1, D_MODEL), a),
            _layer_spec(w_in.shape[1:], a),
            _layer_spec((CONV_W, D_CONV), a),
            _layer_spec((st_rows, HEAD_DIM), a),
            pl.BlockSpec((None, SAMPLE_GB, MEM_LEN, KV_CHUNKS, HEAD_DIM),
                         lambda i: (layer, i, 0, 0, 0)),
            _layer_spec((1, HEAD_DIM), layer),
            _layer_spec(w_out.shape[1:], a),
        ],
        out_specs=[full((nb, D_MODEL)), full((st_rows, HEAD_DIM))],
        out_shape=[jax.ShapeDtypeStruct((nb, D_MODEL), F32),
                   jax.ShapeDtypeStruct((st_rows, HEAD_DIM), F32)],
        scratch_shapes=[pltpu.VMEM((nb, 2 * D_ATT), F32),
                        pltpu.VMEM((nb, D_CONV + D_ATT), F32)],
        compiler_params=_params(("arbitrary",)),
        name=f"a_layer_sample_{layer}",
    )(x, norm_a, w_in, conv_w, state, mem_cache, mem_q_norm, w_out)


def _kv_sample_kernel(x_ref, g_ref, w_ref, kn_ref, o_ref):
    nb = x_ref.shape[0]
    xn = _rms(x_ref[...], g_ref[...]).astype(BF16)
    kv = _dot(xn, w_ref[...])
    n_chunks = N_GROUPS * KV_CHUNKS
    for g in range(N_GROUPS):
        for c in range(KV_CHUNKS):
            chunk = kv[:, _hs(c, g * 2 * D_ATT)]
            if c < N_HEADS:
                chunk = _rms(chunk, kn_ref[g:g + 1, :])
            o_ref[pl.ds(g * KV_CHUNKS + c, nb, stride=n_chunks), :] = chunk


def _kv_sample(x, kv_norm, w_kv, k_norm_dil):
    nb = x.shape[0]
    return pl.pallas_call(
        _kv_sample_kernel,
        out_shape=jax.ShapeDtypeStruct((nb * N_GROUPS * KV_CHUNKS, HEAD_DIM), F32),
        compiler_params=pltpu.CompilerParams(vmem_limit_bytes=VMEM_LIMIT),
        name="kv_sample",
    )(x, kv_norm.reshape(1, D_MODEL), w_kv, k_norm_dil)


def _b_sample_kernel(x_ref, g_ref, win_ref, qn_ref, kvn_ref, c0_ref, c1_ref, c2_ref,
                     mkv_ref, gq_ref, wout_ref, y_ref, z_ref, cat_ref):
    i = pl.program_id(0)

    @pl.when(i == 0)
    def _():
        xn = _rms(x_ref[...], g_ref[...]).astype(BF16)
        z_ref[...] = _dot(xn, win_ref[...])

    caches = (c0_ref, c1_ref, c2_ref)
    base = N_GROUPS * D_ATT
    sub = lax.broadcasted_iota(jnp.int32, (SAMPLE_GB, 2 * D_ATT), 0)

    def body(j, tile):
        row = z_ref[pl.ds(i * SAMPLE_GB + j, 1), :]
        outs, lses = [], []
        for g in range(N_GROUPS):
            qt = _q_tile(row[:, g * D_ATT:(g + 1) * D_ATT], qn_ref[g:g + 1, :])
            kv = jnp.concatenate([caches[g][j], kvn_ref[i * SAMPLE_GB + j, g][None]], axis=0)
            out, lse = _tile_attend(qt, kv)
            outs.append(out)
            lses.append(lse)
        m = jnp.maximum(jnp.maximum(lses[0], lses[1]), lses[2])
        es = [jnp.exp(l - m) for l in lses]
        den = es[0] + es[1] + es[2]
        ws = [_swap_halves(jnp.broadcast_to(e / den, (KV_CHUNKS, HEAD_DIM)), 0) for e in es]
        dil = outs[0] * ws[0] + outs[1] * ws[1] + outs[2] * ws[2]
        dil_row = _head_rows(dil * _low_tile(_silu(row[:, base:base + D_ATT])))
        mem_row = _sample_mem_attn(row[:, base + D_ATT:base + 2 * D_ATT], gq_ref[...],
                                   mkv_ref[j], row[:, base + 2 * D_ATT:])
        return jnp.where(sub == j, jnp.concatenate([dil_row, mem_row], axis=-1), tile)

    rows = pl.ds(pl.multiple_of(i * SAMPLE_GB, SAMPLE_GB), SAMPLE_GB)
    cat_ref[rows, :] = lax.fori_loop(0, SAMPLE_GB, body, jnp.zeros((SAMPLE_GB, 2 * D_ATT), F32))

    @pl.when(i == pl.num_programs(0) - 1)
    def _():
        y_ref[...] = x_ref[...] + _dot(cat_ref[...].astype(BF16), wout_ref[...])


def _b_layer_sample(x, norm_b, w_in, q_norm, kv_new, caches, mem_cache, bl, layer,
                    mem_q_norm, w_out):
    nb = x.shape[0]
    full = lambda shape: pl.BlockSpec(shape, lambda i: (0,) * len(shape))
    cache_spec = pl.BlockSpec((SAMPLE_GB, BAND, None, KV_CHUNKS, HEAD_DIM),
                              lambda i: (i, 0, 0, 0, 0))
    return pl.pallas_call(
        _b_sample_kernel,
        grid=(nb // SAMPLE_GB,),
        in_specs=[
            full((nb, D_MODEL)),
            _layer_spec((1, D_MODEL), bl),
            _layer_spec(w_in.shape[1:], bl),
            _layer_spec((N_GROUPS, HEAD_DIM), bl),
            full(kv_new.shape),
            cache_spec, cache_spec, cache_spec,
            pl.BlockSpec((None, SAMPLE_GB, MEM_LEN, KV_CHUNKS, HEAD_DIM),
                         lambda i: (layer, i, 0, 0, 0)),
            _layer_spec((1, HEAD_DIM), layer),
            _layer_spec(w_out.shape[1:], bl),
        ],
        out_specs=full((nb, D_MODEL)),
        out_shape=jax.ShapeDtypeStruct((nb, D_MODEL), F32),
        scratch_shapes=[pltpu.VMEM((nb, w_in.shape[2]), F32),
                        pltpu.VMEM((nb, 2 * D_ATT), F32)],
        compiler_params=_params(("arbitrary",)),
        name=f"b_layer_sample_{layer}",
    )(x, norm_b, w_in, q_norm, kv_new, *caches, mem_cache, mem_q_norm, w_out)


def kernel(x_prompt, x_sample, cache_mem_kv, state_conv, cache_dil0_kv, cache_dil1_kv, cache_dil2_kv, mem_prompt, norm_a, w_in_a, conv_w_a, w_out_a, norm_b, w_in_b, q_norm_b, w_out_b, kv_norm, w_kv, k_norm_dil, mem_norm, w_mem_kv, mem_k_norm, mem_q_norm):
    n_a = w_in_a.shape[0]
    n_b = w_in_b.shape[0]
    depth = n_a + n_b
    b, s, _ = x_prompt.shape
    nb = x_sample.shape[0]
    assert x_sample.shape[1] == 1 and s % TM == 0 and nb % SAMPLE_GB == 0
    kv_dims = (2, N_HEADS, HEAD_DIM)

    w_in_a16, w_out_a16 = w_in_a.astype(BF16), w_out_a.astype(BF16)
    w_in_b16, w_out_b16 = w_in_b.astype(BF16), w_out_b.astype(BF16)
    w_kv16, w_mem16 = w_kv.astype(BF16), w_mem_kv.astype(BF16)
    norm_a3 = norm_a.reshape(n_a, 1, D_MODEL)
    norm_b3 = norm_b.reshape(n_b, 1, D_MODEL)
    mem_q_norm3 = mem_q_norm.reshape(depth, 1, HEAD_DIM)

    mem_kv_f, mem_kv_b = _memkv_prompt(mem_prompt, mem_norm, w_mem16, mem_k_norm)
    mem_kv_b = mem_kv_b.reshape(depth, b, MEM_LEN, 2 * D_ATT)
    mem_cache = cache_mem_kv.reshape(depth, nb, MEM_LEN, KV_CHUNKS, HEAD_DIM)
    caches = []
    for g, cache in enumerate((cache_dil0_kv, cache_dil1_kv, cache_dil2_kv)):
        assert cache.shape[1] == WINDOWS[g]
        caches.append(cache.reshape(nb, BAND, DILATIONS[g], KV_CHUNKS, HEAD_DIM))
    state = state_conv.reshape(n_a, -1, HEAD_DIM)

    xp = x_prompt
    xs = x_sample.reshape(nb, D_MODEL)
    conv_p, conv_s = [], []
    for a in range(n_a):
        xp, cp = _a_layer_prompt(xp, norm_a3, w_in_a16, conv_w_a, mem_kv_b, a, a,
                                 mem_q_norm3, w_out_a16)
        xs, cs = _a_layer_sample(xs, norm_a3, w_in_a16, conv_w_a, state, mem_cache, a, a,
                                 mem_q_norm3, w_out_a16)
        conv_p.append(cp)
        conv_s.append(cs.reshape(nb, CONV_W - 1, D_CONV))

    kvb0, kvb1, kvb2, st0, st1, st2 = _kv_prompt(xp, kv_norm, w_kv16, k_norm_dil)
    kvbs = (kvb0, kvb1, kvb2)
    kv_new = _kv_sample(xs, kv_norm, w_kv16, k_norm_dil).reshape(nb, N_GROUPS, KV_CHUNKS, HEAD_DIM)

    for bl in range(n_b):
        layer = n_a + bl
        xp = _b_layer_prompt(xp, norm_b3, w_in_b16, q_norm_b, mem_kv_b, kvbs, bl, layer,
                             mem_q_norm3, w_out_b16)
        xs = _b_layer_sample(xs, norm_b3, w_in_b16, q_norm_b, kv_new, caches, mem_cache, bl,
                             layer, mem_q_norm3, w_out_b16)

    dil_s = [kv_new[:, g].reshape(nb, 1, *kv_dims) for g in range(N_GROUPS)]
    return (xp, xs.reshape(nb, 1, D_MODEL),
            mem_kv_f.reshape(depth, b, MEM_LEN, *kv_dims),
            jnp.stack(conv_p), jnp.stack(conv_s),
            st0.reshape(b, WINDOWS[0], *kv_dims), st1.reshape(b, WINDOWS[1], *kv_dims),
            st2.reshape(b, s, *kv_dims), *dil_s)
```

```python
import jax
import jax.numpy as jnp
from jax import lax
from jax.experimental import pallas as pl
from jax.experimental.pallas import tpu as pltpu

F32 = jnp.float32
BF16 = jnp.bfloat16

D_MODEL = 1024
HEAD_DIM = 128
N_HEADS = 4
D_ATT = N_HEADS * HEAD_DIM
KV_CHUNKS = 2 * N_HEADS
D_CONV = D_MODEL
CONV_W = 3
N_GROUPS = 3
DILATIONS = (1, 4, 16)
WINDOWS = (128, 512, 2048)
BAND = 128
SUB_STRIDE = 4
MEM_LEN = 256
EPS = 1e-6
NEG = -1e30
SCALE = HEAD_DIM ** -0.5

TM = 512
SAMPLE_GB = 8
VMEM_LIMIT = 56 * 1024 * 1024


def _rms(x, g):
    return x * lax.rsqrt(jnp.mean(x * x, axis=-1, keepdims=True) + EPS) * g


def _silu(x):
    return x * (1.0 / (1.0 + jnp.exp(-x)))


def _dot(a, b):
    return jnp.dot(a, b, preferred_element_type=F32)


def _dot_nt(a, b):
    return lax.dot_general(a, b, (((1,), (1,)), ((), ())), preferred_element_type=F32)


def _hs(h, base=0):
    return slice(base + h * HEAD_DIM, base + (h + 1) * HEAD_DIM)


def _params(sem):
    return pltpu.CompilerParams(dimension_semantics=sem, vmem_limit_bytes=VMEM_LIMIT)


def _fixed_spec(block, index):
    return pl.BlockSpec(block, lambda *_: index, pipeline_mode=pl.Buffered(1))


def _const_spec(shape):
    return _fixed_spec(shape, (0,) * len(shape))


def _layer_spec(shape, layer):
    return _fixed_spec((None,) + tuple(shape), (layer,) + (0,) * len(shape))


def _memkv_kernel(mem_ref, g_ref, w_ref, kg_ref, of_ref, ob_ref):
    rt = mem_ref.shape[0]
    xn = _rms(mem_ref[...], g_ref[...]).astype(BF16)
    kv = _dot(xn, w_ref[...])
    for c in range(KV_CHUNKS):
        chunk = kv[:, _hs(c)]
        if c < N_HEADS:
            chunk = _rms(chunk, kg_ref[...])
        of_ref[pl.ds(c, rt, stride=KV_CHUNKS), :] = chunk
        ob_ref[:, _hs(c)] = chunk.astype(BF16)


def _memkv_prompt(mem, mem_norm, w_mem_kv, mem_k_norm):
    depth = w_mem_kv.shape[0]
    rows = mem.shape[0] * mem.shape[1]
    rt = 1024
    return pl.pallas_call(
        _memkv_kernel,
        grid=(depth, rows // rt),
        in_specs=[
            pl.BlockSpec((rt, D_MODEL), lambda l, i: (i, 0)),
            pl.BlockSpec((None, 1, D_MODEL), lambda l, i: (l, 0, 0)),
            pl.BlockSpec((None, D_MODEL, 2 * D_ATT), lambda l, i: (l, 0, 0)),
            pl.BlockSpec((None, 1, HEAD_DIM), lambda l, i: (l, 0, 0)),
        ],
        out_specs=[
            pl.BlockSpec((None, rt * KV_CHUNKS, HEAD_DIM), lambda l, i: (l, i, 0)),
            pl.BlockSpec((None, rt, 2 * D_ATT), lambda l, i: (l, i, 0)),
        ],
        out_shape=[jax.ShapeDtypeStruct((depth, rows * KV_CHUNKS, HEAD_DIM), F32),
                   jax.ShapeDtypeStruct((depth, rows, 2 * D_ATT), BF16)],
        compiler_params=_params(("parallel", "parallel")),
        name="memkv_prompt",
    )(mem.reshape(rows, D_MODEL), mem_norm.reshape(depth, 1, D_MODEL), w_mem_kv,
      mem_k_norm.reshape(depth, 1, HEAD_DIM))


def _pv_and_rowsum(p, v):
    ones = jnp.ones((v.shape[0], HEAD_DIM), BF16)
    both = _dot(p, jnp.concatenate([v, ones], axis=1))
    return both[:, :HEAD_DIM], both[:, HEAD_DIM:]


def _mem_attn(mq, mg, mkv_ref, gq, store):
    for h in range(N_HEADS):
        q = _rms(mq[:, _hs(h)], gq).astype(BF16)
        s = _dot_nt(q, mkv_ref[:, _hs(h)])
        m = jnp.max(s, axis=-1, keepdims=True)
        acc, l = _pv_and_rowsum(jnp.exp(s - m).astype(BF16), mkv_ref[:, _hs(h, D_ATT)])
        store(h, acc / l * _silu(mg[:, _hs(h)]))


def _a_layer_kernel(x_ref, g_ref, win_ref, cw_ref, mkv_ref, gq_ref, wout_ref,
                    y_ref, cs_ref, ext_ref, cat_ref):
    t = pl.program_id(1)

    @pl.when(t == 0)
    def _():
        ext_ref[0:8, :] = jnp.zeros((8, D_CONV), F32)

    x = x_ref[...]
    xn = _rms(x, g_ref[...]).astype(BF16)

    def proj(lo, hi):
        return _dot(xn, win_ref[:, lo:hi])

    u = proj(D_CONV, 2 * D_CONV) * proj(0, D_CONV)
    ext_ref[8:8 + TM, :] = u
    y = (cw_ref[0:1, :] * ext_ref[6:6 + TM, :] + cw_ref[1:2, :] * ext_ref[7:7 + TM, :]
         + cw_ref[2:3, :] * u)
    ext_ref[0:8, :] = ext_ref[TM:TM + 8, :]
    cs_ref[...] = ext_ref[6:8, :]
    bg = proj(2 * D_CONV, 3 * D_CONV)
    g = proj(3 * D_CONV, 4 * D_CONV)
    cat_ref[:, 0:D_CONV] = (_silu(g) * bg * y).astype(BF16)
    mq = proj(4 * D_CONV, 4 * D_CONV + D_ATT)
    mg = proj(4 * D_CONV + D_ATT, 4 * D_CONV + 2 * D_ATT)

    def store(h, val):
        cat_ref[:, _hs(h, D_CONV)] = val.astype(BF16)

    _mem_attn(mq, mg, mkv_ref, gq_ref[...] * SCALE, store)
    y_ref[...] = x + _dot(cat_ref[...], wout_ref[...])


def _a_layer_prompt(x, norm_a, w_in, conv_w, mkv_b, a, layer, mem_q_norm, w_out):
    b, s, _ = x.shape
    return pl.pallas_call(
        _a_layer_kernel,
        grid=(b, s // TM),
        in_specs=[
            pl.BlockSpec((None, TM, D_MODEL), lambda i, t: (i, t, 0)),
            _layer_spec((1, D_MODEL), a),
            _layer_spec(w_in.shape[1:], a),
            _layer_spec((CONV_W, D_CONV), a),
            pl.BlockSpec((None, None, MEM_LEN, 2 * D_ATT), lambda i, t: (layer, i, 0, 0)),
            _layer_spec((1, HEAD_DIM), layer),
            _layer_spec(w_out.shape[1:], a),
        ],
        out_specs=[
            pl.BlockSpec((None, TM, D_MODEL), lambda i, t: (i, t, 0)),
            pl.BlockSpec((None, CONV_W - 1, D_CONV), lambda i, t: (i, 0, 0)),
        ],
        out_shape=[jax.ShapeDtypeStruct((b, s, D_MODEL), F32),
                   jax.ShapeDtypeStruct((b, CONV_W - 1, D_CONV), F32)],
        scratch_shapes=[pltpu.VMEM((TM + 8, D_CONV), F32),
                        pltpu.VMEM((TM, D_CONV + D_ATT), BF16)],
        compiler_params=_params(("arbitrary", "arbitrary")),
        name=f"a_layer_prompt_{layer}",
    )(x, norm_a, w_in, conv_w, mkv_b, mem_q_norm, w_out)


def _residue_rows(slab_ref, split_ref, c, d):
    rows = slab_ref.shape[1]
    if d == SUB_STRIDE:
        return lambda r: slab_ref[c, pl.ds(r, rows // d, stride=d), :]
    assert d == SUB_STRIDE * SUB_STRIDE
    for r1 in range(SUB_STRIDE):
        split_ref[c, r1] = slab_ref[c, pl.ds(r1, rows // SUB_STRIDE, stride=SUB_STRIDE), :]
    return lambda r: split_ref[c, r % SUB_STRIDE,
                               pl.ds(r // SUB_STRIDE, rows // d, stride=SUB_STRIDE), :]


def _to_residue_major(dst_ref, slab_ref, split_ref, n_chunks, d):
    for c in range(n_chunks):
        get = _residue_rows(slab_ref, split_ref, c, d)
        for r in range(d):
            dst_ref[r, :, _hs(c)] = get(r).astype(dst_ref.dtype)


def _kv_kernel(x_ref, g_ref, w_ref, kn_ref, kvb0_ref, kvb1_ref, kvb2_ref,
               st0_ref, st1_ref, st2_ref, slab1_ref, slab2_ref, split_ref):
    t = pl.program_id(1)
    last = pl.num_programs(1) - 1
    xn = _rms(x_ref[...], g_ref[...]).astype(BF16)
    kvb_refs = (kvb0_ref, kvb1_ref, kvb2_ref)
    slabs = (None, slab1_ref, slab2_ref)
    for g in reversed(range(N_GROUPS)):
        kv = _dot(xn, w_ref[:, g * 2 * D_ATT:(g + 1) * 2 * D_ATT])
        chunks = [_rms(kv[:, _hs(h)], kn_ref[g:g + 1, :]) for h in range(N_HEADS)]
        chunks += [kv[:, _hs(h, D_ATT)] for h in range(N_HEADS)]
        d = DILATIONS[g]
        for c, chunk in enumerate(chunks):
            if d == 1:
                kvb_refs[g][0, :, _hs(c)] = chunk.astype(BF16)
            else:
                slabs[g][c] = chunk
            if g == 2:
                st2_ref[pl.ds(c, TM, stride=KV_CHUNKS), :] = chunk
        if d > 1:
            _to_residue_major(kvb_refs[g], slabs[g], split_ref, KV_CHUNKS, d)
        if g == 1:
            @pl.when(t == last)
            def _():
                for c, chunk in enumerate(chunks):
                    st1_ref[pl.ds(c, TM, stride=KV_CHUNKS), :] = chunk
        if g == 0:
            @pl.when(t == last)
            def _():
                for c, chunk in enumerate(chunks):
                    st0_ref[pl.ds(c, WINDOWS[0], stride=KV_CHUNKS), :] = chunk[TM - WINDOWS[0]:, :]


def _kv_prompt(x, kv_norm, w_kv, k_norm_dil):
    b, s, _ = x.shape
    assert WINDOWS[1] == TM and WINDOWS[2] == s and WINDOWS[0] <= TM
    first = lambda i, t: (i, 0, 0)
    res_spec = lambda d: pl.BlockSpec((None, d, TM // d, 2 * D_ATT), lambda i, t: (i, 0, t, 0))
    res_sds = lambda d: jax.ShapeDtypeStruct((b, d, s // d, 2 * D_ATT), BF16)
    st_sds = lambda rows: jax.ShapeDtypeStruct((b, rows * KV_CHUNKS, HEAD_DIM), F32)
    return pl.pallas_call(
        _kv_kernel,
        grid=(b, s // TM),
        in_specs=[
            pl.BlockSpec((None, TM, D_MODEL), lambda i, t: (i, t, 0)),
            _const_spec((1, D_MODEL)),
            _const_spec((D_MODEL, N_GROUPS * 2 * D_ATT)),
            _const_spec((N_GROUPS, HEAD_DIM)),
        ],
        out_specs=[
            res_spec(DILATIONS[0]), res_spec(DILATIONS[1]), res_spec(DILATIONS[2]),
            pl.BlockSpec((None, WINDOWS[0] * KV_CHUNKS, HEAD_DIM), first),
            pl.BlockSpec((None, WINDOWS[1] * KV_CHUNKS, HEAD_DIM), first),
            pl.BlockSpec((None, TM * KV_CHUNKS, HEAD_DIM), lambda i, t: (i, t, 0)),
        ],
        out_shape=[res_sds(DILATIONS[0]), res_sds(DILATIONS[1]), res_sds(DILATIONS[2]),
                   st_sds(WINDOWS[0]), st_sds(WINDOWS[1]), st_sds(s)],
        scratch_shapes=[pltpu.VMEM((KV_CHUNKS, TM, HEAD_DIM), F32),
                        pltpu.VMEM((KV_CHUNKS, TM, HEAD_DIM), F32),
                        pltpu.VMEM((KV_CHUNKS, SUB_STRIDE, TM // SUB_STRIDE, HEAD_DIM), F32)],
        compiler_params=_params(("arbitrary", "arbitrary")),
        name="kv_prompt",
    )(x, kv_norm.reshape(1, D_MODEL), w_kv, k_norm_dil)


def _band_scores(qs, ks, masks):
    return jnp.concatenate([jnp.where(mk, _dot_nt(q, k), NEG) for q, k, mk in zip(qs, ks, masks)],
                           axis=0)


def _band_finish(s, vs, nq):
    m = jnp.max(s, axis=-1, keepdims=True)
    pb = jnp.exp(s - m).astype(BF16)
    res = []
    for i, v in enumerate(vs):
        rows = slice(i * nq, (i + 1) * nq)
        acc, l = _pv_and_rowsum(pb[rows], v)
        res.append((acc * (1.0 / l), m[rows] + jnp.log(l)))
    return res


def _b_layer_kernel(x_ref, g_ref, win_ref, qn_ref, mkv_ref, gq_ref, wout_ref,
                    kv0_ref, kv0p_ref, kv1_ref, kv1p_ref, kv2_ref,
                    y_ref, qslab1_ref, qslab2_ref, qsplit_ref, oslab_ref, lslab_ref, cat_ref):
    t = pl.program_id(1)
    x = x_ref[...]
    xn = _rms(x, g_ref[...]).astype(BF16)

    def proj(lo, hi):
        return _dot(xn, win_ref[:, lo:hi])

    qi = lax.broadcasted_iota(jnp.int32, (BAND, 2 * BAND), 0)
    kj = lax.broadcasted_iota(jnp.int32, (BAND, 2 * BAND), 1)
    mask_rest = (kj >= qi) & (kj <= qi + BAND)
    mask_edge = (kj >= jnp.maximum(qi, jnp.where(t == 0, BAND, 0))) & (kj <= qi + BAND)

    def finish(g, h, rows, out, lse):
        oslab_ref[g, h, rows, :] = out
        lslab_ref[g, h, rows, :] = lse

    def normed_heads(g):
        qd = proj(g * D_ATT, (g + 1) * D_ATT)
        gain = qn_ref[g:g + 1, :] * SCALE
        return [_rms(qd[:, _hs(h)], gain) for h in range(N_HEADS)]

    n_sub = TM // BAND
    qh = [q.astype(BF16) for q in normed_heads(0)]
    units0 = [(i, h) for i in range(n_sub) for h in range(N_HEADS)]

    def keys0(i, cols):
        if i == 0:
            return jnp.concatenate([kv0p_ref[0, :, cols], kv0_ref[0, 0:BAND, cols]], axis=0)
        return kv0_ref[0, (i - 1) * BAND:(i + 1) * BAND, cols]

    d1 = DILATIONS[1]
    for h, q in enumerate(normed_heads(1)):
        qslab1_ref[h] = q
    get_q1 = [_residue_rows(qslab1_ref, None, h, d1) for h in range(N_HEADS)]
    units1 = [(r, h) for r in range(d1) for h in range(N_HEADS)]

    def keys1(r, cols):
        return jnp.concatenate([kv1p_ref[r, :, cols], kv1_ref[r, :, cols]], axis=0)

    d2 = DILATIONS[2]
    nq = TM // d2
    for h, q in enumerate(normed_heads(2)):
        qslab2_ref[h] = q
    get_q2 = [_residue_rows(qslab2_ref, qsplit_ref, h, d2) for h in range(N_HEADS)]
    units2 = [(r, h) for r in range(d2) for h in range(N_HEADS)]
    qi2 = lax.broadcasted_iota(jnp.int32, (nq, BAND), 0)
    kj2 = lax.broadcasted_iota(jnp.int32, (nq, BAND), 1)
    mask2 = kj2 <= qi2 + t * nq

    s0 = _band_scores([qh[h][i * BAND:(i + 1) * BAND] for i, h in units0],
                      [keys0(i, _hs(h)) for i, h in units0],
                      [mask_edge if i == 0 else mask_rest for i, h in units0])
    s1 = _band_scores([get_q1[h](r).astype(BF16) for r, h in units1],
                      [keys1(r, _hs(h)) for r, h in units1], [mask_edge] * len(units1))
    s2 = _band_scores([get_q2[h](r).astype(BF16) for r, h in units2],
                      [kv2_ref[r, :, _hs(h)] for r, h in units2], [mask2] * len(units2))
    res = _band_finish(s0, [keys0(i, _hs(h, D_ATT)) for i, h in units0], BAND)
    for (i, h), (out, lse) in zip(units0, res):
        finish(0, h, pl.ds(i * BAND, BAND), out, lse)
    res = _band_finish(s1, [keys1(r, _hs(h, D_ATT)) for r, h in units1], BAND)
    for (r, h), (out, lse) in zip(units1, res):
        finish(1, h, pl.ds(r, BAND, stride=d1), out, lse)
    res = _band_finish(s2, [kv2_ref[r, :, _hs(h, D_ATT)] for r, h in units2], nq)
    for (r, h), (out, lse) in zip(units2, res):
        finish(2, h, pl.ds(r, nq, stride=d2), out, lse)

    base = N_GROUPS * D_ATT
    gd = proj(base, base + D_ATT)
    for h in range(N_HEADS):
        l0, l1, l2 = lslab_ref[0, h], lslab_ref[1, h], lslab_ref[2, h]
        m = jnp.maximum(jnp.maximum(l0, l1), l2)
        e0, e1, e2 = jnp.exp(l0 - m), jnp.exp(l1 - m), jnp.exp(l2 - m)
        dil = ((oslab_ref[0, h] * e0 + oslab_ref[1, h] * e1 + oslab_ref[2, h] * e2)
               * (1.0 / (e0 + e1 + e2)))
        cat_ref[:, _hs(h)] = (dil * _silu(gd[:, _hs(h)])).astype(BF16)
    mq = proj(base + D_ATT, base + 2 * D_ATT)
    mg = proj(base + 2 * D_ATT, base + 3 * D_ATT)

    def store(h, val):
        cat_ref[:, _hs(h, D_ATT)] = val.astype(BF16)

    _mem_attn(mq, mg, mkv_ref, gq_ref[...] * SCALE, store)
    y_ref[...] = x + _dot(cat_ref[...], wout_ref[...])


def _b_layer_prompt(x, norm_b, w_in, q_norm, mkv_b, kvbs, bl, layer, mem_q_norm, w_out):
    b, s, _ = x.shape
    assert TM // BAND == DILATIONS[1] and s // DILATIONS[2] == BAND
    tile = lambda i, t: (i, t, 0)
    sub = TM // BAND
    kv_w = 2 * D_ATT
    return pl.pallas_call(
        _b_layer_kernel,
        grid=(b, s // TM),
        in_specs=[
            pl.BlockSpec((None, TM, D_MODEL), tile),
            _layer_spec((1, D_MODEL), bl),
            _layer_spec(w_in.shape[1:], bl),
            _layer_spec((N_GROUPS, HEAD_DIM), bl),
            pl.BlockSpec((None, None, MEM_LEN, kv_w), lambda i, t: (layer, i, 0, 0)),
            _layer_spec((1, HEAD_DIM), layer),
            _layer_spec(w_out.shape[1:], bl),
            pl.BlockSpec((None, 1, TM, kv_w), lambda i, t: (i, 0, t, 0)),
            pl.BlockSpec((None, 1, BAND, kv_w),
                         lambda i, t: (i, 0, jnp.maximum(t * sub - 1, 0), 0)),
            pl.BlockSpec((None, DILATIONS[1], BAND, kv_w), lambda i, t: (i, 0, t, 0)),
            pl.BlockSpec((None, DILATIONS[1], BAND, kv_w),
                         lambda i, t: (i, 0, jnp.maximum(t - 1, 0), 0)),
            pl.BlockSpec((None, DILATIONS[2], BAND, kv_w), lambda i, t: (i, 0, 0, 0)),
        ],
        out_specs=pl.BlockSpec((None, TM, D_MODEL), tile),
        out_shape=jax.ShapeDtypeStruct((b, s, D_MODEL), F32),
        scratch_shapes=[pltpu.VMEM((N_HEADS, TM, HEAD_DIM), F32),
                        pltpu.VMEM((N_HEADS, TM, HEAD_DIM), F32),
                        pltpu.VMEM((N_HEADS, SUB_STRIDE, TM // SUB_STRIDE, HEAD_DIM), F32),
                        pltpu.VMEM((N_GROUPS, N_HEADS, TM, HEAD_DIM), F32),
                        pltpu.VMEM((N_GROUPS, N_HEADS, TM, HEAD_DIM), F32),
                        pltpu.VMEM((TM, 2 * D_ATT), BF16)],
        compiler_params=_params(("parallel", "parallel")),
        name=f"b_layer_prompt_{layer}",
    )(x, norm_b, w_in, q_norm, mkv_b, mem_q_norm, w_out,
      kvbs[0], kvbs[0], kvbs[1], kvbs[1], kvbs[2])


def _swap_halves(x, axis):
    return pltpu.roll(x, N_HEADS, axis=axis)


def _head_rows(tile):
    return jnp.concatenate([tile[N_HEADS + h:N_HEADS + h + 1, :] for h in range(N_HEADS)],
                           axis=-1)


def _low_tile(row):
    return jnp.concatenate([jnp.zeros((N_HEADS, HEAD_DIM), F32)]
                           + [row[:, _hs(h)] for h in range(N_HEADS)], axis=0)


def _q_tile(row, g):
    return jnp.concatenate([_rms(row[:, _hs(h)], g) * SCALE for h in range(N_HEADS)]
                           + [jnp.zeros((N_HEADS, HEAD_DIM), F32)], axis=0)


def _tile_attend(qt, kv):
    s = jnp.sum(kv * qt[None], axis=-1, keepdims=True)
    m = jnp.max(s, axis=0)
    p = jnp.exp(s - m[None])
    l = jnp.sum(p, axis=0)
    acc = jnp.sum(_swap_halves(jnp.broadcast_to(p, kv.shape), 1) * kv, axis=0)
    out = acc / _swap_halves(jnp.broadcast_to(l, acc.shape), 0)
    return out, m + jnp.log(l)


def _sample_mem_attn(q_row, gq, mkv, mg_row):
    out, _ = _tile_attend(_q_tile(q_row, gq), mkv)
    return _head_rows(out * _low_tile(_silu(mg_row)))


def _gather_chunks(ref, n_rows, n_chunks, offset=0):
    total = ref.shape[0] // n_rows
    return jnp.concatenate([ref[pl.ds(offset + c, n_rows, stride=total), :]
                            for c in range(n_chunks)], axis=-1)


def _a_sample_kernel(x_ref, g_ref, win_ref, cw_ref, st_ref, mkv_ref, gq_ref, wout_ref,
                     y_ref, sto_ref, zm_ref, cat_ref):
    i = pl.program_id(0)
    nb = x_ref.shape[0]
    n_st = D_CONV // HEAD_DIM

    @pl.when(i == 0)
    def _():
        xn = _rms(x_ref[...], g_ref[...]).astype(BF16)
        z = _dot(xn, win_ref[...])
        u = z[:, D_CONV:2 * D_CONV] * z[:, 0:D_CONV]
        s0 = _gather_chunks(st_ref, nb, n_st)
        s1 = _gather_chunks(st_ref, nb, n_st, n_st)
        y = cw_ref[0:1, :] * s0 + cw_ref[1:2, :] * s1 + cw_ref[2:3, :] * u
        for c in range(n_st):
            sto_ref[pl.ds(c, nb, stride=2 * n_st), :] = s1[:, _hs(c)]
            sto_ref[pl.ds(n_st + c, nb, stride=2 * n_st), :] = u[:, _hs(c)]
        cat_ref[:, 0:D_CONV] = _silu(z[:, 3 * D_CONV:4 * D_CONV]) * z[:, 2 * D_CONV:3 * D_CONV] * y
        zm_ref[...] = z[:, 4 * D_CONV:]

    sub = lax.broadcasted_iota(jnp.int32, (SAMPLE_GB, D_ATT), 0)

    def body(j, tile):
        row = zm_ref[pl.ds(i * SAMPLE_GB + j, 1), :]
        out = _sample_mem_attn(row[:, :D_ATT], gq_ref[...], mkv_ref[j], row[:, D_ATT:])
        return jnp.where(sub == j, out, tile)

    rows = pl.ds(pl.multiple_of(i * SAMPLE_GB, SAMPLE_GB), SAMPLE_GB)
    cat_ref[rows, D_CONV:] = lax.fori_loop(0, SAMPLE_GB, body, jnp.zeros((SAMPLE_GB, D_ATT), F32))

    @pl.when(i == pl.num_programs(0) - 1)
    def _():
        y_ref[...] = x_ref[...] + _dot(cat_ref[...].astype(BF16), wout_ref[...])


def _a_layer_sample(x, norm_a, w_in, conv_w, state, mem_cache, a, layer, mem_q_norm, w_out):
    nb = x.shape[0]
    full = lambda shape: pl.BlockSpec(shape, lambda i: (0,) * len(shape))
    st_rows = nb * (CONV_W - 1) * D_CONV // HEAD_DIM
    return pl.pallas_call(
        _a_sample_kernel,
        grid=(nb // SAMPLE_GB,),
        in_specs=[
            full((nb, D_MODEL)),
            _layer_spec((1, D_MODEL), a),
            _layer_spec(w_in.shape[1:], a),
            _layer_spec((CONV_W, D_CONV), a),
            _layer_spec((st_rows, HEAD_DIM), a),
            pl.BlockSpec((None, SAMPLE_GB, MEM_LEN, KV_CHUNKS, HEAD_DIM),
                         lambda i: (layer, i, 0, 0, 0)),
            _layer_spec((1, HEAD_DIM), layer),
            _layer_spec(w_out.shape[1:], a),
        ],
        out_specs=[full((nb, D_MODEL)), full((st_rows, HEAD_DIM))],
        out_shape=[jax.ShapeDtypeStruct((nb, D_MODEL), F32),
                   jax.ShapeDtypeStruct((st_rows, HEAD_DIM), F32)],
        scratch_shapes=[pltpu.VMEM((nb, 2 * D_ATT), F32),
                        pltpu.VMEM((nb, D_CONV + D_ATT), F32)],
        compiler_params=_params(("arbitrary",)),
        name=f"a_layer_sample_{layer}",
    )(x, norm_a, w_in, conv_w, state, mem_cache, mem_q_norm, w_out)


def _kv_sample_kernel(x_ref, g_ref, w_ref, kn_ref, o_ref):
    nb = x_ref.shape[0]
    xn = _rms(x_ref[...], g_ref[...]).astype(BF16)
    kv = _dot(xn, w_ref[...])
    n_chunks = N_GROUPS * KV_CHUNKS
    for g in range(N_GROUPS):
        for c in range(KV_CHUNKS):
            chunk = kv[:, _hs(c, g * 2 * D_ATT)]
            if c < N_HEADS:
                chunk = _rms(chunk, kn_ref[g:g + 1, :])
            o_ref[pl.ds(g * KV_CHUNKS + c, nb, stride=n_chunks), :] = chunk


def _kv_sample(x, kv_norm, w_kv, k_norm_dil):
    nb = x.shape[0]
    return pl.pallas_call(
        _kv_sample_kernel,
        out_shape=jax.ShapeDtypeStruct((nb * N_GROUPS * KV_CHUNKS, HEAD_DIM), F32),
        compiler_params=pltpu.CompilerParams(vmem_limit_bytes=VMEM_LIMIT),
        name="kv_sample",
    )(x, kv_norm.reshape(1, D_MODEL), w_kv, k_norm_dil)


def _b_sample_kernel(x_ref, g_ref, win_ref, qn_ref, kvn_ref, c0_ref, c1_ref, c2_ref,
                     mkv_ref, gq_ref, wout_ref, y_ref, z_ref, cat_ref):
    i = pl.program_id(0)

    @pl.when(i == 0)
    def _():
        xn = _rms(x_ref[...], g_ref[...]).astype(BF16)
        z_ref[...] = _dot(xn, win_ref[...])

    caches = (c0_ref, c1_ref, c2_ref)
    base = N_GROUPS * D_ATT
    sub = lax.broadcasted_iota(jnp.int32, (SAMPLE_GB, 2 * D_ATT), 0)

    def body(j, tile):
        row = z_ref[pl.ds(i * SAMPLE_GB + j, 1), :]
        outs, lses = [], []
        for g in range(N_GROUPS):
            qt = _q_tile(row[:, g * D_ATT:(g + 1) * D_ATT], qn_ref[g:g + 1, :])
            kv = jnp.concatenate([caches[g][j], kvn_ref[i * SAMPLE_GB + j, g][None]], axis=0)
            out, lse = _tile_attend(qt, kv)
            outs.append(out)
            lses.append(lse)
        m = jnp.maximum(jnp.maximum(lses[0], lses[1]), lses[2])
        es = [jnp.exp(l - m) for l in lses]
        den = es[0] + es[1] + es[2]
        ws = [_swap_halves(jnp.broadcast_to(e / den, (KV_CHUNKS, HEAD_DIM)), 0) for e in es]
        dil = outs[0] * ws[0] + outs[1] * ws[1] + outs[2] * ws[2]
        dil_row = _head_rows(dil * _low_tile(_silu(row[:, base:base + D_ATT])))
        mem_row = _sample_mem_attn(row[:, base + D_ATT:base + 2 * D_ATT], gq_ref[...],
                                   mkv_ref[j], row[:, base + 2 * D_ATT:])
        return jnp.where(sub == j, jnp.concatenate([dil_row, mem_row], axis=-1), tile)

    rows = pl.ds(pl.multiple_of(i * SAMPLE_GB, SAMPLE_GB), SAMPLE_GB)
    cat_ref[rows, :] = lax.fori_loop(0, SAMPLE_GB, body, jnp.zeros((SAMPLE_GB, 2 * D_ATT), F32))

    @pl.when(i == pl.num_programs(0) - 1)
    def _():
        y_ref[...] = x_ref[...] + _dot(cat_ref[...].astype(BF16), wout_ref[...])


def _b_layer_sample(x, norm_b, w_in, q_norm, kv_new, caches, mem_cache, bl, layer,
                    mem_q_norm, w_out):
    nb = x.shape[0]
    full = lambda shape: pl.BlockSpec(shape, lambda i: (0,) * len(shape))
    cache_spec = pl.BlockSpec((SAMPLE_GB, BAND, None, KV_CHUNKS, HEAD_DIM),
                              lambda i: (i, 0, 0, 0, 0))
    return pl.pallas_call(
        _b_sample_kernel,
        grid=(nb // SAMPLE_GB,),
        in_specs=[
            full((nb, D_MODEL)),
            _layer_spec((1, D_MODEL), bl),
            _layer_spec(w_in.shape[1:], bl),
            _layer_spec((N_GROUPS, HEAD_DIM), bl),
            full(kv_new.shape),
            cache_spec, cache_spec, cache_spec,
            pl.BlockSpec((None, SAMPLE_GB, MEM_LEN, KV_CHUNKS, HEAD_DIM),
                         lambda i: (layer, i, 0, 0, 0)),
            _layer_spec((1, HEAD_DIM), layer),
            _layer_spec(w_out.shape[1:], bl),
        ],
        out_specs=full((nb, D_MODEL)),
        out_shape=jax.ShapeDtypeStruct((nb, D_MODEL), F32),
        scratch_shapes=[pltpu.VMEM((nb, w_in.shape[2]), F32),
                        pltpu.VMEM((nb, 2 * D_ATT), F32)],
        compiler_params=_params(("arbitrary",)),
        name=f"b_layer_sample_{layer}",
    )(x, norm_b, w_in, q_norm, kv_new, *caches, mem_cache, mem_q_norm, w_out)


def kernel(x_prompt, x_sample, cache_mem_kv, state_conv, cache_dil0_kv, cache_dil1_kv, cache_dil2_kv, mem_prompt, norm_a, w_in_a, conv_w_a, w_out_a, norm_b, w_in_b, q_norm_b, w_out_b, kv_norm, w_kv, k_norm_dil, mem_norm, w_mem_kv, mem_k_norm, mem_q_norm):
    n_a = w_in_a.shape[0]
    n_b = w_in_b.shape[0]
    depth = n_a + n_b
    b, s, _ = x_prompt.shape
    nb = x_sample.shape[0]
    assert x_sample.shape[1] == 1 and s % TM == 0 and nb % SAMPLE_GB == 0
    kv_dims = (2, N_HEADS, HEAD_DIM)

    w_in_a16, w_out_a16 = w_in_a.astype(BF16), w_out_a.astype(BF16)
    w_in_b16, w_out_b16 = w_in_b.astype(BF16), w_out_b.astype(BF16)
    w_kv16, w_mem16 = w_kv.astype(BF16), w_mem_kv.astype(BF16)
    norm_a3 = norm_a.reshape(n_a, 1, D_MODEL)
    norm_b3 = norm_b.reshape(n_b, 1, D_MODEL)
    mem_q_norm3 = mem_q_norm.reshape(depth, 1, HEAD_DIM)

    mem_kv_f, mem_kv_b = _memkv_prompt(mem_prompt, mem_norm, w_mem16, mem_k_norm)
    mem_kv_b = mem_kv_b.reshape(depth, b, MEM_LEN, 2 * D_ATT)
    mem_cache = cache_mem_kv.reshape(depth, nb, MEM_LEN, KV_CHUNKS, HEAD_DIM)
    caches = []
    for g, cache in enumerate((cache_dil0_kv, cache_dil1_kv, cache_dil2_kv)):
        assert cache.shape[1] == WINDOWS[g]
        caches.append(cache.reshape(nb, BAND, DILATIONS[g], KV_CHUNKS, HEAD_DIM))
    state = state_conv.reshape(n_a, -1, HEAD_DIM)

    xp = x_prompt
    xs = x_sample.reshape(nb, D_MODEL)
    conv_p, conv_s = [], []
    for a in range(n_a):
        xp, cp = _a_layer_prompt(xp, norm_a3, w_in_a16, conv_w_a, mem_kv_b, a, a,
                                 mem_q_norm3, w_out_a16)
        xs, cs = _a_layer_sample(xs, norm_a3, w_in_a16, conv_w_a, state, mem_cache, a, a,
                                 mem_q_norm3, w_out_a16)
        conv_p.append(cp)
        conv_s.append(cs.reshape(nb, CONV_W - 1, D_CONV))

    kvb0, kvb1, kvb2, st0, st1, st2 = _kv_prompt(xp, kv_norm, w_kv16, k_norm_dil)
    kvbs = (kvb0, kvb1, kvb2)
    kv_new = _kv_sample(xs, kv_norm, w_kv16, k_norm_dil).reshape(nb, N_GROUPS, KV_CHUNKS, HEAD_DIM)

    for bl in range(n_b):
        layer = n_a + bl
        xp = _b_layer_prompt(xp, norm_b3, w_in_b16, q_norm_b, mem_kv_b, kvbs, bl, layer,
                             mem_q_norm3, w_out_b16)
        xs = _b_layer_sample(xs, norm_b3, w_in_b16, q_norm_b, kv_new, caches, mem_cache, bl,
                             layer, mem_q_norm3, w_out_b16)

    dil_s = [kv_new[:, g].reshape(nb, 1, *kv_dims) for g in range(N_GROUPS)]
    return (xp, xs.reshape(nb, 1, D_MODEL),
            mem_kv_f.reshape(depth, b, MEM_LEN, *kv_dims),
            jnp.stack(conv_p), jnp.stack(conv_s),
            st0.reshape(b, WINDOWS[0], *kv_dims), st1.reshape(b, WINDOWS[1], *kv_dims),
            st2.reshape(b, s, *kv_dims), *dil_s)
```

```python
import jax
import jax.numpy as jnp
from jax import lax
from jax.experimental import pallas as pl
from jax.experimental.pallas import tpu as pltpu

F32 = jnp.float32
BF16 = jnp.bfloat16

D_MODEL = 1024
HEAD_DIM = 128
N_HEADS = 4
D_ATT = N_HEADS * HEAD_DIM
KV_CHUNKS = 2 * N_HEADS
D_CONV = D_MODEL
CONV_W = 3
N_GROUPS = 3
DILATIONS = (1, 4, 16)
WINDOWS = (128, 512, 2048)
BAND = 128
SUB_STRIDE = 4
MEM_LEN = 256
EPS = 1e-6
NEG = -1e30
SCALE = HEAD_DIM ** -0.5

TM = 512
SAMPLE_GB = 8
VMEM_LIMIT = 56 * 1024 * 1024


def _rms(x, g):
    return x * lax.rsqrt(jnp.mean(x * x, axis=-1, keepdims=True) + EPS) * g


def _silu(x):
    return x * (1.0 / (1.0 + jnp.exp(-x)))


def _dot(a, b):
    return jnp.dot(a, b, preferred_element_type=F32)


def _dot_nt(a, b):
    return lax.dot_general(a, b, (((1,), (1,)), ((), ())), preferred_element_type=F32)


def _hs(h, base=0):
    return slice(base + h * HEAD_DIM, base + (h + 1) * HEAD_DIM)


def _params(sem):
    return pltpu.CompilerParams(dimension_semantics=sem, vmem_limit_bytes=VMEM_LIMIT)


def _fixed_spec(block, index):
    return pl.BlockSpec(block, lambda *_: index, pipeline_mode=pl.Buffered(1))


def _const_spec(shape):
    return _fixed_spec(shape, (0,) * len(shape))


def _layer_spec(shape, layer):
    return _fixed_spec((None,) + tuple(shape), (layer,) + (0,) * len(shape))


def _memkv_kernel(mem_ref, g_ref, w_ref, kg_ref, of_ref, ob_ref):
    rt = mem_ref.shape[0]
    xn = _rms(mem_ref[...], g_ref[...]).astype(BF16)
    kv = _dot(xn, w_ref[...])
    for c in range(KV_CHUNKS):
        chunk = kv[:, _hs(c)]
        if c < N_HEADS:
            chunk = _rms(chunk, kg_ref[...])
        of_ref[pl.ds(c, rt, stride=KV_CHUNKS), :] = chunk
        ob_ref[:, _hs(c)] = chunk.astype(BF16)


def _memkv_prompt(mem, mem_norm, w_mem_kv, mem_k_norm):
    depth = w_mem_kv.shape[0]
    rows = mem.shape[0] * mem.shape[1]
    rt = 1024
    return pl.pallas_call(
        _memkv_kernel,
        grid=(depth, rows // rt),
        in_specs=[
            pl.BlockSpec((rt, D_MODEL), lambda l, i: (i, 0)),
            pl.BlockSpec((None, 1, D_MODEL), lambda l, i: (l, 0, 0)),
            pl.BlockSpec((None, D_MODEL, 2 * D_ATT), lambda l, i: (l, 0, 0)),
            pl.BlockSpec((None, 1, HEAD_DIM), lambda l, i: (l, 0, 0)),
        ],
        out_specs=[
            pl.BlockSpec((None, rt * KV_CHUNKS, HEAD_DIM), lambda l, i: (l, i, 0)),
            pl.BlockSpec((None, rt, 2 * D_ATT), lambda l, i: (l, i, 0)),
        ],
        out_shape=[jax.ShapeDtypeStruct((depth, rows * KV_CHUNKS, HEAD_DIM), F32),
                   jax.ShapeDtypeStruct((depth, rows, 2 * D_ATT), BF16)],
        compiler_params=_params(("parallel", "parallel")),
        name="memkv_prompt",
    )(mem.reshape(rows, D_MODEL), mem_norm.reshape(depth, 1, D_MODEL), w_mem_kv,
      mem_k_norm.reshape(depth, 1, HEAD_DIM))


def _pv_and_rowsum(p, v):
    ones = jnp.ones((v.shape[0], HEAD_DIM), BF16)
    both = _dot(p, jnp.concatenate([v, ones], axis=1))
    return both[:, :HEAD_DIM], both[:, HEAD_DIM:]


def _mem_attn(mq, mg, mkv_ref, gq, store):
    for h in range(N_HEADS):
        q = _rms(mq[:, _hs(h)], gq).astype(BF16)
        s = _dot_nt(q, mkv_ref[:, _hs(h)])
        m = jnp.max(s, axis=-1, keepdims=True)
        acc, l = _pv_and_rowsum(jnp.exp(s - m).astype(BF16), mkv_ref[:, _hs(h, D_ATT)])
        store(h, acc / l * _silu(mg[:, _hs(h)]))


def _step_index():
    return pl.program_id(0) * pl.num_programs(1) + pl.program_id(1)


def _is_last_step():
    return _step_index() == pl.num_programs(0) * pl.num_programs(1) - 1


def _a_layer_kernel(x_ref, g_ref, win_ref, cw_ref, mkv_ref, gq_ref, wout_ref,
                    xs_ref, st_ref, mc_ref,
                    y_ref, cs_ref, ys_ref, sto_ref,
                    ext_ref, cat_ref, zs_ref, cats_ref):
    t = pl.program_id(1)
    n = _step_index()

    @pl.when(t == 0)
    def _():
        ext_ref[0:8, :] = jnp.zeros((8, D_CONV), F32)

    @pl.when(n == 0)
    def _():
        _a_sample_first(xs_ref, g_ref, win_ref, cw_ref, st_ref, sto_ref, zs_ref, cats_ref)

    row = zs_ref[pl.ds(n, 1), :]
    _merge_row(cats_ref, slice(D_CONV, D_CONV + D_ATT), n,
               _sample_mem_attn(row[:, :D_ATT], gq_ref[...], mc_ref[0], row[:, D_ATT:]))

    x = x_ref[...]
    xn = _rms(x, g_ref[...]).astype(BF16)

    def proj(lo, hi):
        return _dot(xn, win_ref[:, lo:hi])

    u = proj(D_CONV, 2 * D_CONV) * proj(0, D_CONV)
    ext_ref[8:8 + TM, :] = u
    y = (cw_ref[0:1, :] * ext_ref[6:6 + TM, :] + cw_ref[1:2, :] * ext_ref[7:7 + TM, :]
         + cw_ref[2:3, :] * u)
    ext_ref[0:8, :] = ext_ref[TM:TM + 8, :]
    cs_ref[...] = ext_ref[6:8, :]
    bg = proj(2 * D_CONV, 3 * D_CONV)
    g = proj(3 * D_CONV, 4 * D_CONV)
    cat_ref[:, 0:D_CONV] = (_silu(g) * bg * y).astype(BF16)
    mq = proj(4 * D_CONV, 4 * D_CONV + D_ATT)
    mg = proj(4 * D_CONV + D_ATT, 4 * D_CONV + 2 * D_ATT)

    def store(h, val):
        cat_ref[:, _hs(h, D_CONV)] = val.astype(BF16)

    _mem_attn(mq, mg, mkv_ref, gq_ref[...] * SCALE, store)
    y_ref[...] = x + _dot(cat_ref[...], wout_ref[...])

    @pl.when(_is_last_step())
    def _():
        ys_ref[...] = xs_ref[...] + _dot(cats_ref[...].astype(BF16), wout_ref[...])


def _sample_specs(nb, nt, layer):
    full = lambda shape: pl.BlockSpec(shape, lambda i, t: (0,) * len(shape))
    mem_cache = pl.BlockSpec((None, 1, MEM_LEN, KV_CHUNKS, HEAD_DIM),
                             lambda i, t: (layer, i * nt + t, 0, 0, 0))
    return full, mem_cache


def _a_layer(x, xs, norm_a, w_in, conv_w, mkv_b, state, mem_cache, a, layer, mem_q_norm, w_out):
    b, s, _ = x.shape
    nb = xs.shape[0]
    nt = s // TM
    assert nb == b * nt
    st_rows = nb * (CONV_W - 1) * D_CONV // HEAD_DIM
    full, mem_cache_spec = _sample_specs(nb, nt, layer)
    return pl.pallas_call(
        _a_layer_kernel,
        grid=(b, nt),
        in_specs=[
            pl.BlockSpec((None, TM, D_MODEL), lambda i, t: (i, t, 0)),
            _layer_spec((1, D_MODEL), a),
            _layer_spec(w_in.shape[1:], a),
            _layer_spec((CONV_W, D_CONV), a),
            pl.BlockSpec((None, None, MEM_LEN, 2 * D_ATT), lambda i, t: (layer, i, 0, 0)),
            _layer_spec((1, HEAD_DIM), layer),
            _layer_spec(w_out.shape[1:], a),
            full((nb, D_MODEL)),
            _layer_spec((st_rows, HEAD_DIM), a),
            mem_cache_spec,
        ],
        out_specs=[
            pl.BlockSpec((None, TM, D_MODEL), lambda i, t: (i, t, 0)),
            pl.BlockSpec((None, CONV_W - 1, D_CONV), lambda i, t: (i, 0, 0)),
            full((nb, D_MODEL)),
            full((st_rows, HEAD_DIM)),
        ],
        out_shape=[jax.ShapeDtypeStruct((b, s, D_MODEL), F32),
                   jax.ShapeDtypeStruct((b, CONV_W - 1, D_CONV), F32),
                   jax.ShapeDtypeStruct((nb, D_MODEL), F32),
                   jax.ShapeDtypeStruct((st_rows, HEAD_DIM), F32)],
        scratch_shapes=[pltpu.VMEM((TM + 8, D_CONV), F32),
                        pltpu.VMEM((TM, D_CONV + D_ATT), BF16),
                        pltpu.VMEM((nb, 2 * D_ATT), F32),
                        pltpu.VMEM((nb, D_CONV + D_ATT), F32)],
        compiler_params=_params(("arbitrary", "arbitrary")),
        name=f"a_layer_{layer}",
    )(x, norm_a, w_in, conv_w, mkv_b, mem_q_norm, w_out, xs, state, mem_cache)


def _residue_rows(slab_ref, split_ref, c, d):
    rows = slab_ref.shape[1]
    if d == SUB_STRIDE:
        return lambda r: slab_ref[c, pl.ds(r, rows // d, stride=d), :]
    assert d == SUB_STRIDE * SUB_STRIDE
    for r1 in range(SUB_STRIDE):
        split_ref[c, r1] = slab_ref[c, pl.ds(r1, rows // SUB_STRIDE, stride=SUB_STRIDE), :]
    return lambda r: split_ref[c, r % SUB_STRIDE,
                               pl.ds(r // SUB_STRIDE, rows // d, stride=SUB_STRIDE), :]


def _to_residue_major(dst_ref, slab_ref, split_ref, n_chunks, d):
    for c in range(n_chunks):
        get = _residue_rows(slab_ref, split_ref, c, d)
        for r in range(d):
            dst_ref[r, :, _hs(c)] = get(r).astype(dst_ref.dtype)


def _kv_kernel(x_ref, g_ref, w_ref, kn_ref, xs_ref, kvb0_ref, kvb1_ref, kvb2_ref,
               st0_ref, st1_ref, st2_ref, kvn_ref, slab1_ref, slab2_ref, split_ref):
    t = pl.program_id(1)
    last = pl.num_programs(1) - 1

    @pl.when(_step_index() == 0)
    def _():
        _kv_sample_rows(xs_ref, g_ref, w_ref, kn_ref, kvn_ref)

    xn = _rms(x_ref[...], g_ref[...]).astype(BF16)
    kvb_refs = (kvb0_ref, kvb1_ref, kvb2_ref)
    slabs = (None, slab1_ref, slab2_ref)
    for g in reversed(range(N_GROUPS)):
        kv = _dot(xn, w_ref[:, g * 2 * D_ATT:(g + 1) * 2 * D_ATT])
        chunks = [_rms(kv[:, _hs(h)], kn_ref[g:g + 1, :]) for h in range(N_HEADS)]
        chunks += [kv[:, _hs(h, D_ATT)] for h in range(N_HEADS)]
        d = DILATIONS[g]
        for c, chunk in enumerate(chunks):
            if d == 1:
                kvb_refs[g][0, :, _hs(c)] = chunk.astype(BF16)
            else:
                slabs[g][c] = chunk
            if g == 2:
                st2_ref[pl.ds(c, TM, stride=KV_CHUNKS), :] = chunk
        if d > 1:
            _to_residue_major(kvb_refs[g], slabs[g], split_ref, KV_CHUNKS, d)
        if g == 1:
            @pl.when(t == last)
            def _():
                for c, chunk in enumerate(chunks):
                    st1_ref[pl.ds(c, TM, stride=KV_CHUNKS), :] = chunk
        if g == 0:
            @pl.when(t == last)
            def _():
                for c, chunk in enumerate(chunks):
                    st0_ref[pl.ds(c, WINDOWS[0], stride=KV_CHUNKS), :] = chunk[TM - WINDOWS[0]:, :]


def _kv_shared(x, xs, kv_norm, w_kv, k_norm_dil):
    b, s, _ = x.shape
    nb = xs.shape[0]
    kvn_rows = nb * N_GROUPS * KV_CHUNKS
    assert WINDOWS[1] == TM and WINDOWS[2] == s and WINDOWS[0] <= TM
    first = lambda i, t: (i, 0, 0)
    res_spec = lambda d: pl.BlockSpec((None, d, TM // d, 2 * D_ATT), lambda i, t: (i, 0, t, 0))
    res_sds = lambda d: jax.ShapeDtypeStruct((b, d, s // d, 2 * D_ATT), BF16)
    st_sds = lambda rows: jax.ShapeDtypeStruct((b, rows * KV_CHUNKS, HEAD_DIM), F32)
    return pl.pallas_call(
        _kv_kernel,
        grid=(b, s // TM),
        in_specs=[
            pl.BlockSpec((None, TM, D_MODEL), lambda i, t: (i, t, 0)),
            _const_spec((1, D_MODEL)),
            _const_spec((D_MODEL, N_GROUPS * 2 * D_ATT)),
            _const_spec((N_GROUPS, HEAD_DIM)),
            _const_spec((nb, D_MODEL)),
        ],
        out_specs=[
            res_spec(DILATIONS[0]), res_spec(DILATIONS[1]), res_spec(DILATIONS[2]),
            pl.BlockSpec((None, WINDOWS[0] * KV_CHUNKS, HEAD_DIM), first),
            pl.BlockSpec((None, WINDOWS[1] * KV_CHUNKS, HEAD_DIM), first),
            pl.BlockSpec((None, TM * KV_CHUNKS, HEAD_DIM), lambda i, t: (i, t, 0)),
            pl.BlockSpec((kvn_rows, HEAD_DIM), lambda i, t: (0, 0)),
        ],
        out_shape=[res_sds(DILATIONS[0]), res_sds(DILATIONS[1]), res_sds(DILATIONS[2]),
                   st_sds(WINDOWS[0]), st_sds(WINDOWS[1]), st_sds(s),
                   jax.ShapeDtypeStruct((kvn_rows, HEAD_DIM), F32)],
        scratch_shapes=[pltpu.VMEM((KV_CHUNKS, TM, HEAD_DIM), F32),
                        pltpu.VMEM((KV_CHUNKS, TM, HEAD_DIM), F32),
                        pltpu.VMEM((KV_CHUNKS, SUB_STRIDE, TM // SUB_STRIDE, HEAD_DIM), F32)],
        compiler_params=_params(("arbitrary", "arbitrary")),
        name="kv_prompt",
    )(x, kv_norm.reshape(1, D_MODEL), w_kv, k_norm_dil, xs)


def _band_scores(qs, ks, masks):
    return jnp.concatenate([jnp.where(mk, _dot_nt(q, k), NEG) for q, k, mk in zip(qs, ks, masks)],
                           axis=0)


def _band_finish(s, vs, nq):
    m = jnp.max(s, axis=-1, keepdims=True)
    pb = jnp.exp(s - m).astype(BF16)
    res = []
    for i, v in enumerate(vs):
        rows = slice(i * nq, (i + 1) * nq)
        acc, l = _pv_and_rowsum(pb[rows], v)
        res.append((acc * (1.0 / l), m[rows] + jnp.log(l)))
    return res


def _b_layer_kernel(x_ref, g_ref, win_ref, qn_ref, mkv_ref, gq_ref, wout_ref,
                    kv0_ref, kv0p_ref, kv1_ref, kv1p_ref, kv2_ref,
                    xs_ref, kvn_ref, c0_ref, c1_ref, c2_ref, mc_ref,
                    y_ref, ys_ref,
                    qslab1_ref, qslab2_ref, qsplit_ref, oslab_ref, lslab_ref, cat_ref,
                    zs_ref, cats_ref):
    t = pl.program_id(1)
    n = _step_index()

    @pl.when(n == 0)
    def _():
        zs_ref[...] = _dot(_rms(xs_ref[...], g_ref[...]).astype(BF16), win_ref[...])
        cats_ref[...] = jnp.zeros(cats_ref.shape, F32)

    _merge_row(cats_ref, slice(0, 2 * D_ATT), n,
               _b_sample_row(zs_ref[pl.ds(n, 1), :], kvn_ref[n], qn_ref, gq_ref[...],
                             (c0_ref[0], c1_ref[0], c2_ref[0]), mc_ref[0]))

    x = x_ref[...]
    xn = _rms(x, g_ref[...]).astype(BF16)

    def proj(lo, hi):
        return _dot(xn, win_ref[:, lo:hi])

    qi = lax.broadcasted_iota(jnp.int32, (BAND, 2 * BAND), 0)
    kj = lax.broadcasted_iota(jnp.int32, (BAND, 2 * BAND), 1)
    mask_rest = (kj >= qi) & (kj <= qi + BAND)
    mask_edge = (kj >= jnp.maximum(qi, jnp.where(t == 0, BAND, 0))) & (kj <= qi + BAND)

    def finish(g, h, rows, out, lse):
        oslab_ref[g, h, rows, :] = out
        lslab_ref[g, h, rows, :] = lse

    def normed_heads(g):
        qd = proj(g * D_ATT, (g + 1) * D_ATT)
        gain = qn_ref[g:g + 1, :] * SCALE
        return [_rms(qd[:, _hs(h)], gain) for h in range(N_HEADS)]

    n_sub = TM // BAND
    qh = [q.astype(BF16) for q in normed_heads(0)]
    units0 = [(i, h) for i in range(n_sub) for h in range(N_HEADS)]

    def keys0(i, cols):
        if i == 0:
            return jnp.concatenate([kv0p_ref[0, :, cols], kv0_ref[0, 0:BAND, cols]], axis=0)
        return kv0_ref[0, (i - 1) * BAND:(i + 1) * BAND, cols]

    d1 = DILATIONS[1]
    for h, q in enumerate(normed_heads(1)):
        qslab1_ref[h] = q
    get_q1 = [_residue_rows(qslab1_ref, None, h, d1) for h in range(N_HEADS)]
    units1 = [(r, h) for r in range(d1) for h in range(N_HEADS)]

    def keys1(r, cols):
        return jnp.concatenate([kv1p_ref[r, :, cols], kv1_ref[r, :, cols]], axis=0)

    d2 = DILATIONS[2]
    nq = TM // d2
    for h, q in enumerate(normed_heads(2)):
        qslab2_ref[h] = q
    get_q2 = [_residue_rows(qslab2_ref, qsplit_ref, h, d2) for h in range(N_HEADS)]
    units2 = [(r, h) for r in range(d2) for h in range(N_HEADS)]
    qi2 = lax.broadcasted_iota(jnp.int32, (nq, BAND), 0)
    kj2 = lax.broadcasted_iota(jnp.int32, (nq, BAND), 1)
    mask2 = kj2 <= qi2 + t * nq

    s0 = _band_scores([qh[h][i * BAND:(i + 1) * BAND] for i, h in units0],
                      [keys0(i, _hs(h)) for i, h in units0],
                      [mask_edge if i == 0 else mask_rest for i, h in units0])
    s1 = _band_scores([get_q1[h](r).astype(BF16) for r, h in units1],
                      [keys1(r, _hs(h)) for r, h in units1], [mask_edge] * len(units1))
    s2 = _band_scores([get_q2[h](r).astype(BF16) for r, h in units2],
                      [kv2_ref[r, :, _hs(h)] for r, h in units2], [mask2] * len(units2))
    res = _band_finish(s0, [keys0(i, _hs(h, D_ATT)) for i, h in units0], BAND)
    for (i, h), (out, lse) in zip(units0, res):
        finish(0, h, pl.ds(i * BAND, BAND), out, lse)
    res = _band_finish(s1, [keys1(r, _hs(h, D_ATT)) for r, h in units1], BAND)
    for (r, h), (out, lse) in zip(units1, res):
        finish(1, h, pl.ds(r, BAND, stride=d1), out, lse)
    res = _band_finish(s2, [kv2_ref[r, :, _hs(h, D_ATT)] for r, h in units2], nq)
    for (r, h), (out, lse) in zip(units2, res):
        finish(2, h, pl.ds(r, nq, stride=d2), out, lse)

    base = N_GROUPS * D_ATT
    gd = proj(base, base + D_ATT)
    for h in range(N_HEADS):
        l0, l1, l2 = lslab_ref[0, h], lslab_ref[1, h], lslab_ref[2, h]
        m = jnp.maximum(jnp.maximum(l0, l1), l2)
        e0, e1, e2 = jnp.exp(l0 - m), jnp.exp(l1 - m), jnp.exp(l2 - m)
        dil = ((oslab_ref[0, h] * e0 + oslab_ref[1, h] * e1 + oslab_ref[2, h] * e2)
               * (1.0 / (e0 + e1 + e2)))
        cat_ref[:, _hs(h)] = (dil * _silu(gd[:, _hs(h)])).astype(BF16)
    mq = proj(base + D_ATT, base + 2 * D_ATT)
    mg = proj(base + 2 * D_ATT, base + 3 * D_ATT)

    def store(h, val):
        cat_ref[:, _hs(h, D_ATT)] = val.astype(BF16)

    _mem_attn(mq, mg, mkv_ref, gq_ref[...] * SCALE, store)
    y_ref[...] = x + _dot(cat_ref[...], wout_ref[...])

    @pl.when(_is_last_step())
    def _():
        ys_ref[...] = xs_ref[...] + _dot(cats_ref[...].astype(BF16), wout_ref[...])


def _b_layer(x, xs, norm_b, w_in, q_norm, mkv_b, kvbs, kv_new, caches, mem_cache, bl, layer,
             mem_q_norm, w_out):
    b, s, _ = x.shape
    nb = xs.shape[0]
    nt = s // TM
    assert nb == b * nt
    full, mem_cache_spec = _sample_specs(nb, nt, layer)
    cache_spec = pl.BlockSpec((1, BAND, None, KV_CHUNKS, HEAD_DIM),
                              lambda i, t: (i * nt + t, 0, 0, 0, 0))
    assert TM // BAND == DILATIONS[1] and s // DILATIONS[2] == BAND
    tile = lambda i, t: (i, t, 0)
    sub = TM // BAND
    kv_w = 2 * D_ATT
    return pl.pallas_call(
        _b_layer_kernel,
        grid=(b, s // TM),
        in_specs=[
            pl.BlockSpec((None, TM, D_MODEL), tile),
            _layer_spec((1, D_MODEL), bl),
            _layer_spec(w_in.shape[1:], bl),
            _layer_spec((N_GROUPS, HEAD_DIM), bl),
            pl.BlockSpec((None, None, MEM_LEN, kv_w), lambda i, t: (layer, i, 0, 0)),
            _layer_spec((1, HEAD_DIM), layer),
            _layer_spec(w_out.shape[1:], bl),
            pl.BlockSpec((None, 1, TM, kv_w), lambda i, t: (i, 0, t, 0)),
            pl.BlockSpec((None, 1, BAND, kv_w),
                         lambda i, t: (i, 0, jnp.maximum(t * sub - 1, 0), 0)),
            pl.BlockSpec((None, DILATIONS[1], BAND, kv_w), lambda i, t: (i, 0, t, 0)),
            pl.BlockSpec((None, DILATIONS[1], BAND, kv_w),
                         lambda i, t: (i, 0, jnp.maximum(t - 1, 0), 0)),
            pl.BlockSpec((None, DILATIONS[2], BAND, kv_w), lambda i, t: (i, 0, 0, 0)),
            full((nb, D_MODEL)),
            full(kv_new.shape),
            cache_spec, cache_spec, cache_spec,
            mem_cache_spec,
        ],
        out_specs=[pl.BlockSpec((None, TM, D_MODEL), tile), full((nb, D_MODEL))],
        out_shape=[jax.ShapeDtypeStruct((b, s, D_MODEL), F32),
                   jax.ShapeDtypeStruct((nb, D_MODEL), F32)],
        scratch_shapes=[pltpu.VMEM((N_HEADS, TM, HEAD_DIM), F32),
                        pltpu.VMEM((N_HEADS, TM, HEAD_DIM), F32),
                        pltpu.VMEM((N_HEADS, SUB_STRIDE, TM // SUB_STRIDE, HEAD_DIM), F32),
                        pltpu.VMEM((N_GROUPS, N_HEADS, TM, HEAD_DIM), F32),
                        pltpu.VMEM((N_GROUPS, N_HEADS, TM, HEAD_DIM), F32),
                        pltpu.VMEM((TM, 2 * D_ATT), BF16),
                        pltpu.VMEM((nb, w_in.shape[2]), F32),
                        pltpu.VMEM((nb, 2 * D_ATT), F32)],
        compiler_params=_params(("arbitrary", "arbitrary")),
        name=f"b_layer_{layer}",
    )(x, norm_b, w_in, q_norm, mkv_b, mem_q_norm, w_out,
      kvbs[0], kvbs[0], kvbs[1], kvbs[1], kvbs[2],
      xs, kv_new, *caches, mem_cache)


def _swap_halves(x, axis):
    return pltpu.roll(x, N_HEADS, axis=axis)


def _head_rows(tile):
    return jnp.concatenate([tile[N_HEADS + h:N_HEADS + h + 1, :] for h in range(N_HEADS)],
                           axis=-1)


def _low_tile(row):
    return jnp.concatenate([jnp.zeros((N_HEADS, HEAD_DIM), F32)]
                           + [row[:, _hs(h)] for h in range(N_HEADS)], axis=0)


def _q_tile(row, g):
    return jnp.concatenate([_rms(row[:, _hs(h)], g) * SCALE for h in range(N_HEADS)]
                           + [jnp.zeros((N_HEADS, HEAD_DIM), F32)], axis=0)


def _tile_attend(qt, kv):
    s = jnp.sum(kv * qt[None], axis=-1, keepdims=True)
    m = jnp.max(s, axis=0)
    p = jnp.exp(s - m[None])
    l = jnp.sum(p, axis=0)
    acc = jnp.sum(_swap_halves(jnp.broadcast_to(p, kv.shape), 1) * kv, axis=0)
    out = acc / _swap_halves(jnp.broadcast_to(l, acc.shape), 0)
    return out, m + jnp.log(l)


def _sample_mem_attn(q_row, gq, mkv, mg_row):
    out, _ = _tile_attend(_q_tile(q_row, gq), mkv)
    return _head_rows(out * _low_tile(_silu(mg_row)))


def _gather_chunks(ref, n_rows, n_chunks, offset=0):
    total = ref.shape[0] // n_rows
    return jnp.concatenate([ref[pl.ds(offset + c, n_rows, stride=total), :]
                            for c in range(n_chunks)], axis=-1)


def _merge_row(ref, cols, n, row):
    tile = pl.ds(pl.multiple_of((n // SAMPLE_GB) * SAMPLE_GB, SAMPLE_GB), SAMPLE_GB)
    sub = lax.broadcasted_iota(jnp.int32, (SAMPLE_GB, row.shape[1]), 0)
    ref[tile, cols] = jnp.where(sub == n % SAMPLE_GB, row, ref[tile, cols])


def _a_sample_first(x_ref, g_ref, win_ref, cw_ref, st_ref, sto_ref, zm_ref, cat_ref):
    nb = x_ref.shape[0]
    n_st = D_CONV // HEAD_DIM
    xn = _rms(x_ref[...], g_ref[...]).astype(BF16)
    z = _dot(xn, win_ref[...])
    u = z[:, D_CONV:2 * D_CONV] * z[:, 0:D_CONV]
    s0 = _gather_chunks(st_ref, nb, n_st)
    s1 = _gather_chunks(st_ref, nb, n_st, n_st)
    y = cw_ref[0:1, :] * s0 + cw_ref[1:2, :] * s1 + cw_ref[2:3, :] * u
    for c in range(n_st):
        sto_ref[pl.ds(c, nb, stride=2 * n_st), :] = s1[:, _hs(c)]
        sto_ref[pl.ds(n_st + c, nb, stride=2 * n_st), :] = u[:, _hs(c)]
    cat_ref[:, 0:D_CONV] = _silu(z[:, 3 * D_CONV:4 * D_CONV]) * z[:, 2 * D_CONV:3 * D_CONV] * y
    cat_ref[:, D_CONV:] = jnp.zeros((nb, D_ATT), F32)
    zm_ref[...] = z[:, 4 * D_CONV:]


def _kv_sample_rows(x_ref, g_ref, w_ref, kn_ref, o_ref):
    nb = x_ref.shape[0]
    xn = _rms(x_ref[...], g_ref[...]).astype(BF16)
    kv = _dot(xn, w_ref[...])
    n_chunks = N_GROUPS * KV_CHUNKS
    for g in range(N_GROUPS):
        for c in range(KV_CHUNKS):
            chunk = kv[:, _hs(c, g * 2 * D_ATT)]
            if c < N_HEADS:
                chunk = _rms(chunk, kn_ref[g:g + 1, :])
            o_ref[pl.ds(g * KV_CHUNKS + c, nb, stride=n_chunks), :] = chunk


def _b_sample_row(row, kv_new, qn_ref, gq, caches, mkv):
    base = N_GROUPS * D_ATT
    outs, lses = [], []
    for g in range(N_GROUPS):
        qt = _q_tile(row[:, g * D_ATT:(g + 1) * D_ATT], qn_ref[g:g + 1, :])
        out, lse = _tile_attend(qt, jnp.concatenate([caches[g], kv_new[g][None]], axis=0))
        outs.append(out)
        lses.append(lse)
    m = jnp.maximum(jnp.maximum(lses[0], lses[1]), lses[2])
    es = [jnp.exp(l - m) for l in lses]
    den = es[0] + es[1] + es[2]
    ws = [_swap_halves(jnp.broadcast_to(e / den, (KV_CHUNKS, HEAD_DIM)), 0) for e in es]
    dil = outs[0] * ws[0] + outs[1] * ws[1] + outs[2] * ws[2]
    dil_row = _head_rows(dil * _low_tile(_silu(row[:, base:base + D_ATT])))
    mem_row = _sample_mem_attn(row[:, base + D_ATT:base + 2 * D_ATT], gq, mkv,
                               row[:, base + 2 * D_ATT:])
    return jnp.concatenate([dil_row, mem_row], axis=-1)


def kernel(x_prompt, x_sample, cache_mem_kv, state_conv, cache_dil0_kv, cache_dil1_kv, cache_dil2_kv, mem_prompt, norm_a, w_in_a, conv_w_a, w_out_a, norm_b, w_in_b, q_norm_b, w_out_b, kv_norm, w_kv, k_norm_dil, mem_norm, w_mem_kv, mem_k_norm, mem_q_norm):
    n_a = w_in_a.shape[0]
    n_b = w_in_b.shape[0]
    depth = n_a + n_b
    b, s, _ = x_prompt.shape
    nb = x_sample.shape[0]
    assert x_sample.shape[1] == 1 and s % TM == 0 and nb % SAMPLE_GB == 0
    kv_dims = (2, N_HEADS, HEAD_DIM)

    w_in_a16, w_out_a16 = w_in_a.astype(BF16), w_out_a.astype(BF16)
    w_in_b16, w_out_b16 = w_in_b.astype(BF16), w_out_b.astype(BF16)
    w_kv16, w_mem16 = w_kv.astype(BF16), w_mem_kv.astype(BF16)
    norm_a3 = norm_a.reshape(n_a, 1, D_MODEL)
    norm_b3 = norm_b.reshape(n_b, 1, D_MODEL)
    mem_q_norm3 = mem_q_norm.reshape(depth, 1, HEAD_DIM)

    mem_kv_f, mem_kv_b = _memkv_prompt(mem_prompt, mem_norm, w_mem16, mem_k_norm)
    mem_kv_b = mem_kv_b.reshape(depth, b, MEM_LEN, 2 * D_ATT)
    mem_cache = cache_mem_kv.reshape(depth, nb, MEM_LEN, KV_CHUNKS, HEAD_DIM)
    caches = []
    for g, cache in enumerate((cache_dil0_kv, cache_dil1_kv, cache_dil2_kv)):
        assert cache.shape[1] == WINDOWS[g]
        caches.append(cache.reshape(nb, BAND, DILATIONS[g], KV_CHUNKS, HEAD_DIM))
    state = state_conv.reshape(n_a, -1, HEAD_DIM)

    xp = x_prompt
    xs = x_sample.reshape(nb, D_MODEL)
    conv_p, conv_s = [], []
    for a in range(n_a):
        xp, cp, xs, cs = _a_layer(xp, xs, norm_a3, w_in_a16, conv_w_a, mem_kv_b, state, mem_cache,
                                  a, a, mem_q_norm3, w_out_a16)
        conv_p.append(cp)
        conv_s.append(cs.reshape(nb, CONV_W - 1, D_CONV))

    kvb0, kvb1, kvb2, st0, st1, st2, kv_new = _kv_shared(xp, xs, kv_norm, w_kv16, k_norm_dil)
    kvbs = (kvb0, kvb1, kvb2)
    kv_new = kv_new.reshape(nb, N_GROUPS, KV_CHUNKS, HEAD_DIM)

    for bl in range(n_b):
        layer = n_a + bl
        xp, xs = _b_layer(xp, xs, norm_b3, w_in_b16, q_norm_b, mem_kv_b, kvbs, kv_new, caches,
                          mem_cache, bl, layer, mem_q_norm3, w_out_b16)

    dil_s = [kv_new[:, g].reshape(nb, 1, *kv_dims) for g in range(N_GROUPS)]
    return (xp, xs.reshape(nb, 1, D_MODEL),
            mem_kv_f.reshape(depth, b, MEM_LEN, *kv_dims),
            jnp.stack(conv_p), jnp.stack(conv_s),
            st0.reshape(b, WINDOWS[0], *kv_dims), st1.reshape(b, WINDOWS[1], *kv_dims),
            st2.reshape(b, s, *kv_dims), *dil_s)
```

```python
import functools

import jax
import jax.numpy as jnp
from jax import lax
from jax.experimental import pallas as pl
from jax.experimental.pallas import tpu as pltpu

F32 = jnp.float32
BF16 = jnp.bfloat16

D_MODEL = 1024
HEAD_DIM = 128
N_HEADS = 4
D_ATT = N_HEADS * HEAD_DIM
KV_CHUNKS = 2 * N_HEADS
D_CONV = D_MODEL
CONV_W = 3
N_GROUPS = 3
DILATIONS = (1, 4, 16)
WINDOWS = (128, 512, 2048)
BAND = 128
SUB_STRIDE = 4
MEM_LEN = 256
EPS = 1e-6
NEG = -1e30
SCALE = HEAD_DIM ** -0.5

TM = 512
SAMPLE_GB = 8
W_STAGE_ROWS = 256
VMEM_LIMIT = 56 * 1024 * 1024


def _rms(x, g):
    return x * lax.rsqrt(jnp.mean(x * x, axis=-1, keepdims=True) + EPS) * g


def _silu(x):
    return x * (1.0 / (1.0 + jnp.exp(-x)))


def _dot(a, b):
    return jnp.dot(a, b, preferred_element_type=F32)


def _dot_nt(a, b):
    return lax.dot_general(a, b, (((1,), (1,)), ((), ())), preferred_element_type=F32)


def _hs(h, base=0):
    return slice(base + h * HEAD_DIM, base + (h + 1) * HEAD_DIM)


def _params(sem):
    return pltpu.CompilerParams(dimension_semantics=sem, vmem_limit_bytes=VMEM_LIMIT)


def _fixed_spec(block, index):
    return pl.BlockSpec(block, lambda *_: index, pipeline_mode=pl.Buffered(1))


def _const_spec(shape):
    return _fixed_spec(shape, (0,) * len(shape))


def _layer_spec(shape, layer):
    return _fixed_spec((None,) + tuple(shape), (layer,) + (0,) * len(shape))


def _cast_weight(w_hbm, dst_ref, stage_ref, sem_ref):
    rows = stage_ref.shape[1]
    n_chunks = w_hbm.shape[0] // rows

    def copy(c, slot):
        src = w_hbm.at[pl.ds(pl.multiple_of(c * rows, rows), rows)]
        return pltpu.make_async_copy(src, stage_ref.at[slot], sem_ref.at[slot])

    copy(0, 0).start()

    def body(c, carry):
        slot = c % 2

        @pl.when(c + 1 < n_chunks)
        def _():
            copy(c + 1, 1 - slot).start()

        copy(c, slot).wait()
        dst_ref[pl.ds(pl.multiple_of(c * rows, rows), rows), :] = stage_ref[slot].astype(BF16)
        return carry

    lax.fori_loop(0, n_chunks, body, 0)


def _weight_scratch(shape):
    k, n = shape
    assert k % W_STAGE_ROWS == 0
    return [pltpu.VMEM((k, n), BF16), pltpu.VMEM((2, W_STAGE_ROWS, n), F32),
            pltpu.SemaphoreType.DMA((2,))]


def _memkv_kernel(mem_ref, g_ref, w_ref, kg_ref, of_ref, ob_ref):
    rt = mem_ref.shape[0]
    xn = _rms(mem_ref[...], g_ref[...]).astype(BF16)
    kv = _dot(xn, w_ref[...].astype(BF16))
    for c in range(KV_CHUNKS):
        chunk = kv[:, _hs(c)]
        if c < N_HEADS:
            chunk = _rms(chunk, kg_ref[...])
        of_ref[pl.ds(c, rt, stride=KV_CHUNKS), :] = chunk
        ob_ref[:, _hs(c)] = chunk.astype(BF16)


def _memkv_prompt(mem, mem_norm, w_mem_kv, mem_k_norm):
    depth = w_mem_kv.shape[0]
    rows = mem.shape[0] * mem.shape[1]
    rt = 1024
    return pl.pallas_call(
        _memkv_kernel,
        grid=(depth, rows // rt),
        in_specs=[
            pl.BlockSpec((rt, D_MODEL), lambda l, i: (i, 0)),
            pl.BlockSpec((None, 1, D_MODEL), lambda l, i: (l, 0, 0)),
            pl.BlockSpec((None, D_MODEL, 2 * D_ATT), lambda l, i: (l, 0, 0)),
            pl.BlockSpec((None, 1, HEAD_DIM), lambda l, i: (l, 0, 0)),
        ],
        out_specs=[
            pl.BlockSpec((None, rt * KV_CHUNKS, HEAD_DIM), lambda l, i: (l, i, 0)),
            pl.BlockSpec((None, rt, 2 * D_ATT), lambda l, i: (l, i, 0)),
        ],
        out_shape=[jax.ShapeDtypeStruct((depth, rows * KV_CHUNKS, HEAD_DIM), F32),
                   jax.ShapeDtypeStruct((depth, rows, 2 * D_ATT), BF16)],
        compiler_params=_params(("parallel", "parallel")),
        name="memkv_prompt",
    )(mem.reshape(rows, D_MODEL), mem_norm.reshape(depth, 1, D_MODEL), w_mem_kv,
      mem_k_norm.reshape(depth, 1, HEAD_DIM))


def _pv_and_rowsum(p, v):
    ones = jnp.ones((v.shape[0], HEAD_DIM), BF16)
    both = _dot(p, jnp.concatenate([v, ones], axis=1))
    return both[:, :HEAD_DIM], both[:, HEAD_DIM:]


def _mem_attn(mq, mg, mkv_ref, gq, store):
    for h in range(N_HEADS):
        q = _rms(mq[:, _hs(h)], gq).astype(BF16)
        s = _dot_nt(q, mkv_ref[:, _hs(h)])
        m = jnp.max(s, axis=-1, keepdims=True)
        acc, l = _pv_and_rowsum(jnp.exp(s - m).astype(BF16), mkv_ref[:, _hs(h, D_ATT)])
        store(h, acc / l * _silu(mg[:, _hs(h)]))


def _step_index():
    return pl.program_id(0) * pl.num_programs(1) + pl.program_id(1)


def _is_last_step():
    return _step_index() == pl.num_programs(0) * pl.num_programs(1) - 1


def _a_layer_kernel(x_ref, g_ref, win_hbm, cw_ref, mkv_ref, gq_ref, wout_hbm,
                    xs_ref, st_ref, mc_ref,
                    y_ref, cs_ref, ys_ref, sto_ref,
                    ext_ref, cat_ref, zs_ref, cats_ref,
                    win_ref, win_stage, win_sem, wout_ref, wout_stage, wout_sem, *, a):
    t = pl.program_id(1)
    n = _step_index()

    @pl.when(t == 0)
    def _():
        ext_ref[0:8, :] = jnp.zeros((8, D_CONV), F32)

    @pl.when(n == 0)
    def _():
        _cast_weight(win_hbm.at[a], win_ref, win_stage, win_sem)
        _cast_weight(wout_hbm.at[a], wout_ref, wout_stage, wout_sem)
        _a_sample_first(xs_ref, g_ref, win_ref, cw_ref, st_ref, sto_ref, zs_ref, cats_ref)

    row = zs_ref[pl.ds(n, 1), :]
    _merge_row(cats_ref, slice(D_CONV, D_CONV + D_ATT), n,
               _sample_mem_attn(row[:, :D_ATT], gq_ref[...], mc_ref[0], row[:, D_ATT:]))

    x = x_ref[...]
    xn = _rms(x, g_ref[...]).astype(BF16)

    def proj(lo, hi):
        return _dot(xn, win_ref[:, lo:hi])

    u = proj(D_CONV, 2 * D_CONV) * proj(0, D_CONV)
    ext_ref[8:8 + TM, :] = u
    y = (cw_ref[0:1, :] * ext_ref[6:6 + TM, :] + cw_ref[1:2, :] * ext_ref[7:7 + TM, :]
         + cw_ref[2:3, :] * u)
    ext_ref[0:8, :] = ext_ref[TM:TM + 8, :]
    cs_ref[...] = ext_ref[6:8, :]
    bg = proj(2 * D_CONV, 3 * D_CONV)
    g = proj(3 * D_CONV, 4 * D_CONV)
    cat_ref[:, 0:D_CONV] = (_silu(g) * bg * y).astype(BF16)
    mq = proj(4 * D_CONV, 4 * D_CONV + D_ATT)
    mg = proj(4 * D_CONV + D_ATT, 4 * D_CONV + 2 * D_ATT)

    def store(h, val):
        cat_ref[:, _hs(h, D_CONV)] = val.astype(BF16)

    _mem_attn(mq, mg, mkv_ref, gq_ref[...] * SCALE, store)
    y_ref[...] = x + _dot(cat_ref[...], wout_ref[...])

    @pl.when(_is_last_step())
    def _():
        ys_ref[...] = xs_ref[...] + _dot(cats_ref[...].astype(BF16), wout_ref[...])


def _sample_specs(nb, nt, layer):
    full = lambda shape: pl.BlockSpec(shape, lambda i, t: (0,) * len(shape))
    mem_cache = pl.BlockSpec((None, 1, MEM_LEN, KV_CHUNKS, HEAD_DIM),
                             lambda i, t: (layer, i * nt + t, 0, 0, 0))
    return full, mem_cache


def _a_layer(x, xs, norm_a, w_in, conv_w, mkv_b, state, mem_cache, a, layer, mem_q_norm, w_out):
    b, s, _ = x.shape
    nb = xs.shape[0]
    nt = s // TM
    assert nb == b * nt
    st_rows = nb * (CONV_W - 1) * D_CONV // HEAD_DIM
    full, mem_cache_spec = _sample_specs(nb, nt, layer)
    return pl.pallas_call(
        functools.partial(_a_layer_kernel, a=a),
        grid=(b, nt),
        in_specs=[
            pl.BlockSpec((None, TM, D_MODEL), lambda i, t: (i, t, 0)),
            _layer_spec((1, D_MODEL), a),
            pl.BlockSpec(memory_space=pl.ANY),
            _layer_spec((CONV_W, D_CONV), a),
            pl.BlockSpec((None, None, MEM_LEN, 2 * D_ATT), lambda i, t: (layer, i, 0, 0)),
            _layer_spec((1, HEAD_DIM), layer),
            pl.BlockSpec(memory_space=pl.ANY),
            full((nb, D_MODEL)),
            _layer_spec((st_rows, HEAD_DIM), a),
            mem_cache_spec,
        ],
        out_specs=[
            pl.BlockSpec((None, TM, D_MODEL), lambda i, t: (i, t, 0)),
            pl.BlockSpec((None, CONV_W - 1, D_CONV), lambda i, t: (i, 0, 0)),
            full((nb, D_MODEL)),
            full((st_rows, HEAD_DIM)),
        ],
        out_shape=[jax.ShapeDtypeStruct((b, s, D_MODEL), F32),
                   jax.ShapeDtypeStruct((b, CONV_W - 1, D_CONV), F32),
                   jax.ShapeDtypeStruct((nb, D_MODEL), F32),
                   jax.ShapeDtypeStruct((st_rows, HEAD_DIM), F32)],
        scratch_shapes=[pltpu.VMEM((TM + 8, D_CONV), F32),
                        pltpu.VMEM((TM, D_CONV + D_ATT), BF16),
                        pltpu.VMEM((nb, 2 * D_ATT), F32),
                        pltpu.VMEM((nb, D_CONV + D_ATT), F32)]
        + _weight_scratch(w_in.shape[1:]) + _weight_scratch(w_out.shape[1:]),
        compiler_params=_params(("arbitrary", "arbitrary")),
        name=f"a_layer_{layer}",
    )(x, norm_a, w_in, conv_w, mkv_b, mem_q_norm, w_out, xs, state, mem_cache)


def _residue_rows(slab_ref, split_ref, c, d):
    rows = slab_ref.shape[1]
    if d == SUB_STRIDE:
        return lambda r: slab_ref[c, pl.ds(r, rows // d, stride=d), :]
    assert d == SUB_STRIDE * SUB_STRIDE
    for r1 in range(SUB_STRIDE):
        split_ref[c, r1] = slab_ref[c, pl.ds(r1, rows // SUB_STRIDE, stride=SUB_STRIDE), :]
    return lambda r: split_ref[c, r % SUB_STRIDE,
                               pl.ds(r // SUB_STRIDE, rows // d, stride=SUB_STRIDE), :]


def _to_residue_major(dst_ref, slab_ref, split_ref, n_chunks, d):
    for c in range(n_chunks):
        get = _residue_rows(slab_ref, split_ref, c, d)
        for r in range(d):
            dst_ref[r, :, _hs(c)] = get(r).astype(dst_ref.dtype)


def _kv_kernel(x_ref, g_ref, w_hbm, kn_ref, xs_ref, kvb0_ref, kvb1_ref, kvb2_ref,
               st0_ref, st1_ref, st2_ref, kvn_ref, slab1_ref, slab2_ref, split_ref,
               w_ref, w_stage, w_sem):
    t = pl.program_id(1)
    last = pl.num_programs(1) - 1

    @pl.when(_step_index() == 0)
    def _():
        _cast_weight(w_hbm, w_ref, w_stage, w_sem)
        _kv_sample_rows(xs_ref, g_ref, w_ref, kn_ref, kvn_ref)

    xn = _rms(x_ref[...], g_ref[...]).astype(BF16)
    kvb_refs = (kvb0_ref, kvb1_ref, kvb2_ref)
    slabs = (None, slab1_ref, slab2_ref)
    for g in reversed(range(N_GROUPS)):
        kv = _dot(xn, w_ref[:, g * 2 * D_ATT:(g + 1) * 2 * D_ATT])
        chunks = [_rms(kv[:, _hs(h)], kn_ref[g:g + 1, :]) for h in range(N_HEADS)]
        chunks += [kv[:, _hs(h, D_ATT)] for h in range(N_HEADS)]
        d = DILATIONS[g]
        for c, chunk in enumerate(chunks):
            if d == 1:
                kvb_refs[g][0, :, _hs(c)] = chunk.astype(BF16)
            else:
                slabs[g][c] = chunk
            if g == 2:
                st2_ref[pl.ds(c, TM, stride=KV_CHUNKS), :] = chunk
        if d > 1:
            _to_residue_major(kvb_refs[g], slabs[g], split_ref, KV_CHUNKS, d)
        if g == 1:
            @pl.when(t == last)
            def _():
                for c, chunk in enumerate(chunks):
                    st1_ref[pl.ds(c, TM, stride=KV_CHUNKS), :] = chunk
        if g == 0:
            @pl.when(t == last)
            def _():
                for c, chunk in enumerate(chunks):
                    st0_ref[pl.ds(c, WINDOWS[0], stride=KV_CHUNKS), :] = chunk[TM - WINDOWS[0]:, :]


def _kv_shared(x, xs, kv_norm, w_kv, k_norm_dil):
    b, s, _ = x.shape
    nb = xs.shape[0]
    kvn_rows = nb * N_GROUPS * KV_CHUNKS
    assert WINDOWS[1] == TM and WINDOWS[2] == s and WINDOWS[0] <= TM
    first = lambda i, t: (i, 0, 0)
    res_spec = lambda d: pl.BlockSpec((None, d, TM // d, 2 * D_ATT), lambda i, t: (i, 0, t, 0))
    res_sds = lambda d: jax.ShapeDtypeStruct((b, d, s // d, 2 * D_ATT), BF16)
    st_sds = lambda rows: jax.ShapeDtypeStruct((b, rows * KV_CHUNKS, HEAD_DIM), F32)
    return pl.pallas_call(
        _kv_kernel,
        grid=(b, s // TM),
        in_specs=[
            pl.BlockSpec((None, TM, D_MODEL), lambda i, t: (i, t, 0)),
            _const_spec((1, D_MODEL)),
            pl.BlockSpec(memory_space=pl.ANY),
            _const_spec((N_GROUPS, HEAD_DIM)),
            _const_spec((nb, D_MODEL)),
        ],
        out_specs=[
            res_spec(DILATIONS[0]), res_spec(DILATIONS[1]), res_spec(DILATIONS[2]),
            pl.BlockSpec((None, WINDOWS[0] * KV_CHUNKS, HEAD_DIM), first),
            pl.BlockSpec((None, WINDOWS[1] * KV_CHUNKS, HEAD_DIM), first),
            pl.BlockSpec((None, TM * KV_CHUNKS, HEAD_DIM), lambda i, t: (i, t, 0)),
            pl.BlockSpec((kvn_rows, HEAD_DIM), lambda i, t: (0, 0)),
        ],
        out_shape=[res_sds(DILATIONS[0]), res_sds(DILATIONS[1]), res_sds(DILATIONS[2]),
                   st_sds(WINDOWS[0]), st_sds(WINDOWS[1]), st_sds(s),
                   jax.ShapeDtypeStruct((kvn_rows, HEAD_DIM), F32)],
        scratch_shapes=[pltpu.VMEM((KV_CHUNKS, TM, HEAD_DIM), F32),
                        pltpu.VMEM((KV_CHUNKS, TM, HEAD_DIM), F32),
                        pltpu.VMEM((KV_CHUNKS, SUB_STRIDE, TM // SUB_STRIDE, HEAD_DIM), F32)]
        + _weight_scratch(w_kv.shape),
        compiler_params=_params(("arbitrary", "arbitrary")),
        name="kv_prompt",
    )(x, kv_norm.reshape(1, D_MODEL), w_kv, k_norm_dil, xs)


def _band_scores(qs, ks, masks):
    return jnp.concatenate([jnp.where(mk, _dot_nt(q, k), NEG) for q, k, mk in zip(qs, ks, masks)],
                           axis=0)


def _band_finish(s, vs, nq):
    m = jnp.max(s, axis=-1, keepdims=True)
    pb = jnp.exp(s - m).astype(BF16)
    res = []
    for i, v in enumerate(vs):
        rows = slice(i * nq, (i + 1) * nq)
        acc, l = _pv_and_rowsum(pb[rows], v)
        res.append((acc * (1.0 / l), m[rows] + jnp.log(l)))
    return res


def _b_layer_kernel(x_ref, g_ref, win_ref, qn_ref, mkv_ref, gq_ref, wout_ref,
                    kv0_ref, kv0p_ref, kv1_ref, kv1p_ref, kv2_ref,
                    xs_ref, kvn_ref, c0_ref, c1_ref, c2_ref, mc_ref,
                    y_ref, ys_ref,
                    qslab1_ref, qslab2_ref, qsplit_ref, oslab_ref, lslab_ref, cat_ref,
                    zs_ref, cats_ref):
    t = pl.program_id(1)
    n = _step_index()

    @pl.when(n == 0)
    def _():
        zs_ref[...] = _dot(_rms(xs_ref[...], g_ref[...]).astype(BF16), win_ref[...])
        cats_ref[...] = jnp.zeros(cats_ref.shape, F32)

    _merge_row(cats_ref, slice(0, 2 * D_ATT), n,
               _b_sample_row(zs_ref[pl.ds(n, 1), :], kvn_ref[n], qn_ref, gq_ref[...],
                             (c0_ref[0], c1_ref[0], c2_ref[0]), mc_ref[0]))

    x = x_ref[...]
    xn = _rms(x, g_ref[...]).astype(BF16)

    def proj(lo, hi):
        return _dot(xn, win_ref[:, lo:hi])

    qi = lax.broadcasted_iota(jnp.int32, (BAND, 2 * BAND), 0)
    kj = lax.broadcasted_iota(jnp.int32, (BAND, 2 * BAND), 1)
    mask_rest = (kj >= qi) & (kj <= qi + BAND)
    mask_edge = (kj >= jnp.maximum(qi, jnp.where(t == 0, BAND, 0))) & (kj <= qi + BAND)

    def finish(g, h, rows, out, lse):
        oslab_ref[g, h, rows, :] = out
        lslab_ref[g, h, rows, :] = lse

    def normed_heads(g):
        qd = proj(g * D_ATT, (g + 1) * D_ATT)
        gain = qn_ref[g:g + 1, :] * SCALE
        return [_rms(qd[:, _hs(h)], gain) for h in range(N_HEADS)]

    n_sub = TM // BAND
    qh = [q.astype(BF16) for q in normed_heads(0)]
    units0 = [(i, h) for i in range(n_sub) for h in range(N_HEADS)]

    def keys0(i, cols):
        if i == 0:
            return jnp.concatenate([kv0p_ref[0, :, cols], kv0_ref[0, 0:BAND, cols]], axis=0)
        return kv0_ref[0, (i - 1) * BAND:(i + 1) * BAND, cols]

    d1 = DILATIONS[1]
    for h, q in enumerate(normed_heads(1)):
        qslab1_ref[h] = q
    get_q1 = [_residue_rows(qslab1_ref, None, h, d1) for h in range(N_HEADS)]
    units1 = [(r, h) for r in range(d1) for h in range(N_HEADS)]

    def keys1(r, cols):
        return jnp.concatenate([kv1p_ref[r, :, cols], kv1_ref[r, :, cols]], axis=0)

    d2 = DILATIONS[2]
    nq = TM // d2
    for h, q in enumerate(normed_heads(2)):
        qslab2_ref[h] = q
    get_q2 = [_residue_rows(qslab2_ref, qsplit_ref, h, d2) for h in range(N_HEADS)]
    units2 = [(r, h) for r in range(d2) for h in range(N_HEADS)]
    qi2 = lax.broadcasted_iota(jnp.int32, (nq, BAND), 0)
    kj2 = lax.broadcasted_iota(jnp.int32, (nq, BAND), 1)
    mask2 = kj2 <= qi2 + t * nq

    s0 = _band_scores([qh[h][i * BAND:(i + 1) * BAND] for i, h in units0],
                      [keys0(i, _hs(h)) for i, h in units0],
                      [mask_edge if i == 0 else mask_rest for i, h in units0])
    s1 = _band_scores([get_q1[h](r).astype(BF16) for r, h in units1],
                      [keys1(r, _hs(h)) for r, h in units1], [mask_edge] * len(units1))
    s2 = _band_scores([get_q2[h](r).astype(BF16) for r, h in units2],
                      [kv2_ref[r, :, _hs(h)] for r, h in units2], [mask2] * len(units2))
    res = _band_finish(s0, [keys0(i, _hs(h, D_ATT)) for i, h in units0], BAND)
    for (i, h), (out, lse) in zip(units0, res):
        finish(0, h, pl.ds(i * BAND, BAND), out, lse)
    res = _band_finish(s1, [keys1(r, _hs(h, D_ATT)) for r, h in units1], BAND)
    for (r, h), (out, lse) in zip(units1, res):
        finish(1, h, pl.ds(r, BAND, stride=d1), out, lse)
    res = _band_finish(s2, [kv2_ref[r, :, _hs(h, D_ATT)] for r, h in units2], nq)
    for (r, h), (out, lse) in zip(units2, res):
        finish(2, h, pl.ds(r, nq, stride=d2), out, lse)

    base = N_GROUPS * D_ATT
    gd = proj(base, base + D_ATT)
    for h in range(N_HEADS):
        l0, l1, l2 = lslab_ref[0, h], lslab_ref[1, h], lslab_ref[2, h]
        m = jnp.maximum(jnp.maximum(l0, l1), l2)
        e0, e1, e2 = jnp.exp(l0 - m), jnp.exp(l1 - m), jnp.exp(l2 - m)
        dil = ((oslab_ref[0, h] * e0 + oslab_ref[1, h] * e1 + oslab_ref[2, h] * e2)
               * (1.0 / (e0 + e1 + e2)))
        cat_ref[:, _hs(h)] = (dil * _silu(gd[:, _hs(h)])).astype(BF16)
    mq = proj(base + D_ATT, base + 2 * D_ATT)
    mg = proj(base + 2 * D_ATT, base + 3 * D_ATT)

    def store(h, val):
        cat_ref[:, _hs(h, D_ATT)] = val.astype(BF16)

    _mem_attn(mq, mg, mkv_ref, gq_ref[...] * SCALE, store)
    y_ref[...] = x + _dot(cat_ref[...], wout_ref[...])

    @pl.when(_is_last_step())
    def _():
        ys_ref[...] = xs_ref[...] + _dot(cats_ref[...].astype(BF16), wout_ref[...])


def _b_layer(x, xs, norm_b, w_in, q_norm, mkv_b, kvbs, kv_new, caches, mem_cache, bl, layer,
             mem_q_norm, w_out):
    b, s, _ = x.shape
    nb = xs.shape[0]
    nt = s // TM
    assert nb == b * nt
    full, mem_cache_spec = _sample_specs(nb, nt, layer)
    cache_spec = pl.BlockSpec((1, BAND, None, KV_CHUNKS, HEAD_DIM),
                              lambda i, t: (i * nt + t, 0, 0, 0, 0))
    assert TM // BAND == DILATIONS[1] and s // DILATIONS[2] == BAND
    tile = lambda i, t: (i, t, 0)
    sub = TM // BAND
    kv_w = 2 * D_ATT
    return pl.pallas_call(
        _b_layer_kernel,
        grid=(b, s // TM),
        in_specs=[
            pl.BlockSpec((None, TM, D_MODEL), tile),
            _layer_spec((1, D_MODEL), bl),
            _layer_spec(w_in.shape[1:], bl),
            _layer_spec((N_GROUPS, HEAD_DIM), bl),
            pl.BlockSpec((None, None, MEM_LEN, kv_w), lambda i, t: (layer, i, 0, 0)),
            _layer_spec((1, HEAD_DIM), layer),
            _layer_spec(w_out.shape[1:], bl),
            pl.BlockSpec((None, 1, TM, kv_w), lambda i, t: (i, 0, t, 0)),
            pl.BlockSpec((None, 1, BAND, kv_w),
                         lambda i, t: (i, 0, jnp.maximum(t * sub - 1, 0), 0)),
            pl.BlockSpec((None, DILATIONS[1], BAND, kv_w), lambda i, t: (i, 0, t, 0)),
            pl.BlockSpec((None, DILATIONS[1], BAND, kv_w),
                         lambda i, t: (i, 0, jnp.maximum(t - 1, 0), 0)),
            pl.BlockSpec((None, DILATIONS[2], BAND, kv_w), lambda i, t: (i, 0, 0, 0)),
            full((nb, D_MODEL)),
            full(kv_new.shape),
            cache_spec, cache_spec, cache_spec,
            mem_cache_spec,
        ],
        out_specs=[pl.BlockSpec((None, TM, D_MODEL), tile), full((nb, D_MODEL))],
        out_shape=[jax.ShapeDtypeStruct((b, s, D_MODEL), F32),
                   jax.ShapeDtypeStruct((nb, D_MODEL), F32)],
        scratch_shapes=[pltpu.VMEM((N_HEADS, TM, HEAD_DIM), F32),
                        pltpu.VMEM((N_HEADS, TM, HEAD_DIM), F32),
                        pltpu.VMEM((N_HEADS, SUB_STRIDE, TM // SUB_STRIDE, HEAD_DIM), F32),
                        pltpu.VMEM((N_GROUPS, N_HEADS, TM, HEAD_DIM), F32),
                        pltpu.VMEM((N_GROUPS, N_HEADS, TM, HEAD_DIM), F32),
                        pltpu.VMEM((TM, 2 * D_ATT), BF16),
                        pltpu.VMEM((nb, w_in.shape[2]), F32),
                        pltpu.VMEM((nb, 2 * D_ATT), F32)],
        compiler_params=_params(("arbitrary", "arbitrary")),
        name=f"b_layer_{layer}",
    )(x, norm_b, w_in, q_norm, mkv_b, mem_q_norm, w_out,
      kvbs[0], kvbs[0], kvbs[1], kvbs[1], kvbs[2],
      xs, kv_new, *caches, mem_cache)


def _swap_halves(x, axis):
    return pltpu.roll(x, N_HEADS, axis=axis)


def _head_rows(tile):
    return jnp.concatenate([tile[N_HEADS + h:N_HEADS + h + 1, :] for h in range(N_HEADS)],
                           axis=-1)


def _low_tile(row):
    return jnp.concatenate([jnp.zeros((N_HEADS, HEAD_DIM), F32)]
                           + [row[:, _hs(h)] for h in range(N_HEADS)], axis=0)


def _q_tile(row, g):
    return jnp.concatenate([_rms(row[:, _hs(h)], g) * SCALE for h in range(N_HEADS)]
                           + [jnp.zeros((N_HEADS, HEAD_DIM), F32)], axis=0)


def _tile_attend(qt, kv):
    s = jnp.sum(kv * qt[None], axis=-1, keepdims=True)
    m = jnp.max(s, axis=0)
    p = jnp.exp(s - m[None])
    l = jnp.sum(p, axis=0)
    acc = jnp.sum(_swap_halves(jnp.broadcast_to(p, kv.shape), 1) * kv, axis=0)
    out = acc / _swap_halves(jnp.broadcast_to(l, acc.shape), 0)
    return out, m + jnp.log(l)


def _sample_mem_attn(q_row, gq, mkv, mg_row):
    out, _ = _tile_attend(_q_tile(q_row, gq), mkv)
    return _head_rows(out * _low_tile(_silu(mg_row)))


def _gather_chunks(ref, n_rows, n_chunks, offset=0):
    total = ref.shape[0] // n_rows
    return jnp.concatenate([ref[pl.ds(offset + c, n_rows, stride=total), :]
                            for c in range(n_chunks)], axis=-1)


def _merge_row(ref, cols, n, row):
    tile = pl.ds(pl.multiple_of((n // SAMPLE_GB) * SAMPLE_GB, SAMPLE_GB), SAMPLE_GB)
    sub = lax.broadcasted_iota(jnp.int32, (SAMPLE_GB, row.shape[1]), 0)
    ref[tile, cols] = jnp.where(sub == n % SAMPLE_GB, row, ref[tile, cols])


def _a_sample_first(x_ref, g_ref, win_ref, cw_ref, st_ref, sto_ref, zm_ref, cat_ref):
    nb = x_ref.shape[0]
    n_st = D_CONV // HEAD_DIM
    xn = _rms(x_ref[...], g_ref[...]).astype(BF16)
    z = _dot(xn, win_ref[...])
    u = z[:, D_CONV:2 * D_CONV] * z[:, 0:D_CONV]
    s0 = _gather_chunks(st_ref, nb, n_st)
    s1 = _gather_chunks(st_ref, nb, n_st, n_st)
    y = cw_ref[0:1, :] * s0 + cw_ref[1:2, :] * s1 + cw_ref[2:3, :] * u
    for c in range(n_st):
        sto_ref[pl.ds(c, nb, stride=2 * n_st), :] = s1[:, _hs(c)]
        sto_ref[pl.ds(n_st + c, nb, stride=2 * n_st), :] = u[:, _hs(c)]
    cat_ref[:, 0:D_CONV] = _silu(z[:, 3 * D_CONV:4 * D_CONV]) * z[:, 2 * D_CONV:3 * D_CONV] * y
    cat_ref[:, D_CONV:] = jnp.zeros((nb, D_ATT), F32)
    zm_ref[...] = z[:, 4 * D_CONV:]


def _kv_sample_rows(x_ref, g_ref, w_ref, kn_ref, o_ref):
    nb = x_ref.shape[0]
    xn = _rms(x_ref[...], g_ref[...]).astype(BF16)
    kv = _dot(xn, w_ref[...])
    n_chunks = N_GROUPS * KV_CHUNKS
    for g in range(N_GROUPS):
        for c in range(KV_CHUNKS):
            chunk = kv[:, _hs(c, g * 2 * D_ATT)]
            if c < N_HEADS:
                chunk = _rms(chunk, kn_ref[g:g + 1, :])
            o_ref[pl.ds(g * KV_CHUNKS + c, nb, stride=n_chunks), :] = chunk


def _b_sample_row(row, kv_new, qn_ref, gq, caches, mkv):
    base = N_GROUPS * D_ATT
    outs, lses = [], []
    for g in range(N_GROUPS):
        qt = _q_tile(row[:, g * D_ATT:(g + 1) * D_ATT], qn_ref[g:g + 1, :])
        out, lse = _tile_attend(qt, jnp.concatenate([caches[g], kv_new[g][None]], axis=0))
        outs.append(out)
        lses.append(lse)
    m = jnp.maximum(jnp.maximum(lses[0], lses[1]), lses[2])
    es = [jnp.exp(l - m) for l in lses]
    den = es[0] + es[1] + es[2]
    ws = [_swap_halves(jnp.broadcast_to(e / den, (KV_CHUNKS, HEAD_DIM)), 0) for e in es]
    dil = outs[0] * ws[0] + outs[1] * ws[1] + outs[2] * ws[2]
    dil_row = _head_rows(dil * _low_tile(_silu(row[:, base:base + D_ATT])))
    mem_row = _sample_mem_attn(row[:, base + D_ATT:base + 2 * D_ATT], gq, mkv,
                               row[:, base + 2 * D_ATT:])
    return jnp.concatenate([dil_row, mem_row], axis=-1)


def kernel(x_prompt, x_sample, cache_mem_kv, state_conv, cache_dil0_kv, cache_dil1_kv, cache_dil2_kv, mem_prompt, norm_a, w_in_a, conv_w_a, w_out_a, norm_b, w_in_b, q_norm_b, w_out_b, kv_norm, w_kv, k_norm_dil, mem_norm, w_mem_kv, mem_k_norm, mem_q_norm):
    n_a = w_in_a.shape[0]
    n_b = w_in_b.shape[0]
    depth = n_a + n_b
    b, s, _ = x_prompt.shape
    nb = x_sample.shape[0]
    assert x_sample.shape[1] == 1 and s % TM == 0 and nb % SAMPLE_GB == 0
    kv_dims = (2, N_HEADS, HEAD_DIM)

    w_in_b16, w_out_b16 = w_in_b.astype(BF16), w_out_b.astype(BF16)
    norm_a3 = norm_a.reshape(n_a, 1, D_MODEL)
    norm_b3 = norm_b.reshape(n_b, 1, D_MODEL)
    mem_q_norm3 = mem_q_norm.reshape(depth, 1, HEAD_DIM)

    mem_kv_f, mem_kv_b = _memkv_prompt(mem_prompt, mem_norm, w_mem_kv, mem_k_norm)
    mem_kv_b = mem_kv_b.reshape(depth, b, MEM_LEN, 2 * D_ATT)
    mem_cache = cache_mem_kv.reshape(depth, nb, MEM_LEN, KV_CHUNKS, HEAD_DIM)
    caches = []
    for g, cache in enumerate((cache_dil0_kv, cache_dil1_kv, cache_dil2_kv)):
        assert cache.shape[1] == WINDOWS[g]
        caches.append(cache.reshape(nb, BAND, DILATIONS[g], KV_CHUNKS, HEAD_DIM))
    state = state_conv.reshape(n_a, -1, HEAD_DIM)

    xp = x_prompt
    xs = x_sample.reshape(nb, D_MODEL)
    conv_p, conv_s = [], []
    for a in range(n_a):
        xp, cp, xs, cs = _a_layer(xp, xs, norm_a3, w_in_a, conv_w_a, mem_kv_b, state, mem_cache,
                                  a, a, mem_q_norm3, w_out_a)
        conv_p.append(cp)
        conv_s.append(cs.reshape(nb, CONV_W - 1, D_CONV))

    kvb0, kvb1, kvb2, st0, st1, st2, kv_new = _kv_shared(xp, xs, kv_norm, w_kv, k_norm_dil)
    kvbs = (kvb0, kvb1, kvb2)
    kv_new = kv_new.reshape(nb, N_GROUPS, KV_CHUNKS, HEAD_DIM)

    for bl in range(n_b):
        layer = n_a + bl
        xp, xs = _b_layer(xp, xs, norm_b3, w_in_b16, q_norm_b, mem_kv_b, kvbs, kv_new, caches,
                          mem_cache, bl, layer, mem_q_norm3, w_out_b16)

    dil_s = [kv_new[:, g].reshape(nb, 1, *kv_dims) for g in range(N_GROUPS)]
    return (xp, xs.reshape(nb, 1, D_MODEL),
            mem_kv_f.reshape(depth, b, MEM_LEN, *kv_dims),
            jnp.stack(conv_p), jnp.stack(conv_s),
            st0.reshape(b, WINDOWS[0], *kv_dims), st1.reshape(b, WINDOWS[1], *kv_dims),
            st2.reshape(b, s, *kv_dims), *dil_s)
```

```python
import functools

import jax
import jax.numpy as jnp
from jax import lax
from jax.experimental import pallas as pl
from jax.experimental.pallas import tpu as pltpu

F32 = jnp.float32
BF16 = jnp.bfloat16

D_MODEL = 1024
HEAD_DIM = 128
N_HEADS = 4
D_ATT = N_HEADS * HEAD_DIM
KV_CHUNKS = 2 * N_HEADS
D_CONV = D_MODEL
CONV_W = 3
N_GROUPS = 3
DILATIONS = (1, 4, 16)
WINDOWS = (128, 512, 2048)
BAND = 128
SUB_STRIDE = 4
MEM_LEN = 256
EPS = 1e-6
NEG = -1e30
SCALE = HEAD_DIM ** -0.5
QK_SCALE = SCALE * 1.4426950408889634

TM = 512
SAMPLE_GB = 8
W_STAGE_ROWS = 256
VMEM_LIMIT = 56 * 1024 * 1024


def _rms(x, g):
    return x * lax.rsqrt(jnp.mean(x * x, axis=-1, keepdims=True) + EPS) * g


def _silu(x):
    return x * (1.0 / (1.0 + jnp.exp(-x)))


def _dot(a, b):
    return jnp.dot(a, b, preferred_element_type=F32)


def _dot_nt(a, b):
    return lax.dot_general(a, b, (((1,), (1,)), ((), ())), preferred_element_type=F32)


def _hs(h, base=0):
    return slice(base + h * HEAD_DIM, base + (h + 1) * HEAD_DIM)


def _params(sem):
    return pltpu.CompilerParams(dimension_semantics=sem, vmem_limit_bytes=VMEM_LIMIT)


def _fixed_spec(block, index):
    return pl.BlockSpec(block, lambda *_: index, pipeline_mode=pl.Buffered(1))


def _const_spec(shape):
    return _fixed_spec(shape, (0,) * len(shape))


def _layer_spec(shape, layer):
    return _fixed_spec((None,) + tuple(shape), (layer,) + (0,) * len(shape))


def _cast_weight(w_hbm, dst_ref, stage_ref, sem_ref):
    rows = stage_ref.shape[1]
    n_chunks = w_hbm.shape[0] // rows

    def copy(c, slot):
        src = w_hbm.at[pl.ds(pl.multiple_of(c * rows, rows), rows)]
        return pltpu.make_async_copy(src, stage_ref.at[slot], sem_ref.at[slot])

    copy(0, 0).start()

    def body(c, carry):
        slot = c % 2

        @pl.when(c + 1 < n_chunks)
        def _():
            copy(c + 1, 1 - slot).start()

        copy(c, slot).wait()
        dst_ref[pl.ds(pl.multiple_of(c * rows, rows), rows), :] = stage_ref[slot].astype(BF16)
        return carry

    lax.fori_loop(0, n_chunks, body, 0)


def _weight_scratch(shape):
    k, n = shape
    assert k % W_STAGE_ROWS == 0
    return [pltpu.VMEM((k, n), BF16), pltpu.VMEM((2, W_STAGE_ROWS, n), F32),
            pltpu.SemaphoreType.DMA((2,))]


def _memkv_kernel(mem_ref, g_ref, w_ref, kg_ref, of_ref, ob_ref):
    rt = mem_ref.shape[0]
    xn = _rms(mem_ref[...], g_ref[...]).astype(BF16)
    kv = _dot(xn, w_ref[...].astype(BF16))
    for c in range(KV_CHUNKS):
        chunk = kv[:, _hs(c)]
        if c < N_HEADS:
            chunk = _rms(chunk, kg_ref[...])
        of_ref[pl.ds(c, rt, stride=KV_CHUNKS), :] = chunk
        ob_ref[:, _hs(c)] = chunk.astype(BF16)


def _memkv_prompt(mem, mem_norm, w_mem_kv, mem_k_norm):
    depth = w_mem_kv.shape[0]
    rows = mem.shape[0] * mem.shape[1]
    rt = 1024
    return pl.pallas_call(
        _memkv_kernel,
        grid=(depth, rows // rt),
        in_specs=[
            pl.BlockSpec((rt, D_MODEL), lambda l, i: (i, 0)),
            pl.BlockSpec((None, 1, D_MODEL), lambda l, i: (l, 0, 0)),
            pl.BlockSpec((None, D_MODEL, 2 * D_ATT), lambda l, i: (l, 0, 0)),
            pl.BlockSpec((None, 1, HEAD_DIM), lambda l, i: (l, 0, 0)),
        ],
        out_specs=[
            pl.BlockSpec((None, rt * KV_CHUNKS, HEAD_DIM), lambda l, i: (l, i, 0)),
            pl.BlockSpec((None, rt, 2 * D_ATT), lambda l, i: (l, i, 0)),
        ],
        out_shape=[jax.ShapeDtypeStruct((depth, rows * KV_CHUNKS, HEAD_DIM), F32),
                   jax.ShapeDtypeStruct((depth, rows, 2 * D_ATT), BF16)],
        compiler_params=_params(("parallel", "parallel")),
        name="memkv_prompt",
    )(mem.reshape(rows, D_MODEL), mem_norm.reshape(depth, 1, D_MODEL), w_mem_kv,
      mem_k_norm.reshape(depth, 1, HEAD_DIM))


def _pv_and_rowsum(p, v):
    ones = jnp.ones((v.shape[0], HEAD_DIM), BF16)
    both = _dot(p, jnp.concatenate([v, ones], axis=1))
    return both[:, :HEAD_DIM], both[:, HEAD_DIM:]


def _mem_attn(mq, mg, mkv_ref, gq, store):
    for h in range(N_HEADS):
        q = _rms(mq[:, _hs(h)], gq).astype(BF16)
        s = _dot_nt(q, mkv_ref[:, _hs(h)])
        m = jnp.max(s, axis=-1, keepdims=True)
        acc, l = _pv_and_rowsum(jnp.exp2(s - m).astype(BF16), mkv_ref[:, _hs(h, D_ATT)])
        store(h, acc / l * _silu(mg[:, _hs(h)]))


def _step_index():
    return pl.program_id(0) * pl.num_programs(1) + pl.program_id(1)


def _is_last_step():
    return _step_index() == pl.num_programs(0) * pl.num_programs(1) - 1


def _a_layer_kernel(x_ref, g_ref, win_hbm, cw_ref, mkv_ref, gq_ref, wout_hbm,
                    xs_ref, st_ref, mc_ref,
                    y_ref, cs_ref, ys_ref, sto_ref,
                    ext_ref, cat_ref, zs_ref, cats_ref,
                    win_ref, win_stage, win_sem, wout_ref, wout_stage, wout_sem, *, a):
    t = pl.program_id(1)
    n = _step_index()

    @pl.when(t == 0)
    def _():
        ext_ref[0:8, :] = jnp.zeros((8, D_CONV), F32)

    @pl.when(n == 0)
    def _():
        _cast_weight(win_hbm.at[a], win_ref, win_stage, win_sem)
        _cast_weight(wout_hbm.at[a], wout_ref, wout_stage, wout_sem)
        _a_sample_first(xs_ref, g_ref, win_ref, cw_ref, st_ref, sto_ref, zs_ref, cats_ref)

    row = zs_ref[pl.ds(n, 1), :]
    _merge_row(cats_ref, slice(D_CONV, D_CONV + D_ATT), n,
               _sample_mem_attn(row[:, :D_ATT], gq_ref[...], mc_ref[0], row[:, D_ATT:]))

    x = x_ref[...]
    xn = _rms(x, g_ref[...]).astype(BF16)

    def proj(lo, hi):
        return _dot(xn, win_ref[:, lo:hi])

    u = proj(D_CONV, 2 * D_CONV) * proj(0, D_CONV)
    ext_ref[8:8 + TM, :] = u
    y = (cw_ref[0:1, :] * ext_ref[6:6 + TM, :] + cw_ref[1:2, :] * ext_ref[7:7 + TM, :]
         + cw_ref[2:3, :] * u)
    ext_ref[0:8, :] = ext_ref[TM:TM + 8, :]
    cs_ref[...] = ext_ref[6:8, :]
    bg = proj(2 * D_CONV, 3 * D_CONV)
    g = proj(3 * D_CONV, 4 * D_CONV)
    cat_ref[:, 0:D_CONV] = (_silu(g) * bg * y).astype(BF16)
    mq = proj(4 * D_CONV, 4 * D_CONV + D_ATT)
    mg = proj(4 * D_CONV + D_ATT, 4 * D_CONV + 2 * D_ATT)

    def store(h, val):
        cat_ref[:, _hs(h, D_CONV)] = val.astype(BF16)

    _mem_attn(mq, mg, mkv_ref, gq_ref[...] * QK_SCALE, store)
    y_ref[...] = x + _dot(cat_ref[...], wout_ref[...])

    @pl.when(_is_last_step())
    def _():
        ys_ref[...] = xs_ref[...] + _dot(cats_ref[...].astype(BF16), wout_ref[...])


def _sample_specs(nb, nt, layer):
    full = lambda shape: pl.BlockSpec(shape, lambda i, t: (0,) * len(shape))
    mem_cache = pl.BlockSpec((None, 1, MEM_LEN, KV_CHUNKS, HEAD_DIM),
                             lambda i, t: (layer, i * nt + t, 0, 0, 0))
    return full, mem_cache


def _a_layer(x, xs, norm_a, w_in, conv_w, mkv_b, state, mem_cache, a, layer, mem_q_norm, w_out):
    b, s, _ = x.shape
    nb = xs.shape[0]
    nt = s // TM
    assert nb == b * nt
    st_rows = nb * (CONV_W - 1) * D_CONV // HEAD_DIM
    full, mem_cache_spec = _sample_specs(nb, nt, layer)
    return pl.pallas_call(
        functools.partial(_a_layer_kernel, a=a),
        grid=(b, nt),
        in_specs=[
            pl.BlockSpec((None, TM, D_MODEL), lambda i, t: (i, t, 0)),
            _layer_spec((1, D_MODEL), a),
            pl.BlockSpec(memory_space=pl.ANY),
            _layer_spec((CONV_W, D_CONV), a),
            pl.BlockSpec((None, None, MEM_LEN, 2 * D_ATT), lambda i, t: (layer, i, 0, 0)),
            _layer_spec((1, HEAD_DIM), layer),
            pl.BlockSpec(memory_space=pl.ANY),
            full((nb, D_MODEL)),
            _layer_spec((st_rows, HEAD_DIM), a),
            mem_cache_spec,
        ],
        out_specs=[
            pl.BlockSpec((None, TM, D_MODEL), lambda i, t: (i, t, 0)),
            pl.BlockSpec((None, CONV_W - 1, D_CONV), lambda i, t: (i, 0, 0)),
            full((nb, D_MODEL)),
            full((st_rows, HEAD_DIM)),
        ],
        out_shape=[jax.ShapeDtypeStruct((b, s, D_MODEL), F32),
                   jax.ShapeDtypeStruct((b, CONV_W - 1, D_CONV), F32),
                   jax.ShapeDtypeStruct((nb, D_MODEL), F32),
                   jax.ShapeDtypeStruct((st_rows, HEAD_DIM), F32)],
        scratch_shapes=[pltpu.VMEM((TM + 8, D_CONV), F32),
                        pltpu.VMEM((TM, D_CONV + D_ATT), BF16),
                        pltpu.VMEM((nb, 2 * D_ATT), F32),
                        pltpu.VMEM((nb, D_CONV + D_ATT), F32)]
        + _weight_scratch(w_in.shape[1:]) + _weight_scratch(w_out.shape[1:]),
        compiler_params=_params(("arbitrary", "arbitrary")),
        name=f"a_layer_{layer}",
    )(x, norm_a, w_in, conv_w, mkv_b, mem_q_norm, w_out, xs, state, mem_cache)


def _residue_rows(slab_ref, split_ref, c, d):
    rows = slab_ref.shape[1]
    if d == SUB_STRIDE:
        return lambda r: slab_ref[c, pl.ds(r, rows // d, stride=d), :]
    assert d == SUB_STRIDE * SUB_STRIDE
    for r1 in range(SUB_STRIDE):
        split_ref[c, r1] = slab_ref[c, pl.ds(r1, rows // SUB_STRIDE, stride=SUB_STRIDE), :]
    return lambda r: split_ref[c, r % SUB_STRIDE,
                               pl.ds(r // SUB_STRIDE, rows // d, stride=SUB_STRIDE), :]


def _to_residue_major(dst_ref, slab_ref, split_ref, n_chunks, d):
    for c in range(n_chunks):
        get = _residue_rows(slab_ref, split_ref, c, d)
        for r in range(d):
            dst_ref[r, :, _hs(c)] = get(r).astype(dst_ref.dtype)


def _kv_kernel(x_ref, g_ref, w_hbm, kn_ref, xs_ref, kvb0_ref, kvb1_ref, kvb2_ref,
               st0_ref, st1_ref, st2_ref, kvn_ref, slab1_ref, slab2_ref, split_ref,
               w_ref, w_stage, w_sem):
    t = pl.program_id(1)
    last = pl.num_programs(1) - 1

    @pl.when(_step_index() == 0)
    def _():
        _cast_weight(w_hbm, w_ref, w_stage, w_sem)
        _kv_sample_rows(xs_ref, g_ref, w_ref, kn_ref, kvn_ref)

    xn = _rms(x_ref[...], g_ref[...]).astype(BF16)
    kvb_refs = (kvb0_ref, kvb1_ref, kvb2_ref)
    slabs = (None, slab1_ref, slab2_ref)
    for g in reversed(range(N_GROUPS)):
        kv = _dot(xn, w_ref[:, g * 2 * D_ATT:(g + 1) * 2 * D_ATT])
        chunks = [_rms(kv[:, _hs(h)], kn_ref[g:g + 1, :]) for h in range(N_HEADS)]
        chunks += [kv[:, _hs(h, D_ATT)] for h in range(N_HEADS)]
        d = DILATIONS[g]
        for c, chunk in enumerate(chunks):
            if d == 1:
                kvb_refs[g][0, :, _hs(c)] = chunk.astype(BF16)
            else:
                slabs[g][c] = chunk
            if g == 2:
                st2_ref[pl.ds(c, TM, stride=KV_CHUNKS), :] = chunk
        if d > 1:
            _to_residue_major(kvb_refs[g], slabs[g], split_ref, KV_CHUNKS, d)
        if g == 1:
            @pl.when(t == last)
            def _():
                for c, chunk in enumerate(chunks):
                    st1_ref[pl.ds(c, TM, stride=KV_CHUNKS), :] = chunk
        if g == 0:
            @pl.when(t == last)
            def _():
                for c, chunk in enumerate(chunks):
                    st0_ref[pl.ds(c, WINDOWS[0], stride=KV_CHUNKS), :] = chunk[TM - WINDOWS[0]:, :]


def _kv_shared(x, xs, kv_norm, w_kv, k_norm_dil):
    b, s, _ = x.shape
    nb = xs.shape[0]
    kvn_rows = nb * N_GROUPS * KV_CHUNKS
    assert WINDOWS[1] == TM and WINDOWS[2] == s and WINDOWS[0] <= TM
    first = lambda i, t: (i, 0, 0)
    res_spec = lambda d: pl.BlockSpec((None, d, TM // d, 2 * D_ATT), lambda i, t: (i, 0, t, 0))
    res_sds = lambda d: jax.ShapeDtypeStruct((b, d, s // d, 2 * D_ATT), BF16)
    st_sds = lambda rows: jax.ShapeDtypeStruct((b, rows * KV_CHUNKS, HEAD_DIM), F32)
    return pl.pallas_call(
        _kv_kernel,
        grid=(b, s // TM),
        in_specs=[
            pl.BlockSpec((None, TM, D_MODEL), lambda i, t: (i, t, 0)),
            _const_spec((1, D_MODEL)),
            pl.BlockSpec(memory_space=pl.ANY),
            _const_spec((N_GROUPS, HEAD_DIM)),
            _const_spec((nb, D_MODEL)),
        ],
        out_specs=[
            res_spec(DILATIONS[0]), res_spec(DILATIONS[1]), res_spec(DILATIONS[2]),
            pl.BlockSpec((None, WINDOWS[0] * KV_CHUNKS, HEAD_DIM), first),
            pl.BlockSpec((None, WINDOWS[1] * KV_CHUNKS, HEAD_DIM), first),
            pl.BlockSpec((None, TM * KV_CHUNKS, HEAD_DIM), lambda i, t: (i, t, 0)),
            pl.BlockSpec((kvn_rows, HEAD_DIM), lambda i, t: (0, 0)),
        ],
        out_shape=[res_sds(DILATIONS[0]), res_sds(DILATIONS[1]), res_sds(DILATIONS[2]),
                   st_sds(WINDOWS[0]), st_sds(WINDOWS[1]), st_sds(s),
                   jax.ShapeDtypeStruct((kvn_rows, HEAD_DIM), F32)],
        scratch_shapes=[pltpu.VMEM((KV_CHUNKS, TM, HEAD_DIM), F32),
                        pltpu.VMEM((KV_CHUNKS, TM, HEAD_DIM), F32),
                        pltpu.VMEM((KV_CHUNKS, SUB_STRIDE, TM // SUB_STRIDE, HEAD_DIM), F32)]
        + _weight_scratch(w_kv.shape),
        compiler_params=_params(("arbitrary", "arbitrary")),
        name="kv_prompt",
    )(x, kv_norm.reshape(1, D_MODEL), w_kv, k_norm_dil, xs)


def _band_scores(qs, ks, masks):
    return jnp.concatenate([jnp.where(mk, _dot_nt(q, k), NEG) for q, k, mk in zip(qs, ks, masks)],
                           axis=0)


def _band_finish(s, vs, nq):
    m = jnp.max(s, axis=-1, keepdims=True)
    pb = jnp.exp2(s - m).astype(BF16)
    res = []
    for i, v in enumerate(vs):
        rows = slice(i * nq, (i + 1) * nq)
        acc, l = _pv_and_rowsum(pb[rows], v)
        res.append((acc * (1.0 / l), m[rows] + jnp.log2(l)))
    return res


def _b_layer_kernel(x_ref, g_ref, win_ref, qn_ref, mkv_ref, gq_ref, wout_ref,
                    kv0_ref, kv0p_ref, kv1_ref, kv1p_ref, kv2_ref,
                    xs_ref, kvn_ref, c0_ref, c1_ref, c2_ref, mc_ref,
                    y_ref, ys_ref,
                    qslab1_ref, qslab2_ref, qsplit_ref, oslab_ref, lslab_ref, cat_ref,
                    zs_ref, cats_ref):
    t = pl.program_id(1)
    n = _step_index()

    @pl.when(n == 0)
    def _():
        zs_ref[...] = _dot(_rms(xs_ref[...], g_ref[...]).astype(BF16), win_ref[...])
        cats_ref[...] = jnp.zeros(cats_ref.shape, F32)

    _merge_row(cats_ref, slice(0, 2 * D_ATT), n,
               _b_sample_row(zs_ref[pl.ds(n, 1), :], kvn_ref[n], qn_ref, gq_ref[...],
                             (c0_ref[0], c1_ref[0], c2_ref[0]), mc_ref[0]))

    x = x_ref[...]
    xn = _rms(x, g_ref[...]).astype(BF16)

    def proj(lo, hi):
        return _dot(xn, win_ref[:, lo:hi])

    qi = lax.broadcasted_iota(jnp.int32, (BAND, 2 * BAND), 0)
    kj = lax.broadcasted_iota(jnp.int32, (BAND, 2 * BAND), 1)
    mask_rest = (kj >= qi) & (kj <= qi + BAND)
    mask_edge = (kj >= jnp.maximum(qi, jnp.where(t == 0, BAND, 0))) & (kj <= qi + BAND)

    def finish(g, h, rows, out, lse):
        oslab_ref[g, h, rows, :] = out
        lslab_ref[g, h, rows, :] = lse

    def normed_heads(g):
        qd = proj(g * D_ATT, (g + 1) * D_ATT)
        gain = qn_ref[g:g + 1, :] * QK_SCALE
        return [_rms(qd[:, _hs(h)], gain) for h in range(N_HEADS)]

    n_sub = TM // BAND
    qh = [q.astype(BF16) for q in normed_heads(0)]
    units0 = [(i, h) for i in range(n_sub) for h in range(N_HEADS)]

    def keys0(i, cols):
        if i == 0:
            return jnp.concatenate([kv0p_ref[0, :, cols], kv0_ref[0, 0:BAND, cols]], axis=0)
        return kv0_ref[0, (i - 1) * BAND:(i + 1) * BAND, cols]

    d1 = DILATIONS[1]
    for h, q in enumerate(normed_heads(1)):
        qslab1_ref[h] = q
    get_q1 = [_residue_rows(qslab1_ref, None, h, d1) for h in range(N_HEADS)]
    units1 = [(r, h) for r in range(d1) for h in range(N_HEADS)]

    def keys1(r, cols):
        return jnp.concatenate([kv1p_ref[r, :, cols], kv1_ref[r, :, cols]], axis=0)

    d2 = DILATIONS[2]
    nq = TM // d2
    for h, q in enumerate(normed_heads(2)):
        qslab2_ref[h] = q
    get_q2 = [_residue_rows(qslab2_ref, qsplit_ref, h, d2) for h in range(N_HEADS)]
    units2 = [(r, h) for r in range(d2) for h in range(N_HEADS)]
    qi2 = lax.broadcasted_iota(jnp.int32, (nq, BAND), 0)
    kj2 = lax.broadcasted_iota(jnp.int32, (nq, BAND), 1)
    mask2 = kj2 <= qi2 + t * nq

    s0 = _band_scores([qh[h][i * BAND:(i + 1) * BAND] for i, h in units0],
                      [keys0(i, _hs(h)) for i, h in units0],
                      [mask_edge if i == 0 else mask_rest for i, h in units0])
    s1 = _band_scores([get_q1[h](r).astype(BF16) for r, h in units1],
                      [keys1(r, _hs(h)) for r, h in units1], [mask_edge] * len(units1))
    s2 = _band_scores([get_q2[h](r).astype(BF16) for r, h in units2],
                      [kv2_ref[r, :, _hs(h)] for r, h in units2], [mask2] * len(units2))
    res = _band_finish(s0, [keys0(i, _hs(h, D_ATT)) for i, h in units0], BAND)
    for (i, h), (out, lse) in zip(units0, res):
        finish(0, h, pl.ds(i * BAND, BAND), out, lse)
    res = _band_finish(s1, [keys1(r, _hs(h, D_ATT)) for r, h in units1], BAND)
    for (r, h), (out, lse) in zip(units1, res):
        finish(1, h, pl.ds(r, BAND, stride=d1), out, lse)
    res = _band_finish(s2, [kv2_ref[r, :, _hs(h, D_ATT)] for r, h in units2], nq)
    for (r, h), (out, lse) in zip(units2, res):
        finish(2, h, pl.ds(r, nq, stride=d2), out, lse)

    base = N_GROUPS * D_ATT
    gd = proj(base, base + D_ATT)
    for h in range(N_HEADS):
        l0, l1, l2 = lslab_ref[0, h], lslab_ref[1, h], lslab_ref[2, h]
        m = jnp.maximum(jnp.maximum(l0, l1), l2)
        e0, e1, e2 = jnp.exp2(l0 - m), jnp.exp2(l1 - m), jnp.exp2(l2 - m)
        dil = ((oslab_ref[0, h] * e0 + oslab_ref[1, h] * e1 + oslab_ref[2, h] * e2)
               * (1.0 / (e0 + e1 + e2)))
        cat_ref[:, _hs(h)] = (dil * _silu(gd[:, _hs(h)])).astype(BF16)
    mq = proj(base + D_ATT, base + 2 * D_ATT)
    mg = proj(base + 2 * D_ATT, base + 3 * D_ATT)

    def store(h, val):
        cat_ref[:, _hs(h, D_ATT)] = val.astype(BF16)

    _mem_attn(mq, mg, mkv_ref, gq_ref[...] * QK_SCALE, store)
    y_ref[...] = x + _dot(cat_ref[...], wout_ref[...])

    @pl.when(_is_last_step())
    def _():
        ys_ref[...] = xs_ref[...] + _dot(cats_ref[...].astype(BF16), wout_ref[...])


def _b_layer(x, xs, norm_b, w_in, q_norm, mkv_b, kvbs, kv_new, caches, mem_cache, bl, layer,
             mem_q_norm, w_out):
    b, s, _ = x.shape
    nb = xs.shape[0]
    nt = s // TM
    assert nb == b * nt
    full, mem_cache_spec = _sample_specs(nb, nt, layer)
    cache_spec = pl.BlockSpec((1, BAND, None, KV_CHUNKS, HEAD_DIM),
                              lambda i, t: (i * nt + t, 0, 0, 0, 0))
    assert TM // BAND == DILATIONS[1] and s // DILATIONS[2] == BAND
    tile = lambda i, t: (i, t, 0)
    sub = TM // BAND
    kv_w = 2 * D_ATT
    return pl.pallas_call(
        _b_layer_kernel,
        grid=(b, s // TM),
        in_specs=[
            pl.BlockSpec((None, TM, D_MODEL), tile),
            _layer_spec((1, D_MODEL), bl),
            _layer_spec(w_in.shape[1:], bl),
            _layer_spec((N_GROUPS, HEAD_DIM), bl),
            pl.BlockSpec((None, None, MEM_LEN, kv_w), lambda i, t: (layer, i, 0, 0)),
            _layer_spec((1, HEAD_DIM), layer),
            _layer_spec(w_out.shape[1:], bl),
            pl.BlockSpec((None, 1, TM, kv_w), lambda i, t: (i, 0, t, 0)),
            pl.BlockSpec((None, 1, BAND, kv_w),
                         lambda i, t: (i, 0, jnp.maximum(t * sub - 1, 0), 0)),
            pl.BlockSpec((None, DILATIONS[1], BAND, kv_w), lambda i, t: (i, 0, t, 0)),
            pl.BlockSpec((None, DILATIONS[1], BAND, kv_w),
                         lambda i, t: (i, 0, jnp.maximum(t - 1, 0), 0)),
            pl.BlockSpec((None, DILATIONS[2], BAND, kv_w), lambda i, t: (i, 0, 0, 0)),
            full((nb, D_MODEL)),
            full(kv_new.shape),
            cache_spec, cache_spec, cache_spec,
            mem_cache_spec,
        ],
        out_specs=[pl.BlockSpec((None, TM, D_MODEL), tile), full((nb, D_MODEL))],
        out_shape=[jax.ShapeDtypeStruct((b, s, D_MODEL), F32),
                   jax.ShapeDtypeStruct((nb, D_MODEL), F32)],
        scratch_shapes=[pltpu.VMEM((N_HEADS, TM, HEAD_DIM), F32),
                        pltpu.VMEM((N_HEADS, TM, HEAD_DIM), F32),
                        pltpu.VMEM((N_HEADS, SUB_STRIDE, TM // SUB_STRIDE, HEAD_DIM), F32),
                        pltpu.VMEM((N_GROUPS, N_HEADS, TM, HEAD_DIM), F32),
                        pltpu.VMEM((N_GROUPS, N_HEADS, TM, HEAD_DIM), F32),
                        pltpu.VMEM((TM, 2 * D_ATT), BF16),
                        pltpu.VMEM((nb, w_in.shape[2]), F32),
                        pltpu.VMEM((nb, 2 * D_ATT), F32)],
        compiler_params=_params(("arbitrary", "arbitrary")),
        name=f"b_layer_{layer}",
    )(x, norm_b, w_in, q_norm, mkv_b, mem_q_norm, w_out,
      kvbs[0], kvbs[0], kvbs[1], kvbs[1], kvbs[2],
      xs, kv_new, *caches, mem_cache)


def _swap_halves(x, axis):
    return pltpu.roll(x, N_HEADS, axis=axis)


def _head_rows(tile):
    return jnp.concatenate([tile[N_HEADS + h:N_HEADS + h + 1, :] for h in range(N_HEADS)],
                           axis=-1)


def _low_tile(row):
    return jnp.concatenate([jnp.zeros((N_HEADS, HEAD_DIM), F32)]
                           + [row[:, _hs(h)] for h in range(N_HEADS)], axis=0)


def _q_tile(row, g):
    return jnp.concatenate([_rms(row[:, _hs(h)], g) * QK_SCALE for h in range(N_HEADS)]
                           + [jnp.zeros((N_HEADS, HEAD_DIM), F32)], axis=0)


def _tile_attend(qt, kv):
    s = jnp.sum(kv * qt[None], axis=-1, keepdims=True)
    m = jnp.max(s, axis=0)
    p = jnp.exp2(s - m[None])
    l = jnp.sum(p, axis=0)
    acc = jnp.sum(_swap_halves(jnp.broadcast_to(p, kv.shape), 1) * kv, axis=0)
    out = acc / _swap_halves(jnp.broadcast_to(l, acc.shape), 0)
    return out, m + jnp.log2(l)


def _sample_mem_attn(q_row, gq, mkv, mg_row):
    out, _ = _tile_attend(_q_tile(q_row, gq), mkv)
    return _head_rows(out * _low_tile(_silu(mg_row)))


def _gather_chunks(ref, n_rows, n_chunks, offset=0):
    total = ref.shape[0] // n_rows
    return jnp.concatenate([ref[pl.ds(offset + c, n_rows, stride=total), :]
                            for c in range(n_chunks)], axis=-1)


def _merge_row(ref, cols, n, row):
    tile = pl.ds(pl.multiple_of((n // SAMPLE_GB) * SAMPLE_GB, SAMPLE_GB), SAMPLE_GB)
    sub = lax.broadcasted_iota(jnp.int32, (SAMPLE_GB, row.shape[1]), 0)
    ref[tile, cols] = jnp.where(sub == n % SAMPLE_GB, row, ref[tile, cols])


def _a_sample_first(x_ref, g_ref, win_ref, cw_ref, st_ref, sto_ref, zm_ref, cat_ref):
    nb = x_ref.shape[0]
    n_st = D_CONV // HEAD_DIM
    xn = _rms(x_ref[...], g_ref[...]).astype(BF16)
    z = _dot(xn, win_ref[...])
    u = z[:, D_CONV:2 * D_CONV] * z[:, 0:D_CONV]
    s0 = _gather_chunks(st_ref, nb, n_st)
    s1 = _gather_chunks(st_ref, nb, n_st, n_st)
    y = cw_ref[0:1, :] * s0 + cw_ref[1:2, :] * s1 + cw_ref[2:3, :] * u
    for c in range(n_st):
        sto_ref[pl.ds(c, nb, stride=2 * n_st), :] = s1[:, _hs(c)]
        sto_ref[pl.ds(n_st + c, nb, stride=2 * n_st), :] = u[:, _hs(c)]
    cat_ref[:, 0:D_CONV] = _silu(z[:, 3 * D_CONV:4 * D_CONV]) * z[:, 2 * D_CONV:3 * D_CONV] * y
    cat_ref[:, D_CONV:] = jnp.zeros((nb, D_ATT), F32)
    zm_ref[...] = z[:, 4 * D_CONV:]


def _kv_sample_rows(x_ref, g_ref, w_ref, kn_ref, o_ref):
    nb = x_ref.shape[0]
    xn = _rms(x_ref[...], g_ref[...]).astype(BF16)
    kv = _dot(xn, w_ref[...])
    n_chunks = N_GROUPS * KV_CHUNKS
    for g in range(N_GROUPS):
        for c in range(KV_CHUNKS):
            chunk = kv[:, _hs(c, g * 2 * D_ATT)]
            if c < N_HEADS:
                chunk = _rms(chunk, kn_ref[g:g + 1, :])
            o_ref[pl.ds(g * KV_CHUNKS + c, nb, stride=n_chunks), :] = chunk


def _b_sample_row(row, kv_new, qn_ref, gq, caches, mkv):
    base = N_GROUPS * D_ATT
    outs, lses = [], []
    for g in range(N_GROUPS):
        qt = _q_tile(row[:, g * D_ATT:(g + 1) * D_ATT], qn_ref[g:g + 1, :])
        out, lse = _tile_attend(qt, jnp.concatenate([caches[g], kv_new[g][None]], axis=0))
        outs.append(out)
        lses.append(lse)
    m = jnp.maximum(jnp.maximum(lses[0], lses[1]), lses[2])
    es = [jnp.exp2(l - m) for l in lses]
    den = es[0] + es[1] + es[2]
    ws = [_swap_halves(jnp.broadcast_to(e / den, (KV_CHUNKS, HEAD_DIM)), 0) for e in es]
    dil = outs[0] * ws[0] + outs[1] * ws[1] + outs[2] * ws[2]
    dil_row = _head_rows(dil * _low_tile(_silu(row[:, base:base + D_ATT])))
    mem_row = _sample_mem_attn(row[:, base + D_ATT:base + 2 * D_ATT], gq, mkv,
                               row[:, base + 2 * D_ATT:])
    return jnp.concatenate([dil_row, mem_row], axis=-1)


def kernel(x_prompt, x_sample, cache_mem_kv, state_conv, cache_dil0_kv, cache_dil1_kv, cache_dil2_kv, mem_prompt, norm_a, w_in_a, conv_w_a, w_out_a, norm_b, w_in_b, q_norm_b, w_out_b, kv_norm, w_kv, k_norm_dil, mem_norm, w_mem_kv, mem_k_norm, mem_q_norm):
    n_a = w_in_a.shape[0]
    n_b = w_in_b.shape[0]
    depth = n_a + n_b
    b, s, _ = x_prompt.shape
    nb = x_sample.shape[0]
    assert x_sample.shape[1] == 1 and s % TM == 0 and nb % SAMPLE_GB == 0
    kv_dims = (2, N_HEADS, HEAD_DIM)

    w_in_b16, w_out_b16 = w_in_b.astype(BF16), w_out_b.astype(BF16)
    norm_a3 = norm_a.reshape(n_a, 1, D_MODEL)
    norm_b3 = norm_b.reshape(n_b, 1, D_MODEL)
    mem_q_norm3 = mem_q_norm.reshape(depth, 1, HEAD_DIM)

    mem_kv_f, mem_kv_b = _memkv_prompt(mem_prompt, mem_norm, w_mem_kv, mem_k_norm)
    mem_kv_b = mem_kv_b.reshape(depth, b, MEM_LEN, 2 * D_ATT)
    mem_cache = cache_mem_kv.reshape(depth, nb, MEM_LEN, KV_CHUNKS, HEAD_DIM)
    caches = []
    for g, cache in enumerate((cache_dil0_kv, cache_dil1_kv, cache_dil2_kv)):
        assert cache.shape[1] == WINDOWS[g]
        caches.append(cache.reshape(nb, BAND, DILATIONS[g], KV_CHUNKS, HEAD_DIM))
    state = state_conv.reshape(n_a, -1, HEAD_DIM)

    xp = x_prompt
    xs = x_sample.reshape(nb, D_MODEL)
    conv_p, conv_s = [], []
    for a in range(n_a):
        xp, cp, xs, cs = _a_layer(xp, xs, norm_a3, w_in_a, conv_w_a, mem_kv_b, state, mem_cache,
                                  a, a, mem_q_norm3, w_out_a)
        conv_p.append(cp)
        conv_s.append(cs.reshape(nb, CONV_W - 1, D_CONV))

    kvb0, kvb1, kvb2, st0, st1, st2, kv_new = _kv_shared(xp, xs, kv_norm, w_kv, k_norm_dil)
    kvbs = (kvb0, kvb1, kvb2)
    kv_new = kv_new.reshape(nb, N_GROUPS, KV_CHUNKS, HEAD_DIM)

    for bl in range(n_b):
        layer = n_a + bl
        xp, xs = _b_layer(xp, xs, norm_b3, w_in_b16, q_norm_b, mem_kv_b, kvbs, kv_new, caches,
                          mem_cache, bl, layer, mem_q_norm3, w_out_b16)

    dil_s = [kv_new[:, g].reshape(nb, 1, *kv_dims) for g in range(N_GROUPS)]
    return (xp, xs.reshape(nb, 1, D_MODEL),
            mem_kv_f.reshape(depth, b, MEM_LEN, *kv_dims),
            jnp.stack(conv_p), jnp.stack(conv_s),
            st0.reshape(b, WINDOWS[0], *kv_dims), st1.reshape(b, WINDOWS[1], *kv_dims),
            st2.reshape(b, s, *kv_dims), *dil_s)
```

```python
import functools

import jax
import jax.numpy as jnp
from jax import lax
from jax.experimental import pallas as pl
from jax.experimental.pallas import tpu as pltpu

F32 = jnp.float32
BF16 = jnp.bfloat16

D_MODEL = 1024
HEAD_DIM = 128
N_HEADS = 4
D_ATT = N_HEADS * HEAD_DIM
KV_CHUNKS = 2 * N_HEADS
D_CONV = D_MODEL
CONV_W = 3
N_GROUPS = 3
DILATIONS = (1, 4, 16)
WINDOWS = (128, 512, 2048)
BAND = 128
SUB_STRIDE = 4
MEM_LEN = 256
EPS = 1e-6
NEG = -1e30
SCALE = HEAD_DIM ** -0.5
QK_SCALE = SCALE * 1.4426950408889634

TM = 512
SAMPLE_GB = 8
W_STAGE_ROWS = 256
VMEM_LIMIT = 56 * 1024 * 1024


def _rms(x, g):
    return x * lax.rsqrt(jnp.mean(x * x, axis=-1, keepdims=True) + EPS) * g


def _silu(x):
    return x * (1.0 / (1.0 + jnp.exp(-x)))


def _dot(a, b):
    return jnp.dot(a, b, preferred_element_type=F32)


def _dot_nt(a, b):
    return lax.dot_general(a, b, (((1,), (1,)), ((), ())), preferred_element_type=F32)


def _hs(h, base=0):
    return slice(base + h * HEAD_DIM, base + (h + 1) * HEAD_DIM)


def _params(sem):
    return pltpu.CompilerParams(dimension_semantics=sem, vmem_limit_bytes=VMEM_LIMIT)


def _fixed_spec(block, index):
    return pl.BlockSpec(block, lambda *_: index, pipeline_mode=pl.Buffered(1))


def _const_spec(shape):
    return _fixed_spec(shape, (0,) * len(shape))


def _layer_spec(shape, layer):
    return _fixed_spec((None,) + tuple(shape), (layer,) + (0,) * len(shape))


def _cast_weight(w_hbm, dst_ref, stage_ref, sem_ref):
    rows = stage_ref.shape[1]
    n_chunks = w_hbm.shape[0] // rows

    def copy(c, slot):
        src = w_hbm.at[pl.ds(pl.multiple_of(c * rows, rows), rows)]
        return pltpu.make_async_copy(src, stage_ref.at[slot], sem_ref.at[slot])

    copy(0, 0).start()

    def body(c, carry):
        slot = c % 2

        @pl.when(c + 1 < n_chunks)
        def _():
            copy(c + 1, 1 - slot).start()

        copy(c, slot).wait()
        dst_ref[pl.ds(pl.multiple_of(c * rows, rows), rows), :] = stage_ref[slot].astype(BF16)
        return carry

    lax.fori_loop(0, n_chunks, body, 0)


def _weight_scratch(shape):
    k, n = shape
    assert k % W_STAGE_ROWS == 0
    return [pltpu.VMEM((k, n), BF16), pltpu.VMEM((2, W_STAGE_ROWS, n), F32),
            pltpu.SemaphoreType.DMA((2,))]


def _memkv_kernel(mem_ref, g_ref, w_ref, kg_ref, of_ref, ob_ref, w16_ref):
    i = pl.program_id(1)
    rt = ob_ref.shape[0]

    @pl.when(i == 0)
    def _():
        w16_ref[...] = w_ref[...].astype(BF16)

    rows = pl.ds(pl.multiple_of(i * rt, rt), rt)
    xn = _rms(mem_ref[rows, :], g_ref[...]).astype(BF16)
    kv = _dot(xn, w16_ref[...])
    for c in range(KV_CHUNKS):
        chunk = kv[:, _hs(c)]
        if c < N_HEADS:
            chunk = _rms(chunk, kg_ref[...])
        of_ref[pl.ds(c, rt, stride=KV_CHUNKS), :] = chunk
        ob_ref[:, _hs(c)] = chunk.astype(BF16)


def _memkv_prompt(mem, mem_norm, w_mem_kv, mem_k_norm):
    depth = w_mem_kv.shape[0]
    rows = mem.shape[0] * mem.shape[1]
    rt = TM
    return pl.pallas_call(
        _memkv_kernel,
        grid=(depth, rows // rt),
        in_specs=[
            _const_spec((rows, D_MODEL)),
            pl.BlockSpec((None, 1, D_MODEL), lambda l, i: (l, 0, 0)),
            pl.BlockSpec((None, D_MODEL, 2 * D_ATT), lambda l, i: (l, 0, 0)),
            pl.BlockSpec((None, 1, HEAD_DIM), lambda l, i: (l, 0, 0)),
        ],
        out_specs=[
            pl.BlockSpec((None, rt * KV_CHUNKS, HEAD_DIM), lambda l, i: (l, i, 0)),
            pl.BlockSpec((None, rt, 2 * D_ATT), lambda l, i: (l, i, 0)),
        ],
        out_shape=[jax.ShapeDtypeStruct((depth, rows * KV_CHUNKS, HEAD_DIM), F32),
                   jax.ShapeDtypeStruct((depth, rows, 2 * D_ATT), BF16)],
        scratch_shapes=[pltpu.VMEM((D_MODEL, 2 * D_ATT), BF16)],
        compiler_params=_params(("arbitrary", "arbitrary")),
        name="memkv_prompt",
    )(mem.reshape(rows, D_MODEL), mem_norm.reshape(depth, 1, D_MODEL), w_mem_kv,
      mem_k_norm.reshape(depth, 1, HEAD_DIM))


def _pv_and_rowsum(p, v):
    ones = jnp.ones((v.shape[0], HEAD_DIM), BF16)
    both = _dot(p, jnp.concatenate([v, ones], axis=1))
    return both[:, :HEAD_DIM], both[:, HEAD_DIM:]


def _mem_attn(mq, mg, mkv_ref, gq, store):
    for h in range(N_HEADS):
        q = _rms(mq[:, _hs(h)], gq).astype(BF16)
        s = _dot_nt(q, mkv_ref[:, _hs(h)])
        m = jnp.max(s, axis=-1, keepdims=True)
        acc, l = _pv_and_rowsum(jnp.exp2(s - m).astype(BF16), mkv_ref[:, _hs(h, D_ATT)])
        store(h, acc / l * _silu(mg[:, _hs(h)]))


def _step_index():
    return pl.program_id(0) * pl.num_programs(1) + pl.program_id(1)


def _is_last_step():
    return _step_index() == pl.num_programs(0) * pl.num_programs(1) - 1


def _a_layer_kernel(x_ref, g_ref, win_hbm, cw_ref, mkv_ref, gq_ref, wout_hbm,
                    xs_ref, st_ref, mc_ref,
                    y_ref, cs_ref, ys_ref, sto_ref,
                    ext_ref, cat_ref, zs_ref, cats_ref,
                    win_ref, win_stage, win_sem, wout_ref, wout_stage, wout_sem, *, a):
    t = pl.program_id(1)
    n = _step_index()

    @pl.when(t == 0)
    def _():
        ext_ref[0:8, :] = jnp.zeros((8, D_CONV), F32)

    @pl.when(n == 0)
    def _():
        _cast_weight(win_hbm.at[a], win_ref, win_stage, win_sem)
        _cast_weight(wout_hbm.at[a], wout_ref, wout_stage, wout_sem)
        _a_sample_first(xs_ref, g_ref, win_ref, cw_ref, st_ref, sto_ref, zs_ref, cats_ref)

    row = zs_ref[pl.ds(n, 1), :]
    _merge_row(cats_ref, slice(D_CONV, D_CONV + D_ATT), n,
               _sample_mem_attn(row[:, :D_ATT], gq_ref[...], mc_ref[0], row[:, D_ATT:]))

    x = x_ref[...]
    xn = _rms(x, g_ref[...]).astype(BF16)

    def proj(lo, hi):
        return _dot(xn, win_ref[:, lo:hi])

    u = proj(D_CONV, 2 * D_CONV) * proj(0, D_CONV)
    ext_ref[8:8 + TM, :] = u
    y = (cw_ref[0:1, :] * ext_ref[6:6 + TM, :] + cw_ref[1:2, :] * ext_ref[7:7 + TM, :]
         + cw_ref[2:3, :] * u)
    ext_ref[0:8, :] = ext_ref[TM:TM + 8, :]
    cs_ref[...] = ext_ref[6:8, :]
    bg = proj(2 * D_CONV, 3 * D_CONV)
    g = proj(3 * D_CONV, 4 * D_CONV)
    cat_ref[:, 0:D_CONV] = (_silu(g) * bg * y).astype(BF16)
    mq = proj(4 * D_CONV, 4 * D_CONV + D_ATT)
    mg = proj(4 * D_CONV + D_ATT, 4 * D_CONV + 2 * D_ATT)

    def store(h, val):
        cat_ref[:, _hs(h, D_CONV)] = val.astype(BF16)

    _mem_attn(mq, mg, mkv_ref, gq_ref[...] * QK_SCALE, store)
    y_ref[...] = x + _dot(cat_ref[...], wout_ref[...])

    @pl.when(_is_last_step())
    def _():
        ys_ref[...] = xs_ref[...] + _dot(cats_ref[...].astype(BF16), wout_ref[...])


def _sample_specs(nb, nt, layer):
    full = lambda shape: pl.BlockSpec(shape, lambda i, t: (0,) * len(shape))
    mem_cache = pl.BlockSpec((None, 1, MEM_LEN, KV_CHUNKS, HEAD_DIM),
                             lambda i, t: (layer, i * nt + t, 0, 0, 0))
    return full, mem_cache


def _a_layer(x, xs, norm_a, w_in, conv_w, mkv_b, state, mem_cache, a, layer, mem_q_norm, w_out):
    b, s, _ = x.shape
    nb = xs.shape[0]
    nt = s // TM
    assert nb == b * nt
    st_rows = nb * (CONV_W - 1) * D_CONV // HEAD_DIM
    full, mem_cache_spec = _sample_specs(nb, nt, layer)
    return pl.pallas_call(
        functools.partial(_a_layer_kernel, a=a),
        grid=(b, nt),
        in_specs=[
            pl.BlockSpec((None, TM, D_MODEL), lambda i, t: (i, t, 0)),
            _layer_spec((1, D_MODEL), a),
            pl.BlockSpec(memory_space=pl.ANY),
            _layer_spec((CONV_W, D_CONV), a),
            pl.BlockSpec((None, None, MEM_LEN, 2 * D_ATT), lambda i, t: (layer, i, 0, 0)),
            _layer_spec((1, HEAD_DIM), layer),
            pl.BlockSpec(memory_space=pl.ANY),
            full((nb, D_MODEL)),
            _layer_spec((st_rows, HEAD_DIM), a),
            mem_cache_spec,
        ],
        out_specs=[
            pl.BlockSpec((None, TM, D_MODEL), lambda i, t: (i, t, 0)),
            pl.BlockSpec((None, CONV_W - 1, D_CONV), lambda i, t: (i, 0, 0)),
            full((nb, D_MODEL)),
            full((st_rows, HEAD_DIM)),
        ],
        out_shape=[jax.ShapeDtypeStruct((b, s, D_MODEL), F32),
                   jax.ShapeDtypeStruct((b, CONV_W - 1, D_CONV), F32),
                   jax.ShapeDtypeStruct((nb, D_MODEL), F32),
                   jax.ShapeDtypeStruct((st_rows, HEAD_DIM), F32)],
        scratch_shapes=[pltpu.VMEM((TM + 8, D_CONV), F32),
                        pltpu.VMEM((TM, D_CONV + D_ATT), BF16),
                        pltpu.VMEM((nb, 2 * D_ATT), F32),
                        pltpu.VMEM((nb, D_CONV + D_ATT), F32)]
        + _weight_scratch(w_in.shape[1:]) + _weight_scratch(w_out.shape[1:]),
        compiler_params=_params(("arbitrary", "arbitrary")),
        name=f"a_layer_{layer}",
    )(x, norm_a, w_in, conv_w, mkv_b, mem_q_norm, w_out, xs, state, mem_cache)


def _residue_rows(slab_ref, split_ref, c, d):
    rows = slab_ref.shape[1]
    if d == SUB_STRIDE:
        return lambda r: slab_ref[c, pl.ds(r, rows // d, stride=d), :]
    assert d == SUB_STRIDE * SUB_STRIDE
    for r1 in range(SUB_STRIDE):
        split_ref[c, r1] = slab_ref[c, pl.ds(r1, rows // SUB_STRIDE, stride=SUB_STRIDE), :]
    return lambda r: split_ref[c, r % SUB_STRIDE,
                               pl.ds(r // SUB_STRIDE, rows // d, stride=SUB_STRIDE), :]


def _to_residue_major(dst_ref, slab_ref, split_ref, n_chunks, d):
    for c in range(n_chunks):
        get = _residue_rows(slab_ref, split_ref, c, d)
        for r in range(d):
            dst_ref[r, :, _hs(c)] = get(r).astype(dst_ref.dtype)


def _kv_kernel(x_ref, g_ref, w_hbm, kn_ref, xs_ref, kvb0_ref, kvb1_ref, kvb2_ref,
               st0_ref, st1_ref, st2_ref, kvn_ref, slab1_ref, slab2_ref, split_ref,
               w_ref, w_stage, w_sem):
    t = pl.program_id(1)
    last = pl.num_programs(1) - 1

    @pl.when(_step_index() == 0)
    def _():
        _cast_weight(w_hbm, w_ref, w_stage, w_sem)
        _kv_sample_rows(xs_ref, g_ref, w_ref, kn_ref, kvn_ref)

    xn = _rms(x_ref[...], g_ref[...]).astype(BF16)
    kvb_refs = (kvb0_ref, kvb1_ref, kvb2_ref)
    slabs = (None, slab1_ref, slab2_ref)
    for g in reversed(range(N_GROUPS)):
        kv = _dot(xn, w_ref[:, g * 2 * D_ATT:(g + 1) * 2 * D_ATT])
        chunks = [_rms(kv[:, _hs(h)], kn_ref[g:g + 1, :]) for h in range(N_HEADS)]
        chunks += [kv[:, _hs(h, D_ATT)] for h in range(N_HEADS)]
        d = DILATIONS[g]
        for c, chunk in enumerate(chunks):
            if d == 1:
                kvb_refs[g][0, :, _hs(c)] = chunk.astype(BF16)
            else:
                slabs[g][c] = chunk
            if g == 2:
                st2_ref[pl.ds(c, TM, stride=KV_CHUNKS), :] = chunk
        if d > 1:
            _to_residue_major(kvb_refs[g], slabs[g], split_ref, KV_CHUNKS, d)
        if g == 1:
            @pl.when(t == last)
            def _():
                for c, chunk in enumerate(chunks):
                    st1_ref[pl.ds(c, TM, stride=KV_CHUNKS), :] = chunk
        if g == 0:
            @pl.when(t == last)
            def _():
                for c, chunk in enumerate(chunks):
                    st0_ref[pl.ds(c, WINDOWS[0], stride=KV_CHUNKS), :] = chunk[TM - WINDOWS[0]:, :]


def _kv_shared(x, xs, kv_norm, w_kv, k_norm_dil):
    b, s, _ = x.shape
    nb = xs.shape[0]
    kvn_rows = nb * N_GROUPS * KV_CHUNKS
    assert WINDOWS[1] == TM and WINDOWS[2] == s and WINDOWS[0] <= TM
    first = lambda i, t: (i, 0, 0)
    res_spec = lambda d: pl.BlockSpec((None, d, TM // d, 2 * D_ATT), lambda i, t: (i, 0, t, 0))
    res_sds = lambda d: jax.ShapeDtypeStruct((b, d, s // d, 2 * D_ATT), BF16)
    st_sds = lambda rows: jax.ShapeDtypeStruct((b, rows * KV_CHUNKS, HEAD_DIM), F32)
    return pl.pallas_call(
        _kv_kernel,
        grid=(b, s // TM),
        in_specs=[
            pl.BlockSpec((None, TM, D_MODEL), lambda i, t: (i, t, 0)),
            _const_spec((1, D_MODEL)),
            pl.BlockSpec(memory_space=pl.ANY),
            _const_spec((N_GROUPS, HEAD_DIM)),
            _const_spec((nb, D_MODEL)),
        ],
        out_specs=[
            res_spec(DILATIONS[0]), res_spec(DILATIONS[1]), res_spec(DILATIONS[2]),
            pl.BlockSpec((None, WINDOWS[0] * KV_CHUNKS, HEAD_DIM), first),
            pl.BlockSpec((None, WINDOWS[1] * KV_CHUNKS, HEAD_DIM), first),
            pl.BlockSpec((None, TM * KV_CHUNKS, HEAD_DIM), lambda i, t: (i, t, 0)),
            pl.BlockSpec((kvn_rows, HEAD_DIM), lambda i, t: (0, 0)),
        ],
        out_shape=[res_sds(DILATIONS[0]), res_sds(DILATIONS[1]), res_sds(DILATIONS[2]),
                   st_sds(WINDOWS[0]), st_sds(WINDOWS[1]), st_sds(s),
                   jax.ShapeDtypeStruct((kvn_rows, HEAD_DIM), F32)],
        scratch_shapes=[pltpu.VMEM((KV_CHUNKS, TM, HEAD_DIM), F32),
                        pltpu.VMEM((KV_CHUNKS, TM, HEAD_DIM), F32),
                        pltpu.VMEM((KV_CHUNKS, SUB_STRIDE, TM // SUB_STRIDE, HEAD_DIM), F32)]
        + _weight_scratch(w_kv.shape),
        compiler_params=_params(("arbitrary", "arbitrary")),
        name="kv_prompt",
    )(x, kv_norm.reshape(1, D_MODEL), w_kv, k_norm_dil, xs)


def _band_scores(qs, ks, masks):
    return jnp.concatenate([jnp.where(mk, _dot_nt(q, k), NEG) for q, k, mk in zip(qs, ks, masks)],
                           axis=0)


def _band_finish(s, vs, nq):
    m = jnp.max(s, axis=-1, keepdims=True)
    pb = jnp.exp2(s - m).astype(BF16)
    res = []
    for i, v in enumerate(vs):
        rows = slice(i * nq, (i + 1) * nq)
        acc, l = _pv_and_rowsum(pb[rows], v)
        res.append((acc * (1.0 / l), m[rows] + jnp.log2(l)))
    return res


def _b_layer_kernel(x_ref, g_ref, win_ref, qn_ref, mkv_ref, gq_ref, wout_ref,
                    kv0_ref, kv0p_ref, kv1_ref, kv1p_ref, kv2_ref,
                    xs_ref, kvn_ref, c0_ref, c1_ref, c2_ref, mc_ref,
                    y_ref, ys_ref,
                    qslab1_ref, qslab2_ref, qsplit_ref, oslab_ref, lslab_ref, cat_ref,
                    zs_ref, cats_ref):
    t = pl.program_id(1)
    n = _step_index()

    @pl.when(n == 0)
    def _():
        zs_ref[...] = _dot(_rms(xs_ref[...], g_ref[...]).astype(BF16), win_ref[...])
        cats_ref[...] = jnp.zeros(cats_ref.shape, F32)

    _merge_row(cats_ref, slice(0, 2 * D_ATT), n,
               _b_sample_row(zs_ref[pl.ds(n, 1), :], kvn_ref[n], qn_ref, gq_ref[...],
                             (c0_ref[0], c1_ref[0], c2_ref[0]), mc_ref[0]))

    x = x_ref[...]
    xn = _rms(x, g_ref[...]).astype(BF16)

    def proj(lo, hi):
        return _dot(xn, win_ref[:, lo:hi])

    qi = lax.broadcasted_iota(jnp.int32, (BAND, 2 * BAND), 0)
    kj = lax.broadcasted_iota(jnp.int32, (BAND, 2 * BAND), 1)
    mask_rest = (kj >= qi) & (kj <= qi + BAND)
    mask_edge = (kj >= jnp.maximum(qi, jnp.where(t == 0, BAND, 0))) & (kj <= qi + BAND)

    def finish(g, h, rows, out, lse):
        oslab_ref[g, h, rows, :] = out
        lslab_ref[g, h, rows, :] = lse

    def normed_heads(g):
        qd = proj(g * D_ATT, (g + 1) * D_ATT)
        gain = qn_ref[g:g + 1, :] * QK_SCALE
        return [_rms(qd[:, _hs(h)], gain) for h in range(N_HEADS)]

    n_sub = TM // BAND
    qh = [q.astype(BF16) for q in normed_heads(0)]
    units0 = [(i, h) for i in range(n_sub) for h in range(N_HEADS)]

    def keys0(i, cols):
        if i == 0:
            return jnp.concatenate([kv0p_ref[0, :, cols], kv0_ref[0, 0:BAND, cols]], axis=0)
        return kv0_ref[0, (i - 1) * BAND:(i + 1) * BAND, cols]

    d1 = DILATIONS[1]
    for h, q in enumerate(normed_heads(1)):
        qslab1_ref[h] = q
    get_q1 = [_residue_rows(qslab1_ref, None, h, d1) for h in range(N_HEADS)]
    units1 = [(r, h) for r in range(d1) for h in range(N_HEADS)]

    def keys1(r, cols):
        return jnp.concatenate([kv1p_ref[r, :, cols], kv1_ref[r, :, cols]], axis=0)

    d2 = DILATIONS[2]
    nq = TM // d2
    for h, q in enumerate(normed_heads(2)):
        qslab2_ref[h] = q
    get_q2 = [_residue_rows(qslab2_ref, qsplit_ref, h, d2) for h in range(N_HEADS)]
    units2 = [(r, h) for r in range(d2) for h in range(N_HEADS)]
    qi2 = lax.broadcasted_iota(jnp.int32, (nq, BAND), 0)
    kj2 = lax.broadcasted_iota(jnp.int32, (nq, BAND), 1)
    mask2 = kj2 <= qi2 + t * nq

    s0 = _band_scores([qh[h][i * BAND:(i + 1) * BAND] for i, h in units0],
                      [keys0(i, _hs(h)) for i, h in units0],
                      [mask_edge if i == 0 else mask_rest for i, h in units0])
    s1 = _band_scores([get_q1[h](r).astype(BF16) for r, h in units1],
                      [keys1(r, _hs(h)) for r, h in units1], [mask_edge] * len(units1))
    s2 = _band_scores([get_q2[h](r).astype(BF16) for r, h in units2],
                      [kv2_ref[r, :, _hs(h)] for r, h in units2], [mask2] * len(units2))
    res = _band_finish(s0, [keys0(i, _hs(h, D_ATT)) for i, h in units0], BAND)
    for (i, h), (out, lse) in zip(units0, res):
        finish(0, h, pl.ds(i * BAND, BAND), out, lse)
    res = _band_finish(s1, [keys1(r, _hs(h, D_ATT)) for r, h in units1], BAND)
    for (r, h), (out, lse) in zip(units1, res):
        finish(1, h, pl.ds(r, BAND, stride=d1), out, lse)
    res = _band_finish(s2, [kv2_ref[r, :, _hs(h, D_ATT)] for r, h in units2], nq)
    for (r, h), (out, lse) in zip(units2, res):
        finish(2, h, pl.ds(r, nq, stride=d2), out, lse)

    base = N_GROUPS * D_ATT
    gd = proj(base, base + D_ATT)
    for h in range(N_HEADS):
        l0, l1, l2 = lslab_ref[0, h], lslab_ref[1, h], lslab_ref[2, h]
        m = jnp.maximum(jnp.maximum(l0, l1), l2)
        e0, e1, e2 = jnp.exp2(l0 - m), jnp.exp2(l1 - m), jnp.exp2(l2 - m)
        dil = ((oslab_ref[0, h] * e0 + oslab_ref[1, h] * e1 + oslab_ref[2, h] * e2)
               * (1.0 / (e0 + e1 + e2)))
        cat_ref[:, _hs(h)] = (dil * _silu(gd[:, _hs(h)])).astype(BF16)
    mq = proj(base + D_ATT, base + 2 * D_ATT)
    mg = proj(base + 2 * D_ATT, base + 3 * D_ATT)

    def store(h, val):
        cat_ref[:, _hs(h, D_ATT)] = val.astype(BF16)

    _mem_attn(mq, mg, mkv_ref, gq_ref[...] * QK_SCALE, store)
    y_ref[...] = x + _dot(cat_ref[...], wout_ref[...])

    @pl.when(_is_last_step())
    def _():
        ys_ref[...] = xs_ref[...] + _dot(cats_ref[...].astype(BF16), wout_ref[...])


def _b_layer(x, xs, norm_b, w_in, q_norm, mkv_b, kvbs, kv_new, caches, mem_cache, bl, layer,
             mem_q_norm, w_out):
    b, s, _ = x.shape
    nb = xs.shape[0]
    nt = s // TM
    assert nb == b * nt
    full, mem_cache_spec = _sample_specs(nb, nt, layer)
    cache_spec = pl.BlockSpec((1, BAND, None, KV_CHUNKS, HEAD_DIM),
                              lambda i, t: (i * nt + t, 0, 0, 0, 0))
    assert TM // BAND == DILATIONS[1] and s // DILATIONS[2] == BAND
    tile = lambda i, t: (i, t, 0)
    sub = TM // BAND
    kv_w = 2 * D_ATT
    return pl.pallas_call(
        _b_layer_kernel,
        grid=(b, s // TM),
        in_specs=[
            pl.BlockSpec((None, TM, D_MODEL), tile),
            _layer_spec((1, D_MODEL), bl),
            _layer_spec(w_in.shape[1:], bl),
            _layer_spec((N_GROUPS, HEAD_DIM), bl),
            pl.BlockSpec((None, None, MEM_LEN, kv_w), lambda i, t: (layer, i, 0, 0)),
            _layer_spec((1, HEAD_DIM), layer),
            _layer_spec(w_out.shape[1:], bl),
            pl.BlockSpec((None, 1, TM, kv_w), lambda i, t: (i, 0, t, 0)),
            pl.BlockSpec((None, 1, BAND, kv_w),
                         lambda i, t: (i, 0, jnp.maximum(t * sub - 1, 0), 0)),
            pl.BlockSpec((None, DILATIONS[1], BAND, kv_w), lambda i, t: (i, 0, t, 0)),
            pl.BlockSpec((None, DILATIONS[1], BAND, kv_w),
                         lambda i, t: (i, 0, jnp.maximum(t - 1, 0), 0)),
            pl.BlockSpec((None, DILATIONS[2], BAND, kv_w), lambda i, t: (i, 0, 0, 0)),
            full((nb, D_MODEL)),
            full(kv_new.shape),
            cache_spec, cache_spec, cache_spec,
            mem_cache_spec,
        ],
        out_specs=[pl.BlockSpec((None, TM, D_MODEL), tile), full((nb, D_MODEL))],
        out_shape=[jax.ShapeDtypeStruct((b, s, D_MODEL), F32),
                   jax.ShapeDtypeStruct((nb, D_MODEL), F32)],
        scratch_shapes=[pltpu.VMEM((N_HEADS, TM, HEAD_DIM), F32),
                        pltpu.VMEM((N_HEADS, TM, HEAD_DIM), F32),
                        pltpu.VMEM((N_HEADS, SUB_STRIDE, TM // SUB_STRIDE, HEAD_DIM), F32),
                        pltpu.VMEM((N_GROUPS, N_HEADS, TM, HEAD_DIM), F32),
                        pltpu.VMEM((N_GROUPS, N_HEADS, TM, HEAD_DIM), F32),
                        pltpu.VMEM((TM, 2 * D_ATT), BF16),
                        pltpu.VMEM((nb, w_in.shape[2]), F32),
                        pltpu.VMEM((nb, 2 * D_ATT), F32)],
        compiler_params=_params(("arbitrary", "arbitrary")),
        name=f"b_layer_{layer}",
    )(x, norm_b, w_in, q_norm, mkv_b, mem_q_norm, w_out,
      kvbs[0], kvbs[0], kvbs[1], kvbs[1], kvbs[2],
      xs, kv_new, *caches, mem_cache)


def _swap_halves(x, axis):
    return pltpu.roll(x, N_HEADS, axis=axis)


def _head_rows(tile):
    return jnp.concatenate([tile[N_HEADS + h:N_HEADS + h + 1, :] for h in range(N_HEADS)],
                           axis=-1)


def _low_tile(row):
    return jnp.concatenate([jnp.zeros((N_HEADS, HEAD_DIM), F32)]
                           + [row[:, _hs(h)] for h in range(N_HEADS)], axis=0)


def _q_tile(row, g):
    return jnp.concatenate([_rms(row[:, _hs(h)], g) * QK_SCALE for h in range(N_HEADS)]
                           + [jnp.zeros((N_HEADS, HEAD_DIM), F32)], axis=0)


def _tile_attend(qt, kv):
    s = jnp.sum(kv * qt[None], axis=-1, keepdims=True)
    m = jnp.max(s, axis=0)
    p = jnp.exp2(s - m[None])
    l = jnp.sum(p, axis=0)
    acc = jnp.sum(_swap_halves(p, 1) * kv, axis=0)
    out = acc / _swap_halves(jnp.broadcast_to(l, acc.shape), 0)
    return out, m + jnp.log2(l)


def _sample_mem_attn(q_row, gq, mkv, mg_row):
    out, _ = _tile_attend(_q_tile(q_row, gq), mkv)
    return _head_rows(out * _low_tile(_silu(mg_row)))


def _gather_chunks(ref, n_rows, n_chunks, offset=0):
    total = ref.shape[0] // n_rows
    return jnp.concatenate([ref[pl.ds(offset + c, n_rows, stride=total), :]
                            for c in range(n_chunks)], axis=-1)


def _merge_row(ref, cols, n, row):
    tile = pl.ds(pl.multiple_of((n // SAMPLE_GB) * SAMPLE_GB, SAMPLE_GB), SAMPLE_GB)
    sub = lax.broadcasted_iota(jnp.int32, (SAMPLE_GB, row.shape[1]), 0)
    ref[tile, cols] = jnp.where(sub == n % SAMPLE_GB, row, ref[tile, cols])


def _a_sample_first(x_ref, g_ref, win_ref, cw_ref, st_ref, sto_ref, zm_ref, cat_ref):
    nb = x_ref.shape[0]
    n_st = D_CONV // HEAD_DIM
    xn = _rms(x_ref[...], g_ref[...]).astype(BF16)
    z = _dot(xn, win_ref[...])
    u = z[:, D_CONV:2 * D_CONV] * z[:, 0:D_CONV]
    s0 = _gather_chunks(st_ref, nb, n_st)
    s1 = _gather_chunks(st_ref, nb, n_st, n_st)
    y = cw_ref[0:1, :] * s0 + cw_ref[1:2, :] * s1 + cw_ref[2:3, :] * u
    for c in range(n_st):
        sto_ref[pl.ds(c, nb, stride=2 * n_st), :] = s1[:, _hs(c)]
        sto_ref[pl.ds(n_st + c, nb, stride=2 * n_st), :] = u[:, _hs(c)]
    cat_ref[:, 0:D_CONV] = _silu(z[:, 3 * D_CONV:4 * D_CONV]) * z[:, 2 * D_CONV:3 * D_CONV] * y
    cat_ref[:, D_CONV:] = jnp.zeros((nb, D_ATT), F32)
    zm_ref[...] = z[:, 4 * D_CONV:]


def _kv_sample_rows(x_ref, g_ref, w_ref, kn_ref, o_ref):
    nb = x_ref.shape[0]
    xn = _rms(x_ref[...], g_ref[...]).astype(BF16)
    kv = _dot(xn, w_ref[...])
    n_chunks = N_GROUPS * KV_CHUNKS
    for g in range(N_GROUPS):
        for c in range(KV_CHUNKS):
            chunk = kv[:, _hs(c, g * 2 * D_ATT)]
            if c < N_HEADS:
                chunk = _rms(chunk, kn_ref[g:g + 1, :])
            o_ref[pl.ds(g * KV_CHUNKS + c, nb, stride=n_chunks), :] = chunk


def _b_sample_row(row, kv_new, qn_ref, gq, caches, mkv):
    base = N_GROUPS * D_ATT
    outs, lses = [], []
    for g in range(N_GROUPS):
        qt = _q_tile(row[:, g * D_ATT:(g + 1) * D_ATT], qn_ref[g:g + 1, :])
        out, lse = _tile_attend(qt, jnp.concatenate([caches[g], kv_new[g][None]], axis=0))
        outs.append(out)
        lses.append(lse)
    m = jnp.maximum(jnp.maximum(lses[0], lses[1]), lses[2])
    es = [jnp.exp2(l - m) for l in lses]
    den = es[0] + es[1] + es[2]
    ws = [_swap_halves(jnp.broadcast_to(e / den, (KV_CHUNKS, HEAD_DIM)), 0) for e in es]
    dil = outs[0] * ws[0] + outs[1] * ws[1] + outs[2] * ws[2]
    dil_row = _head_rows(dil * _low_tile(_silu(row[:, base:base + D_ATT])))
    mem_row = _sample_mem_attn(row[:, base + D_ATT:base + 2 * D_ATT], gq, mkv,
                               row[:, base + 2 * D_ATT:])
    return jnp.concatenate([dil_row, mem_row], axis=-1)


def kernel(x_prompt, x_sample, cache_mem_kv, state_conv, cache_dil0_kv, cache_dil1_kv, cache_dil2_kv, mem_prompt, norm_a, w_in_a, conv_w_a, w_out_a, norm_b, w_in_b, q_norm_b, w_out_b, kv_norm, w_kv, k_norm_dil, mem_norm, w_mem_kv, mem_k_norm, mem_q_norm):
    n_a = w_in_a.shape[0]
    n_b = w_in_b.shape[0]
    depth = n_a + n_b
    b, s, _ = x_prompt.shape
    nb = x_sample.shape[0]
    assert x_sample.shape[1] == 1 and s % TM == 0 and nb % SAMPLE_GB == 0
    kv_dims = (2, N_HEADS, HEAD_DIM)

    w_in_b16, w_out_b16 = w_in_b.astype(BF16), w_out_b.astype(BF16)
    norm_a3 = norm_a.reshape(n_a, 1, D_MODEL)
    norm_b3 = norm_b.reshape(n_b, 1, D_MODEL)
    mem_q_norm3 = mem_q_norm.reshape(depth, 1, HEAD_DIM)

    mem_kv_f, mem_kv_b = _memkv_prompt(mem_prompt, mem_norm, w_mem_kv, mem_k_norm)
    mem_kv_b = mem_kv_b.reshape(depth, b, MEM_LEN, 2 * D_ATT)
    mem_cache = cache_mem_kv.reshape(depth, nb, MEM_LEN, KV_CHUNKS, HEAD_DIM)
    caches = []
    for g, cache in enumerate((cache_dil0_kv, cache_dil1_kv, cache_dil2_kv)):
        assert cache.shape[1] == WINDOWS[g]
        caches.append(cache.reshape(nb, BAND, DILATIONS[g], KV_CHUNKS, HEAD_DIM))
    state = state_conv.reshape(n_a, -1, HEAD_DIM)

    xp = x_prompt
    xs = x_sample.reshape(nb, D_MODEL)
    conv_p, conv_s = [], []
    for a in range(n_a):
        xp, cp, xs, cs = _a_layer(xp, xs, norm_a3, w_in_a, conv_w_a, mem_kv_b, state, mem_cache,
                                  a, a, mem_q_norm3, w_out_a)
        conv_p.append(cp)
        conv_s.append(cs.reshape(nb, CONV_W - 1, D_CONV))

    kvb0, kvb1, kvb2, st0, st1, st2, kv_new = _kv_shared(xp, xs, kv_norm, w_kv, k_norm_dil)
    kvbs = (kvb0, kvb1, kvb2)
    kv_new = kv_new.reshape(nb, N_GROUPS, KV_CHUNKS, HEAD_DIM)

    for bl in range(n_b):
        layer = n_a + bl
        xp, xs = _b_layer(xp, xs, norm_b3, w_in_b16, q_norm_b, mem_kv_b, kvbs, kv_new, caches,
                          mem_cache, bl, layer, mem_q_norm3, w_out_b16)

    dil_s = [kv_new[:, g].reshape(nb, 1, *kv_dims) for g in range(N_GROUPS)]
    return (xp, xs.reshape(nb, 1, D_MODEL),
            mem_kv_f.reshape(depth, b, MEM_LEN, *kv_dims),
            jnp.stack(conv_p), jnp.stack(conv_s),
            st0.reshape(b, WINDOWS[0], *kv_dims), st1.reshape(b, WINDOWS[1], *kv_dims),
            st2.reshape(b, s, *kv_dims), *dil_s)
```

```python
import functools

import jax
import jax.numpy as jnp
from jax import lax
from jax.experimental import pallas as pl
from jax.experimental.pallas import tpu as pltpu

F32 = jnp.float32
BF16 = jnp.bfloat16

D_MODEL = 1024
HEAD_DIM = 128
N_HEADS = 4
D_ATT = N_HEADS * HEAD_DIM
KV_CHUNKS = 2 * N_HEADS
D_CONV = D_MODEL
CONV_W = 3
N_GROUPS = 3
DILATIONS = (1, 4, 16)
WINDOWS = (128, 512, 2048)
BAND = 128
SUB_STRIDE = 4
MEM_LEN = 256
EPS = 1e-6
NEG = -1e30
SCALE = HEAD_DIM ** -0.5
QK_SCALE = SCALE * 1.4426950408889634

TM = 512
TM_A = 1024
SAMPLE_GB = 8
W_STAGE_ROWS = 128
VMEM_LIMIT = 56 * 1024 * 1024
VMEM_LIMIT_A = 60 * 1024 * 1024


def _rms(x, g):
    return x * lax.rsqrt(jnp.mean(x * x, axis=-1, keepdims=True) + EPS) * g


def _silu(x):
    return x * (1.0 / (1.0 + jnp.exp(-x)))


def _dot(a, b):
    return jnp.dot(a, b, preferred_element_type=F32)


def _dot_nt(a, b):
    return lax.dot_general(a, b, (((1,), (1,)), ((), ())), preferred_element_type=F32)


def _hs(h, base=0):
    return slice(base + h * HEAD_DIM, base + (h + 1) * HEAD_DIM)


def _params(sem, vmem_limit=VMEM_LIMIT):
    return pltpu.CompilerParams(dimension_semantics=sem, vmem_limit_bytes=vmem_limit)


def _fixed_spec(block, index):
    return pl.BlockSpec(block, lambda *_: index, pipeline_mode=pl.Buffered(1))


def _const_spec(shape):
    return _fixed_spec(shape, (0,) * len(shape))


def _layer_spec(shape, layer):
    return _fixed_spec((None,) + tuple(shape), (layer,) + (0,) * len(shape))


def _cast_weight(w_hbm, dst_ref, stage_ref, sem_ref):
    rows = stage_ref.shape[1]
    n_chunks = w_hbm.shape[0] // rows

    def copy(c, slot):
        src = w_hbm.at[pl.ds(pl.multiple_of(c * rows, rows), rows)]
        return pltpu.make_async_copy(src, stage_ref.at[slot], sem_ref.at[slot])

    copy(0, 0).start()

    def body(c, carry):
        slot = c % 2

        @pl.when(c + 1 < n_chunks)
        def _():
            copy(c + 1, 1 - slot).start()

        copy(c, slot).wait()
        dst_ref[pl.ds(pl.multiple_of(c * rows, rows), rows), :] = stage_ref[slot].astype(BF16)
        return carry

    lax.fori_loop(0, n_chunks, body, 0)


def _weight_scratch(shape):
    k, n = shape
    assert k % W_STAGE_ROWS == 0
    return [pltpu.VMEM((k, n), BF16), pltpu.VMEM((2, W_STAGE_ROWS, n), F32),
            pltpu.SemaphoreType.DMA((2,))]


def _memkv_kernel(mem_ref, g_ref, w_ref, kg_ref, of_ref, ob_ref):
    rt = mem_ref.shape[0]
    xn = _rms(mem_ref[...], g_ref[...]).astype(BF16)
    kv = _dot(xn, w_ref[...].astype(BF16))
    for c in range(KV_CHUNKS):
        chunk = kv[:, _hs(c)]
        if c < N_HEADS:
            chunk = _rms(chunk, kg_ref[...])
        of_ref[pl.ds(c, rt, stride=KV_CHUNKS), :] = chunk
        ob_ref[:, _hs(c)] = chunk.astype(BF16)


def _memkv_prompt(mem, mem_norm, w_mem_kv, mem_k_norm):
    depth = w_mem_kv.shape[0]
    rows = mem.shape[0] * mem.shape[1]
    rt = 1024
    return pl.pallas_call(
        _memkv_kernel,
        grid=(depth, rows // rt),
        in_specs=[
            pl.BlockSpec((rt, D_MODEL), lambda l, i: (i, 0)),
            pl.BlockSpec((None, 1, D_MODEL), lambda l, i: (l, 0, 0)),
            pl.BlockSpec((None, D_MODEL, 2 * D_ATT), lambda l, i: (l, 0, 0)),
            pl.BlockSpec((None, 1, HEAD_DIM), lambda l, i: (l, 0, 0)),
        ],
        out_specs=[
            pl.BlockSpec((None, rt * KV_CHUNKS, HEAD_DIM), lambda l, i: (l, i, 0)),
            pl.BlockSpec((None, rt, 2 * D_ATT), lambda l, i: (l, i, 0)),
        ],
        out_shape=[jax.ShapeDtypeStruct((depth, rows * KV_CHUNKS, HEAD_DIM), F32),
                   jax.ShapeDtypeStruct((depth, rows, 2 * D_ATT), BF16)],
        compiler_params=_params(("parallel", "parallel")),
        name="memkv_prompt",
    )(mem.reshape(rows, D_MODEL), mem_norm.reshape(depth, 1, D_MODEL), w_mem_kv,
      mem_k_norm.reshape(depth, 1, HEAD_DIM))


def _pv_and_rowsum(p, v):
    ones = jnp.ones((v.shape[0], HEAD_DIM), BF16)
    both = _dot(p, jnp.concatenate([v, ones], axis=1))
    return both[:, :HEAD_DIM], both[:, HEAD_DIM:]


def _mem_attn(mq, mg, mkv_ref, gq, store):
    for h in range(N_HEADS):
        q = _rms(mq[:, _hs(h)], gq).astype(BF16)
        s = _dot_nt(q, mkv_ref[:, _hs(h)])
        m = jnp.max(s, axis=-1, keepdims=True)
        acc, l = _pv_and_rowsum(jnp.exp2(s - m).astype(BF16), mkv_ref[:, _hs(h, D_ATT)])
        store(h, acc / l * _silu(mg[:, _hs(h)]))


def _step_index():
    return pl.program_id(0) * pl.num_programs(1) + pl.program_id(1)


def _is_last_step():
    return _step_index() == pl.num_programs(0) * pl.num_programs(1) - 1


def _a_layer_kernel(x_ref, g_ref, win_hbm, cw_ref, mkv_ref, gq_ref, wout_hbm,
                    xs_ref, st_ref, mc_ref,
                    y_ref, cs_ref, ys_ref, sto_ref,
                    ext_ref, cat_ref, zs_ref, cats_ref,
                    win_ref, win_stage, win_sem, wout_ref, wout_stage, wout_sem, *, a):
    t = pl.program_id(1)
    n = _step_index()

    @pl.when(t == 0)
    def _():
        ext_ref[0:8, :] = jnp.zeros((8, D_CONV), F32)

    @pl.when(n == 0)
    def _():
        _cast_weight(win_hbm.at[a], win_ref, win_stage, win_sem)
        _cast_weight(wout_hbm.at[a], wout_ref, wout_stage, wout_sem)
        _a_sample_first(xs_ref, g_ref, win_ref, cw_ref, st_ref, sto_ref, zs_ref, cats_ref)

    per_step = mc_ref.shape[0]
    for j in range(per_step):
        ns = n * per_step + j
        row = zs_ref[pl.ds(ns, 1), :]
        _merge_row(cats_ref, slice(D_CONV, D_CONV + D_ATT), ns,
                   _sample_mem_attn(row[:, :D_ATT], gq_ref[...], mc_ref[j], row[:, D_ATT:]))

    tm = x_ref.shape[0]
    x = x_ref[...]
    xn = _rms(x, g_ref[...]).astype(BF16)

    def proj(lo, hi):
        return _dot(xn, win_ref[:, lo:hi])

    u = proj(D_CONV, 2 * D_CONV) * proj(0, D_CONV)
    ext_ref[8:8 + tm, :] = u
    y = (cw_ref[0:1, :] * ext_ref[6:6 + tm, :] + cw_ref[1:2, :] * ext_ref[7:7 + tm, :]
         + cw_ref[2:3, :] * u)
    ext_ref[0:8, :] = ext_ref[tm:tm + 8, :]
    cs_ref[...] = ext_ref[6:8, :]
    bg = proj(2 * D_CONV, 3 * D_CONV)
    g = proj(3 * D_CONV, 4 * D_CONV)
    cat_ref[:, 0:D_CONV] = (_silu(g) * bg * y).astype(BF16)
    mq = proj(4 * D_CONV, 4 * D_CONV + D_ATT)
    mg = proj(4 * D_CONV + D_ATT, 4 * D_CONV + 2 * D_ATT)

    def store(h, val):
        cat_ref[:, _hs(h, D_CONV)] = val.astype(BF16)

    _mem_attn(mq, mg, mkv_ref, gq_ref[...] * QK_SCALE, store)
    y_ref[...] = x + _dot(cat_ref[...], wout_ref[...])

    @pl.when(_is_last_step())
    def _():
        ys_ref[...] = xs_ref[...] + _dot(cats_ref[...].astype(BF16), wout_ref[...])


def _sample_specs(nb, nt, layer):
    full = lambda shape: pl.BlockSpec(shape, lambda i, t: (0,) * len(shape))
    mem_cache = pl.BlockSpec((None, 1, MEM_LEN, KV_CHUNKS, HEAD_DIM),
                             lambda i, t: (layer, i * nt + t, 0, 0, 0))
    return full, mem_cache


def _a_layer(x, xs, norm_a, w_in, conv_w, mkv_b, state, mem_cache, a, layer, mem_q_norm, w_out):
    b, s, _ = x.shape
    nb = xs.shape[0]
    nt = s // TM_A
    assert nb % (b * nt) == 0
    per_step = nb // (b * nt)
    st_rows = nb * (CONV_W - 1) * D_CONV // HEAD_DIM
    full, _ = _sample_specs(nb, nt, layer)
    mem_cache_spec = pl.BlockSpec((None, per_step, MEM_LEN, KV_CHUNKS, HEAD_DIM),
                                  lambda i, t: (layer, i * nt + t, 0, 0, 0))
    return pl.pallas_call(
        functools.partial(_a_layer_kernel, a=a),
        grid=(b, nt),
        in_specs=[
            pl.BlockSpec((None, TM_A, D_MODEL), lambda i, t: (i, t, 0)),
            _layer_spec((1, D_MODEL), a),
            pl.BlockSpec(memory_space=pl.ANY),
            _layer_spec((CONV_W, D_CONV), a),
            pl.BlockSpec((None, None, MEM_LEN, 2 * D_ATT), lambda i, t: (layer, i, 0, 0)),
            _layer_spec((1, HEAD_DIM), layer),
            pl.BlockSpec(memory_space=pl.ANY),
            full((nb, D_MODEL)),
            _layer_spec((st_rows, HEAD_DIM), a),
            mem_cache_spec,
        ],
        out_specs=[
            pl.BlockSpec((None, TM_A, D_MODEL), lambda i, t: (i, t, 0)),
            pl.BlockSpec((None, CONV_W - 1, D_CONV), lambda i, t: (i, 0, 0)),
            full((nb, D_MODEL)),
            full((st_rows, HEAD_DIM)),
        ],
        out_shape=[jax.ShapeDtypeStruct((b, s, D_MODEL), F32),
                   jax.ShapeDtypeStruct((b, CONV_W - 1, D_CONV), F32),
                   jax.ShapeDtypeStruct((nb, D_MODEL), F32),
                   jax.ShapeDtypeStruct((st_rows, HEAD_DIM), F32)],
        scratch_shapes=[pltpu.VMEM((TM_A + 8, D_CONV), F32),
                        pltpu.VMEM((TM_A, D_CONV + D_ATT), BF16),
                        pltpu.VMEM((nb, 2 * D_ATT), F32),
                        pltpu.VMEM((nb, D_CONV + D_ATT), F32)]
        + _weight_scratch(w_in.shape[1:]) + _weight_scratch(w_out.shape[1:]),
        compiler_params=_params(("arbitrary", "arbitrary"), VMEM_LIMIT_A),
        name=f"a_layer_{layer}",
    )(x, norm_a, w_in, conv_w, mkv_b, mem_q_norm, w_out, xs, state, mem_cache)


def _residue_rows(slab_ref, split_ref, c, d):
    rows = slab_ref.shape[1]
    if d == SUB_STRIDE:
        return lambda r: slab_ref[c, pl.ds(r, rows // d, stride=d), :]
    assert d == SUB_STRIDE * SUB_STRIDE
    for r1 in range(SUB_STRIDE):
        split_ref[c, r1] = slab_ref[c, pl.ds(r1, rows // SUB_STRIDE, stride=SUB_STRIDE), :]
    return lambda r: split_ref[c, r % SUB_STRIDE,
                               pl.ds(r // SUB_STRIDE, rows // d, stride=SUB_STRIDE), :]


def _to_residue_major(dst_ref, slab_ref, split_ref, n_chunks, d):
    for c in range(n_chunks):
        get = _residue_rows(slab_ref, split_ref, c, d)
        for r in range(d):
            dst_ref[r, :, _hs(c)] = get(r).astype(dst_ref.dtype)


def _kv_kernel(x_ref, g_ref, w_hbm, kn_ref, xs_ref, kvb0_ref, kvb1_ref, kvb2_ref,
               st0_ref, st1_ref, st2_ref, kvn_ref, slab1_ref, slab2_ref, split_ref,
               w_ref, w_stage, w_sem):
    t = pl.program_id(1)
    last = pl.num_programs(1) - 1

    @pl.when(_step_index() == 0)
    def _():
        _cast_weight(w_hbm, w_ref, w_stage, w_sem)
        _kv_sample_rows(xs_ref, g_ref, w_ref, kn_ref, kvn_ref)

    xn = _rms(x_ref[...], g_ref[...]).astype(BF16)
    kvb_refs = (kvb0_ref, kvb1_ref, kvb2_ref)
    slabs = (None, slab1_ref, slab2_ref)
    for g in reversed(range(N_GROUPS)):
        kv = _dot(xn, w_ref[:, g * 2 * D_ATT:(g + 1) * 2 * D_ATT])
        chunks = [_rms(kv[:, _hs(h)], kn_ref[g:g + 1, :]) for h in range(N_HEADS)]
        chunks += [kv[:, _hs(h, D_ATT)] for h in range(N_HEADS)]
        d = DILATIONS[g]
        for c, chunk in enumerate(chunks):
            if d == 1:
                kvb_refs[g][0, :, _hs(c)] = chunk.astype(BF16)
            else:
                slabs[g][c] = chunk
            if g == 2:
                st2_ref[pl.ds(c, TM, stride=KV_CHUNKS), :] = chunk
        if d > 1:
            _to_residue_major(kvb_refs[g], slabs[g], split_ref, KV_CHUNKS, d)
        if g == 1:
            @pl.when(t == last)
            def _():
                for c, chunk in enumerate(chunks):
                    st1_ref[pl.ds(c, TM, stride=KV_CHUNKS), :] = chunk
        if g == 0:
            @pl.when(t == last)
            def _():
                for c, chunk in enumerate(chunks):
                    st0_ref[pl.ds(c, WINDOWS[0], stride=KV_CHUNKS), :] = chunk[TM - WINDOWS[0]:, :]


def _kv_shared(x, xs, kv_norm, w_kv, k_norm_dil):
    b, s, _ = x.shape
    nb = xs.shape[0]
    kvn_rows = nb * N_GROUPS * KV_CHUNKS
    assert WINDOWS[1] == TM and WINDOWS[2] == s and WINDOWS[0] <= TM
    first = lambda i, t: (i, 0, 0)
    res_spec = lambda d: pl.BlockSpec((None, d, TM // d, 2 * D_ATT), lambda i, t: (i, 0, t, 0))
    res_sds = lambda d: jax.ShapeDtypeStruct((b, d, s // d, 2 * D_ATT), BF16)
    st_sds = lambda rows: jax.ShapeDtypeStruct((b, rows * KV_CHUNKS, HEAD_DIM), F32)
    return pl.pallas_call(
        _kv_kernel,
        grid=(b, s // TM),
        in_specs=[
            pl.BlockSpec((None, TM, D_MODEL), lambda i, t: (i, t, 0)),
            _const_spec((1, D_MODEL)),
            pl.BlockSpec(memory_space=pl.ANY),
            _const_spec((N_GROUPS, HEAD_DIM)),
            _const_spec((nb, D_MODEL)),
        ],
        out_specs=[
            res_spec(DILATIONS[0]), res_spec(DILATIONS[1]), res_spec(DILATIONS[2]),
            pl.BlockSpec((None, WINDOWS[0] * KV_CHUNKS, HEAD_DIM), first),
            pl.BlockSpec((None, WINDOWS[1] * KV_CHUNKS, HEAD_DIM), first),
            pl.BlockSpec((None, TM * KV_CHUNKS, HEAD_DIM), lambda i, t: (i, t, 0)),
            pl.BlockSpec((kvn_rows, HEAD_DIM), lambda i, t: (0, 0)),
        ],
        out_shape=[res_sds(DILATIONS[0]), res_sds(DILATIONS[1]), res_sds(DILATIONS[2]),
                   st_sds(WINDOWS[0]), st_sds(WINDOWS[1]), st_sds(s),
                   jax.ShapeDtypeStruct((kvn_rows, HEAD_DIM), F32)],
        scratch_shapes=[pltpu.VMEM((KV_CHUNKS, TM, HEAD_DIM), F32),
                        pltpu.VMEM((KV_CHUNKS, TM, HEAD_DIM), F32),
                        pltpu.VMEM((KV_CHUNKS, SUB_STRIDE, TM // SUB_STRIDE, HEAD_DIM), F32)]
        + _weight_scratch(w_kv.shape),
        compiler_params=_params(("arbitrary", "arbitrary")),
        name="kv_prompt",
    )(x, kv_norm.reshape(1, D_MODEL), w_kv, k_norm_dil, xs)


def _band_scores(qs, ks, masks):
    return jnp.concatenate([jnp.where(mk, _dot_nt(q, k), NEG) for q, k, mk in zip(qs, ks, masks)],
                           axis=0)


def _band_finish(s, vs, nq):
    m = jnp.max(s, axis=-1, keepdims=True)
    pb = jnp.exp2(s - m).astype(BF16)
    res = []
    for i, v in enumerate(vs):
        rows = slice(i * nq, (i + 1) * nq)
        acc, l = _pv_and_rowsum(pb[rows], v)
        res.append((acc * (1.0 / l), m[rows] + jnp.log2(l)))
    return res


def _b_layer_kernel(x_ref, g_ref, win_ref, qn_ref, mkv_ref, gq_ref, wout_ref,
                    kv0_ref, kv0p_ref, kv1_ref, kv1p_ref, kv2_ref,
                    xs_ref, kvn_ref, c0_ref, c1_ref, c2_ref, mc_ref,
                    y_ref, ys_ref,
                    qslab1_ref, qslab2_ref, qsplit_ref, oslab_ref, lslab_ref, cat_ref,
                    zs_ref, cats_ref):
    t = pl.program_id(1)
    n = _step_index()

    @pl.when(n == 0)
    def _():
        zs_ref[...] = _dot(_rms(xs_ref[...], g_ref[...]).astype(BF16), win_ref[...])
        cats_ref[...] = jnp.zeros(cats_ref.shape, F32)

    _merge_row(cats_ref, slice(0, 2 * D_ATT), n,
               _b_sample_row(zs_ref[pl.ds(n, 1), :], kvn_ref[n], qn_ref, gq_ref[...],
                             (c0_ref[0], c1_ref[0], c2_ref[0]), mc_ref[0]))

    x = x_ref[...]
    xn = _rms(x, g_ref[...]).astype(BF16)

    def proj(lo, hi):
        return _dot(xn, win_ref[:, lo:hi])

    qi = lax.broadcasted_iota(jnp.int32, (BAND, 2 * BAND), 0)
    kj = lax.broadcasted_iota(jnp.int32, (BAND, 2 * BAND), 1)
    mask_rest = (kj >= qi) & (kj <= qi + BAND)
    mask_edge = (kj >= jnp.maximum(qi, jnp.where(t == 0, BAND, 0))) & (kj <= qi + BAND)

    def finish(g, h, rows, out, lse):
        oslab_ref[g, h, rows, :] = out
        lslab_ref[g, h, rows, :] = lse

    def normed_heads(g):
        qd = proj(g * D_ATT, (g + 1) * D_ATT)
        gain = qn_ref[g:g + 1, :] * QK_SCALE
        return [_rms(qd[:, _hs(h)], gain) for h in range(N_HEADS)]

    n_sub = TM // BAND
    qh = [q.astype(BF16) for q in normed_heads(0)]
    units0 = [(i, h) for i in range(n_sub) for h in range(N_HEADS)]

    def keys0(i, cols):
        if i == 0:
            return jnp.concatenate([kv0p_ref[0, :, cols], kv0_ref[0, 0:BAND, cols]], axis=0)
        return kv0_ref[0, (i - 1) * BAND:(i + 1) * BAND, cols]

    d1 = DILATIONS[1]
    for h, q in enumerate(normed_heads(1)):
        qslab1_ref[h] = q
    get_q1 = [_residue_rows(qslab1_ref, None, h, d1) for h in range(N_HEADS)]
    units1 = [(r, h) for r in range(d1) for h in range(N_HEADS)]

    def keys1(r, cols):
        return jnp.concatenate([kv1p_ref[r, :, cols], kv1_ref[r, :, cols]], axis=0)

    d2 = DILATIONS[2]
    nq = TM // d2
    for h, q in enumerate(normed_heads(2)):
        qslab2_ref[h] = q
    get_q2 = [_residue_rows(qslab2_ref, qsplit_ref, h, d2) for h in range(N_HEADS)]
    units2 = [(r, h) for r in range(d2) for h in range(N_HEADS)]
    qi2 = lax.broadcasted_iota(jnp.int32, (nq, BAND), 0)
    kj2 = lax.broadcasted_iota(jnp.int32, (nq, BAND), 1)
    mask2 = kj2 <= qi2 + t * nq

    s0 = _band_scores([qh[h][i * BAND:(i + 1) * BAND] for i, h in units0],
                      [keys0(i, _hs(h)) for i, h in units0],
                      [mask_edge if i == 0 else mask_rest for i, h in units0])
    s1 = _band_scores([get_q1[h](r).astype(BF16) for r, h in units1],
                      [keys1(r, _hs(h)) for r, h in units1], [mask_edge] * len(units1))
    s2 = _band_scores([get_q2[h](r).astype(BF16) for r, h in units2],
                      [kv2_ref[r, :, _hs(h)] for r, h in units2], [mask2] * len(units2))
    res = _band_finish(s0, [keys0(i, _hs(h, D_ATT)) for i, h in units0], BAND)
    for (i, h), (out, lse) in zip(units0, res):
        finish(0, h, pl.ds(i * BAND, BAND), out, lse)
    res = _band_finish(s1, [keys1(r, _hs(h, D_ATT)) for r, h in units1], BAND)
    for (r, h), (out, lse) in zip(units1, res):
        finish(1, h, pl.ds(r, BAND, stride=d1), out, lse)
    res = _band_finish(s2, [kv2_ref[r, :, _hs(h, D_ATT)] for r, h in units2], nq)
    for (r, h), (out, lse) in zip(units2, res):
        finish(2, h, pl.ds(r, nq, stride=d2), out, lse)

    base = N_GROUPS * D_ATT
    gd = proj(base, base + D_ATT)
    for h in range(N_HEADS):
        l0, l1, l2 = lslab_ref[0, h], lslab_ref[1, h], lslab_ref[2, h]
        m = jnp.maximum(jnp.maximum(l0, l1), l2)
        e0, e1, e2 = jnp.exp2(l0 - m), jnp.exp2(l1 - m), jnp.exp2(l2 - m)
        dil = ((oslab_ref[0, h] * e0 + oslab_ref[1, h] * e1 + oslab_ref[2, h] * e2)
               * (1.0 / (e0 + e1 + e2)))
        cat_ref[:, _hs(h)] = (dil * _silu(gd[:, _hs(h)])).astype(BF16)
    mq = proj(base + D_ATT, base + 2 * D_ATT)
    mg = proj(base + 2 * D_ATT, base + 3 * D_ATT)

    def store(h, val):
        cat_ref[:, _hs(h, D_ATT)] = val.astype(BF16)

    _mem_attn(mq, mg, mkv_ref, gq_ref[...] * QK_SCALE, store)
    y_ref[...] = x + _dot(cat_ref[...], wout_ref[...])

    @pl.when(_is_last_step())
    def _():
        ys_ref[...] = xs_ref[...] + _dot(cats_ref[...].astype(BF16), wout_ref[...])


def _b_layer(x, xs, norm_b, w_in, q_norm, mkv_b, kvbs, kv_new, caches, mem_cache, bl, layer,
             mem_q_norm, w_out):
    b, s, _ = x.shape
    nb = xs.shape[0]
    nt = s // TM
    assert nb == b * nt
    full, mem_cache_spec = _sample_specs(nb, nt, layer)
    cache_spec = pl.BlockSpec((1, BAND, None, KV_CHUNKS, HEAD_DIM),
                              lambda i, t: (i * nt + t, 0, 0, 0, 0))
    assert TM // BAND == DILATIONS[1] and s // DILATIONS[2] == BAND
    tile = lambda i, t: (i, t, 0)
    sub = TM // BAND
    kv_w = 2 * D_ATT
    return pl.pallas_call(
        _b_layer_kernel,
        grid=(b, s // TM),
        in_specs=[
            pl.BlockSpec((None, TM, D_MODEL), tile),
            _layer_spec((1, D_MODEL), bl),
            _layer_spec(w_in.shape[1:], bl),
            _layer_spec((N_GROUPS, HEAD_DIM), bl),
            pl.BlockSpec((None, None, MEM_LEN, kv_w), lambda i, t: (layer, i, 0, 0)),
            _layer_spec((1, HEAD_DIM), layer),
            _layer_spec(w_out.shape[1:], bl),
            pl.BlockSpec((None, 1, TM, kv_w), lambda i, t: (i, 0, t, 0)),
            pl.BlockSpec((None, 1, BAND, kv_w),
                         lambda i, t: (i, 0, jnp.maximum(t * sub - 1, 0), 0)),
            pl.BlockSpec((None, DILATIONS[1], BAND, kv_w), lambda i, t: (i, 0, t, 0)),
            pl.BlockSpec((None, DILATIONS[1], BAND, kv_w),
                         lambda i, t: (i, 0, jnp.maximum(t - 1, 0), 0)),
            pl.BlockSpec((None, DILATIONS[2], BAND, kv_w), lambda i, t: (i, 0, 0, 0)),
            full((nb, D_MODEL)),
            full(kv_new.shape),
            cache_spec, cache_spec, cache_spec,
            mem_cache_spec,
        ],
        out_specs=[pl.BlockSpec((None, TM, D_MODEL), tile), full((nb, D_MODEL))],
        out_shape=[jax.ShapeDtypeStruct((b, s, D_MODEL), F32),
                   jax.ShapeDtypeStruct((nb, D_MODEL), F32)],
        scratch_shapes=[pltpu.VMEM((N_HEADS, TM, HEAD_DIM), F32),
                        pltpu.VMEM((N_HEADS, TM, HEAD_DIM), F32),
                        pltpu.VMEM((N_HEADS, SUB_STRIDE, TM // SUB_STRIDE, HEAD_DIM), F32),
                        pltpu.VMEM((N_GROUPS, N_HEADS, TM, HEAD_DIM), F32),
                        pltpu.VMEM((N_GROUPS, N_HEADS, TM, HEAD_DIM), F32),
                        pltpu.VMEM((TM, 2 * D_ATT), BF16),
                        pltpu.VMEM((nb, w_in.shape[2]), F32),
                        pltpu.VMEM((nb, 2 * D_ATT), F32)],
        compiler_params=_params(("arbitrary", "arbitrary")),
        name=f"b_layer_{layer}",
    )(x, norm_b, w_in, q_norm, mkv_b, mem_q_norm, w_out,
      kvbs[0], kvbs[0], kvbs[1], kvbs[1], kvbs[2],
      xs, kv_new, *caches, mem_cache)


def _swap_halves(x, axis):
    return pltpu.roll(x, N_HEADS, axis=axis)


def _head_rows(tile):
    return jnp.concatenate([tile[N_HEADS + h:N_HEADS + h + 1, :] for h in range(N_HEADS)],
                           axis=-1)


def _low_tile(row):
    return jnp.concatenate([jnp.zeros((N_HEADS, HEAD_DIM), F32)]
                           + [row[:, _hs(h)] for h in range(N_HEADS)], axis=0)


def _q_tile(row, g):
    return jnp.concatenate([_rms(row[:, _hs(h)], g) * QK_SCALE for h in range(N_HEADS)]
                           + [jnp.zeros((N_HEADS, HEAD_DIM), F32)], axis=0)


def _tile_attend(qt, kv):
    s = jnp.sum(kv * qt[None], axis=-1, keepdims=True)
    m = jnp.max(s, axis=0)
    p = jnp.exp2(s - m[None])
    l = jnp.sum(p, axis=0)
    acc = jnp.sum(_swap_halves(jnp.broadcast_to(p, kv.shape), 1) * kv, axis=0)
    out = acc / _swap_halves(jnp.broadcast_to(l, acc.shape), 0)
    return out, m + jnp.log2(l)


def _sample_mem_attn(q_row, gq, mkv, mg_row):
    out, _ = _tile_attend(_q_tile(q_row, gq), mkv)
    return _head_rows(out * _low_tile(_silu(mg_row)))


def _gather_chunks(ref, n_rows, n_chunks, offset=0):
    total = ref.shape[0] // n_rows
    return jnp.concatenate([ref[pl.ds(offset + c, n_rows, stride=total), :]
                            for c in range(n_chunks)], axis=-1)


def _merge_row(ref, cols, n, row):
    tile = pl.ds(pl.multiple_of((n // SAMPLE_GB) * SAMPLE_GB, SAMPLE_GB), SAMPLE_GB)
    sub = lax.broadcasted_iota(jnp.int32, (SAMPLE_GB, row.shape[1]), 0)
    ref[tile, cols] = jnp.where(sub == n % SAMPLE_GB, row, ref[tile, cols])


def _a_sample_first(x_ref, g_ref, win_ref, cw_ref, st_ref, sto_ref, zm_ref, cat_ref):
    nb = x_ref.shape[0]
    n_st = D_CONV // HEAD_DIM
    xn = _rms(x_ref[...], g_ref[...]).astype(BF16)
    z = _dot(xn, win_ref[...])
    u = z[:, D_CONV:2 * D_CONV] * z[:, 0:D_CONV]
    s0 = _gather_chunks(st_ref, nb, n_st)
    s1 = _gather_chunks(st_ref, nb, n_st, n_st)
    y = cw_ref[0:1, :] * s0 + cw_ref[1:2, :] * s1 + cw_ref[2:3, :] * u
    for c in range(n_st):
        sto_ref[pl.ds(c, nb, stride=2 * n_st), :] = s1[:, _hs(c)]
        sto_ref[pl.ds(n_st + c, nb, stride=2 * n_st), :] = u[:, _hs(c)]
    cat_ref[:, 0:D_CONV] = _silu(z[:, 3 * D_CONV:4 * D_CONV]) * z[:, 2 * D_CONV:3 * D_CONV] * y
    cat_ref[:, D_CONV:] = jnp.zeros((nb, D_ATT), F32)
    zm_ref[...] = z[:, 4 * D_CONV:]


def _kv_sample_rows(x_ref, g_ref, w_ref, kn_ref, o_ref):
    nb = x_ref.shape[0]
    xn = _rms(x_ref[...], g_ref[...]).astype(BF16)
    kv = _dot(xn, w_ref[...])
    n_chunks = N_GROUPS * KV_CHUNKS
    for g in range(N_GROUPS):
        for c in range(KV_CHUNKS):
            chunk = kv[:, _hs(c, g * 2 * D_ATT)]
            if c < N_HEADS:
                chunk = _rms(chunk, kn_ref[g:g + 1, :])
            o_ref[pl.ds(g * KV_CHUNKS + c, nb, stride=n_chunks), :] = chunk


def _b_sample_row(row, kv_new, qn_ref, gq, caches, mkv):
    base = N_GROUPS * D_ATT
    outs, lses = [], []
    for g in range(N_GROUPS):
        qt = _q_tile(row[:, g * D_ATT:(g + 1) * D_ATT], qn_ref[g:g + 1, :])
        out, lse = _tile_attend(qt, jnp.concatenate([caches[g], kv_new[g][None]], axis=0))
        outs.append(out)
        lses.append(lse)
    m = jnp.maximum(jnp.maximum(lses[0], lses[1]), lses[2])
    es = [jnp.exp2(l - m) for l in lses]
    den = es[0] + es[1] + es[2]
    ws = [_swap_halves(jnp.broadcast_to(e / den, (KV_CHUNKS, HEAD_DIM)), 0) for e in es]
    dil = outs[0] * ws[0] + outs[1] * ws[1] + outs[2] * ws[2]
    dil_row = _head_rows(dil * _low_tile(_silu(row[:, base:base + D_ATT])))
    mem_row = _sample_mem_attn(row[:, base + D_ATT:base + 2 * D_ATT], gq, mkv,
                               row[:, base + 2 * D_ATT:])
    return jnp.concatenate([dil_row, mem_row], axis=-1)


def kernel(x_prompt, x_sample, cache_mem_kv, state_conv, cache_dil0_kv, cache_dil1_kv, cache_dil2_kv, mem_prompt, norm_a, w_in_a, conv_w_a, w_out_a, norm_b, w_in_b, q_norm_b, w_out_b, kv_norm, w_kv, k_norm_dil, mem_norm, w_mem_kv, mem_k_norm, mem_q_norm):
    n_a = w_in_a.shape[0]
    n_b = w_in_b.shape[0]
    depth = n_a + n_b
    b, s, _ = x_prompt.shape
    nb = x_sample.shape[0]
    assert x_sample.shape[1] == 1 and s % TM == 0 and s % TM_A == 0 and nb % SAMPLE_GB == 0
    kv_dims = (2, N_HEADS, HEAD_DIM)

    w_in_b16, w_out_b16 = w_in_b.astype(BF16), w_out_b.astype(BF16)
    norm_a3 = norm_a.reshape(n_a, 1, D_MODEL)
    norm_b3 = norm_b.reshape(n_b, 1, D_MODEL)
    mem_q_norm3 = mem_q_norm.reshape(depth, 1, HEAD_DIM)

    mem_kv_f, mem_kv_b = _memkv_prompt(mem_prompt, mem_norm, w_mem_kv, mem_k_norm)
    mem_kv_b = mem_kv_b.reshape(depth, b, MEM_LEN, 2 * D_ATT)
    mem_cache = cache_mem_kv.reshape(depth, nb, MEM_LEN, KV_CHUNKS, HEAD_DIM)
    caches = []
    for g, cache in enumerate((cache_dil0_kv, cache_dil1_kv, cache_dil2_kv)):
        assert cache.shape[1] == WINDOWS[g]
        caches.append(cache.reshape(nb, BAND, DILATIONS[g], KV_CHUNKS, HEAD_DIM))
    state = state_conv.reshape(n_a, -1, HEAD_DIM)

    xp = x_prompt
    xs = x_sample.reshape(nb, D_MODEL)
    conv_p, conv_s = [], []
    for a in range(n_a):
        xp, cp, xs, cs = _a_layer(xp, xs, norm_a3, w_in_a, conv_w_a, mem_kv_b, state, mem_cache,
                                  a, a, mem_q_norm3, w_out_a)
        conv_p.append(cp)
        conv_s.append(cs.reshape(nb, CONV_W - 1, D_CONV))

    kvb0, kvb1, kvb2, st0, st1, st2, kv_new = _kv_shared(xp, xs, kv_norm, w_kv, k_norm_dil)
    kvbs = (kvb0, kvb1, kvb2)
    kv_new = kv_new.reshape(nb, N_GROUPS, KV_CHUNKS, HEAD_DIM)

    for bl in range(n_b):
        layer = n_a + bl
        xp, xs = _b_layer(xp, xs, norm_b3, w_in_b16, q_norm_b, mem_kv_b, kvbs, kv_new, caches,
                          mem_cache, bl, layer, mem_q_norm3, w_out_b16)

    dil_s = [kv_new[:, g].reshape(nb, 1, *kv_dims) for g in range(N_GROUPS)]
    return (xp, xs.reshape(nb, 1, D_MODEL),
            mem_kv_f.reshape(depth, b, MEM_LEN, *kv_dims),
            jnp.stack(conv_p), jnp.stack(conv_s),
            st0.reshape(b, WINDOWS[0], *kv_dims), st1.reshape(b, WINDOWS[1], *kv_dims),
            st2.reshape(b, s, *kv_dims), *dil_s)
```

```python
import functools

import jax
import jax.numpy as jnp
from jax import lax
from jax.experimental import pallas as pl
from jax.experimental.pallas import tpu as pltpu

F32 = jnp.float32
BF16 = jnp.bfloat16

D_MODEL = 1024
HEAD_DIM = 128
N_HEADS = 4
D_ATT = N_HEADS * HEAD_DIM
KV_CHUNKS = 2 * N_HEADS
D_CONV = D_MODEL
CONV_W = 3
N_GROUPS = 3
DILATIONS = (1, 4, 16)
WINDOWS = (128, 512, 2048)
BAND = 128
SUB_STRIDE = 4
MEM_LEN = 256
EPS = 1e-6
NEG = -1e30
SCALE = HEAD_DIM ** -0.5
QK_SCALE = SCALE * 1.4426950408889634

TM = 512
TM_A = 512
SAMPLE_GB = 8
W_STAGE_ROWS = 256
VMEM_LIMIT = 56 * 1024 * 1024
VMEM_LIMIT_A = VMEM_LIMIT
VMEM_LIMIT_KV = 60 * 1024 * 1024


def _rms(x, g):
    return x * lax.rsqrt(jnp.mean(x * x, axis=-1, keepdims=True) + EPS) * g


def _silu(x):
    return x * (1.0 / (1.0 + jnp.exp(-x)))


def _dot(a, b):
    return jnp.dot(a, b, preferred_element_type=F32)


def _dot_nt(a, b):
    return lax.dot_general(a, b, (((1,), (1,)), ((), ())), preferred_element_type=F32)


def _hs(h, base=0):
    return slice(base + h * HEAD_DIM, base + (h + 1) * HEAD_DIM)


def _params(sem, vmem_limit=VMEM_LIMIT):
    return pltpu.CompilerParams(dimension_semantics=sem, vmem_limit_bytes=vmem_limit)


def _fixed_spec(block, index):
    return pl.BlockSpec(block, lambda *_: index, pipeline_mode=pl.Buffered(1))


def _const_spec(shape):
    return _fixed_spec(shape, (0,) * len(shape))


def _layer_spec(shape, layer):
    return _fixed_spec((None,) + tuple(shape), (layer,) + (0,) * len(shape))


def _cast_weight(w_hbm, dst_ref, stage_ref, sem_ref):
    rows = stage_ref.shape[1]
    n_chunks = w_hbm.shape[0] // rows

    def copy(c, slot):
        src = w_hbm.at[pl.ds(pl.multiple_of(c * rows, rows), rows)]
        return pltpu.make_async_copy(src, stage_ref.at[slot], sem_ref.at[slot])

    copy(0, 0).start()

    def body(c, carry):
        slot = c % 2

        @pl.when(c + 1 < n_chunks)
        def _():
            copy(c + 1, 1 - slot).start()

        copy(c, slot).wait()
        dst_ref[pl.ds(pl.multiple_of(c * rows, rows), rows), :] = stage_ref[slot].astype(BF16)
        return carry

    lax.fori_loop(0, n_chunks, body, 0)


def _weight_scratch(shape):
    k, n = shape
    assert k % W_STAGE_ROWS == 0
    return [pltpu.VMEM((k, n), BF16), pltpu.VMEM((2, W_STAGE_ROWS, n), F32),
            pltpu.SemaphoreType.DMA((2,))]


def _memkv_kernel(mem_ref, g_ref, w_ref, kg_ref, of_ref, ob_ref):
    rt = mem_ref.shape[0]
    xn = _rms(mem_ref[...], g_ref[...]).astype(BF16)
    kv = _dot(xn, w_ref[...].astype(BF16))
    for c in range(KV_CHUNKS):
        chunk = kv[:, _hs(c)]
        if c < N_HEADS:
            chunk = _rms(chunk, kg_ref[...])
        of_ref[pl.ds(c, rt, stride=KV_CHUNKS), :] = chunk
        ob_ref[:, _hs(c)] = chunk.astype(BF16)


def _memkv_prompt(mem, mem_norm, w_mem_kv, mem_k_norm):
    depth = w_mem_kv.shape[0]
    rows = mem.shape[0] * mem.shape[1]
    rt = 1024
    return pl.pallas_call(
        _memkv_kernel,
        grid=(depth, rows // rt),
        in_specs=[
            pl.BlockSpec((rt, D_MODEL), lambda l, i: (i, 0)),
            pl.BlockSpec((None, 1, D_MODEL), lambda l, i: (l, 0, 0)),
            pl.BlockSpec((None, D_MODEL, 2 * D_ATT), lambda l, i: (l, 0, 0)),
            pl.BlockSpec((None, 1, HEAD_DIM), lambda l, i: (l, 0, 0)),
        ],
        out_specs=[
            pl.BlockSpec((None, rt * KV_CHUNKS, HEAD_DIM), lambda l, i: (l, i, 0)),
            pl.BlockSpec((None, rt, 2 * D_ATT), lambda l, i: (l, i, 0)),
        ],
        out_shape=[jax.ShapeDtypeStruct((depth, rows * KV_CHUNKS, HEAD_DIM), F32),
                   jax.ShapeDtypeStruct((depth, rows, 2 * D_ATT), BF16)],
        compiler_params=_params(("parallel", "parallel")),
        name="memkv_prompt",
    )(mem.reshape(rows, D_MODEL), mem_norm.reshape(depth, 1, D_MODEL), w_mem_kv,
      mem_k_norm.reshape(depth, 1, HEAD_DIM))


def _pv_and_rowsum(p, v):
    ones = jnp.ones((v.shape[0], HEAD_DIM), BF16)
    both = _dot(p, jnp.concatenate([v, ones], axis=1))
    return both[:, :HEAD_DIM], both[:, HEAD_DIM:]


def _mem_attn(mq, mg, mkv_ref, gq, store):
    for h in range(N_HEADS):
        q = _rms(mq[:, _hs(h)], gq).astype(BF16)
        s = _dot_nt(q, mkv_ref[:, _hs(h)])
        m = jnp.max(s, axis=-1, keepdims=True)
        acc, l = _pv_and_rowsum(jnp.exp2(s - m).astype(BF16), mkv_ref[:, _hs(h, D_ATT)])
        store(h, acc / l * _silu(mg[:, _hs(h)]))


def _step_index():
    return pl.program_id(0) * pl.num_programs(1) + pl.program_id(1)


def _is_last_step():
    return _step_index() == pl.num_programs(0) * pl.num_programs(1) - 1


def _a_layer_kernel(x_ref, g_ref, win_hbm, cw_ref, mkv_ref, gq_ref, wout_hbm,
                    xs_ref, st_ref, mc_ref,
                    y_ref, cs_ref, ys_ref, sto_ref,
                    ext_ref, cat_ref, zs_ref, cats_ref,
                    win_ref, win_stage, win_sem, wout_ref, wout_stage, wout_sem, *, a):
    t = pl.program_id(1)
    n = _step_index()

    @pl.when(t == 0)
    def _():
        ext_ref[0:8, :] = jnp.zeros((8, D_CONV), F32)

    @pl.when(n == 0)
    def _():
        _cast_weight(win_hbm.at[a], win_ref, win_stage, win_sem)
        _cast_weight(wout_hbm.at[a], wout_ref, wout_stage, wout_sem)
        _a_sample_first(xs_ref, g_ref, win_ref, cw_ref, st_ref, sto_ref, zs_ref, cats_ref)

    per_step = mc_ref.shape[0]
    for j in range(per_step):
        ns = n * per_step + j
        row = zs_ref[pl.ds(ns, 1), :]
        _merge_row(cats_ref, slice(D_CONV, D_CONV + D_ATT), ns,
                   _sample_mem_attn(row[:, :D_ATT], gq_ref[...], mc_ref[j], row[:, D_ATT:]))

    tm = x_ref.shape[0]
    x = x_ref[...]
    xn = _rms(x, g_ref[...]).astype(BF16)

    def proj(lo, hi):
        return _dot(xn, win_ref[:, lo:hi])

    u = proj(D_CONV, 2 * D_CONV) * proj(0, D_CONV)
    ext_ref[8:8 + tm, :] = u
    y = (cw_ref[0:1, :] * ext_ref[6:6 + tm, :] + cw_ref[1:2, :] * ext_ref[7:7 + tm, :]
         + cw_ref[2:3, :] * u)
    ext_ref[0:8, :] = ext_ref[tm:tm + 8, :]
    cs_ref[...] = ext_ref[6:8, :]
    bg = proj(2 * D_CONV, 3 * D_CONV)
    g = proj(3 * D_CONV, 4 * D_CONV)
    cat_ref[:, 0:D_CONV] = (_silu(g) * bg * y).astype(BF16)
    mq = proj(4 * D_CONV, 4 * D_CONV + D_ATT)
    mg = proj(4 * D_CONV + D_ATT, 4 * D_CONV + 2 * D_ATT)

    def store(h, val):
        cat_ref[:, _hs(h, D_CONV)] = val.astype(BF16)

    _mem_attn(mq, mg, mkv_ref, gq_ref[...] * QK_SCALE, store)
    y_ref[...] = x + _dot(cat_ref[...], wout_ref[...])

    @pl.when(_is_last_step())
    def _():
        ys_ref[...] = xs_ref[...] + _dot(cats_ref[...].astype(BF16), wout_ref[...])


def _sample_specs(nb, nt, layer):
    full = lambda shape: pl.BlockSpec(shape, lambda i, t: (0,) * len(shape))
    mem_cache = pl.BlockSpec((None, 1, MEM_LEN, KV_CHUNKS, HEAD_DIM),
                             lambda i, t: (layer, i * nt + t, 0, 0, 0))
    return full, mem_cache


def _cache_spec(nt):
    return pl.BlockSpec((1, BAND, None, KV_CHUNKS, HEAD_DIM),
                        lambda i, t: (i * nt + t, 0, 0, 0, 0))


def _a_layer(x, xs, norm_a, w_in, conv_w, mkv_b, state, mem_cache, a, layer, mem_q_norm, w_out):
    b, s, _ = x.shape
    nb = xs.shape[0]
    nt = s // TM_A
    assert nb % (b * nt) == 0
    per_step = nb // (b * nt)
    st_rows = nb * (CONV_W - 1) * D_CONV // HEAD_DIM
    full, _ = _sample_specs(nb, nt, layer)
    mem_cache_spec = pl.BlockSpec((None, per_step, MEM_LEN, KV_CHUNKS, HEAD_DIM),
                                  lambda i, t: (layer, i * nt + t, 0, 0, 0))
    return pl.pallas_call(
        functools.partial(_a_layer_kernel, a=a),
        grid=(b, nt),
        in_specs=[
            pl.BlockSpec((None, TM_A, D_MODEL), lambda i, t: (i, t, 0)),
            _layer_spec((1, D_MODEL), a),
            pl.BlockSpec(memory_space=pl.ANY),
            _layer_spec((CONV_W, D_CONV), a),
            pl.BlockSpec((None, None, MEM_LEN, 2 * D_ATT), lambda i, t: (layer, i, 0, 0)),
            _layer_spec((1, HEAD_DIM), layer),
            pl.BlockSpec(memory_space=pl.ANY),
            full((nb, D_MODEL)),
            _layer_spec((st_rows, HEAD_DIM), a),
            mem_cache_spec,
        ],
        out_specs=[
            pl.BlockSpec((None, TM_A, D_MODEL), lambda i, t: (i, t, 0)),
            pl.BlockSpec((None, CONV_W - 1, D_CONV), lambda i, t: (i, 0, 0)),
            full((nb, D_MODEL)),
            full((st_rows, HEAD_DIM)),
        ],
        out_shape=[jax.ShapeDtypeStruct((b, s, D_MODEL), F32),
                   jax.ShapeDtypeStruct((b, CONV_W - 1, D_CONV), F32),
                   jax.ShapeDtypeStruct((nb, D_MODEL), F32),
                   jax.ShapeDtypeStruct((st_rows, HEAD_DIM), F32)],
        scratch_shapes=[pltpu.VMEM((TM_A + 8, D_CONV), F32),
                        pltpu.VMEM((TM_A, D_CONV + D_ATT), BF16),
                        pltpu.VMEM((nb, 2 * D_ATT), F32),
                        pltpu.VMEM((nb, D_CONV + D_ATT), F32)]
        + _weight_scratch(w_in.shape[1:]) + _weight_scratch(w_out.shape[1:]),
        compiler_params=_params(("arbitrary", "arbitrary"), VMEM_LIMIT_A),
        name=f"a_layer_{layer}",
    )(x, norm_a, w_in, conv_w, mkv_b, mem_q_norm, w_out, xs, state, mem_cache)


def _residue_rows(slab_ref, split_ref, c, d):
    rows = slab_ref.shape[1]
    if d == SUB_STRIDE:
        return lambda r: slab_ref[c, pl.ds(r, rows // d, stride=d), :]
    assert d == SUB_STRIDE * SUB_STRIDE
    for r1 in range(SUB_STRIDE):
        split_ref[c, r1] = slab_ref[c, pl.ds(r1, rows // SUB_STRIDE, stride=SUB_STRIDE), :]
    return lambda r: split_ref[c, r % SUB_STRIDE,
                               pl.ds(r // SUB_STRIDE, rows // d, stride=SUB_STRIDE), :]


def _to_residue_major(dst_ref, slab_ref, split_ref, n_chunks, d):
    for c in range(n_chunks):
        get = _residue_rows(slab_ref, split_ref, c, d)
        for r in range(d):
            dst_ref[r, :, _hs(c)] = get(r).astype(dst_ref.dtype)


def _kv_kernel(x_ref, g_ref, w_hbm, kn_ref, xs_ref,
               gb_ref, winb_ref, qn_ref, gq_ref, woutb_ref, c0_ref, c1_ref, c2_ref, mc_ref,
               kvb0_ref, kvb1_ref, kvb2_ref, st0_ref, st1_ref, st2_ref, kvn_ref, ys_ref,
               slab1_ref, slab2_ref, split_ref, w_ref, w_stage, w_sem, zs_ref, cats_ref):
    t = pl.program_id(1)
    last = pl.num_programs(1) - 1
    n = _step_index()

    @pl.when(n == 0)
    def _():
        _cast_weight(w_hbm, w_ref, w_stage, w_sem)
        _kv_sample_rows(xs_ref, g_ref, w_ref, kn_ref, kvn_ref)
        zs_ref[...] = _dot(_rms(xs_ref[...], gb_ref[...]).astype(BF16), winb_ref[...])
        cats_ref[...] = jnp.zeros(cats_ref.shape, F32)

    kv_new = [kvn_ref[pl.ds(pl.multiple_of((n * N_GROUPS + g) * KV_CHUNKS, KV_CHUNKS),
                            KV_CHUNKS), :] for g in range(N_GROUPS)]
    _merge_row(cats_ref, slice(0, 2 * D_ATT), n,
               _b_sample_row(zs_ref[pl.ds(n, 1), :], kv_new, qn_ref, gq_ref[...],
                             (c0_ref[0], c1_ref[0], c2_ref[0]), mc_ref[0]))

    xn = _rms(x_ref[...], g_ref[...]).astype(BF16)
    kvb_refs = (kvb0_ref, kvb1_ref, kvb2_ref)
    slabs = (None, slab1_ref, slab2_ref)
    for g in reversed(range(N_GROUPS)):
        kv = _dot(xn, w_ref[:, g * 2 * D_ATT:(g + 1) * 2 * D_ATT])
        chunks = [_rms(kv[:, _hs(h)], kn_ref[g:g + 1, :]) for h in range(N_HEADS)]
        chunks += [kv[:, _hs(h, D_ATT)] for h in range(N_HEADS)]
        d = DILATIONS[g]
        for c, chunk in enumerate(chunks):
            if d == 1:
                kvb_refs[g][0, :, _hs(c)] = chunk.astype(BF16)
            else:
                slabs[g][c] = chunk
            if g == 2:
                st2_ref[pl.ds(c, TM, stride=KV_CHUNKS), :] = chunk
        if d > 1:
            _to_residue_major(kvb_refs[g], slabs[g], split_ref, KV_CHUNKS, d)
        if g == 1:
            @pl.when(t == last)
            def _():
                for c, chunk in enumerate(chunks):
                    st1_ref[pl.ds(c, TM, stride=KV_CHUNKS), :] = chunk
        if g == 0:
            @pl.when(t == last)
            def _():
                for c, chunk in enumerate(chunks):
                    st0_ref[pl.ds(c, WINDOWS[0], stride=KV_CHUNKS), :] = chunk[TM - WINDOWS[0]:, :]

    @pl.when(_is_last_step())
    def _():
        ys_ref[...] = xs_ref[...] + _dot(cats_ref[...].astype(BF16), woutb_ref[...])


def _kv_shared(x, xs, kv_norm, w_kv, k_norm_dil, norm_b, w_in_b, q_norm_b, mem_q_norm, w_out_b,
               caches, mem_cache, layer):
    b, s, _ = x.shape
    nb = xs.shape[0]
    nt = s // TM
    assert nb == b * nt
    kvn_rows = nb * N_GROUPS * KV_CHUNKS
    full, mem_cache_spec = _sample_specs(nb, nt, layer)
    cache_spec = _cache_spec(nt)
    assert WINDOWS[1] == TM and WINDOWS[2] == s and WINDOWS[0] <= TM
    first = lambda i, t: (i, 0, 0)
    res_spec = lambda d: pl.BlockSpec((None, d, TM // d, 2 * D_ATT), lambda i, t: (i, 0, t, 0))
    res_sds = lambda d: jax.ShapeDtypeStruct((b, d, s // d, 2 * D_ATT), BF16)
    st_sds = lambda rows: jax.ShapeDtypeStruct((b, rows * KV_CHUNKS, HEAD_DIM), F32)
    return pl.pallas_call(
        _kv_kernel,
        grid=(b, s // TM),
        in_specs=[
            pl.BlockSpec((None, TM, D_MODEL), lambda i, t: (i, t, 0)),
            _const_spec((1, D_MODEL)),
            pl.BlockSpec(memory_space=pl.ANY),
            _const_spec((N_GROUPS, HEAD_DIM)),
            _const_spec((nb, D_MODEL)),
            _layer_spec((1, D_MODEL), 0),
            _layer_spec(w_in_b.shape[1:], 0),
            _layer_spec((N_GROUPS, HEAD_DIM), 0),
            _layer_spec((1, HEAD_DIM), layer),
            _layer_spec(w_out_b.shape[1:], 0),
            cache_spec, cache_spec, cache_spec,
            mem_cache_spec,
        ],
        out_specs=[
            res_spec(DILATIONS[0]), res_spec(DILATIONS[1]), res_spec(DILATIONS[2]),
            pl.BlockSpec((None, WINDOWS[0] * KV_CHUNKS, HEAD_DIM), first),
            pl.BlockSpec((None, WINDOWS[1] * KV_CHUNKS, HEAD_DIM), first),
            pl.BlockSpec((None, TM * KV_CHUNKS, HEAD_DIM), lambda i, t: (i, t, 0)),
            pl.BlockSpec((kvn_rows, HEAD_DIM), lambda i, t: (0, 0)),
            full((nb, D_MODEL)),
        ],
        out_shape=[res_sds(DILATIONS[0]), res_sds(DILATIONS[1]), res_sds(DILATIONS[2]),
                   st_sds(WINDOWS[0]), st_sds(WINDOWS[1]), st_sds(s),
                   jax.ShapeDtypeStruct((kvn_rows, HEAD_DIM), F32),
                   jax.ShapeDtypeStruct((nb, D_MODEL), F32)],
        scratch_shapes=[pltpu.VMEM((KV_CHUNKS, TM, HEAD_DIM), F32),
                        pltpu.VMEM((KV_CHUNKS, TM, HEAD_DIM), F32),
                        pltpu.VMEM((KV_CHUNKS, SUB_STRIDE, TM // SUB_STRIDE, HEAD_DIM), F32)]
        + _weight_scratch(w_kv.shape)
        + [pltpu.VMEM((nb, w_in_b.shape[2]), F32), pltpu.VMEM((nb, 2 * D_ATT), F32)],
        compiler_params=_params(("arbitrary", "arbitrary"), VMEM_LIMIT_KV),
        name="kv_prompt",
    )(x, kv_norm.reshape(1, D_MODEL), w_kv, k_norm_dil, xs,
      norm_b, w_in_b, q_norm_b, mem_q_norm, w_out_b, *caches, mem_cache)


def _band_scores(qs, ks, masks):
    return jnp.concatenate([jnp.where(mk, _dot_nt(q, k), NEG) for q, k, mk in zip(qs, ks, masks)],
                           axis=0)


def _band_finish(s, vs, nq):
    m = jnp.max(s, axis=-1, keepdims=True)
    pb = jnp.exp2(s - m).astype(BF16)
    res = []
    for i, v in enumerate(vs):
        rows = slice(i * nq, (i + 1) * nq)
        acc, l = _pv_and_rowsum(pb[rows], v)
        res.append((acc * (1.0 / l), m[rows] + jnp.log2(l)))
    return res


def _b_layer_kernel(*refs, with_sample):
    (x_ref, g_ref, win_ref, qn_ref, mkv_ref, gq_ref, wout_ref,
     kv0_ref, kv0p_ref, kv1_ref, kv1p_ref, kv2_ref) = refs[:12]
    if with_sample:
        xs_ref, kvn_ref, c0_ref, c1_ref, c2_ref, mc_ref, y_ref, ys_ref = refs[12:20]
        (qslab1_ref, qslab2_ref, qsplit_ref, oslab_ref, lslab_ref, cat_ref,
         zs_ref, cats_ref) = refs[20:]
    else:
        y_ref = refs[12]
        qslab1_ref, qslab2_ref, qsplit_ref, oslab_ref, lslab_ref, cat_ref = refs[13:]
    t = pl.program_id(1)
    n = _step_index()

    if with_sample:
        @pl.when(n == 0)
        def _():
            zs_ref[...] = _dot(_rms(xs_ref[...], g_ref[...]).astype(BF16), win_ref[...])
            cats_ref[...] = jnp.zeros(cats_ref.shape, F32)

        _merge_row(cats_ref, slice(0, 2 * D_ATT), n,
                   _b_sample_row(zs_ref[pl.ds(n, 1), :], kvn_ref[n], qn_ref, gq_ref[...],
                                 (c0_ref[0], c1_ref[0], c2_ref[0]), mc_ref[0]))

    x = x_ref[...]
    xn = _rms(x, g_ref[...]).astype(BF16)

    def proj(lo, hi):
        return _dot(xn, win_ref[:, lo:hi])

    qi = lax.broadcasted_iota(jnp.int32, (BAND, 2 * BAND), 0)
    kj = lax.broadcasted_iota(jnp.int32, (BAND, 2 * BAND), 1)
    mask_rest = (kj >= qi) & (kj <= qi + BAND)
    mask_edge = (kj >= jnp.maximum(qi, jnp.where(t == 0, BAND, 0))) & (kj <= qi + BAND)

    def finish(g, h, rows, out, lse):
        oslab_ref[g, h, rows, :] = out
        lslab_ref[g, h, rows, :] = lse

    def normed_heads(g):
        qd = proj(g * D_ATT, (g + 1) * D_ATT)
        gain = qn_ref[g:g + 1, :] * QK_SCALE
        return [_rms(qd[:, _hs(h)], gain) for h in range(N_HEADS)]

    n_sub = TM // BAND
    qh = [q.astype(BF16) for q in normed_heads(0)]
    units0 = [(i, h) for i in range(n_sub) for h in range(N_HEADS)]

    def keys0(i, cols):
        if i == 0:
            return jnp.concatenate([kv0p_ref[0, :, cols], kv0_ref[0, 0:BAND, cols]], axis=0)
        return kv0_ref[0, (i - 1) * BAND:(i + 1) * BAND, cols]

    d1 = DILATIONS[1]
    for h, q in enumerate(normed_heads(1)):
        qslab1_ref[h] = q
    get_q1 = [_residue_rows(qslab1_ref, None, h, d1) for h in range(N_HEADS)]
    units1 = [(r, h) for r in range(d1) for h in range(N_HEADS)]

    def keys1(r, cols):
        return jnp.concatenate([kv1p_ref[r, :, cols], kv1_ref[r, :, cols]], axis=0)

    d2 = DILATIONS[2]
    nq = TM // d2
    for h, q in enumerate(normed_heads(2)):
        qslab2_ref[h] = q
    get_q2 = [_residue_rows(qslab2_ref, qsplit_ref, h, d2) for h in range(N_HEADS)]
    units2 = [(r, h) for r in range(d2) for h in range(N_HEADS)]
    qi2 = lax.broadcasted_iota(jnp.int32, (nq, BAND), 0)
    kj2 = lax.broadcasted_iota(jnp.int32, (nq, BAND), 1)
    mask2 = kj2 <= qi2 + t * nq

    s0 = _band_scores([qh[h][i * BAND:(i + 1) * BAND] for i, h in units0],
                      [keys0(i, _hs(h)) for i, h in units0],
                      [mask_edge if i == 0 else mask_rest for i, h in units0])
    s1 = _band_scores([get_q1[h](r).astype(BF16) for r, h in units1],
                      [keys1(r, _hs(h)) for r, h in units1], [mask_edge] * len(units1))
    s2 = _band_scores([get_q2[h](r).astype(BF16) for r, h in units2],
                      [kv2_ref[r, :, _hs(h)] for r, h in units2], [mask2] * len(units2))
    res = _band_finish(s0, [keys0(i, _hs(h, D_ATT)) for i, h in units0], BAND)
    for (i, h), (out, lse) in zip(units0, res):
        finish(0, h, pl.ds(i * BAND, BAND), out, lse)
    res = _band_finish(s1, [keys1(r, _hs(h, D_ATT)) for r, h in units1], BAND)
    for (r, h), (out, lse) in zip(units1, res):
        finish(1, h, pl.ds(r, BAND, stride=d1), out, lse)
    res = _band_finish(s2, [kv2_ref[r, :, _hs(h, D_ATT)] for r, h in units2], nq)
    for (r, h), (out, lse) in zip(units2, res):
        finish(2, h, pl.ds(r, nq, stride=d2), out, lse)

    base = N_GROUPS * D_ATT
    gd = proj(base, base + D_ATT)
    for h in range(N_HEADS):
        l0, l1, l2 = lslab_ref[0, h], lslab_ref[1, h], lslab_ref[2, h]
        m = jnp.maximum(jnp.maximum(l0, l1), l2)
        e0, e1, e2 = jnp.exp2(l0 - m), jnp.exp2(l1 - m), jnp.exp2(l2 - m)
        dil = ((oslab_ref[0, h] * e0 + oslab_ref[1, h] * e1 + oslab_ref[2, h] * e2)
               * (1.0 / (e0 + e1 + e2)))
        cat_ref[:, _hs(h)] = (dil * _silu(gd[:, _hs(h)])).astype(BF16)
    mq = proj(base + D_ATT, base + 2 * D_ATT)
    mg = proj(base + 2 * D_ATT, base + 3 * D_ATT)

    def store(h, val):
        cat_ref[:, _hs(h, D_ATT)] = val.astype(BF16)

    _mem_attn(mq, mg, mkv_ref, gq_ref[...] * QK_SCALE, store)
    y_ref[...] = x + _dot(cat_ref[...], wout_ref[...])

    if with_sample:
        @pl.when(_is_last_step())
        def _():
            ys_ref[...] = xs_ref[...] + _dot(cats_ref[...].astype(BF16), wout_ref[...])


def _b_layer(x, xs, norm_b, w_in, q_norm, mkv_b, kvbs, kv_new, caches, mem_cache, bl, layer,
             mem_q_norm, w_out):
    b, s, _ = x.shape
    nt = s // TM
    with_sample = xs is not None
    sample_specs, sample_args, sample_outs, sample_shapes, sample_scratch = [], [], [], [], []
    if with_sample:
        nb = xs.shape[0]
        assert nb == b * nt
        full, mem_cache_spec = _sample_specs(nb, nt, layer)
        cache_spec = _cache_spec(nt)
        sample_specs = [full((nb, D_MODEL)), full(kv_new.shape), cache_spec, cache_spec,
                        cache_spec, mem_cache_spec]
        sample_args = [xs, kv_new, *caches, mem_cache]
        sample_outs = [full((nb, D_MODEL))]
        sample_shapes = [jax.ShapeDtypeStruct((nb, D_MODEL), F32)]
        sample_scratch = [pltpu.VMEM((nb, w_in.shape[2]), F32), pltpu.VMEM((nb, 2 * D_ATT), F32)]
    assert TM // BAND == DILATIONS[1] and s // DILATIONS[2] == BAND
    tile = lambda i, t: (i, t, 0)
    sub = TM // BAND
    kv_w = 2 * D_ATT
    outs = pl.pallas_call(
        functools.partial(_b_layer_kernel, with_sample=with_sample),
        grid=(b, s // TM),
        in_specs=[
            pl.BlockSpec((None, TM, D_MODEL), tile),
            _layer_spec((1, D_MODEL), bl),
            _layer_spec(w_in.shape[1:], bl),
            _layer_spec((N_GROUPS, HEAD_DIM), bl),
            pl.BlockSpec((None, None, MEM_LEN, kv_w), lambda i, t: (layer, i, 0, 0)),
            _layer_spec((1, HEAD_DIM), layer),
            _layer_spec(w_out.shape[1:], bl),
            pl.BlockSpec((None, 1, TM, kv_w), lambda i, t: (i, 0, t, 0)),
            pl.BlockSpec((None, 1, BAND, kv_w),
                         lambda i, t: (i, 0, jnp.maximum(t * sub - 1, 0), 0)),
            pl.BlockSpec((None, DILATIONS[1], BAND, kv_w), lambda i, t: (i, 0, t, 0)),
            pl.BlockSpec((None, DILATIONS[1], BAND, kv_w),
                         lambda i, t: (i, 0, jnp.maximum(t - 1, 0), 0)),
            pl.BlockSpec((None, DILATIONS[2], BAND, kv_w), lambda i, t: (i, 0, 0, 0)),
        ] + sample_specs,
        out_specs=[pl.BlockSpec((None, TM, D_MODEL), tile)] + sample_outs,
        out_shape=[jax.ShapeDtypeStruct((b, s, D_MODEL), F32)] + sample_shapes,
        scratch_shapes=[pltpu.VMEM((N_HEADS, TM, HEAD_DIM), F32),
                        pltpu.VMEM((N_HEADS, TM, HEAD_DIM), F32),
                        pltpu.VMEM((N_HEADS, SUB_STRIDE, TM // SUB_STRIDE, HEAD_DIM), F32),
                        pltpu.VMEM((N_GROUPS, N_HEADS, TM, HEAD_DIM), F32),
                        pltpu.VMEM((N_GROUPS, N_HEADS, TM, HEAD_DIM), F32),
                        pltpu.VMEM((TM, 2 * D_ATT), BF16)] + sample_scratch,
        compiler_params=_params(("arbitrary", "arbitrary")),
        name=f"b_layer_{layer}",
    )(x, norm_b, w_in, q_norm, mkv_b, mem_q_norm, w_out,
      kvbs[0], kvbs[0], kvbs[1], kvbs[1], kvbs[2], *sample_args)
    return (outs[0], outs[1]) if with_sample else (outs[0], None)


def _swap_halves(x, axis):
    return pltpu.roll(x, N_HEADS, axis=axis)


def _head_rows(tile):
    return jnp.concatenate([tile[N_HEADS + h:N_HEADS + h + 1, :] for h in range(N_HEADS)],
                           axis=-1)


def _low_tile(row):
    return jnp.concatenate([jnp.zeros((N_HEADS, HEAD_DIM), F32)]
                           + [row[:, _hs(h)] for h in range(N_HEADS)], axis=0)


def _q_tile(row, g):
    return jnp.concatenate([_rms(row[:, _hs(h)], g) * QK_SCALE for h in range(N_HEADS)]
                           + [jnp.zeros((N_HEADS, HEAD_DIM), F32)], axis=0)


def _tile_attend(qt, kv):
    s = jnp.sum(kv * qt[None], axis=-1, keepdims=True)
    m = jnp.max(s, axis=0)
    p = jnp.exp2(s - m[None])
    l = jnp.sum(p, axis=0)
    acc = jnp.sum(_swap_halves(jnp.broadcast_to(p, kv.shape), 1) * kv, axis=0)
    out = acc / _swap_halves(jnp.broadcast_to(l, acc.shape), 0)
    return out, m + jnp.log2(l)


def _sample_mem_attn(q_row, gq, mkv, mg_row):
    out, _ = _tile_attend(_q_tile(q_row, gq), mkv)
    return _head_rows(out * _low_tile(_silu(mg_row)))


def _gather_chunks(ref, n_rows, n_chunks, offset=0):
    total = ref.shape[0] // n_rows
    return jnp.concatenate([ref[pl.ds(offset + c, n_rows, stride=total), :]
                            for c in range(n_chunks)], axis=-1)


def _merge_row(ref, cols, n, row):
    tile = pl.ds(pl.multiple_of((n // SAMPLE_GB) * SAMPLE_GB, SAMPLE_GB), SAMPLE_GB)
    sub = lax.broadcasted_iota(jnp.int32, (SAMPLE_GB, row.shape[1]), 0)
    ref[tile, cols] = jnp.where(sub == n % SAMPLE_GB, row, ref[tile, cols])


def _a_sample_first(x_ref, g_ref, win_ref, cw_ref, st_ref, sto_ref, zm_ref, cat_ref):
    nb = x_ref.shape[0]
    n_st = D_CONV // HEAD_DIM
    xn = _rms(x_ref[...], g_ref[...]).astype(BF16)
    z = _dot(xn, win_ref[...])
    u = z[:, D_CONV:2 * D_CONV] * z[:, 0:D_CONV]
    s0 = _gather_chunks(st_ref, nb, n_st)
    s1 = _gather_chunks(st_ref, nb, n_st, n_st)
    y = cw_ref[0:1, :] * s0 + cw_ref[1:2, :] * s1 + cw_ref[2:3, :] * u
    for c in range(n_st):
        sto_ref[pl.ds(c, nb, stride=2 * n_st), :] = s1[:, _hs(c)]
        sto_ref[pl.ds(n_st + c, nb, stride=2 * n_st), :] = u[:, _hs(c)]
    cat_ref[:, 0:D_CONV] = _silu(z[:, 3 * D_CONV:4 * D_CONV]) * z[:, 2 * D_CONV:3 * D_CONV] * y
    cat_ref[:, D_CONV:] = jnp.zeros((nb, D_ATT), F32)
    zm_ref[...] = z[:, 4 * D_CONV:]


def _kv_sample_rows(x_ref, g_ref, w_ref, kn_ref, o_ref):
    nb = x_ref.shape[0]
    xn = _rms(x_ref[...], g_ref[...]).astype(BF16)
    kv = _dot(xn, w_ref[...])
    n_chunks = N_GROUPS * KV_CHUNKS
    for g in range(N_GROUPS):
        for c in range(KV_CHUNKS):
            chunk = kv[:, _hs(c, g * 2 * D_ATT)]
            if c < N_HEADS:
                chunk = _rms(chunk, kn_ref[g:g + 1, :])
            o_ref[pl.ds(g * KV_CHUNKS + c, nb, stride=n_chunks), :] = chunk


def _b_sample_row(row, kv_new, qn_ref, gq, caches, mkv):
    base = N_GROUPS * D_ATT
    outs, lses = [], []
    for g in range(N_GROUPS):
        qt = _q_tile(row[:, g * D_ATT:(g + 1) * D_ATT], qn_ref[g:g + 1, :])
        out, lse = _tile_attend(qt, jnp.concatenate([caches[g], kv_new[g][None]], axis=0))
        outs.append(out)
        lses.append(lse)
    m = jnp.maximum(jnp.maximum(lses[0], lses[1]), lses[2])
    es = [jnp.exp2(l - m) for l in lses]
    den = es[0] + es[1] + es[2]
    ws = [_swap_halves(jnp.broadcast_to(e / den, (KV_CHUNKS, HEAD_DIM)), 0) for e in es]
    dil = outs[0] * ws[0] + outs[1] * ws[1] + outs[2] * ws[2]
    dil_row = _head_rows(dil * _low_tile(_silu(row[:, base:base + D_ATT])))
    mem_row = _sample_mem_attn(row[:, base + D_ATT:base + 2 * D_ATT], gq, mkv,
                               row[:, base + 2 * D_ATT:])
    return jnp.concatenate([dil_row, mem_row], axis=-1)


def kernel(x_prompt, x_sample, cache_mem_kv, state_conv, cache_dil0_kv, cache_dil1_kv, cache_dil2_kv, mem_prompt, norm_a, w_in_a, conv_w_a, w_out_a, norm_b, w_in_b, q_norm_b, w_out_b, kv_norm, w_kv, k_norm_dil, mem_norm, w_mem_kv, mem_k_norm, mem_q_norm):
    n_a = w_in_a.shape[0]
    n_b = w_in_b.shape[0]
    depth = n_a + n_b
    b, s, _ = x_prompt.shape
    nb = x_sample.shape[0]
    assert x_sample.shape[1] == 1 and s % TM == 0 and s % TM_A == 0 and nb % SAMPLE_GB == 0
    kv_dims = (2, N_HEADS, HEAD_DIM)

    w_in_b16, w_out_b16 = w_in_b.astype(BF16), w_out_b.astype(BF16)
    norm_a3 = norm_a.reshape(n_a, 1, D_MODEL)
    norm_b3 = norm_b.reshape(n_b, 1, D_MODEL)
    mem_q_norm3 = mem_q_norm.reshape(depth, 1, HEAD_DIM)

    mem_kv_f, mem_kv_b = _memkv_prompt(mem_prompt, mem_norm, w_mem_kv, mem_k_norm)
    mem_kv_b = mem_kv_b.reshape(depth, b, MEM_LEN, 2 * D_ATT)
    mem_cache = cache_mem_kv.reshape(depth, nb, MEM_LEN, KV_CHUNKS, HEAD_DIM)
    caches = []
    for g, cache in enumerate((cache_dil0_kv, cache_dil1_kv, cache_dil2_kv)):
        assert cache.shape[1] == WINDOWS[g]
        caches.append(cache.reshape(nb, BAND, DILATIONS[g], KV_CHUNKS, HEAD_DIM))
    state = state_conv.reshape(n_a, -1, HEAD_DIM)

    xp = x_prompt
    xs = x_sample.reshape(nb, D_MODEL)
    conv_p, conv_s = [], []
    for a in range(n_a):
        xp, cp, xs, cs = _a_layer(xp, xs, norm_a3, w_in_a, conv_w_a, mem_kv_b, state, mem_cache,
                                  a, a, mem_q_norm3, w_out_a)
        conv_p.append(cp)
        conv_s.append(cs.reshape(nb, CONV_W - 1, D_CONV))

    kvb0, kvb1, kvb2, st0, st1, st2, kv_new, xs = _kv_shared(
        xp, xs, kv_norm, w_kv, k_norm_dil, norm_b3, w_in_b16, q_norm_b, mem_q_norm3, w_out_b16,
        caches, mem_cache, n_a)
    kvbs = (kvb0, kvb1, kvb2)
    kv_new = kv_new.reshape(nb, N_GROUPS, KV_CHUNKS, HEAD_DIM)

    for bl in range(n_b):
        layer = n_a + bl
        xp, xs_next = _b_layer(xp, xs if bl > 0 else None, norm_b3, w_in_b16, q_norm_b, mem_kv_b,
                               kvbs, kv_new, caches, mem_cache, bl, layer, mem_q_norm3, w_out_b16)
        if bl > 0:
            xs = xs_next

    dil_s = [kv_new[:, g].reshape(nb, 1, *kv_dims) for g in range(N_GROUPS)]
    return (xp, xs.reshape(nb, 1, D_MODEL),
            mem_kv_f.reshape(depth, b, MEM_LEN, *kv_dims),
            jnp.stack(conv_p), jnp.stack(conv_s),
            st0.reshape(b, WINDOWS[0], *kv_dims), st1.reshape(b, WINDOWS[1], *kv_dims),
            st2.reshape(b, s, *kv_dims), *dil_s)
```

```python
import functools

import jax
import jax.numpy as jnp
from jax import lax
from jax.experimental import pallas as pl
from jax.experimental.pallas import tpu as pltpu

F32 = jnp.float32
BF16 = jnp.bfloat16

D_MODEL = 1024
HEAD_DIM = 128
N_HEADS = 4
D_ATT = N_HEADS * HEAD_DIM
KV_CHUNKS = 2 * N_HEADS
D_CONV = D_MODEL
CONV_W = 3
N_GROUPS = 3
DILATIONS = (1, 4, 16)
WINDOWS = (128, 512, 2048)
BAND = 128
SUB_STRIDE = 4
MEM_LEN = 256
EPS = 1e-6
NEG = -1e30
SCALE = HEAD_DIM ** -0.5
QK_SCALE = SCALE * 1.4426950408889634

TM = 512
SAMPLE_GB = 8
CARRY = 8
W_STAGE_ROWS = 256
VMEM_LIMIT = 56 * 1024 * 1024


def _rms(x, g):
    return x * lax.rsqrt(jnp.mean(x * x, axis=-1, keepdims=True) + EPS) * g


def _silu(x):
    return x * (1.0 / (1.0 + jnp.exp(-x)))


def _dot(a, b):
    return jnp.dot(a, b, preferred_element_type=F32)


def _dot_nt(a, b):
    return lax.dot_general(a, b, (((1,), (1,)), ((), ())), preferred_element_type=F32)


def _hs(h, base=0):
    return slice(base + h * HEAD_DIM, base + (h + 1) * HEAD_DIM)


def _params(sem):
    return pltpu.CompilerParams(dimension_semantics=sem, vmem_limit_bytes=VMEM_LIMIT)


def _fixed_spec(block, index):
    return pl.BlockSpec(block, lambda *_: index, pipeline_mode=pl.Buffered(1))


def _const_spec(shape):
    return _fixed_spec(shape, (0,) * len(shape))


def _layer_spec(shape, layer):
    return _fixed_spec((None,) + tuple(shape), (layer,) + (0,) * len(shape))


def _cast_weight(w_hbm, dst_ref, stage_ref, sem_ref):
    rows = stage_ref.shape[1]
    n_chunks = w_hbm.shape[0] // rows

    def copy(c, slot):
        src = w_hbm.at[pl.ds(pl.multiple_of(c * rows, rows), rows)]
        return pltpu.make_async_copy(src, stage_ref.at[slot], sem_ref.at[slot])

    copy(0, 0).start()

    def body(c, carry):
        slot = c % 2

        @pl.when(c + 1 < n_chunks)
        def _():
            copy(c + 1, 1 - slot).start()

        copy(c, slot).wait()
        dst_ref[pl.ds(pl.multiple_of(c * rows, rows), rows), :] = stage_ref[slot].astype(BF16)
        return carry

    lax.fori_loop(0, n_chunks, body, 0)


def _weight_scratch(shape):
    k, n = shape
    assert k % W_STAGE_ROWS == 0
    return [pltpu.VMEM((k, n), BF16), pltpu.VMEM((2, W_STAGE_ROWS, n), F32),
            pltpu.SemaphoreType.DMA((2,))]


def _memkv_kernel(mem_ref, g_ref, w_ref, kg_ref, of_ref, ob_ref):
    rt = mem_ref.shape[0]
    xn = _rms(mem_ref[...], g_ref[...]).astype(BF16)
    kv = _dot(xn, w_ref[...].astype(BF16))
    for c in range(KV_CHUNKS):
        chunk = kv[:, _hs(c)]
        if c < N_HEADS:
            chunk = _rms(chunk, kg_ref[...])
        of_ref[pl.ds(c, rt, stride=KV_CHUNKS), :] = chunk
        ob_ref[:, _hs(c)] = chunk.astype(BF16)


def _memkv_prompt(mem, mem_norm, w_mem_kv, mem_k_norm):
    depth = w_mem_kv.shape[0]
    rows = mem.shape[0] * mem.shape[1]
    rt = 1024
    return pl.pallas_call(
        _memkv_kernel,
        grid=(depth, rows // rt),
        in_specs=[
            pl.BlockSpec((rt, D_MODEL), lambda l, i: (i, 0)),
            pl.BlockSpec((None, 1, D_MODEL), lambda l, i: (l, 0, 0)),
            pl.BlockSpec((None, D_MODEL, 2 * D_ATT), lambda l, i: (l, 0, 0)),
            pl.BlockSpec((None, 1, HEAD_DIM), lambda l, i: (l, 0, 0)),
        ],
        out_specs=[
            pl.BlockSpec((None, rt * KV_CHUNKS, HEAD_DIM), lambda l, i: (l, i, 0)),
            pl.BlockSpec((None, rt, 2 * D_ATT), lambda l, i: (l, i, 0)),
        ],
        out_shape=[jax.ShapeDtypeStruct((depth, rows * KV_CHUNKS, HEAD_DIM), F32),
                   jax.ShapeDtypeStruct((depth, rows, 2 * D_ATT), BF16)],
        compiler_params=_params(("parallel", "parallel")),
        name="memkv_prompt",
    )(mem.reshape(rows, D_MODEL), mem_norm.reshape(depth, 1, D_MODEL), w_mem_kv,
      mem_k_norm.reshape(depth, 1, HEAD_DIM))


def _pv_and_rowsum(p, v):
    ones = jnp.ones((v.shape[0], HEAD_DIM), BF16)
    both = _dot(p, jnp.concatenate([v, ones], axis=1))
    return both[:, :HEAD_DIM], both[:, HEAD_DIM:]


def _mem_attn(mq, mg, mkv_ref, gq, store):
    for h in range(N_HEADS):
        q = _rms(mq[:, _hs(h)], gq).astype(BF16)
        s = _dot_nt(q, mkv_ref[:, _hs(h)])
        m = jnp.max(s, axis=-1, keepdims=True)
        acc, l = _pv_and_rowsum(jnp.exp2(s - m).astype(BF16), mkv_ref[:, _hs(h, D_ATT)])
        store(h, acc / l * _silu(mg[:, _hs(h)]))


def _step_index():
    return pl.program_id(0) * pl.num_programs(1) + pl.program_id(1)


def _is_last_step():
    return _step_index() == pl.num_programs(0) * pl.num_programs(1) - 1


def _a_layer_kernel(x_ref, g_ref, win_hbm, cw_ref, mkv_ref, gq_ref, wout_hbm,
                    xs_ref, st_ref, mc_ref,
                    y_ref, cs_ref, ys_ref, sto_ref,
                    ext_ref, cat_ref, zs_ref, cats_ref,
                    win_ref, win_stage, win_sem, wout_ref, wout_stage, wout_sem, *, a):
    t = pl.program_id(1)
    n = _step_index()

    @pl.when(t == 0)
    def _():
        ext_ref[0:CARRY, :] = jnp.zeros((CARRY, D_CONV), F32)

    @pl.when(n == 0)
    def _():
        _cast_weight(win_hbm.at[a], win_ref, win_stage, win_sem)
        _cast_weight(wout_hbm.at[a], wout_ref, wout_stage, wout_sem)
        _a_sample_first(xs_ref, g_ref, win_ref, cw_ref, st_ref, sto_ref, zs_ref, cats_ref)

    row = zs_ref[pl.ds(n, 1), :]
    _merge_row(cats_ref, slice(D_CONV, D_CONV + D_ATT), n,
               _sample_mem_attn(row[:, :D_ATT], gq_ref[...], mc_ref[0], row[:, D_ATT:]))

    x = x_ref[...]
    xn = _rms(x, g_ref[...]).astype(BF16)

    def proj(lo, hi):
        return _dot(xn, win_ref[:, lo:hi])

    u = proj(D_CONV, 2 * D_CONV) * proj(0, D_CONV)
    ext_ref[CARRY:CARRY + TM, :] = u
    y = (cw_ref[0:1, :] * ext_ref[CARRY - 2:CARRY - 2 + TM, :]
         + cw_ref[1:2, :] * ext_ref[CARRY - 1:CARRY - 1 + TM, :] + cw_ref[2:3, :] * u)
    ext_ref[0:CARRY, :] = ext_ref[TM:TM + CARRY, :]
    cs_ref[...] = ext_ref[CARRY - (CONV_W - 1):CARRY, :]
    bg = proj(2 * D_CONV, 3 * D_CONV)
    g = proj(3 * D_CONV, 4 * D_CONV)
    cat_ref[:, 0:D_CONV] = (_silu(g) * bg * y).astype(BF16)
    mq = proj(4 * D_CONV, 4 * D_CONV + D_ATT)
    mg = proj(4 * D_CONV + D_ATT, 4 * D_CONV + 2 * D_ATT)

    def store(h, val):
        cat_ref[:, _hs(h, D_CONV)] = val.astype(BF16)

    _mem_attn(mq, mg, mkv_ref, gq_ref[...] * QK_SCALE, store)
    y_ref[...] = x + _dot(cat_ref[...], wout_ref[...])

    @pl.when(_is_last_step())
    def _():
        ys_ref[...] = xs_ref[...] + _dot(cats_ref[...].astype(BF16), wout_ref[...])


def _sample_specs(nb, nt, layer):
    full = lambda shape: pl.BlockSpec(shape, lambda i, t: (0,) * len(shape))
    mem_cache = pl.BlockSpec((None, 1, MEM_LEN, KV_CHUNKS, HEAD_DIM),
                             lambda i, t: (layer, i * nt + t, 0, 0, 0))
    return full, mem_cache


def _a_layer(x, xs, norm_a, w_in, conv_w, mkv_b, state, mem_cache, a, layer, mem_q_norm, w_out):
    b, s, _ = x.shape
    nb = xs.shape[0]
    nt = s // TM
    assert nb == b * nt
    st_rows = nb * (CONV_W - 1) * D_CONV // HEAD_DIM
    full, mem_cache_spec = _sample_specs(nb, nt, layer)
    return pl.pallas_call(
        functools.partial(_a_layer_kernel, a=a),
        grid=(b, nt),
        in_specs=[
            pl.BlockSpec((None, TM, D_MODEL), lambda i, t: (i, t, 0)),
            _layer_spec((1, D_MODEL), a),
            pl.BlockSpec(memory_space=pl.ANY),
            _layer_spec((CONV_W, D_CONV), a),
            pl.BlockSpec((None, None, MEM_LEN, 2 * D_ATT), lambda i, t: (layer, i, 0, 0)),
            _layer_spec((1, HEAD_DIM), layer),
            pl.BlockSpec(memory_space=pl.ANY),
            full((nb, D_MODEL)),
            _layer_spec((st_rows, HEAD_DIM), a),
            mem_cache_spec,
        ],
        out_specs=[
            pl.BlockSpec((None, TM, D_MODEL), lambda i, t: (i, t, 0)),
            pl.BlockSpec((None, CONV_W - 1, D_CONV), lambda i, t: (i, 0, 0)),
            full((nb, D_MODEL)),
            full((st_rows, HEAD_DIM)),
        ],
        out_shape=[jax.ShapeDtypeStruct((b, s, D_MODEL), F32),
                   jax.ShapeDtypeStruct((b, CONV_W - 1, D_CONV), F32),
                   jax.ShapeDtypeStruct((nb, D_MODEL), F32),
                   jax.ShapeDtypeStruct((st_rows, HEAD_DIM), F32)],
        scratch_shapes=[pltpu.VMEM((TM + CARRY, D_CONV), F32),
                        pltpu.VMEM((TM, D_CONV + D_ATT), BF16),
                        pltpu.VMEM((nb, 2 * D_ATT), F32),
                        pltpu.VMEM((nb, D_CONV + D_ATT), F32)]
        + _weight_scratch(w_in.shape[1:]) + _weight_scratch(w_out.shape[1:]),
        compiler_params=_params(("arbitrary", "arbitrary")),
        name=f"a_layer_{layer}",
    )(x, norm_a, w_in, conv_w, mkv_b, mem_q_norm, w_out, xs, state, mem_cache)


def _residue_rows(slab_ref, split_ref, c, d):
    rows = slab_ref.shape[1]
    if d == SUB_STRIDE:
        return lambda r: slab_ref[c, pl.ds(r, rows // d, stride=d), :]
    assert d == SUB_STRIDE * SUB_STRIDE
    for r1 in range(SUB_STRIDE):
        split_ref[c, r1] = slab_ref[c, pl.ds(r1, rows // SUB_STRIDE, stride=SUB_STRIDE), :]
    return lambda r: split_ref[c, r % SUB_STRIDE,
                               pl.ds(r // SUB_STRIDE, rows // d, stride=SUB_STRIDE), :]


def _to_residue_major(dst_ref, slab_ref, split_ref, n_chunks, d):
    for c in range(n_chunks):
        get = _residue_rows(slab_ref, split_ref, c, d)
        for r in range(d):
            dst_ref[r, :, _hs(c)] = get(r).astype(dst_ref.dtype)


def _kv_kernel(x_ref, g_ref, w_hbm, kn_ref, xs_ref, kvb0_ref, kvb1_ref, kvb2_ref,
               st0_ref, st1_ref, st2_ref, kvn_ref, slab1_ref, slab2_ref, split_ref,
               w_ref, w_stage, w_sem):
    t = pl.program_id(1)
    last = pl.num_programs(1) - 1

    @pl.when(_step_index() == 0)
    def _():
        _cast_weight(w_hbm, w_ref, w_stage, w_sem)
        _kv_sample_rows(xs_ref, g_ref, w_ref, kn_ref, kvn_ref)

    xn = _rms(x_ref[...], g_ref[...]).astype(BF16)
    kvb_refs = (kvb0_ref, kvb1_ref, kvb2_ref)
    slabs = (None, slab1_ref, slab2_ref)
    for g in reversed(range(N_GROUPS)):
        kv = _dot(xn, w_ref[:, g * 2 * D_ATT:(g + 1) * 2 * D_ATT])
        chunks = [_rms(kv[:, _hs(h)], kn_ref[g:g + 1, :]) for h in range(N_HEADS)]
        chunks += [kv[:, _hs(h, D_ATT)] for h in range(N_HEADS)]
        d = DILATIONS[g]
        for c, chunk in enumerate(chunks):
            if d == 1:
                kvb_refs[g][0, :, _hs(c)] = chunk.astype(BF16)
            else:
                slabs[g][c] = chunk
            if g == 2:
                st2_ref[pl.ds(c, TM, stride=KV_CHUNKS), :] = chunk
        if d > 1:
            _to_residue_major(kvb_refs[g], slabs[g], split_ref, KV_CHUNKS, d)
        if g == 1:
            @pl.when(t == last)
            def _():
                for c, chunk in enumerate(chunks):
                    st1_ref[pl.ds(c, TM, stride=KV_CHUNKS), :] = chunk
        if g == 0:
            @pl.when(t == last)
            def _():
                for c, chunk in enumerate(chunks):
                    st0_ref[pl.ds(c, WINDOWS[0], stride=KV_CHUNKS), :] = chunk[TM - WINDOWS[0]:, :]


def _kv_shared(x, xs, kv_norm, w_kv, k_norm_dil):
    b, s, _ = x.shape
    nb = xs.shape[0]
    kvn_rows = nb * N_GROUPS * KV_CHUNKS
    assert WINDOWS[1] == TM and WINDOWS[2] == s and WINDOWS[0] <= TM
    first = lambda i, t: (i, 0, 0)
    res_spec = lambda d: pl.BlockSpec((None, d, TM // d, 2 * D_ATT), lambda i, t: (i, 0, t, 0))
    res_sds = lambda d: jax.ShapeDtypeStruct((b, d, s // d, 2 * D_ATT), BF16)
    st_sds = lambda rows: jax.ShapeDtypeStruct((b, rows * KV_CHUNKS, HEAD_DIM), F32)
    return pl.pallas_call(
        _kv_kernel,
        grid=(b, s // TM),
        in_specs=[
            pl.BlockSpec((None, TM, D_MODEL), lambda i, t: (i, t, 0)),
            _const_spec((1, D_MODEL)),
            pl.BlockSpec(memory_space=pl.ANY),
            _const_spec((N_GROUPS, HEAD_DIM)),
            _const_spec((nb, D_MODEL)),
        ],
        out_specs=[
            res_spec(DILATIONS[0]), res_spec(DILATIONS[1]), res_spec(DILATIONS[2]),
            pl.BlockSpec((None, WINDOWS[0] * KV_CHUNKS, HEAD_DIM), first),
            pl.BlockSpec((None, WINDOWS[1] * KV_CHUNKS, HEAD_DIM), first),
            pl.BlockSpec((None, TM * KV_CHUNKS, HEAD_DIM), lambda i, t: (i, t, 0)),
            pl.BlockSpec((kvn_rows, HEAD_DIM), lambda i, t: (0, 0)),
        ],
        out_shape=[res_sds(DILATIONS[0]), res_sds(DILATIONS[1]), res_sds(DILATIONS[2]),
                   st_sds(WINDOWS[0]), st_sds(WINDOWS[1]), st_sds(s),
                   jax.ShapeDtypeStruct((kvn_rows, HEAD_DIM), F32)],
        scratch_shapes=[pltpu.VMEM((KV_CHUNKS, TM, HEAD_DIM), F32),
                        pltpu.VMEM((KV_CHUNKS, TM, HEAD_DIM), F32),
                        pltpu.VMEM((KV_CHUNKS, SUB_STRIDE, TM // SUB_STRIDE, HEAD_DIM), F32)]
        + _weight_scratch(w_kv.shape),
        compiler_params=_params(("arbitrary", "arbitrary")),
        name="kv_prompt",
    )(x, kv_norm.reshape(1, D_MODEL), w_kv, k_norm_dil, xs)


def _band_scores(qs, ks, masks):
    return jnp.concatenate([jnp.where(mk, _dot_nt(q, k), NEG) for q, k, mk in zip(qs, ks, masks)],
                           axis=0)


def _band_finish(s, vs, nq):
    m = jnp.max(s, axis=-1, keepdims=True)
    pb = jnp.exp2(s - m).astype(BF16)
    res = []
    for i, v in enumerate(vs):
        rows = slice(i * nq, (i + 1) * nq)
        acc, l = _pv_and_rowsum(pb[rows], v)
        res.append((acc * (1.0 / l), m[rows] + jnp.log2(l)))
    return res


def _b_layer_kernel(x_ref, g_ref, win_ref, qn_ref, mkv_ref, gq_ref, wout_ref,
                    kv0_ref, kv0p_ref, kv1_ref, kv1p_ref, kv2_ref,
                    xs_ref, kvn_ref, c0_ref, c1_ref, c2_ref, mc_ref,
                    y_ref, ys_ref,
                    qslab1_ref, qslab2_ref, qsplit_ref, oslab_ref, lslab_ref, cat_ref,
                    zs_ref, cats_ref):
    t = pl.program_id(1)
    n = _step_index()

    @pl.when(n == 0)
    def _():
        zs_ref[...] = _dot(_rms(xs_ref[...], g_ref[...]).astype(BF16), win_ref[...])
        cats_ref[...] = jnp.zeros(cats_ref.shape, F32)

    _merge_row(cats_ref, slice(0, 2 * D_ATT), n,
               _b_sample_row(zs_ref[pl.ds(n, 1), :], kvn_ref[n], qn_ref, gq_ref[...],
                             (c0_ref[0], c1_ref[0], c2_ref[0]), mc_ref[0]))

    x = x_ref[...]
    xn = _rms(x, g_ref[...]).astype(BF16)

    def proj(lo, hi):
        return _dot(xn, win_ref[:, lo:hi])

    qi = lax.broadcasted_iota(jnp.int32, (BAND, 2 * BAND), 0)
    kj = lax.broadcasted_iota(jnp.int32, (BAND, 2 * BAND), 1)
    mask_rest = (kj >= qi) & (kj <= qi + BAND)
    mask_edge = (kj >= jnp.maximum(qi, jnp.where(t == 0, BAND, 0))) & (kj <= qi + BAND)

    def finish(g, h, rows, out, lse):
        oslab_ref[g, h, rows, :] = out
        lslab_ref[g, h, rows, :] = lse

    def normed_heads(g):
        qd = proj(g * D_ATT, (g + 1) * D_ATT)
        gain = qn_ref[g:g + 1, :] * QK_SCALE
        return [_rms(qd[:, _hs(h)], gain) for h in range(N_HEADS)]

    n_sub = TM // BAND
    qh = [q.astype(BF16) for q in normed_heads(0)]
    units0 = [(i, h) for i in range(n_sub) for h in range(N_HEADS)]

    def keys0(i, cols):
        if i == 0:
            return jnp.concatenate([kv0p_ref[0, :, cols], kv0_ref[0, 0:BAND, cols]], axis=0)
        return kv0_ref[0, (i - 1) * BAND:(i + 1) * BAND, cols]

    d1 = DILATIONS[1]
    for h, q in enumerate(normed_heads(1)):
        qslab1_ref[h] = q
    get_q1 = [_residue_rows(qslab1_ref, None, h, d1) for h in range(N_HEADS)]
    units1 = [(r, h) for r in range(d1) for h in range(N_HEADS)]

    def keys1(r, cols):
        return jnp.concatenate([kv1p_ref[r, :, cols], kv1_ref[r, :, cols]], axis=0)

    d2 = DILATIONS[2]
    nq = TM // d2
    for h, q in enumerate(normed_heads(2)):
        qslab2_ref[h] = q
    get_q2 = [_residue_rows(qslab2_ref, qsplit_ref, h, d2) for h in range(N_HEADS)]
    units2 = [(r, h) for r in range(d2) for h in range(N_HEADS)]
    qi2 = lax.broadcasted_iota(jnp.int32, (nq, BAND), 0)
    kj2 = lax.broadcasted_iota(jnp.int32, (nq, BAND), 1)
    mask2 = kj2 <= qi2 + t * nq

    s0 = _band_scores([qh[h][i * BAND:(i + 1) * BAND] for i, h in units0],
                      [keys0(i, _hs(h)) for i, h in units0],
                      [mask_edge if i == 0 else mask_rest for i, h in units0])
    s1 = _band_scores([get_q1[h](r).astype(BF16) for r, h in units1],
                      [keys1(r, _hs(h)) for r, h in units1], [mask_edge] * len(units1))
    s2 = _band_scores([get_q2[h](r).astype(BF16) for r, h in units2],
                      [kv2_ref[r, :, _hs(h)] for r, h in units2], [mask2] * len(units2))
    res = _band_finish(s0, [keys0(i, _hs(h, D_ATT)) for i, h in units0], BAND)
    for (i, h), (out, lse) in zip(units0, res):
        finish(0, h, pl.ds(i * BAND, BAND), out, lse)
    res = _band_finish(s1, [keys1(r, _hs(h, D_ATT)) for r, h in units1], BAND)
    for (r, h), (out, lse) in zip(units1, res):
        finish(1, h, pl.ds(r, BAND, stride=d1), out, lse)
    res = _band_finish(s2, [kv2_ref[r, :, _hs(h, D_ATT)] for r, h in units2], nq)
    for (r, h), (out, lse) in zip(units2, res):
        finish(2, h, pl.ds(r, nq, stride=d2), out, lse)

    base = N_GROUPS * D_ATT
    gd = proj(base, base + D_ATT)
    for h in range(N_HEADS):
        l0, l1, l2 = lslab_ref[0, h], lslab_ref[1, h], lslab_ref[2, h]
        m = jnp.maximum(jnp.maximum(l0, l1), l2)
        e0, e1, e2 = jnp.exp2(l0 - m), jnp.exp2(l1 - m), jnp.exp2(l2 - m)
        dil = ((oslab_ref[0, h] * e0 + oslab_ref[1, h] * e1 + oslab_ref[2, h] * e2)
               * (1.0 / (e0 + e1 + e2)))
        cat_ref[:, _hs(h)] = (dil * _silu(gd[:, _hs(h)])).astype(BF16)
    mq = proj(base + D_ATT, base + 2 * D_ATT)
    mg = proj(base + 2 * D_ATT, base + 3 * D_ATT)

    def store(h, val):
        cat_ref[:, _hs(h, D_ATT)] = val.astype(BF16)

    _mem_attn(mq, mg, mkv_ref, gq_ref[...] * QK_SCALE, store)
    y_ref[...] = x + _dot(cat_ref[...], wout_ref[...])

    @pl.when(_is_last_step())
    def _():
        ys_ref[...] = xs_ref[...] + _dot(cats_ref[...].astype(BF16), wout_ref[...])


def _b_layer(x, xs, norm_b, w_in, q_norm, mkv_b, kvbs, kv_new, caches, mem_cache, bl, layer,
             mem_q_norm, w_out):
    b, s, _ = x.shape
    nb = xs.shape[0]
    nt = s // TM
    assert nb == b * nt
    full, mem_cache_spec = _sample_specs(nb, nt, layer)
    cache_spec = pl.BlockSpec((1, BAND, None, KV_CHUNKS, HEAD_DIM),
                              lambda i, t: (i * nt + t, 0, 0, 0, 0))
    assert TM // BAND == DILATIONS[1] and s // DILATIONS[2] == BAND
    tile = lambda i, t: (i, t, 0)
    sub = TM // BAND
    kv_w = 2 * D_ATT
    return pl.pallas_call(
        _b_layer_kernel,
        grid=(b, s // TM),
        in_specs=[
            pl.BlockSpec((None, TM, D_MODEL), tile),
            _layer_spec((1, D_MODEL), bl),
            _layer_spec(w_in.shape[1:], bl),
            _layer_spec((N_GROUPS, HEAD_DIM), bl),
            pl.BlockSpec((None, None, MEM_LEN, kv_w), lambda i, t: (layer, i, 0, 0)),
            _layer_spec((1, HEAD_DIM), layer),
            _layer_spec(w_out.shape[1:], bl),
            pl.BlockSpec((None, 1, TM, kv_w), lambda i, t: (i, 0, t, 0)),
            pl.BlockSpec((None, 1, BAND, kv_w),
                         lambda i, t: (i, 0, jnp.maximum(t * sub - 1, 0), 0)),
            pl.BlockSpec((None, DILATIONS[1], BAND, kv_w), lambda i, t: (i, 0, t, 0)),
            pl.BlockSpec((None, DILATIONS[1], BAND, kv_w),
                         lambda i, t: (i, 0, jnp.maximum(t - 1, 0), 0)),
            pl.BlockSpec((None, DILATIONS[2], BAND, kv_w), lambda i, t: (i, 0, 0, 0)),
            full((nb, D_MODEL)),
            full(kv_new.shape),
            cache_spec, cache_spec, cache_spec,
            mem_cache_spec,
        ],
        out_specs=[pl.BlockSpec((None, TM, D_MODEL), tile), full((nb, D_MODEL))],
        out_shape=[jax.ShapeDtypeStruct((b, s, D_MODEL), F32),
                   jax.ShapeDtypeStruct((nb, D_MODEL), F32)],
        scratch_shapes=[pltpu.VMEM((N_HEADS, TM, HEAD_DIM), F32),
                        pltpu.VMEM((N_HEADS, TM, HEAD_DIM), F32),
                        pltpu.VMEM((N_HEADS, SUB_STRIDE, TM // SUB_STRIDE, HEAD_DIM), F32),
                        pltpu.VMEM((N_GROUPS, N_HEADS, TM, HEAD_DIM), F32),
                        pltpu.VMEM((N_GROUPS, N_HEADS, TM, HEAD_DIM), F32),
                        pltpu.VMEM((TM, 2 * D_ATT), BF16),
                        pltpu.VMEM((nb, w_in.shape[2]), F32),
                        pltpu.VMEM((nb, 2 * D_ATT), F32)],
        compiler_params=_params(("arbitrary", "arbitrary")),
        name=f"b_layer_{layer}",
    )(x, norm_b, w_in, q_norm, mkv_b, mem_q_norm, w_out,
      kvbs[0], kvbs[0], kvbs[1], kvbs[1], kvbs[2],
      xs, kv_new, *caches, mem_cache)


def _swap_halves(x, axis):
    return pltpu.roll(x, N_HEADS, axis=axis)


def _head_rows(tile):
    return jnp.concatenate([tile[N_HEADS + h:N_HEADS + h + 1, :] for h in range(N_HEADS)],
                           axis=-1)


def _low_tile(row):
    return jnp.concatenate([jnp.zeros((N_HEADS, HEAD_DIM), F32)]
                           + [row[:, _hs(h)] for h in range(N_HEADS)], axis=0)


def _q_tile(row, g):
    return jnp.concatenate([_rms(row[:, _hs(h)], g) * QK_SCALE for h in range(N_HEADS)]
                           + [jnp.zeros((N_HEADS, HEAD_DIM), F32)], axis=0)


def _tile_attend(qt, kv):
    s = jnp.sum(kv * qt[None], axis=-1, keepdims=True)
    m = jnp.max(s, axis=0)
    p = jnp.exp2(s - m[None])
    l = jnp.sum(p, axis=0)
    acc = jnp.sum(_swap_halves(jnp.broadcast_to(p, kv.shape), 1) * kv, axis=0)
    out = acc / _swap_halves(jnp.broadcast_to(l, acc.shape), 0)
    return out, m + jnp.log2(l)


def _sample_mem_attn(q_row, gq, mkv, mg_row):
    out, _ = _tile_attend(_q_tile(q_row, gq), mkv)
    return _head_rows(out * _low_tile(_silu(mg_row)))


def _gather_chunks(ref, n_rows, n_chunks, offset=0):
    total = ref.shape[0] // n_rows
    return jnp.concatenate([ref[pl.ds(offset + c, n_rows, stride=total), :]
                            for c in range(n_chunks)], axis=-1)


def _merge_row(ref, cols, n, row):
    tile = pl.ds(pl.multiple_of((n // SAMPLE_GB) * SAMPLE_GB, SAMPLE_GB), SAMPLE_GB)
    sub = lax.broadcasted_iota(jnp.int32, (SAMPLE_GB, row.shape[1]), 0)
    ref[tile, cols] = jnp.where(sub == n % SAMPLE_GB, row, ref[tile, cols])


def _a_sample_first(x_ref, g_ref, win_ref, cw_ref, st_ref, sto_ref, zm_ref, cat_ref):
    nb = x_ref.shape[0]
    n_st = D_CONV // HEAD_DIM
    xn = _rms(x_ref[...], g_ref[...]).astype(BF16)
    z = _dot(xn, win_ref[...])
    u = z[:, D_CONV:2 * D_CONV] * z[:, 0:D_CONV]
    s0 = _gather_chunks(st_ref, nb, n_st)
    s1 = _gather_chunks(st_ref, nb, n_st, n_st)
    y = cw_ref[0:1, :] * s0 + cw_ref[1:2, :] * s1 + cw_ref[2:3, :] * u
    for c in range(n_st):
        sto_ref[pl.ds(c, nb, stride=2 * n_st), :] = s1[:, _hs(c)]
        sto_ref[pl.ds(n_st + c, nb, stride=2 * n_st), :] = u[:, _hs(c)]
    cat_ref[:, 0:D_CONV] = _silu(z[:, 3 * D_CONV:4 * D_CONV]) * z[:, 2 * D_CONV:3 * D_CONV] * y
    cat_ref[:, D_CONV:] = jnp.zeros((nb, D_ATT), F32)
    zm_ref[...] = z[:, 4 * D_CONV:]


def _kv_sample_rows(x_ref, g_ref, w_ref, kn_ref, o_ref):
    nb = x_ref.shape[0]
    xn = _rms(x_ref[...], g_ref[...]).astype(BF16)
    kv = _dot(xn, w_ref[...])
    n_chunks = N_GROUPS * KV_CHUNKS
    for g in range(N_GROUPS):
        for c in range(KV_CHUNKS):
            chunk = kv[:, _hs(c, g * 2 * D_ATT)]
            if c < N_HEADS:
                chunk = _rms(chunk, kn_ref[g:g + 1, :])
            o_ref[pl.ds(g * KV_CHUNKS + c, nb, stride=n_chunks), :] = chunk


def _b_sample_row(row, kv_new, qn_ref, gq, caches, mkv):
    base = N_GROUPS * D_ATT
    outs, lses = [], []
    for g in range(N_GROUPS):
        qt = _q_tile(row[:, g * D_ATT:(g + 1) * D_ATT], qn_ref[g:g + 1, :])
        out, lse = _tile_attend(qt, jnp.concatenate([caches[g], kv_new[g][None]], axis=0))
        outs.append(out)
        lses.append(lse)
    m = jnp.maximum(jnp.maximum(lses[0], lses[1]), lses[2])
    es = [jnp.exp2(l - m) for l in lses]
    den = es[0] + es[1] + es[2]
    ws = [_swap_halves(jnp.broadcast_to(e / den, (KV_CHUNKS, HEAD_DIM)), 0) for e in es]
    dil = outs[0] * ws[0] + outs[1] * ws[1] + outs[2] * ws[2]
    dil_row = _head_rows(dil * _low_tile(_silu(row[:, base:base + D_ATT])))
    mem_row = _sample_mem_attn(row[:, base + D_ATT:base + 2 * D_ATT], gq, mkv,
                               row[:, base + 2 * D_ATT:])
    return jnp.concatenate([dil_row, mem_row], axis=-1)


def kernel(x_prompt, x_sample, cache_mem_kv, state_conv, cache_dil0_kv, cache_dil1_kv, cache_dil2_kv, mem_prompt, norm_a, w_in_a, conv_w_a, w_out_a, norm_b, w_in_b, q_norm_b, w_out_b, kv_norm, w_kv, k_norm_dil, mem_norm, w_mem_kv, mem_k_norm, mem_q_norm):
    n_a = w_in_a.shape[0]
    n_b = w_in_b.shape[0]
    depth = n_a + n_b
    b, s, _ = x_prompt.shape
    nb = x_sample.shape[0]
    assert x_sample.shape[1] == 1 and s % TM == 0 and nb % SAMPLE_GB == 0
    kv_dims = (2, N_HEADS, HEAD_DIM)

    w_in_b16, w_out_b16 = w_in_b.astype(BF16), w_out_b.astype(BF16)
    norm_a3 = norm_a.reshape(n_a, 1, D_MODEL)
    norm_b3 = norm_b.reshape(n_b, 1, D_MODEL)
    mem_q_norm3 = mem_q_norm.reshape(depth, 1, HEAD_DIM)

    mem_kv_f, mem_kv_b = _memkv_prompt(mem_prompt, mem_norm, w_mem_kv, mem_k_norm)
    mem_kv_b = mem_kv_b.reshape(depth, b, MEM_LEN, 2 * D_ATT)
    mem_cache = cache_mem_kv.reshape(depth, nb, MEM_LEN, KV_CHUNKS, HEAD_DIM)
    caches = []
    for g, cache in enumerate((cache_dil0_kv, cache_dil1_kv, cache_dil2_kv)):
        assert cache.shape[1] == WINDOWS[g]
        caches.append(cache.reshape(nb, BAND, DILATIONS[g], KV_CHUNKS, HEAD_DIM))
    state = state_conv.reshape(n_a, -1, HEAD_DIM)

    xp = x_prompt
    xs = x_sample.reshape(nb, D_MODEL)
    conv_p, conv_s = [], []
    for a in range(n_a):
        xp, cp, xs, cs = _a_layer(xp, xs, norm_a3, w_in_a, conv_w_a, mem_kv_b, state, mem_cache,
                                  a, a, mem_q_norm3, w_out_a)
        conv_p.append(cp)
        conv_s.append(cs.reshape(nb, CONV_W - 1, D_CONV))

    kvb0, kvb1, kvb2, st0, st1, st2, kv_new = _kv_shared(xp, xs, kv_norm, w_kv, k_norm_dil)
    kvbs = (kvb0, kvb1, kvb2)
    kv_new = kv_new.reshape(nb, N_GROUPS, KV_CHUNKS, HEAD_DIM)

    for bl in range(n_b):
        layer = n_a + bl
        xp, xs = _b_layer(xp, xs, norm_b3, w_in_b16, q_norm_b, mem_kv_b, kvbs, kv_new, caches,
                          mem_cache, bl, layer, mem_q_norm3, w_out_b16)

    dil_s = [kv_new[:, g].reshape(nb, 1, *kv_dims) for g in range(N_GROUPS)]
    return (xp, xs.reshape(nb, 1, D_MODEL),
            mem_kv_f.reshape(depth, b, MEM_LEN, *kv_dims),
            jnp.stack(conv_p), jnp.stack(conv_s),
            st0.reshape(b, WINDOWS[0], *kv_dims), st1.reshape(b, WINDOWS[1], *kv_dims),
            st2.reshape(b, s, *kv_dims), *dil_s)
```

```python
import functools

import jax
import jax.numpy as jnp
from jax import lax
from jax.experimental import pallas as pl
from jax.experimental.pallas import tpu as pltpu

F32 = jnp.float32
BF16 = jnp.bfloat16

D_MODEL = 1024
HEAD_DIM = 128
N_HEADS = 4
D_ATT = N_HEADS * HEAD_DIM
KV_CHUNKS = 2 * N_HEADS
D_CONV = D_MODEL
CONV_W = 3
N_GROUPS = 3
DILATIONS = (1, 4, 16)
WINDOWS = (128, 512, 2048)
BAND = 128
SUB_STRIDE = 4
MEM_LEN = 256
EPS = 1e-6
NEG = -1e30
SCALE = HEAD_DIM ** -0.5
QK_SCALE = SCALE * 1.4426950408889634

TM = 512
SAMPLE_GB = 8
CARRY = 8
W_STAGE_ROWS = 256
VMEM_LIMIT = 56 * 1024 * 1024


def _rms(x, g):
    return x * lax.rsqrt(jnp.mean(x * x, axis=-1, keepdims=True) + EPS) * g


def _silu(x):
    return x * (1.0 / (1.0 + jnp.exp(-x)))


def _dot(a, b):
    return jnp.dot(a, b, preferred_element_type=F32)


def _dot_nt(a, b):
    return lax.dot_general(a, b, (((1,), (1,)), ((), ())), preferred_element_type=F32)


def _hs(h, base=0):
    return slice(base + h * HEAD_DIM, base + (h + 1) * HEAD_DIM)


def _params(sem):
    return pltpu.CompilerParams(dimension_semantics=sem, vmem_limit_bytes=VMEM_LIMIT)


def _fixed_spec(block, index):
    return pl.BlockSpec(block, lambda *_: index, pipeline_mode=pl.Buffered(1))


def _const_spec(shape):
    return _fixed_spec(shape, (0,) * len(shape))


def _layer_spec(shape, layer):
    return _fixed_spec((None,) + tuple(shape), (layer,) + (0,) * len(shape))


def _cast_weight(w_hbm, dst_ref, stage_ref, sem_ref):
    rows = stage_ref.shape[1]
    n_chunks = w_hbm.shape[0] // rows

    def copy(c, slot):
        src = w_hbm.at[pl.ds(pl.multiple_of(c * rows, rows), rows)]
        return pltpu.make_async_copy(src, stage_ref.at[slot], sem_ref.at[slot])

    copy(0, 0).start()

    def body(c, carry):
        slot = c % 2

        @pl.when(c + 1 < n_chunks)
        def _():
            copy(c + 1, 1 - slot).start()

        copy(c, slot).wait()
        dst_ref[pl.ds(pl.multiple_of(c * rows, rows), rows), :] = stage_ref[slot].astype(BF16)
        return carry

    lax.fori_loop(0, n_chunks, body, 0)


def _weight_scratch(shape):
    k, n = shape
    assert k % W_STAGE_ROWS == 0
    return [pltpu.VMEM((k, n), BF16), pltpu.VMEM((2, W_STAGE_ROWS, n), F32),
            pltpu.SemaphoreType.DMA((2,))]


def _memkv_kernel(mem_ref, g_ref, w_ref, kg_ref, of_ref, ob_ref):
    rt = mem_ref.shape[0]
    xn = _rms(mem_ref[...], g_ref[...]).astype(BF16)
    kv = _dot(xn, w_ref[...].astype(BF16))
    for c in range(KV_CHUNKS):
        chunk = kv[:, _hs(c)]
        if c < N_HEADS:
            chunk = _rms(chunk, kg_ref[...])
        of_ref[pl.ds(c, rt, stride=KV_CHUNKS), :] = chunk
        ob_ref[:, _hs(c)] = chunk.astype(BF16)


def _memkv_prompt(mem, mem_norm, w_mem_kv, mem_k_norm):
    depth = w_mem_kv.shape[0]
    rows = mem.shape[0] * mem.shape[1]
    rt = 1024
    return pl.pallas_call(
        _memkv_kernel,
        grid=(depth, rows // rt),
        in_specs=[
            pl.BlockSpec((rt, D_MODEL), lambda l, i: (i, 0)),
            pl.BlockSpec((None, 1, D_MODEL), lambda l, i: (l, 0, 0)),
            pl.BlockSpec((None, D_MODEL, 2 * D_ATT), lambda l, i: (l, 0, 0)),
            pl.BlockSpec((None, 1, HEAD_DIM), lambda l, i: (l, 0, 0)),
        ],
        out_specs=[
            pl.BlockSpec((None, rt * KV_CHUNKS, HEAD_DIM), lambda l, i: (l, i, 0)),
            pl.BlockSpec((None, rt, 2 * D_ATT), lambda l, i: (l, i, 0)),
        ],
        out_shape=[jax.ShapeDtypeStruct((depth, rows * KV_CHUNKS, HEAD_DIM), F32),
                   jax.ShapeDtypeStruct((depth, rows, 2 * D_ATT), BF16)],
        compiler_params=_params(("parallel", "parallel")),
        name="memkv_prompt",
    )(mem.reshape(rows, D_MODEL), mem_norm.reshape(depth, 1, D_MODEL), w_mem_kv,
      mem_k_norm.reshape(depth, 1, HEAD_DIM))


def _pv_and_rowsum(p, v):
    ones = jnp.ones((v.shape[0], HEAD_DIM), BF16)
    both = _dot(p, jnp.concatenate([v, ones], axis=1))
    return both[:, :HEAD_DIM], both[:, HEAD_DIM:]


def _row_max(s):
    return jnp.max(s.astype(BF16), axis=-1, keepdims=True).astype(F32)


def _mem_attn(mq, mg, mkv_ref, gq, store):
    for h in range(N_HEADS):
        q = _rms(mq[:, _hs(h)], gq).astype(BF16)
        s = _dot_nt(q, mkv_ref[:, _hs(h)])
        m = _row_max(s)
        acc, l = _pv_and_rowsum(jnp.exp2(s - m).astype(BF16), mkv_ref[:, _hs(h, D_ATT)])
        store(h, acc / l * _silu(mg[:, _hs(h)]))


def _step_index():
    return pl.program_id(0) * pl.num_programs(1) + pl.program_id(1)


def _is_last_step():
    return _step_index() == pl.num_programs(0) * pl.num_programs(1) - 1


def _a_layer_kernel(x_ref, g_ref, win_hbm, cw_ref, mkv_ref, gq_ref, wout_hbm,
                    xs_ref, st_ref, mc_ref,
                    y_ref, cs_ref, ys_ref, sto_ref,
                    ext_ref, cat_ref, zs_ref, cats_ref,
                    win_ref, win_stage, win_sem, wout_ref, wout_stage, wout_sem, *, a):
    t = pl.program_id(1)
    n = _step_index()

    @pl.when(t == 0)
    def _():
        ext_ref[0:CARRY, :] = jnp.zeros((CARRY, D_CONV), F32)

    @pl.when(n == 0)
    def _():
        _cast_weight(win_hbm.at[a], win_ref, win_stage, win_sem)
        _cast_weight(wout_hbm.at[a], wout_ref, wout_stage, wout_sem)
        _a_sample_first(xs_ref, g_ref, win_ref, cw_ref, st_ref, sto_ref, zs_ref, cats_ref)

    row = zs_ref[pl.ds(n, 1), :]
    _merge_row(cats_ref, slice(D_CONV, D_CONV + D_ATT), n,
               _sample_mem_attn(row[:, :D_ATT], gq_ref[...], mc_ref[0], row[:, D_ATT:]))

    x = x_ref[...]
    xn = _rms(x, g_ref[...]).astype(BF16)

    def proj(lo, hi):
        return _dot(xn, win_ref[:, lo:hi])

    u = proj(D_CONV, 2 * D_CONV) * proj(0, D_CONV)
    ext_ref[CARRY:CARRY + TM, :] = u
    y = (cw_ref[0:1, :] * ext_ref[CARRY - 2:CARRY - 2 + TM, :]
         + cw_ref[1:2, :] * ext_ref[CARRY - 1:CARRY - 1 + TM, :] + cw_ref[2:3, :] * u)
    ext_ref[0:CARRY, :] = ext_ref[TM:TM + CARRY, :]
    cs_ref[...] = ext_ref[CARRY - (CONV_W - 1):CARRY, :]
    bg = proj(2 * D_CONV, 3 * D_CONV)
    g = proj(3 * D_CONV, 4 * D_CONV)
    cat_ref[:, 0:D_CONV] = (_silu(g) * bg * y).astype(BF16)
    mq = proj(4 * D_CONV, 4 * D_CONV + D_ATT)
    mg = proj(4 * D_CONV + D_ATT, 4 * D_CONV + 2 * D_ATT)

    def store(h, val):
        cat_ref[:, _hs(h, D_CONV)] = val.astype(BF16)

    _mem_attn(mq, mg, mkv_ref, gq_ref[...] * QK_SCALE, store)
    y_ref[...] = x + _dot(cat_ref[...], wout_ref[...])

    @pl.when(_is_last_step())
    def _():
        ys_ref[...] = xs_ref[...] + _dot(cats_ref[...].astype(BF16), wout_ref[...])


def _sample_specs(nb, nt, layer):
    full = lambda shape: pl.BlockSpec(shape, lambda i, t: (0,) * len(shape))
    mem_cache = pl.BlockSpec((None, 1, MEM_LEN, KV_CHUNKS, HEAD_DIM),
                             lambda i, t: (layer, i * nt + t, 0, 0, 0))
    return full, mem_cache


def _a_layer(x, xs, norm_a, w_in, conv_w, mkv_b, state, mem_cache, a, layer, mem_q_norm, w_out):
    b, s, _ = x.shape
    nb = xs.shape[0]
    nt = s // TM
    assert nb == b * nt
    st_rows = nb * (CONV_W - 1) * D_CONV // HEAD_DIM
    full, mem_cache_spec = _sample_specs(nb, nt, layer)
    return pl.pallas_call(
        functools.partial(_a_layer_kernel, a=a),
        grid=(b, nt),
        in_specs=[
            pl.BlockSpec((None, TM, D_MODEL), lambda i, t: (i, t, 0)),
            _layer_spec((1, D_MODEL), a),
            pl.BlockSpec(memory_space=pl.ANY),
            _layer_spec((CONV_W, D_CONV), a),
            pl.BlockSpec((None, None, MEM_LEN, 2 * D_ATT), lambda i, t: (layer, i, 0, 0)),
            _layer_spec((1, HEAD_DIM), layer),
            pl.BlockSpec(memory_space=pl.ANY),
            full((nb, D_MODEL)),
            _layer_spec((st_rows, HEAD_DIM), a),
            mem_cache_spec,
        ],
        out_specs=[
            pl.BlockSpec((None, TM, D_MODEL), lambda i, t: (i, t, 0)),
            pl.BlockSpec((None, CONV_W - 1, D_CONV), lambda i, t: (i, 0, 0)),
            full((nb, D_MODEL)),
            full((st_rows, HEAD_DIM)),
        ],
        out_shape=[jax.ShapeDtypeStruct((b, s, D_MODEL), F32),
                   jax.ShapeDtypeStruct((b, CONV_W - 1, D_CONV), F32),
                   jax.ShapeDtypeStruct((nb, D_MODEL), F32),
                   jax.ShapeDtypeStruct((st_rows, HEAD_DIM), F32)],
        scratch_shapes=[pltpu.VMEM((TM + CARRY, D_CONV), F32),
                        pltpu.VMEM((TM, D_CONV + D_ATT), BF16),
                        pltpu.VMEM((nb, 2 * D_ATT), F32),
                        pltpu.VMEM((nb, D_CONV + D_ATT), F32)]
        + _weight_scratch(w_in.shape[1:]) + _weight_scratch(w_out.shape[1:]),
        compiler_params=_params(("arbitrary", "arbitrary")),
        name=f"a_layer_{layer}",
    )(x, norm_a, w_in, conv_w, mkv_b, mem_q_norm, w_out, xs, state, mem_cache)


def _residue_rows(slab_ref, split_ref, c, d):
    rows = slab_ref.shape[1]
    if d == SUB_STRIDE:
        return lambda r: slab_ref[c, pl.ds(r, rows // d, stride=d), :]
    assert d == SUB_STRIDE * SUB_STRIDE
    for r1 in range(SUB_STRIDE):
        split_ref[c, r1] = slab_ref[c, pl.ds(r1, rows // SUB_STRIDE, stride=SUB_STRIDE), :]
    return lambda r: split_ref[c, r % SUB_STRIDE,
                               pl.ds(r // SUB_STRIDE, rows // d, stride=SUB_STRIDE), :]


def _to_residue_major(dst_ref, slab_ref, split_ref, n_chunks, d):
    for c in range(n_chunks):
        get = _residue_rows(slab_ref, split_ref, c, d)
        for r in range(d):
            dst_ref[r, :, _hs(c)] = get(r).astype(dst_ref.dtype)


def _kv_kernel(x_ref, g_ref, w_hbm, kn_ref, xs_ref, kvb0_ref, kvb1_ref, kvb2_ref,
               st0_ref, st1_ref, st2_ref, kvn_ref, slab1_ref, slab2_ref, split_ref,
               w_ref, w_stage, w_sem):
    t = pl.program_id(1)
    last = pl.num_programs(1) - 1

    @pl.when(_step_index() == 0)
    def _():
        _cast_weight(w_hbm, w_ref, w_stage, w_sem)
        _kv_sample_rows(xs_ref, g_ref, w_ref, kn_ref, kvn_ref)

    xn = _rms(x_ref[...], g_ref[...]).astype(BF16)
    kvb_refs = (kvb0_ref, kvb1_ref, kvb2_ref)
    slabs = (None, slab1_ref, slab2_ref)
    for g in reversed(range(N_GROUPS)):
        kv = _dot(xn, w_ref[:, g * 2 * D_ATT:(g + 1) * 2 * D_ATT])
        chunks = [_rms(kv[:, _hs(h)], kn_ref[g:g + 1, :]) for h in range(N_HEADS)]
        chunks += [kv[:, _hs(h, D_ATT)] for h in range(N_HEADS)]
        d = DILATIONS[g]
        for c, chunk in enumerate(chunks):
            if d == 1:
                kvb_refs[g][0, :, _hs(c)] = chunk.astype(BF16)
            else:
                slabs[g][c] = chunk
            if g == 2:
                st2_ref[pl.ds(c, TM, stride=KV_CHUNKS), :] = chunk
        if d > 1:
            _to_residue_major(kvb_refs[g], slabs[g], split_ref, KV_CHUNKS, d)
        if g == 1:
            @pl.when(t == last)
            def _():
                for c, chunk in enumerate(chunks):
                    st1_ref[pl.ds(c, TM, stride=KV_CHUNKS), :] = chunk
        if g == 0:
            @pl.when(t == last)
            def _():
                for c, chunk in enumerate(chunks):
                    st0_ref[pl.ds(c, WINDOWS[0], stride=KV_CHUNKS), :] = chunk[TM - WINDOWS[0]:, :]


def _kv_shared(x, xs, kv_norm, w_kv, k_norm_dil):
    b, s, _ = x.shape
    nb = xs.shape[0]
    kvn_rows = nb * N_GROUPS * KV_CHUNKS
    assert WINDOWS[1] == TM and WINDOWS[2] == s and WINDOWS[0] <= TM
    first = lambda i, t: (i, 0, 0)
    res_spec = lambda d: pl.BlockSpec((None, d, TM // d, 2 * D_ATT), lambda i, t: (i, 0, t, 0))
    res_sds = lambda d: jax.ShapeDtypeStruct((b, d, s // d, 2 * D_ATT), BF16)
    st_sds = lambda rows: jax.ShapeDtypeStruct((b, rows * KV_CHUNKS, HEAD_DIM), F32)
    return pl.pallas_call(
        _kv_kernel,
        grid=(b, s // TM),
        in_specs=[
            pl.BlockSpec((None, TM, D_MODEL), lambda i, t: (i, t, 0)),
            _const_spec((1, D_MODEL)),
            pl.BlockSpec(memory_space=pl.ANY),
            _const_spec((N_GROUPS, HEAD_DIM)),
            _const_spec((nb, D_MODEL)),
        ],
        out_specs=[
            res_spec(DILATIONS[0]), res_spec(DILATIONS[1]), res_spec(DILATIONS[2]),
            pl.BlockSpec((None, WINDOWS[0] * KV_CHUNKS, HEAD_DIM), first),
            pl.BlockSpec((None, WINDOWS[1] * KV_CHUNKS, HEAD_DIM), first),
            pl.BlockSpec((None, TM * KV_CHUNKS, HEAD_DIM), lambda i, t: (i, t, 0)),
            pl.BlockSpec((kvn_rows, HEAD_DIM), lambda i, t: (0, 0)),
        ],
        out_shape=[res_sds(DILATIONS[0]), res_sds(DILATIONS[1]), res_sds(DILATIONS[2]),
                   st_sds(WINDOWS[0]), st_sds(WINDOWS[1]), st_sds(s),
                   jax.ShapeDtypeStruct((kvn_rows, HEAD_DIM), F32)],
        scratch_shapes=[pltpu.VMEM((KV_CHUNKS, TM, HEAD_DIM), F32),
                        pltpu.VMEM((KV_CHUNKS, TM, HEAD_DIM), F32),
                        pltpu.VMEM((KV_CHUNKS, SUB_STRIDE, TM // SUB_STRIDE, HEAD_DIM), F32)]
        + _weight_scratch(w_kv.shape),
        compiler_params=_params(("arbitrary", "arbitrary")),
        name="kv_prompt",
    )(x, kv_norm.reshape(1, D_MODEL), w_kv, k_norm_dil, xs)


def _band_scores(qs, ks, masks):
    return jnp.concatenate([jnp.where(mk, _dot_nt(q, k), NEG) for q, k, mk in zip(qs, ks, masks)],
                           axis=0)


def _band_finish(s, vs, nq):
    m = _row_max(s)
    pb = jnp.exp2(s - m).astype(BF16)
    res = []
    for i, v in enumerate(vs):
        rows = slice(i * nq, (i + 1) * nq)
        acc, l = _pv_and_rowsum(pb[rows], v)
        res.append((acc * (1.0 / l), m[rows] + jnp.log2(l)))
    return res


def _b_layer_kernel(x_ref, g_ref, win_ref, qn_ref, mkv_ref, gq_ref, wout_ref,
                    kv0_ref, kv0p_ref, kv1_ref, kv1p_ref, kv2_ref,
                    xs_ref, kvn_ref, c0_ref, c1_ref, c2_ref, mc_ref,
                    y_ref, ys_ref,
                    qslab1_ref, qslab2_ref, qsplit_ref, oslab_ref, lslab_ref, cat_ref,
                    zs_ref, cats_ref):
    t = pl.program_id(1)
    n = _step_index()

    @pl.when(n == 0)
    def _():
        zs_ref[...] = _dot(_rms(xs_ref[...], g_ref[...]).astype(BF16), win_ref[...])
        cats_ref[...] = jnp.zeros(cats_ref.shape, F32)

    _merge_row(cats_ref, slice(0, 2 * D_ATT), n,
               _b_sample_row(zs_ref[pl.ds(n, 1), :], kvn_ref[n], qn_ref, gq_ref[...],
                             (c0_ref[0], c1_ref[0], c2_ref[0]), mc_ref[0]))

    x = x_ref[...]
    xn = _rms(x, g_ref[...]).astype(BF16)

    def proj(lo, hi):
        return _dot(xn, win_ref[:, lo:hi])

    qi = lax.broadcasted_iota(jnp.int32, (BAND, 2 * BAND), 0)
    kj = lax.broadcasted_iota(jnp.int32, (BAND, 2 * BAND), 1)
    mask_rest = (kj >= qi) & (kj <= qi + BAND)
    mask_edge = (kj >= jnp.maximum(qi, jnp.where(t == 0, BAND, 0))) & (kj <= qi + BAND)

    def finish(g, h, rows, out, lse):
        oslab_ref[g, h, rows, :] = out
        lslab_ref[g, h, rows, :] = lse

    def normed_heads(g):
        qd = proj(g * D_ATT, (g + 1) * D_ATT)
        gain = qn_ref[g:g + 1, :] * QK_SCALE
        return [_rms(qd[:, _hs(h)], gain) for h in range(N_HEADS)]

    n_sub = TM // BAND
    qh = [q.astype(BF16) for q in normed_heads(0)]
    units0 = [(i, h) for i in range(n_sub) for h in range(N_HEADS)]

    def keys0(i, cols):
        if i == 0:
            return jnp.concatenate([kv0p_ref[0, :, cols], kv0_ref[0, 0:BAND, cols]], axis=0)
        return kv0_ref[0, (i - 1) * BAND:(i + 1) * BAND, cols]

    d1 = DILATIONS[1]
    for h, q in enumerate(normed_heads(1)):
        qslab1_ref[h] = q
    get_q1 = [_residue_rows(qslab1_ref, None, h, d1) for h in range(N_HEADS)]
    units1 = [(r, h) for r in range(d1) for h in range(N_HEADS)]

    def keys1(r, cols):
        return jnp.concatenate([kv1p_ref[r, :, cols], kv1_ref[r, :, cols]], axis=0)

    d2 = DILATIONS[2]
    nq = TM // d2
    for h, q in enumerate(normed_heads(2)):
        qslab2_ref[h] = q
    get_q2 = [_residue_rows(qslab2_ref, qsplit_ref, h, d2) for h in range(N_HEADS)]
    units2 = [(r, h) for r in range(d2) for h in range(N_HEADS)]
    qi2 = lax.broadcasted_iota(jnp.int32, (nq, BAND), 0)
    kj2 = lax.broadcasted_iota(jnp.int32, (nq, BAND), 1)
    mask2 = kj2 <= qi2 + t * nq

    s0 = _band_scores([qh[h][i * BAND:(i + 1) * BAND] for i, h in units0],
                      [keys0(i, _hs(h)) for i, h in units0],
                      [mask_edge if i == 0 else mask_rest for i, h in units0])
    s1 = _band_scores([get_q1[h](r).astype(BF16) for r, h in units1],
                      [keys1(r, _hs(h)) for r, h in units1], [mask_edge] * len(units1))
    s2 = _band_scores([get_q2[h](r).astype(BF16) for r, h in units2],
                      [kv2_ref[r, :, _hs(h)] for r, h in units2], [mask2] * len(units2))
    res = _band_finish(s0, [keys0(i, _hs(h, D_ATT)) for i, h in units0], BAND)
    for (i, h), (out, lse) in zip(units0, res):
        finish(0, h, pl.ds(i * BAND, BAND), out, lse)
    res = _band_finish(s1, [keys1(r, _hs(h, D_ATT)) for r, h in units1], BAND)
    for (r, h), (out, lse) in zip(units1, res):
        finish(1, h, pl.ds(r, BAND, stride=d1), out, lse)
    res = _band_finish(s2, [kv2_ref[r, :, _hs(h, D_ATT)] for r, h in units2], nq)
    for (r, h), (out, lse) in zip(units2, res):
        finish(2, h, pl.ds(r, nq, stride=d2), out, lse)

    base = N_GROUPS * D_ATT
    gd = proj(base, base + D_ATT)
    for h in range(N_HEADS):
        l0, l1, l2 = lslab_ref[0, h], lslab_ref[1, h], lslab_ref[2, h]
        m = jnp.maximum(jnp.maximum(l0, l1), l2)
        e0, e1, e2 = jnp.exp2(l0 - m), jnp.exp2(l1 - m), jnp.exp2(l2 - m)
        dil = ((oslab_ref[0, h] * e0 + oslab_ref[1, h] * e1 + oslab_ref[2, h] * e2)
               * (1.0 / (e0 + e1 + e2)))
        cat_ref[:, _hs(h)] = (dil * _silu(gd[:, _hs(h)])).astype(BF16)
    mq = proj(base + D_ATT, base + 2 * D_ATT)
    mg = proj(base + 2 * D_ATT, base + 3 * D_ATT)

    def store(h, val):
        cat_ref[:, _hs(h, D_ATT)] = val.astype(BF16)

    _mem_attn(mq, mg, mkv_ref, gq_ref[...] * QK_SCALE, store)
    y_ref[...] = x + _dot(cat_ref[...], wout_ref[...])

    @pl.when(_is_last_step())
    def _():
        ys_ref[...] = xs_ref[...] + _dot(cats_ref[...].astype(BF16), wout_ref[...])


def _b_layer(x, xs, norm_b, w_in, q_norm, mkv_b, kvbs, kv_new, caches, mem_cache, bl, layer,
             mem_q_norm, w_out):
    b, s, _ = x.shape
    nb = xs.shape[0]
    nt = s // TM
    assert nb == b * nt
    full, mem_cache_spec = _sample_specs(nb, nt, layer)
    cache_spec = pl.BlockSpec((1, BAND, None, KV_CHUNKS, HEAD_DIM),
                              lambda i, t: (i * nt + t, 0, 0, 0, 0))
    assert TM // BAND == DILATIONS[1] and s // DILATIONS[2] == BAND
    tile = lambda i, t: (i, t, 0)
    sub = TM // BAND
    kv_w = 2 * D_ATT
    return pl.pallas_call(
        _b_layer_kernel,
        grid=(b, s // TM),
        in_specs=[
            pl.BlockSpec((None, TM, D_MODEL), tile),
            _layer_spec((1, D_MODEL), bl),
            _layer_spec(w_in.shape[1:], bl),
            _layer_spec((N_GROUPS, HEAD_DIM), bl),
            pl.BlockSpec((None, None, MEM_LEN, kv_w), lambda i, t: (layer, i, 0, 0)),
            _layer_spec((1, HEAD_DIM), layer),
            _layer_spec(w_out.shape[1:], bl),
            pl.BlockSpec((None, 1, TM, kv_w), lambda i, t: (i, 0, t, 0)),
            pl.BlockSpec((None, 1, BAND, kv_w),
                         lambda i, t: (i, 0, jnp.maximum(t * sub - 1, 0), 0)),
            pl.BlockSpec((None, DILATIONS[1], BAND, kv_w), lambda i, t: (i, 0, t, 0)),
            pl.BlockSpec((None, DILATIONS[1], BAND, kv_w),
                         lambda i, t: (i, 0, jnp.maximum(t - 1, 0), 0)),
            pl.BlockSpec((None, DILATIONS[2], BAND, kv_w), lambda i, t: (i, 0, 0, 0)),
            full((nb, D_MODEL)),
            full(kv_new.shape),
            cache_spec, cache_spec, cache_spec,
            mem_cache_spec,
        ],
        out_specs=[pl.BlockSpec((None, TM, D_MODEL), tile), full((nb, D_MODEL))],
        out_shape=[jax.ShapeDtypeStruct((b, s, D_MODEL), F32),
                   jax.ShapeDtypeStruct((nb, D_MODEL), F32)],
        scratch_shapes=[pltpu.VMEM((N_HEADS, TM, HEAD_DIM), F32),
                        pltpu.VMEM((N_HEADS, TM, HEAD_DIM), F32),
                        pltpu.VMEM((N_HEADS, SUB_STRIDE, TM // SUB_STRIDE, HEAD_DIM), F32),
                        pltpu.VMEM((N_GROUPS, N_HEADS, TM, HEAD_DIM), F32),
                        pltpu.VMEM((N_GROUPS, N_HEADS, TM, HEAD_DIM), F32),
                        pltpu.VMEM((TM, 2 * D_ATT), BF16),
                        pltpu.VMEM((nb, w_in.shape[2]), F32),
                        pltpu.VMEM((nb, 2 * D_ATT), F32)],
        compiler_params=_params(("arbitrary", "arbitrary")),
        name=f"b_layer_{layer}",
    )(x, norm_b, w_in, q_norm, mkv_b, mem_q_norm, w_out,
      kvbs[0], kvbs[0], kvbs[1], kvbs[1], kvbs[2],
      xs, kv_new, *caches, mem_cache)


def _swap_halves(x, axis):
    return pltpu.roll(x, N_HEADS, axis=axis)


def _head_rows(tile):
    return jnp.concatenate([tile[N_HEADS + h:N_HEADS + h + 1, :] for h in range(N_HEADS)],
                           axis=-1)


def _low_tile(row):
    return jnp.concatenate([jnp.zeros((N_HEADS, HEAD_DIM), F32)]
                           + [row[:, _hs(h)] for h in range(N_HEADS)], axis=0)


def _q_tile(row, g):
    return jnp.concatenate([_rms(row[:, _hs(h)], g) * QK_SCALE for h in range(N_HEADS)]
                           + [jnp.zeros((N_HEADS, HEAD_DIM), F32)], axis=0)


def _tile_attend(qt, kv):
    s = jnp.sum(kv * qt[None], axis=-1, keepdims=True)
    m = jnp.max(s, axis=0)
    p = jnp.exp2(s - m[None])
    l = jnp.sum(p, axis=0)
    acc = jnp.sum(_swap_halves(jnp.broadcast_to(p, kv.shape), 1) * kv, axis=0)
    out = acc / _swap_halves(jnp.broadcast_to(l, acc.shape), 0)
    return out, m + jnp.log2(l)


def _sample_mem_attn(q_row, gq, mkv, mg_row):
    out, _ = _tile_attend(_q_tile(q_row, gq), mkv)
    return _head_rows(out * _low_tile(_silu(mg_row)))


def _gather_chunks(ref, n_rows, n_chunks, offset=0):
    total = ref.shape[0] // n_rows
    return jnp.concatenate([ref[pl.ds(offset + c, n_rows, stride=total), :]
                            for c in range(n_chunks)], axis=-1)


def _merge_row(ref, cols, n, row):
    tile = pl.ds(pl.multiple_of((n // SAMPLE_GB) * SAMPLE_GB, SAMPLE_GB), SAMPLE_GB)
    sub = lax.broadcasted_iota(jnp.int32, (SAMPLE_GB, row.shape[1]), 0)
    ref[tile, cols] = jnp.where(sub == n % SAMPLE_GB, row, ref[tile, cols])


def _a_sample_first(x_ref, g_ref, win_ref, cw_ref, st_ref, sto_ref, zm_ref, cat_ref):
    nb = x_ref.shape[0]
    n_st = D_CONV // HEAD_DIM
    xn = _rms(x_ref[...], g_ref[...]).astype(BF16)
    z = _dot(xn, win_ref[...])
    u = z[:, D_CONV:2 * D_CONV] * z[:, 0:D_CONV]
    s0 = _gather_chunks(st_ref, nb, n_st)
    s1 = _gather_chunks(st_ref, nb, n_st, n_st)
    y = cw_ref[0:1, :] * s0 + cw_ref[1:2, :] * s1 + cw_ref[2:3, :] * u
    for c in range(n_st):
        sto_ref[pl.ds(c, nb, stride=2 * n_st), :] = s1[:, _hs(c)]
        sto_ref[pl.ds(n_st + c, nb, stride=2 * n_st), :] = u[:, _hs(c)]
    cat_ref[:, 0:D_CONV] = _silu(z[:, 3 * D_CONV:4 * D_CONV]) * z[:, 2 * D_CONV:3 * D_CONV] * y
    cat_ref[:, D_CONV:] = jnp.zeros((nb, D_ATT), F32)
    zm_ref[...] = z[:, 4 * D_CONV:]


def _kv_sample_rows(x_ref, g_ref, w_ref, kn_ref, o_ref):
    nb = x_ref.shape[0]
    xn = _rms(x_ref[...], g_ref[...]).astype(BF16)
    kv = _dot(xn, w_ref[...])
    n_chunks = N_GROUPS * KV_CHUNKS
    for g in range(N_GROUPS):
        for c in range(KV_CHUNKS):
            chunk = kv[:, _hs(c, g * 2 * D_ATT)]
            if c < N_HEADS:
                chunk = _rms(chunk, kn_ref[g:g + 1, :])
            o_ref[pl.ds(g * KV_CHUNKS + c, nb, stride=n_chunks), :] = chunk


def _b_sample_row(row, kv_new, qn_ref, gq, caches, mkv):
    base = N_GROUPS * D_ATT
    outs, lses = [], []
    for g in range(N_GROUPS):
        qt = _q_tile(row[:, g * D_ATT:(g + 1) * D_ATT], qn_ref[g:g + 1, :])
        out, lse = _tile_attend(qt, jnp.concatenate([caches[g], kv_new[g][None]], axis=0))
        outs.append(out)
        lses.append(lse)
    m = jnp.maximum(jnp.maximum(lses[0], lses[1]), lses[2])
    es = [jnp.exp2(l - m) for l in lses]
    den = es[0] + es[1] + es[2]
    ws = [_swap_halves(jnp.broadcast_to(e / den, (KV_CHUNKS, HEAD_DIM)), 0) for e in es]
    dil = outs[0] * ws[0] + outs[1] * ws[1] + outs[2] * ws[2]
    dil_row = _head_rows(dil * _low_tile(_silu(row[:, base:base + D_ATT])))
    mem_row = _sample_mem_attn(row[:, base + D_ATT:base + 2 * D_ATT], gq, mkv,
                               row[:, base + 2 * D_ATT:])
    return jnp.concatenate([dil_row, mem_row], axis=-1)


def kernel(x_prompt, x_sample, cache_mem_kv, state_conv, cache_dil0_kv, cache_dil1_kv, cache_dil2_kv, mem_prompt, norm_a, w_in_a, conv_w_a, w_out_a, norm_b, w_in_b, q_norm_b, w_out_b, kv_norm, w_kv, k_norm_dil, mem_norm, w_mem_kv, mem_k_norm, mem_q_norm):
    n_a = w_in_a.shape[0]
    n_b = w_in_b.shape[0]
    depth = n_a + n_b
    b, s, _ = x_prompt.shape
    nb = x_sample.shape[0]
    assert x_sample.shape[1] == 1 and s % TM == 0 and nb % SAMPLE_GB == 0
    kv_dims = (2, N_HEADS, HEAD_DIM)

    w_in_b16, w_out_b16 = w_in_b.astype(BF16), w_out_b.astype(BF16)
    norm_a3 = norm_a.reshape(n_a, 1, D_MODEL)
    norm_b3 = norm_b.reshape(n_b, 1, D_MODEL)
    mem_q_norm3 = mem_q_norm.reshape(depth, 1, HEAD_DIM)

    mem_kv_f, mem_kv_b = _memkv_prompt(mem_prompt, mem_norm, w_mem_kv, mem_k_norm)
    mem_kv_b = mem_kv_b.reshape(depth, b, MEM_LEN, 2 * D_ATT)
    mem_cache = cache_mem_kv.reshape(depth, nb, MEM_LEN, KV_CHUNKS, HEAD_DIM)
    caches = []
    for g, cache in enumerate((cache_dil0_kv, cache_dil1_kv, cache_dil2_kv)):
        assert cache.shape[1] == WINDOWS[g]
        caches.append(cache.reshape(nb, BAND, DILATIONS[g], KV_CHUNKS, HEAD_DIM))
    state = state_conv.reshape(n_a, -1, HEAD_DIM)

    xp = x_prompt
    xs = x_sample.reshape(nb, D_MODEL)
    conv_p, conv_s = [], []
    for a in range(n_a):
        xp, cp, xs, cs = _a_layer(xp, xs, norm_a3, w_in_a, conv_w_a, mem_kv_b, state, mem_cache,
                                  a, a, mem_q_norm3, w_out_a)
        conv_p.append(cp)
        conv_s.append(cs.reshape(nb, CONV_W - 1, D_CONV))

    kvb0, kvb1, kvb2, st0, st1, st2, kv_new = _kv_shared(xp, xs, kv_norm, w_kv, k_norm_dil)
    kvbs = (kvb0, kvb1, kvb2)
    kv_new = kv_new.reshape(nb, N_GROUPS, KV_CHUNKS, HEAD_DIM)

    for bl in range(n_b):
        layer = n_a + bl
        xp, xs = _b_layer(xp, xs, norm_b3, w_in_b16, q_norm_b, mem_kv_b, kvbs, kv_new, caches,
                          mem_cache, bl, layer, mem_q_norm3, w_out_b16)

    dil_s = [kv_new[:, g].reshape(nb, 1, *kv_dims) for g in range(N_GROUPS)]
    return (xp, xs.reshape(nb, 1, D_MODEL),
            mem_kv_f.reshape(depth, b, MEM_LEN, *kv_dims),
            jnp.stack(conv_p), jnp.stack(conv_s),
            st0.reshape(b, WINDOWS[0], *kv_dims), st1.reshape(b, WINDOWS[1], *kv_dims),
            st2.reshape(b, s, *kv_dims), *dil_s)
```

```python
import functools

import jax
import jax.numpy as jnp
from jax import lax
from jax.experimental import pallas as pl
from jax.experimental.pallas import tpu as pltpu

F32 = jnp.float32
BF16 = jnp.bfloat16

D_MODEL = 1024
HEAD_DIM = 128
N_HEADS = 4
D_ATT = N_HEADS * HEAD_DIM
KV_CHUNKS = 2 * N_HEADS
D_CONV = D_MODEL
CONV_W = 3
N_GROUPS = 3
DILATIONS = (1, 4, 16)
WINDOWS = (128, 512, 2048)
BAND = 128
SUB_STRIDE = 4
MEM_LEN = 256
EPS = 1e-6
NEG = -1e30
SCALE = HEAD_DIM ** -0.5
QK_SCALE = SCALE * 1.4426950408889634

TM = 512
SAMPLE_GB = 8
CARRY = 8
W_STAGE_ROWS = 256
VMEM_LIMIT = 56 * 1024 * 1024


def _rms(x, g):
    return x * lax.rsqrt(jnp.mean(x * x, axis=-1, keepdims=True) + EPS) * g


def _silu(x):
    return x * (1.0 / (1.0 + jnp.exp(-x)))


def _dot(a, b):
    return jnp.dot(a, b, preferred_element_type=F32)


def _dot_nt(a, b):
    return lax.dot_general(a, b, (((1,), (1,)), ((), ())), preferred_element_type=F32)


def _hs(h, base=0):
    return slice(base + h * HEAD_DIM, base + (h + 1) * HEAD_DIM)


def _params(sem):
    return pltpu.CompilerParams(dimension_semantics=sem, vmem_limit_bytes=VMEM_LIMIT)


def _fixed_spec(block, index):
    return pl.BlockSpec(block, lambda *_: index, pipeline_mode=pl.Buffered(1))


def _const_spec(shape):
    return _fixed_spec(shape, (0,) * len(shape))


def _layer_spec(shape, layer):
    return _fixed_spec((None,) + tuple(shape), (layer,) + (0,) * len(shape))


def _cast_weight(w_hbm, dst_ref, stage_ref, sem_ref):
    rows = stage_ref.shape[1]
    n_chunks = w_hbm.shape[0] // rows

    def copy(c, slot):
        src = w_hbm.at[pl.ds(pl.multiple_of(c * rows, rows), rows)]
        return pltpu.make_async_copy(src, stage_ref.at[slot], sem_ref.at[slot])

    copy(0, 0).start()

    def body(c, carry):
        slot = c % 2

        @pl.when(c + 1 < n_chunks)
        def _():
            copy(c + 1, 1 - slot).start()

        copy(c, slot).wait()
        dst_ref[pl.ds(pl.multiple_of(c * rows, rows), rows), :] = stage_ref[slot].astype(BF16)
        return carry

    lax.fori_loop(0, n_chunks, body, 0)


def _weight_scratch(shape):
    k, n = shape
    assert k % W_STAGE_ROWS == 0
    return [pltpu.VMEM((k, n), BF16), pltpu.VMEM((2, W_STAGE_ROWS, n), F32),
            pltpu.SemaphoreType.DMA((2,))]


def _memkv_kernel(mem_ref, g_ref, w_ref, kg_ref, of_ref, ob_ref):
    rt = mem_ref.shape[0]
    xn = _rms(mem_ref[...], g_ref[...]).astype(BF16)
    kv = _dot(xn, w_ref[...].astype(BF16))
    for c in range(KV_CHUNKS):
        chunk = kv[:, _hs(c)]
        if c < N_HEADS:
            chunk = _rms(chunk, kg_ref[...])
        of_ref[pl.ds(c, rt, stride=KV_CHUNKS), :] = chunk
        ob_ref[:, _hs(c)] = chunk.astype(BF16)


def _memkv_prompt(mem, mem_norm, w_mem_kv, mem_k_norm):
    depth = w_mem_kv.shape[0]
    rows = mem.shape[0] * mem.shape[1]
    rt = 1024
    return pl.pallas_call(
        _memkv_kernel,
        grid=(depth, rows // rt),
        in_specs=[
            pl.BlockSpec((rt, D_MODEL), lambda l, i: (i, 0)),
            pl.BlockSpec((None, 1, D_MODEL), lambda l, i: (l, 0, 0)),
            pl.BlockSpec((None, D_MODEL, 2 * D_ATT), lambda l, i: (l, 0, 0)),
            pl.BlockSpec((None, 1, HEAD_DIM), lambda l, i: (l, 0, 0)),
        ],
        out_specs=[
            pl.BlockSpec((None, rt * KV_CHUNKS, HEAD_DIM), lambda l, i: (l, i, 0)),
            pl.BlockSpec((None, rt, 2 * D_ATT), lambda l, i: (l, i, 0)),
        ],
        out_shape=[jax.ShapeDtypeStruct((depth, rows * KV_CHUNKS, HEAD_DIM), F32),
                   jax.ShapeDtypeStruct((depth, rows, 2 * D_ATT), BF16)],
        compiler_params=_params(("parallel", "parallel")),
        name="memkv_prompt",
    )(mem.reshape(rows, D_MODEL), mem_norm.reshape(depth, 1, D_MODEL), w_mem_kv,
      mem_k_norm.reshape(depth, 1, HEAD_DIM))


def _pv_and_rowsum(p, v):
    ones = jnp.ones((v.shape[0], HEAD_DIM), BF16)
    both = _dot(p, jnp.concatenate([v, ones], axis=1))
    return both[:, :HEAD_DIM], both[:, HEAD_DIM:]


def _mem_attn(mq, mg, mkv_ref, gq, store):
    for h in range(N_HEADS):
        q = _rms(mq[:, _hs(h)], gq).astype(BF16)
        s = _dot_nt(q, mkv_ref[:, _hs(h)])
        m = jnp.max(s, axis=-1, keepdims=True)
        acc, l = _pv_and_rowsum(jnp.exp2(s - m).astype(BF16), mkv_ref[:, _hs(h, D_ATT)])
        store(h, acc / l * _silu(mg[:, _hs(h)]))


def _step_index():
    return pl.program_id(0) * pl.num_programs(1) + pl.program_id(1)


def _is_last_step():
    return _step_index() == pl.num_programs(0) * pl.num_programs(1) - 1


def _a_layer_kernel(x_ref, g_ref, win_hbm, cw_ref, mkv_ref, gq_ref, wout_hbm,
                    xs_ref, st_ref, mc_ref,
                    y_ref, cs_ref, ys_ref, sto_ref,
                    ext_ref, cat_ref, zs_ref, cats_ref,
                    win_ref, win_stage, win_sem, wout_ref, wout_stage, wout_sem, *, a):
    t = pl.program_id(1)
    n = _step_index()

    @pl.when(t == 0)
    def _():
        ext_ref[0:CARRY, :] = jnp.zeros((CARRY, D_CONV), F32)

    @pl.when(n == 0)
    def _():
        _cast_weight(win_hbm.at[a], win_ref, win_stage, win_sem)
        _cast_weight(wout_hbm.at[a], wout_ref, wout_stage, wout_sem)
        _a_sample_first(xs_ref, g_ref, win_ref, cw_ref, st_ref, sto_ref, zs_ref, cats_ref)

    row = zs_ref[pl.ds(n, 1), :]
    _merge_row(cats_ref, slice(D_CONV, D_CONV + D_ATT), n,
               _sample_mem_attn(row[:, :D_ATT], gq_ref[...], mc_ref[0], row[:, D_ATT:]))

    x = x_ref[...]
    xn = _rms(x, g_ref[...]).astype(BF16)

    def proj(lo, hi):
        return _dot(xn, win_ref[:, lo:hi])

    u = proj(D_CONV, 2 * D_CONV) * proj(0, D_CONV)
    ext_ref[CARRY:CARRY + TM, :] = u
    y = (cw_ref[0:1, :] * ext_ref[CARRY - 2:CARRY - 2 + TM, :]
         + cw_ref[1:2, :] * ext_ref[CARRY - 1:CARRY - 1 + TM, :] + cw_ref[2:3, :] * u)
    ext_ref[0:CARRY, :] = ext_ref[TM:TM + CARRY, :]
    cs_ref[...] = ext_ref[CARRY - (CONV_W - 1):CARRY, :]
    bg = proj(2 * D_CONV, 3 * D_CONV)
    g = proj(3 * D_CONV, 4 * D_CONV)
    cat_ref[:, 0:D_CONV] = (_silu(g) * bg * y).astype(BF16)
    mq = proj(4 * D_CONV, 4 * D_CONV + D_ATT)
    mg = proj(4 * D_CONV + D_ATT, 4 * D_CONV + 2 * D_ATT)

    def store(h, val):
        cat_ref[:, _hs(h, D_CONV)] = val.astype(BF16)

    _mem_attn(mq, mg, mkv_ref, gq_ref[...] * QK_SCALE, store)
    y_ref[...] = x + _dot(cat_ref[...], wout_ref[...])

    @pl.when(_is_last_step())
    def _():
        ys_ref[...] = xs_ref[...] + _dot(cats_ref[...].astype(BF16), wout_ref[...])


def _sample_specs(nb, nt, layer):
    full = lambda shape: pl.BlockSpec(shape, lambda i, t: (0,) * len(shape))
    mem_cache = pl.BlockSpec((None, 1, MEM_LEN, KV_CHUNKS, HEAD_DIM),
                             lambda i, t: (layer, i * nt + t, 0, 0, 0))
    return full, mem_cache


def _a_layer(x, xs, norm_a, w_in, conv_w, mkv_b, state, mem_cache, a, layer, mem_q_norm, w_out):
    b, s, _ = x.shape
    nb = xs.shape[0]
    nt = s // TM
    assert nb == b * nt
    st_rows = nb * (CONV_W - 1) * D_CONV // HEAD_DIM
    full, mem_cache_spec = _sample_specs(nb, nt, layer)
    return pl.pallas_call(
        functools.partial(_a_layer_kernel, a=a),
        grid=(b, nt),
        in_specs=[
            pl.BlockSpec((None, TM, D_MODEL), lambda i, t: (i, t, 0)),
            _layer_spec((1, D_MODEL), a),
            pl.BlockSpec(memory_space=pl.ANY),
            _layer_spec((CONV_W, D_CONV), a),
            pl.BlockSpec((None, None, MEM_LEN, 2 * D_ATT), lambda i, t: (layer, i, 0, 0)),
            _layer_spec((1, HEAD_DIM), layer),
            pl.BlockSpec(memory_space=pl.ANY),
            full((nb, D_MODEL)),
            _layer_spec((st_rows, HEAD_DIM), a),
            mem_cache_spec,
        ],
        out_specs=[
            pl.BlockSpec((None, TM, D_MODEL), lambda i, t: (i, t, 0)),
            pl.BlockSpec((None, CONV_W - 1, D_CONV), lambda i, t: (i, 0, 0)),
            full((nb, D_MODEL)),
            full((st_rows, HEAD_DIM)),
        ],
        out_shape=[jax.ShapeDtypeStruct((b, s, D_MODEL), F32),
                   jax.ShapeDtypeStruct((b, CONV_W - 1, D_CONV), F32),
                   jax.ShapeDtypeStruct((nb, D_MODEL), F32),
                   jax.ShapeDtypeStruct((st_rows, HEAD_DIM), F32)],
        scratch_shapes=[pltpu.VMEM((TM + CARRY, D_CONV), F32),
                        pltpu.VMEM((TM, D_CONV + D_ATT), BF16),
                        pltpu.VMEM((nb, 2 * D_ATT), F32),
                        pltpu.VMEM((nb, D_CONV + D_ATT), F32)]
        + _weight_scratch(w_in.shape[1:]) + _weight_scratch(w_out.shape[1:]),
        compiler_params=_params(("arbitrary", "arbitrary")),
        name=f"a_layer_{layer}",
    )(x, norm_a, w_in, conv_w, mkv_b, mem_q_norm, w_out, xs, state, mem_cache)


def _residue_rows(slab_ref, split_ref, c, d):
    rows = slab_ref.shape[1]
    if d == SUB_STRIDE:
        return lambda r: slab_ref[c, pl.ds(r, rows // d, stride=d), :]
    assert d == SUB_STRIDE * SUB_STRIDE
    for r1 in range(SUB_STRIDE):
        split_ref[c, r1] = slab_ref[c, pl.ds(r1, rows // SUB_STRIDE, stride=SUB_STRIDE), :]
    return lambda r: split_ref[c, r % SUB_STRIDE,
                               pl.ds(r // SUB_STRIDE, rows // d, stride=SUB_STRIDE), :]


def _to_residue_major(dst_ref, slab_ref, split_ref, n_chunks, d):
    for c in range(n_chunks):
        get = _residue_rows(slab_ref, split_ref, c, d)
        for r in range(d):
            dst_ref[r, :, _hs(c)] = get(r).astype(dst_ref.dtype)


def _kv_kernel(x_ref, g_ref, w_hbm, kn_ref, xs_ref, kvb0_ref, kvb1_ref, kvb2_ref,
               st0_ref, st1_ref, st2_ref, kvn_ref, slab1_ref, slab2_ref, split_ref,
               w_ref, w_stage, w_sem):
    t = pl.program_id(1)
    last = pl.num_programs(1) - 1

    @pl.when(_step_index() == 0)
    def _():
        _cast_weight(w_hbm, w_ref, w_stage, w_sem)
        _kv_sample_rows(xs_ref, g_ref, w_ref, kn_ref, kvn_ref)

    xn = _rms(x_ref[...], g_ref[...]).astype(BF16)
    kvb_refs = (kvb0_ref, kvb1_ref, kvb2_ref)
    slabs = (None, slab1_ref, slab2_ref)
    for g in reversed(range(N_GROUPS)):
        kv = _dot(xn, w_ref[:, g * 2 * D_ATT:(g + 1) * 2 * D_ATT])
        chunks = [_rms(kv[:, _hs(h)], kn_ref[g:g + 1, :]) for h in range(N_HEADS)]
        chunks += [kv[:, _hs(h, D_ATT)] for h in range(N_HEADS)]
        d = DILATIONS[g]
        for c, chunk in enumerate(chunks):
            if d == 1:
                kvb_refs[g][0, :, _hs(c)] = chunk.astype(BF16)
            else:
                slabs[g][c] = chunk
            if g == 2:
                st2_ref[pl.ds(c, TM, stride=KV_CHUNKS), :] = chunk
        if d > 1:
            _to_residue_major(kvb_refs[g], slabs[g], split_ref, KV_CHUNKS, d)
        if g == 1:
            @pl.when(t == last)
            def _():
                for c, chunk in enumerate(chunks):
                    st1_ref[pl.ds(c, TM, stride=KV_CHUNKS), :] = chunk
        if g == 0:
            @pl.when(t == last)
            def _():
                for c, chunk in enumerate(chunks):
                    st0_ref[pl.ds(c, WINDOWS[0], stride=KV_CHUNKS), :] = chunk[TM - WINDOWS[0]:, :]


def _kv_shared(x, xs, kv_norm, w_kv, k_norm_dil):
    b, s, _ = x.shape
    nb = xs.shape[0]
    kvn_rows = nb * N_GROUPS * KV_CHUNKS
    assert WINDOWS[1] == TM and WINDOWS[2] == s and WINDOWS[0] <= TM
    first = lambda i, t: (i, 0, 0)
    res_spec = lambda d: pl.BlockSpec((None, d, TM // d, 2 * D_ATT), lambda i, t: (i, 0, t, 0))
    res_sds = lambda d: jax.ShapeDtypeStruct((b, d, s // d, 2 * D_ATT), BF16)
    st_sds = lambda rows: jax.ShapeDtypeStruct((b, rows * KV_CHUNKS, HEAD_DIM), F32)
    return pl.pallas_call(
        _kv_kernel,
        grid=(b, s // TM),
        in_specs=[
            pl.BlockSpec((None, TM, D_MODEL), lambda i, t: (i, t, 0)),
            _const_spec((1, D_MODEL)),
            pl.BlockSpec(memory_space=pl.ANY),
            _const_spec((N_GROUPS, HEAD_DIM)),
            _const_spec((nb, D_MODEL)),
        ],
        out_specs=[
            res_spec(DILATIONS[0]), res_spec(DILATIONS[1]), res_spec(DILATIONS[2]),
            pl.BlockSpec((None, WINDOWS[0] * KV_CHUNKS, HEAD_DIM), first),
            pl.BlockSpec((None, WINDOWS[1] * KV_CHUNKS, HEAD_DIM), first),
            pl.BlockSpec((None, TM * KV_CHUNKS, HEAD_DIM), lambda i, t: (i, t, 0)),
            pl.BlockSpec((kvn_rows, HEAD_DIM), lambda i, t: (0, 0)),
        ],
        out_shape=[res_sds(DILATIONS[0]), res_sds(DILATIONS[1]), res_sds(DILATIONS[2]),
                   st_sds(WINDOWS[0]), st_sds(WINDOWS[1]), st_sds(s),
                   jax.ShapeDtypeStruct((kvn_rows, HEAD_DIM), F32)],
        scratch_shapes=[pltpu.VMEM((KV_CHUNKS, TM, HEAD_DIM), F32),
                        pltpu.VMEM((KV_CHUNKS, TM, HEAD_DIM), F32),
                        pltpu.VMEM((KV_CHUNKS, SUB_STRIDE, TM // SUB_STRIDE, HEAD_DIM), F32)]
        + _weight_scratch(w_kv.shape),
        compiler_params=_params(("arbitrary", "arbitrary")),
        name="kv_prompt",
    )(x, kv_norm.reshape(1, D_MODEL), w_kv, k_norm_dil, xs)


def _band_scores(qs, ks, masks):
    return jnp.concatenate([jnp.where(mk, _dot_nt(q, k), NEG) for q, k, mk in zip(qs, ks, masks)],
                           axis=0)


def _band_finish(s, vs, nq):
    m = jnp.max(s, axis=-1, keepdims=True)
    pb = jnp.exp2(s - m).astype(BF16)
    res = []
    for i, v in enumerate(vs):
        rows = slice(i * nq, (i + 1) * nq)
        acc, l = _pv_and_rowsum(pb[rows], v)
        res.append((acc * (1.0 / l), m[rows] + jnp.log2(l)))
    return res


def _b_layer_kernel(x_ref, g_ref, win_ref, qn_ref, mkv_ref, gq_ref, wout_ref,
                    kv0_ref, kv0p_ref, kv1_ref, kv1p_ref, kv2_ref,
                    xs_ref, kvn_ref, c0_ref, c1_ref, c2_ref, mc_ref,
                    y_ref, ys_ref,
                    qslab1_ref, qslab2_ref, qsplit_ref, oslab_ref, lslab_ref, cat_ref,
                    zs_ref, cats_ref):
    t = pl.program_id(1)
    n = _step_index()

    @pl.when(n == 0)
    def _():
        zs_ref[...] = _dot(_rms(xs_ref[...], g_ref[...]).astype(BF16), win_ref[...])
        cats_ref[...] = jnp.zeros(cats_ref.shape, F32)

    _merge_row(cats_ref, slice(0, 2 * D_ATT), n,
               _b_sample_row(zs_ref[pl.ds(n, 1), :], kvn_ref[n], qn_ref, gq_ref[...],
                             (c0_ref[0], c1_ref[0], c2_ref[0]), mc_ref[0]))

    x = x_ref[...]
    xn = _rms(x, g_ref[...]).astype(BF16)

    def proj(lo, hi):
        return _dot(xn, win_ref[:, lo:hi])

    qi = lax.broadcasted_iota(jnp.int32, (BAND, 2 * BAND), 0)
    kj = lax.broadcasted_iota(jnp.int32, (BAND, 2 * BAND), 1)
    mask_rest = (kj >= qi) & (kj <= qi + BAND)
    mask_edge = (kj >= jnp.maximum(qi, jnp.where(t == 0, BAND, 0))) & (kj <= qi + BAND)

    lane = lax.broadcasted_iota(jnp.int32, (1, HEAD_DIM), 1)

    def finish(g, rows, res):
        packed = jnp.zeros(res[0][1].shape, F32)
        for h, (out, lse) in enumerate(res):
            oslab_ref[g, h, rows, :] = out
            packed = jnp.where(lane == h, lse, packed)
        lslab_ref[g, rows, :] = packed

    def normed_heads(g):
        qd = proj(g * D_ATT, (g + 1) * D_ATT)
        gain = qn_ref[g:g + 1, :] * QK_SCALE
        return [_rms(qd[:, _hs(h)], gain) for h in range(N_HEADS)]

    n_sub = TM // BAND
    qh = [q.astype(BF16) for q in normed_heads(0)]
    units0 = [(i, h) for i in range(n_sub) for h in range(N_HEADS)]

    def keys0(i, cols):
        if i == 0:
            return jnp.concatenate([kv0p_ref[0, :, cols], kv0_ref[0, 0:BAND, cols]], axis=0)
        return kv0_ref[0, (i - 1) * BAND:(i + 1) * BAND, cols]

    d1 = DILATIONS[1]
    for h, q in enumerate(normed_heads(1)):
        qslab1_ref[h] = q
    get_q1 = [_residue_rows(qslab1_ref, None, h, d1) for h in range(N_HEADS)]
    units1 = [(r, h) for r in range(d1) for h in range(N_HEADS)]

    def keys1(r, cols):
        return jnp.concatenate([kv1p_ref[r, :, cols], kv1_ref[r, :, cols]], axis=0)

    d2 = DILATIONS[2]
    nq = TM // d2
    for h, q in enumerate(normed_heads(2)):
        qslab2_ref[h] = q
    get_q2 = [_residue_rows(qslab2_ref, qsplit_ref, h, d2) for h in range(N_HEADS)]
    units2 = [(r, h) for r in range(d2) for h in range(N_HEADS)]
    qi2 = lax.broadcasted_iota(jnp.int32, (nq, BAND), 0)
    kj2 = lax.broadcasted_iota(jnp.int32, (nq, BAND), 1)
    mask2 = kj2 <= qi2 + t * nq

    s0 = _band_scores([qh[h][i * BAND:(i + 1) * BAND] for i, h in units0],
                      [keys0(i, _hs(h)) for i, h in units0],
                      [mask_edge if i == 0 else mask_rest for i, h in units0])
    s1 = _band_scores([get_q1[h](r).astype(BF16) for r, h in units1],
                      [keys1(r, _hs(h)) for r, h in units1], [mask_edge] * len(units1))
    s2 = _band_scores([get_q2[h](r).astype(BF16) for r, h in units2],
                      [kv2_ref[r, :, _hs(h)] for r, h in units2], [mask2] * len(units2))
    res = _band_finish(s0, [keys0(i, _hs(h, D_ATT)) for i, h in units0], BAND)
    for i in range(n_sub):
        finish(0, pl.ds(i * BAND, BAND), res[i * N_HEADS:(i + 1) * N_HEADS])
    res = _band_finish(s1, [keys1(r, _hs(h, D_ATT)) for r, h in units1], BAND)
    for r in range(d1):
        finish(1, pl.ds(r, BAND, stride=d1), res[r * N_HEADS:(r + 1) * N_HEADS])
    res = _band_finish(s2, [kv2_ref[r, :, _hs(h, D_ATT)] for r, h in units2], nq)
    for r in range(d2):
        finish(2, pl.ds(r, nq, stride=d2), res[r * N_HEADS:(r + 1) * N_HEADS])

    base = N_GROUPS * D_ATT
    gd = proj(base, base + D_ATT)
    l0, l1, l2 = lslab_ref[0], lslab_ref[1], lslab_ref[2]
    m = jnp.maximum(jnp.maximum(l0, l1), l2)
    e0, e1, e2 = jnp.exp2(l0 - m), jnp.exp2(l1 - m), jnp.exp2(l2 - m)
    inv = 1.0 / (e0 + e1 + e2)
    w0, w1, w2 = e0 * inv, e1 * inv, e2 * inv
    for h in range(N_HEADS):
        dil = (oslab_ref[0, h] * w0[:, h:h + 1] + oslab_ref[1, h] * w1[:, h:h + 1]
               + oslab_ref[2, h] * w2[:, h:h + 1])
        cat_ref[:, _hs(h)] = (dil * _silu(gd[:, _hs(h)])).astype(BF16)
    mq = proj(base + D_ATT, base + 2 * D_ATT)
    mg = proj(base + 2 * D_ATT, base + 3 * D_ATT)

    def store(h, val):
        cat_ref[:, _hs(h, D_ATT)] = val.astype(BF16)

    _mem_attn(mq, mg, mkv_ref, gq_ref[...] * QK_SCALE, store)
    y_ref[...] = x + _dot(cat_ref[...], wout_ref[...])

    @pl.when(_is_last_step())
    def _():
        ys_ref[...] = xs_ref[...] + _dot(cats_ref[...].astype(BF16), wout_ref[...])


def _b_layer(x, xs, norm_b, w_in, q_norm, mkv_b, kvbs, kv_new, caches, mem_cache, bl, layer,
             mem_q_norm, w_out):
    b, s, _ = x.shape
    nb = xs.shape[0]
    nt = s // TM
    assert nb == b * nt
    full, mem_cache_spec = _sample_specs(nb, nt, layer)
    cache_spec = pl.BlockSpec((1, BAND, None, KV_CHUNKS, HEAD_DIM),
                              lambda i, t: (i * nt + t, 0, 0, 0, 0))
    assert TM // BAND == DILATIONS[1] and s // DILATIONS[2] == BAND
    tile = lambda i, t: (i, t, 0)
    sub = TM // BAND
    kv_w = 2 * D_ATT
    return pl.pallas_call(
        _b_layer_kernel,
        grid=(b, s // TM),
        in_specs=[
            pl.BlockSpec((None, TM, D_MODEL), tile),
            _layer_spec((1, D_MODEL), bl),
            _layer_spec(w_in.shape[1:], bl),
            _layer_spec((N_GROUPS, HEAD_DIM), bl),
            pl.BlockSpec((None, None, MEM_LEN, kv_w), lambda i, t: (layer, i, 0, 0)),
            _layer_spec((1, HEAD_DIM), layer),
            _layer_spec(w_out.shape[1:], bl),
            pl.BlockSpec((None, 1, TM, kv_w), lambda i, t: (i, 0, t, 0)),
            pl.BlockSpec((None, 1, BAND, kv_w),
                         lambda i, t: (i, 0, jnp.maximum(t * sub - 1, 0), 0)),
            pl.BlockSpec((None, DILATIONS[1], BAND, kv_w), lambda i, t: (i, 0, t, 0)),
            pl.BlockSpec((None, DILATIONS[1], BAND, kv_w),
                         lambda i, t: (i, 0, jnp.maximum(t - 1, 0), 0)),
            pl.BlockSpec((None, DILATIONS[2], BAND, kv_w), lambda i, t: (i, 0, 0, 0)),
            full((nb, D_MODEL)),
            full(kv_new.shape),
            cache_spec, cache_spec, cache_spec,
            mem_cache_spec,
        ],
        out_specs=[pl.BlockSpec((None, TM, D_MODEL), tile), full((nb, D_MODEL))],
        out_shape=[jax.ShapeDtypeStruct((b, s, D_MODEL), F32),
                   jax.ShapeDtypeStruct((nb, D_MODEL), F32)],
        scratch_shapes=[pltpu.VMEM((N_HEADS, TM, HEAD_DIM), F32),
                        pltpu.VMEM((N_HEADS, TM, HEAD_DIM), F32),
                        pltpu.VMEM((N_HEADS, SUB_STRIDE, TM // SUB_STRIDE, HEAD_DIM), F32),
                        pltpu.VMEM((N_GROUPS, N_HEADS, TM, HEAD_DIM), F32),
                        pltpu.VMEM((N_GROUPS, TM, HEAD_DIM), F32),
                        pltpu.VMEM((TM, 2 * D_ATT), BF16),
                        pltpu.VMEM((nb, w_in.shape[2]), F32),
                        pltpu.VMEM((nb, 2 * D_ATT), F32)],
        compiler_params=_params(("arbitrary", "arbitrary")),
        name=f"b_layer_{layer}",
    )(x, norm_b, w_in, q_norm, mkv_b, mem_q_norm, w_out,
      kvbs[0], kvbs[0], kvbs[1], kvbs[1], kvbs[2],
      xs, kv_new, *caches, mem_cache)


def _swap_halves(x, axis):
    return pltpu.roll(x, N_HEADS, axis=axis)


def _head_rows(tile):
    return jnp.concatenate([tile[N_HEADS + h:N_HEADS + h + 1, :] for h in range(N_HEADS)],
                           axis=-1)


def _low_tile(row):
    return jnp.concatenate([jnp.zeros((N_HEADS, HEAD_DIM), F32)]
                           + [row[:, _hs(h)] for h in range(N_HEADS)], axis=0)


def _q_tile(row, g):
    return jnp.concatenate([_rms(row[:, _hs(h)], g) * QK_SCALE for h in range(N_HEADS)]
                           + [jnp.zeros((N_HEADS, HEAD_DIM), F32)], axis=0)


def _tile_attend(qt, kv):
    s = jnp.sum(kv * qt[None], axis=-1, keepdims=True)
    m = jnp.max(s, axis=0)
    p = jnp.exp2(s - m[None])
    l = jnp.sum(p, axis=0)
    acc = jnp.sum(_swap_halves(jnp.broadcast_to(p, kv.shape), 1) * kv, axis=0)
    out = acc / _swap_halves(jnp.broadcast_to(l, acc.shape), 0)
    return out, m + jnp.log2(l)


def _sample_mem_attn(q_row, gq, mkv, mg_row):
    out, _ = _tile_attend(_q_tile(q_row, gq), mkv)
    return _head_rows(out * _low_tile(_silu(mg_row)))


def _gather_chunks(ref, n_rows, n_chunks, offset=0):
    total = ref.shape[0] // n_rows
    return jnp.concatenate([ref[pl.ds(offset + c, n_rows, stride=total), :]
                            for c in range(n_chunks)], axis=-1)


def _merge_row(ref, cols, n, row):
    tile = pl.ds(pl.multiple_of((n // SAMPLE_GB) * SAMPLE_GB, SAMPLE_GB), SAMPLE_GB)
    sub = lax.broadcasted_iota(jnp.int32, (SAMPLE_GB, row.shape[1]), 0)
    ref[tile, cols] = jnp.where(sub == n % SAMPLE_GB, row, ref[tile, cols])


def _a_sample_first(x_ref, g_ref, win_ref, cw_ref, st_ref, sto_ref, zm_ref, cat_ref):
    nb = x_ref.shape[0]
    n_st = D_CONV // HEAD_DIM
    xn = _rms(x_ref[...], g_ref[...]).astype(BF16)
    z = _dot(xn, win_ref[...])
    u = z[:, D_CONV:2 * D_CONV] * z[:, 0:D_CONV]
    s0 = _gather_chunks(st_ref, nb, n_st)
    s1 = _gather_chunks(st_ref, nb, n_st, n_st)
    y = cw_ref[0:1, :] * s0 + cw_ref[1:2, :] * s1 + cw_ref[2:3, :] * u
    for c in range(n_st):
        sto_ref[pl.ds(c, nb, stride=2 * n_st), :] = s1[:, _hs(c)]
        sto_ref[pl.ds(n_st + c, nb, stride=2 * n_st), :] = u[:, _hs(c)]
    cat_ref[:, 0:D_CONV] = _silu(z[:, 3 * D_CONV:4 * D_CONV]) * z[:, 2 * D_CONV:3 * D_CONV] * y
    cat_ref[:, D_CONV:] = jnp.zeros((nb, D_ATT), F32)
    zm_ref[...] = z[:, 4 * D_CONV:]


def _kv_sample_rows(x_ref, g_ref, w_ref, kn_ref, o_ref):
    nb = x_ref.shape[0]
    xn = _rms(x_ref[...], g_ref[...]).astype(BF16)
    kv = _dot(xn, w_ref[...])
    n_chunks = N_GROUPS * KV_CHUNKS
    for g in range(N_GROUPS):
        for c in range(KV_CHUNKS):
            chunk = kv[:, _hs(c, g * 2 * D_ATT)]
            if c < N_HEADS:
                chunk = _rms(chunk, kn_ref[g:g + 1, :])
            o_ref[pl.ds(g * KV_CHUNKS + c, nb, stride=n_chunks), :] = chunk


def _b_sample_row(row, kv_new, qn_ref, gq, caches, mkv):
    base = N_GROUPS * D_ATT
    outs, lses = [], []
    for g in range(N_GROUPS):
        qt = _q_tile(row[:, g * D_ATT:(g + 1) * D_ATT], qn_ref[g:g + 1, :])
        out, lse = _tile_attend(qt, jnp.concatenate([caches[g], kv_new[g][None]], axis=0))
        outs.append(out)
        lses.append(lse)
    m = jnp.maximum(jnp.maximum(lses[0], lses[1]), lses[2])
    es = [jnp.exp2(l - m) for l in lses]
    den = es[0] + es[1] + es[2]
    ws = [_swap_halves(jnp.broadcast_to(e / den, (KV_CHUNKS, HEAD_DIM)), 0) for e in es]
    dil = outs[0] * ws[0] + outs[1] * ws[1] + outs[2] * ws[2]
    dil_row = _head_rows(dil * _low_tile(_silu(row[:, base:base + D_ATT])))
    mem_row = _sample_mem_attn(row[:, base + D_ATT:base + 2 * D_ATT], gq, mkv,
                               row[:, base + 2 * D_ATT:])
    return jnp.concatenate([dil_row, mem_row], axis=-1)


def kernel(x_prompt, x_sample, cache_mem_kv, state_conv, cache_dil0_kv, cache_dil1_kv, cache_dil2_kv, mem_prompt, norm_a, w_in_a, conv_w_a, w_out_a, norm_b, w_in_b, q_norm_b, w_out_b, kv_norm, w_kv, k_norm_dil, mem_norm, w_mem_kv, mem_k_norm, mem_q_norm):
    n_a = w_in_a.shape[0]
    n_b = w_in_b.shape[0]
    depth = n_a + n_b
    b, s, _ = x_prompt.shape
    nb = x_sample.shape[0]
    assert x_sample.shape[1] == 1 and s % TM == 0 and nb % SAMPLE_GB == 0
    kv_dims = (2, N_HEADS, HEAD_DIM)

    w_in_b16, w_out_b16 = w_in_b.astype(BF16), w_out_b.astype(BF16)
    norm_a3 = norm_a.reshape(n_a, 1, D_MODEL)
    norm_b3 = norm_b.reshape(n_b, 1, D_MODEL)
    mem_q_norm3 = mem_q_norm.reshape(depth, 1, HEAD_DIM)

    mem_kv_f, mem_kv_b = _memkv_prompt(mem_prompt, mem_norm, w_mem_kv, mem_k_norm)
    mem_kv_b = mem_kv_b.reshape(depth, b, MEM_LEN, 2 * D_ATT)
    mem_cache = cache_mem_kv.reshape(depth, nb, MEM_LEN, KV_CHUNKS, HEAD_DIM)
    caches = []
    for g, cache in enumerate((cache_dil0_kv, cache_dil1_kv, cache_dil2_kv)):
        assert cache.shape[1] == WINDOWS[g]
        caches.append(cache.reshape(nb, BAND, DILATIONS[g], KV_CHUNKS, HEAD_DIM))
    state = state_conv.reshape(n_a, -1, HEAD_DIM)

    xp = x_prompt
    xs = x_sample.reshape(nb, D_MODEL)
    conv_p, conv_s = [], []
    for a in range(n_a):
        xp, cp, xs, cs = _a_layer(xp, xs, norm_a3, w_in_a, conv_w_a, mem_kv_b, state, mem_cache,
                                  a, a, mem_q_norm3, w_out_a)
        conv_p.append(cp)
        conv_s.append(cs.reshape(nb, CONV_W - 1, D_CONV))

    kvb0, kvb1, kvb2, st0, st1, st2, kv_new = _kv_shared(xp, xs, kv_norm, w_kv, k_norm_dil)
    kvbs = (kvb0, kvb1, kvb2)
    kv_new = kv_new.reshape(nb, N_GROUPS, KV_CHUNKS, HEAD_DIM)

    for bl in range(n_b):
        layer = n_a + bl
        xp, xs = _b_layer(xp, xs, norm_b3, w_in_b16, q_norm_b, mem_kv_b, kvbs, kv_new, caches,
                          mem_cache, bl, layer, mem_q_norm3, w_out_b16)

    dil_s = [kv_new[:, g].reshape(nb, 1, *kv_dims) for g in range(N_GROUPS)]
    return (xp, xs.reshape(nb, 1, D_MODEL),
            mem_kv_f.reshape(depth, b, MEM_LEN, *kv_dims),
            jnp.stack(conv_p), jnp.stack(conv_s),
            st0.reshape(b, WINDOWS[0], *kv_dims), st1.reshape(b, WINDOWS[1], *kv_dims),
            st2.reshape(b, s, *kv_dims), *dil_s)
```

```python
import functools

import jax
import jax.numpy as jnp
from jax import lax
from jax.experimental import pallas as pl
from jax.experimental.pallas import tpu as pltpu

F32 = jnp.float32
BF16 = jnp.bfloat16

D_MODEL = 1024
HEAD_DIM = 128
N_HEADS = 4
D_ATT = N_HEADS * HEAD_DIM
KV_CHUNKS = 2 * N_HEADS
D_CONV = D_MODEL
CONV_W = 3
N_GROUPS = 3
DILATIONS = (1, 4, 16)
WINDOWS = (128, 512, 2048)
BAND = 128
SUB_STRIDE = 4
MEM_LEN = 256
EPS = 1e-6
NEG = -1e30
SCALE = HEAD_DIM ** -0.5
QK_SCALE = SCALE * 1.4426950408889634

TM = 512
SAMPLE_GB = 8
CARRY = 8
W_STAGE_ROWS = 256
VMEM_LIMIT = 56 * 1024 * 1024


def _rms(x, g):
    return x * lax.rsqrt(jnp.mean(x * x, axis=-1, keepdims=True) + EPS) * g


def _silu(x):
    return x * (1.0 / (1.0 + jnp.exp(-x)))


def _dot(a, b):
    return jnp.dot(a, b, preferred_element_type=F32)


def _dot_nt(a, b):
    return lax.dot_general(a, b, (((1,), (1,)), ((), ())), preferred_element_type=F32)


def _hs(h, base=0):
    return slice(base + h * HEAD_DIM, base + (h + 1) * HEAD_DIM)


def _params(sem):
    return pltpu.CompilerParams(dimension_semantics=sem, vmem_limit_bytes=VMEM_LIMIT)


def _fixed_spec(block, index):
    return pl.BlockSpec(block, lambda *_: index, pipeline_mode=pl.Buffered(1))


def _const_spec(shape):
    return _fixed_spec(shape, (0,) * len(shape))


def _layer_spec(shape, layer):
    return _fixed_spec((None,) + tuple(shape), (layer,) + (0,) * len(shape))


def _cast_weight(w_hbm, dst_ref, stage_ref, sem_ref):
    rows = stage_ref.shape[1]
    n_chunks = w_hbm.shape[0] // rows

    def copy(c, slot):
        src = w_hbm.at[pl.ds(pl.multiple_of(c * rows, rows), rows)]
        return pltpu.make_async_copy(src, stage_ref.at[slot], sem_ref.at[slot])

    copy(0, 0).start()

    def body(c, carry):
        slot = c % 2

        @pl.when(c + 1 < n_chunks)
        def _():
            copy(c + 1, 1 - slot).start()

        copy(c, slot).wait()
        dst_ref[pl.ds(pl.multiple_of(c * rows, rows), rows), :] = stage_ref[slot].astype(BF16)
        return carry

    lax.fori_loop(0, n_chunks, body, 0)


def _weight_scratch(shape):
    k, n = shape
    assert k % W_STAGE_ROWS == 0
    return [pltpu.VMEM((k, n), BF16), pltpu.VMEM((2, W_STAGE_ROWS, n), F32),
            pltpu.SemaphoreType.DMA((2,))]


def _memkv_kernel(mem_ref, g_ref, w_ref, kg_ref, of_ref, ob_ref):
    rt = mem_ref.shape[0]
    xn = _rms(mem_ref[...], g_ref[...]).astype(BF16)
    kv = _dot(xn, w_ref[...].astype(BF16))
    for c in range(KV_CHUNKS):
        chunk = kv[:, _hs(c)]
        if c < N_HEADS:
            chunk = _rms(chunk, kg_ref[...])
        of_ref[pl.ds(c, rt, stride=KV_CHUNKS), :] = chunk
        ob_ref[:, _hs(c)] = chunk.astype(BF16)


def _memkv_prompt(mem, mem_norm, w_mem_kv, mem_k_norm):
    depth = w_mem_kv.shape[0]
    rows = mem.shape[0] * mem.shape[1]
    rt = 1024
    return pl.pallas_call(
        _memkv_kernel,
        grid=(depth, rows // rt),
        in_specs=[
            pl.BlockSpec((rt, D_MODEL), lambda l, i: (i, 0)),
            pl.BlockSpec((None, 1, D_MODEL), lambda l, i: (l, 0, 0)),
            pl.BlockSpec((None, D_MODEL, 2 * D_ATT), lambda l, i: (l, 0, 0)),
            pl.BlockSpec((None, 1, HEAD_DIM), lambda l, i: (l, 0, 0)),
        ],
        out_specs=[
            pl.BlockSpec((None, rt * KV_CHUNKS, HEAD_DIM), lambda l, i: (l, i, 0)),
            pl.BlockSpec((None, rt, 2 * D_ATT), lambda l, i: (l, i, 0)),
        ],
        out_shape=[jax.ShapeDtypeStruct((depth, rows * KV_CHUNKS, HEAD_DIM), F32),
                   jax.ShapeDtypeStruct((depth, rows, 2 * D_ATT), BF16)],
        compiler_params=_params(("parallel", "parallel")),
        name="memkv_prompt",
    )(mem.reshape(rows, D_MODEL), mem_norm.reshape(depth, 1, D_MODEL), w_mem_kv,
      mem_k_norm.reshape(depth, 1, HEAD_DIM))


def _pv_and_rowsum(p, v):
    ones = jnp.ones((v.shape[0], HEAD_DIM), BF16)
    both = _dot(p, jnp.concatenate([v, ones], axis=1))
    return both[:, :HEAD_DIM], both[:, HEAD_DIM:]


def _mem_attn(mq, mg, mkv_ref, gq, store):
    for h in range(N_HEADS):
        q = _rms(mq[:, _hs(h)], gq).astype(BF16)
        s = _dot_nt(q, mkv_ref[:, _hs(h)])
        m = jnp.max(s, axis=-1, keepdims=True)
        acc, l = _pv_and_rowsum(jnp.exp2(s - m).astype(BF16), mkv_ref[:, _hs(h, D_ATT)])
        store(h, acc / l * _silu(mg[:, _hs(h)]))


def _step_index():
    return pl.program_id(0) * pl.num_programs(1) + pl.program_id(1)


def _is_last_step():
    return _step_index() == pl.num_programs(0) * pl.num_programs(1) - 1


def _a_layer_kernel(x_ref, g_ref, win_hbm, cw_ref, mkv_ref, gq_ref, wout_hbm,
                    xs_ref, st_ref, mc_ref,
                    y_ref, cs_ref, ys_ref, sto_ref,
                    ext_ref, cat_ref, zs_ref, cats_ref,
                    win_ref, win_stage, win_sem, wout_ref, wout_stage, wout_sem, *, a):
    t = pl.program_id(1)
    n = _step_index()

    @pl.when(t == 0)
    def _():
        ext_ref[0:CARRY, :] = jnp.zeros((CARRY, D_CONV), F32)

    @pl.when(n == 0)
    def _():
        _cast_weight(win_hbm.at[a], win_ref, win_stage, win_sem)
        _cast_weight(wout_hbm.at[a], wout_ref, wout_stage, wout_sem)
        _a_sample_first(xs_ref, g_ref, win_ref, cw_ref, st_ref, sto_ref, zs_ref, cats_ref)

    row = zs_ref[pl.ds(n, 1), :]
    _merge_row(cats_ref, slice(D_CONV, D_CONV + D_ATT), n,
               _sample_mem_attn(row[:, :D_ATT], gq_ref[...], mc_ref[0], row[:, D_ATT:]))

    x = x_ref[...]
    xn = _rms(x, g_ref[...]).astype(BF16)

    def proj(lo, hi):
        return _dot(xn, win_ref[:, lo:hi])

    u = proj(D_CONV, 2 * D_CONV) * proj(0, D_CONV)
    ext_ref[CARRY:CARRY + TM, :] = u
    y = (cw_ref[0:1, :] * ext_ref[CARRY - 2:CARRY - 2 + TM, :]
         + cw_ref[1:2, :] * ext_ref[CARRY - 1:CARRY - 1 + TM, :] + cw_ref[2:3, :] * u)
    ext_ref[0:CARRY, :] = ext_ref[TM:TM + CARRY, :]
    cs_ref[...] = ext_ref[CARRY - (CONV_W - 1):CARRY, :]
    bg = proj(2 * D_CONV, 3 * D_CONV)
    g = proj(3 * D_CONV, 4 * D_CONV)
    cat_ref[:, 0:D_CONV] = (_silu(g) * bg * y).astype(BF16)
    mq = proj(4 * D_CONV, 4 * D_CONV + D_ATT)
    mg = proj(4 * D_CONV + D_ATT, 4 * D_CONV + 2 * D_ATT)

    def store(h, val):
        cat_ref[:, _hs(h, D_CONV)] = val.astype(BF16)

    _mem_attn(mq, mg, mkv_ref, gq_ref[...] * QK_SCALE, store)
    y_ref[...] = x + _dot(cat_ref[...], wout_ref[...])

    @pl.when(_is_last_step())
    def _():
        ys_ref[...] = xs_ref[...] + _dot(cats_ref[...].astype(BF16), wout_ref[...])


def _sample_specs(nb, nt, layer):
    full = lambda shape: pl.BlockSpec(shape, lambda i, t: (0,) * len(shape))
    mem_cache = pl.BlockSpec((None, 1, MEM_LEN, KV_CHUNKS, HEAD_DIM),
                             lambda i, t: (layer, i * nt + t, 0, 0, 0))
    return full, mem_cache


def _a_layer(x, xs, norm_a, w_in, conv_w, mkv_b, state, mem_cache, a, layer, mem_q_norm, w_out):
    b, s, _ = x.shape
    nb = xs.shape[0]
    nt = s // TM
    assert nb == b * nt
    st_rows = nb * (CONV_W - 1) * D_CONV // HEAD_DIM
    full, mem_cache_spec = _sample_specs(nb, nt, layer)
    return pl.pallas_call(
        functools.partial(_a_layer_kernel, a=a),
        grid=(b, nt),
        in_specs=[
            pl.BlockSpec((None, TM, D_MODEL), lambda i, t: (i, t, 0)),
            _layer_spec((1, D_MODEL), a),
            pl.BlockSpec(memory_space=pl.ANY),
            _layer_spec((CONV_W, D_CONV), a),
            pl.BlockSpec((None, None, MEM_LEN, 2 * D_ATT), lambda i, t: (layer, i, 0, 0)),
            _layer_spec((1, HEAD_DIM), layer),
            pl.BlockSpec(memory_space=pl.ANY),
            full((nb, D_MODEL)),
            _layer_spec((st_rows, HEAD_DIM), a),
            mem_cache_spec,
        ],
        out_specs=[
            pl.BlockSpec((None, TM, D_MODEL), lambda i, t: (i, t, 0)),
            pl.BlockSpec((None, CONV_W - 1, D_CONV), lambda i, t: (i, 0, 0)),
            full((nb, D_MODEL)),
            full((st_rows, HEAD_DIM)),
        ],
        out_shape=[jax.ShapeDtypeStruct((b, s, D_MODEL), F32),
                   jax.ShapeDtypeStruct((b, CONV_W - 1, D_CONV), F32),
                   jax.ShapeDtypeStruct((nb, D_MODEL), F32),
                   jax.ShapeDtypeStruct((st_rows, HEAD_DIM), F32)],
        scratch_shapes=[pltpu.VMEM((TM + CARRY, D_CONV), F32),
                        pltpu.VMEM((TM, D_CONV + D_ATT), BF16),
                        pltpu.VMEM((nb, 2 * D_ATT), F32),
                        pltpu.VMEM((nb, D_CONV + D_ATT), F32)]
        + _weight_scratch(w_in.shape[1:]) + _weight_scratch(w_out.shape[1:]),
        compiler_params=_params(("arbitrary", "arbitrary")),
        name=f"a_layer_{layer}",
    )(x, norm_a, w_in, conv_w, mkv_b, mem_q_norm, w_out, xs, state, mem_cache)


def _residue_rows(slab_ref, split_ref, c, d):
    rows = slab_ref.shape[1]
    if d == SUB_STRIDE:
        return lambda r: slab_ref[c, pl.ds(r, rows // d, stride=d), :]
    assert d == SUB_STRIDE * SUB_STRIDE
    for r1 in range(SUB_STRIDE):
        split_ref[c, r1] = slab_ref[c, pl.ds(r1, rows // SUB_STRIDE, stride=SUB_STRIDE), :]
    return lambda r: split_ref[c, r % SUB_STRIDE,
                               pl.ds(r // SUB_STRIDE, rows // d, stride=SUB_STRIDE), :]


def _to_residue_major(dst_ref, slab_ref, split_ref, n_chunks, d):
    for c in range(n_chunks):
        get = _residue_rows(slab_ref, split_ref, c, d)
        for r in range(d):
            dst_ref[r, :, _hs(c)] = get(r).astype(dst_ref.dtype)


def _kv_kernel(x_ref, g_ref, w_hbm, kn_ref, xs_ref, kvb0_ref, kvb1_ref, kvb2_ref,
               st0_ref, st1_ref, st2_ref, kvn_ref, slab1_ref, slab2_ref, split_ref,
               w_ref, w_stage, w_sem):
    t = pl.program_id(1)
    last = pl.num_programs(1) - 1

    @pl.when(_step_index() == 0)
    def _():
        _cast_weight(w_hbm, w_ref, w_stage, w_sem)
        _kv_sample_rows(xs_ref, g_ref, w_ref, kn_ref, kvn_ref)

    xn = _rms(x_ref[...], g_ref[...]).astype(BF16)
    kvb_refs = (kvb0_ref, kvb1_ref, kvb2_ref)
    slabs = (None, slab1_ref, slab2_ref)
    for g in reversed(range(N_GROUPS)):
        kv = _dot(xn, w_ref[:, g * 2 * D_ATT:(g + 1) * 2 * D_ATT])
        chunks = [_rms(kv[:, _hs(h)], kn_ref[g:g + 1, :]) for h in range(N_HEADS)]
        chunks += [kv[:, _hs(h, D_ATT)] for h in range(N_HEADS)]
        d = DILATIONS[g]
        for c, chunk in enumerate(chunks):
            if d == 1:
                kvb_refs[g][0, :, _hs(c)] = chunk.astype(BF16)
            else:
                slabs[g][c] = chunk
            if g == 2:
                st2_ref[pl.ds(c, TM, stride=KV_CHUNKS), :] = chunk
        if d > 1:
            _to_residue_major(kvb_refs[g], slabs[g], split_ref, KV_CHUNKS, d)
        if g == 1:
            @pl.when(t == last)
            def _():
                for c, chunk in enumerate(chunks):
                    st1_ref[pl.ds(c, TM, stride=KV_CHUNKS), :] = chunk
        if g == 0:
            @pl.when(t == last)
            def _():
                for c, chunk in enumerate(chunks):
                    st0_ref[pl.ds(c, WINDOWS[0], stride=KV_CHUNKS), :] = chunk[TM - WINDOWS[0]:, :]


def _kv_shared(x, xs, kv_norm, w_kv, k_norm_dil):
    b, s, _ = x.shape
    nb = xs.shape[0]
    kvn_rows = nb * N_GROUPS * KV_CHUNKS
    assert WINDOWS[1] == TM and WINDOWS[2] == s and WINDOWS[0] <= TM
    first = lambda i, t: (i, 0, 0)
    res_spec = lambda d: pl.BlockSpec((None, d, TM // d, 2 * D_ATT), lambda i, t: (i, 0, t, 0))
    res_sds = lambda d: jax.ShapeDtypeStruct((b, d, s // d, 2 * D_ATT), BF16)
    st_sds = lambda rows: jax.ShapeDtypeStruct((b, rows * KV_CHUNKS, HEAD_DIM), F32)
    return pl.pallas_call(
        _kv_kernel,
        grid=(b, s // TM),
        in_specs=[
            pl.BlockSpec((None, TM, D_MODEL), lambda i, t: (i, t, 0)),
            _const_spec((1, D_MODEL)),
            pl.BlockSpec(memory_space=pl.ANY),
            _const_spec((N_GROUPS, HEAD_DIM)),
            _const_spec((nb, D_MODEL)),
        ],
        out_specs=[
            res_spec(DILATIONS[0]), res_spec(DILATIONS[1]), res_spec(DILATIONS[2]),
            pl.BlockSpec((None, WINDOWS[0] * KV_CHUNKS, HEAD_DIM), first),
            pl.BlockSpec((None, WINDOWS[1] * KV_CHUNKS, HEAD_DIM), first),
            pl.BlockSpec((None, TM * KV_CHUNKS, HEAD_DIM), lambda i, t: (i, t, 0)),
            pl.BlockSpec((kvn_rows, HEAD_DIM), lambda i, t: (0, 0)),
        ],
        out_shape=[res_sds(DILATIONS[0]), res_sds(DILATIONS[1]), res_sds(DILATIONS[2]),
                   st_sds(WINDOWS[0]), st_sds(WINDOWS[1]), st_sds(s),
                   jax.ShapeDtypeStruct((kvn_rows, HEAD_DIM), F32)],
        scratch_shapes=[pltpu.VMEM((KV_CHUNKS, TM, HEAD_DIM), F32),
                        pltpu.VMEM((KV_CHUNKS, TM, HEAD_DIM), F32),
                        pltpu.VMEM((KV_CHUNKS, SUB_STRIDE, TM // SUB_STRIDE, HEAD_DIM), F32)]
        + _weight_scratch(w_kv.shape),
        compiler_params=_params(("arbitrary", "arbitrary")),
        name="kv_prompt",
    )(x, kv_norm.reshape(1, D_MODEL), w_kv, k_norm_dil, xs)


def _band_scores(qs, ks, masks):
    return jnp.concatenate([jnp.where(mk, _dot_nt(q, k), NEG) for q, k, mk in zip(qs, ks, masks)],
                           axis=0)


def _band_finish(s, vs, nq):
    m = jnp.max(s, axis=-1, keepdims=True)
    pb = jnp.exp2(s - m).astype(BF16)
    res = []
    for i, v in enumerate(vs):
        rows = slice(i * nq, (i + 1) * nq)
        acc, l = _pv_and_rowsum(pb[rows], v)
        res.append((acc, l, m[rows]))
    return res


def _b_layer_kernel(x_ref, g_ref, win_ref, qn_ref, mkv_ref, gq_ref, wout_ref,
                    kv0_ref, kv0p_ref, kv1_ref, kv1p_ref, kv2_ref,
                    xs_ref, kvn_ref, c0_ref, c1_ref, c2_ref, mc_ref,
                    y_ref, ys_ref,
                    qslab1_ref, qslab2_ref, qsplit_ref, oslab_ref, lslab_ref, mslab_ref, cat_ref,
                    zs_ref, cats_ref):
    t = pl.program_id(1)
    n = _step_index()

    @pl.when(n == 0)
    def _():
        zs_ref[...] = _dot(_rms(xs_ref[...], g_ref[...]).astype(BF16), win_ref[...])
        cats_ref[...] = jnp.zeros(cats_ref.shape, F32)

    _merge_row(cats_ref, slice(0, 2 * D_ATT), n,
               _b_sample_row(zs_ref[pl.ds(n, 1), :], kvn_ref[n], qn_ref, gq_ref[...],
                             (c0_ref[0], c1_ref[0], c2_ref[0]), mc_ref[0]))

    x = x_ref[...]
    xn = _rms(x, g_ref[...]).astype(BF16)

    def proj(lo, hi):
        return _dot(xn, win_ref[:, lo:hi])

    qi = lax.broadcasted_iota(jnp.int32, (BAND, 2 * BAND), 0)
    kj = lax.broadcasted_iota(jnp.int32, (BAND, 2 * BAND), 1)
    mask_rest = (kj >= qi) & (kj <= qi + BAND)
    mask_edge = (kj >= jnp.maximum(qi, jnp.where(t == 0, BAND, 0))) & (kj <= qi + BAND)

    lane = lax.broadcasted_iota(jnp.int32, (1, HEAD_DIM), 1)

    def finish(g, rows, res):
        sums = jnp.ones(res[0][1].shape, F32)
        maxes = jnp.zeros(res[0][1].shape, F32)
        for h, (acc, l, m) in enumerate(res):
            oslab_ref[g, h, rows, :] = acc
            sums = jnp.where(lane == h, l, sums)
            maxes = jnp.where(lane == h, m, maxes)
        lslab_ref[g, rows, :] = sums
        mslab_ref[g, rows, :] = maxes

    def normed_heads(g):
        qd = proj(g * D_ATT, (g + 1) * D_ATT)
        gain = qn_ref[g:g + 1, :] * QK_SCALE
        return [_rms(qd[:, _hs(h)], gain) for h in range(N_HEADS)]

    n_sub = TM // BAND
    qh = [q.astype(BF16) for q in normed_heads(0)]
    units0 = [(i, h) for i in range(n_sub) for h in range(N_HEADS)]

    def keys0(i, cols):
        if i == 0:
            return jnp.concatenate([kv0p_ref[0, :, cols], kv0_ref[0, 0:BAND, cols]], axis=0)
        return kv0_ref[0, (i - 1) * BAND:(i + 1) * BAND, cols]

    d1 = DILATIONS[1]
    for h, q in enumerate(normed_heads(1)):
        qslab1_ref[h] = q
    get_q1 = [_residue_rows(qslab1_ref, None, h, d1) for h in range(N_HEADS)]
    units1 = [(r, h) for r in range(d1) for h in range(N_HEADS)]

    def keys1(r, cols):
        return jnp.concatenate([kv1p_ref[r, :, cols], kv1_ref[r, :, cols]], axis=0)

    d2 = DILATIONS[2]
    nq = TM // d2
    for h, q in enumerate(normed_heads(2)):
        qslab2_ref[h] = q
    get_q2 = [_residue_rows(qslab2_ref, qsplit_ref, h, d2) for h in range(N_HEADS)]
    units2 = [(r, h) for r in range(d2) for h in range(N_HEADS)]
    qi2 = lax.broadcasted_iota(jnp.int32, (nq, BAND), 0)
    kj2 = lax.broadcasted_iota(jnp.int32, (nq, BAND), 1)
    mask2 = kj2 <= qi2 + t * nq

    s0 = _band_scores([qh[h][i * BAND:(i + 1) * BAND] for i, h in units0],
                      [keys0(i, _hs(h)) for i, h in units0],
                      [mask_edge if i == 0 else mask_rest for i, h in units0])
    s1 = _band_scores([get_q1[h](r).astype(BF16) for r, h in units1],
                      [keys1(r, _hs(h)) for r, h in units1], [mask_edge] * len(units1))
    s2 = _band_scores([get_q2[h](r).astype(BF16) for r, h in units2],
                      [kv2_ref[r, :, _hs(h)] for r, h in units2], [mask2] * len(units2))
    res = _band_finish(s0, [keys0(i, _hs(h, D_ATT)) for i, h in units0], BAND)
    for i in range(n_sub):
        finish(0, pl.ds(i * BAND, BAND), res[i * N_HEADS:(i + 1) * N_HEADS])
    res = _band_finish(s1, [keys1(r, _hs(h, D_ATT)) for r, h in units1], BAND)
    for r in range(d1):
        finish(1, pl.ds(r, BAND, stride=d1), res[r * N_HEADS:(r + 1) * N_HEADS])
    res = _band_finish(s2, [kv2_ref[r, :, _hs(h, D_ATT)] for r, h in units2], nq)
    for r in range(d2):
        finish(2, pl.ds(r, nq, stride=d2), res[r * N_HEADS:(r + 1) * N_HEADS])

    base = N_GROUPS * D_ATT
    gd = proj(base, base + D_ATT)
    m0, m1, m2 = mslab_ref[0], mslab_ref[1], mslab_ref[2]
    m = jnp.maximum(jnp.maximum(m0, m1), m2)
    e0, e1, e2 = jnp.exp2(m0 - m), jnp.exp2(m1 - m), jnp.exp2(m2 - m)
    inv = 1.0 / (e0 * lslab_ref[0] + e1 * lslab_ref[1] + e2 * lslab_ref[2])
    w0, w1, w2 = e0 * inv, e1 * inv, e2 * inv
    for h in range(N_HEADS):
        dil = (oslab_ref[0, h] * w0[:, h:h + 1] + oslab_ref[1, h] * w1[:, h:h + 1]
               + oslab_ref[2, h] * w2[:, h:h + 1])
        cat_ref[:, _hs(h)] = (dil * _silu(gd[:, _hs(h)])).astype(BF16)
    mq = proj(base + D_ATT, base + 2 * D_ATT)
    mg = proj(base + 2 * D_ATT, base + 3 * D_ATT)

    def store(h, val):
        cat_ref[:, _hs(h, D_ATT)] = val.astype(BF16)

    _mem_attn(mq, mg, mkv_ref, gq_ref[...] * QK_SCALE, store)
    y_ref[...] = x + _dot(cat_ref[...], wout_ref[...])

    @pl.when(_is_last_step())
    def _():
        ys_ref[...] = xs_ref[...] + _dot(cats_ref[...].astype(BF16), wout_ref[...])


def _b_layer(x, xs, norm_b, w_in, q_norm, mkv_b, kvbs, kv_new, caches, mem_cache, bl, layer,
             mem_q_norm, w_out):
    b, s, _ = x.shape
    nb = xs.shape[0]
    nt = s // TM
    assert nb == b * nt
    full, mem_cache_spec = _sample_specs(nb, nt, layer)
    cache_spec = pl.BlockSpec((1, BAND, None, KV_CHUNKS, HEAD_DIM),
                              lambda i, t: (i * nt + t, 0, 0, 0, 0))
    assert TM // BAND == DILATIONS[1] and s // DILATIONS[2] == BAND
    tile = lambda i, t: (i, t, 0)
    sub = TM // BAND
    kv_w = 2 * D_ATT
    return pl.pallas_call(
        _b_layer_kernel,
        grid=(b, s // TM),
        in_specs=[
            pl.BlockSpec((None, TM, D_MODEL), tile),
            _layer_spec((1, D_MODEL), bl),
            _layer_spec(w_in.shape[1:], bl),
            _layer_spec((N_GROUPS, HEAD_DIM), bl),
            pl.BlockSpec((None, None, MEM_LEN, kv_w), lambda i, t: (layer, i, 0, 0)),
            _layer_spec((1, HEAD_DIM), layer),
            _layer_spec(w_out.shape[1:], bl),
            pl.BlockSpec((None, 1, TM, kv_w), lambda i, t: (i, 0, t, 0)),
            pl.BlockSpec((None, 1, BAND, kv_w),
                         lambda i, t: (i, 0, jnp.maximum(t * sub - 1, 0), 0)),
            pl.BlockSpec((None, DILATIONS[1], BAND, kv_w), lambda i, t: (i, 0, t, 0)),
            pl.BlockSpec((None, DILATIONS[1], BAND, kv_w),
                         lambda i, t: (i, 0, jnp.maximum(t - 1, 0), 0)),
            pl.BlockSpec((None, DILATIONS[2], BAND, kv_w), lambda i, t: (i, 0, 0, 0)),
            full((nb, D_MODEL)),
            full(kv_new.shape),
            cache_spec, cache_spec, cache_spec,
            mem_cache_spec,
        ],
        out_specs=[pl.BlockSpec((None, TM, D_MODEL), tile), full((nb, D_MODEL))],
        out_shape=[jax.ShapeDtypeStruct((b, s, D_MODEL), F32),
                   jax.ShapeDtypeStruct((nb, D_MODEL), F32)],
        scratch_shapes=[pltpu.VMEM((N_HEADS, TM, HEAD_DIM), F32),
                        pltpu.VMEM((N_HEADS, TM, HEAD_DIM), F32),
                        pltpu.VMEM((N_HEADS, SUB_STRIDE, TM // SUB_STRIDE, HEAD_DIM), F32),
                        pltpu.VMEM((N_GROUPS, N_HEADS, TM, HEAD_DIM), F32),
                        pltpu.VMEM((N_GROUPS, TM, HEAD_DIM), F32),
                        pltpu.VMEM((N_GROUPS, TM, HEAD_DIM), F32),
                        pltpu.VMEM((TM, 2 * D_ATT), BF16),
                        pltpu.VMEM((nb, w_in.shape[2]), F32),
                        pltpu.VMEM((nb, 2 * D_ATT), F32)],
        compiler_params=_params(("arbitrary", "arbitrary")),
        name=f"b_layer_{layer}",
    )(x, norm_b, w_in, q_norm, mkv_b, mem_q_norm, w_out,
      kvbs[0], kvbs[0], kvbs[1], kvbs[1], kvbs[2],
      xs, kv_new, *caches, mem_cache)


def _swap_halves(x, axis):
    return pltpu.roll(x, N_HEADS, axis=axis)


def _head_rows(tile):
    return jnp.concatenate([tile[N_HEADS + h:N_HEADS + h + 1, :] for h in range(N_HEADS)],
                           axis=-1)


def _low_tile(row):
    return jnp.concatenate([jnp.zeros((N_HEADS, HEAD_DIM), F32)]
                           + [row[:, _hs(h)] for h in range(N_HEADS)], axis=0)


def _q_tile(row, g):
    return jnp.concatenate([_rms(row[:, _hs(h)], g) * QK_SCALE for h in range(N_HEADS)]
                           + [jnp.zeros((N_HEADS, HEAD_DIM), F32)], axis=0)


def _tile_attend(qt, kv):
    s = jnp.sum(kv * qt[None], axis=-1, keepdims=True)
    m = jnp.max(s, axis=0)
    p = jnp.exp2(s - m[None])
    l = jnp.sum(p, axis=0)
    acc = jnp.sum(_swap_halves(jnp.broadcast_to(p, kv.shape), 1) * kv, axis=0)
    out = acc / _swap_halves(jnp.broadcast_to(l, acc.shape), 0)
    return out, m + jnp.log2(l)


def _sample_mem_attn(q_row, gq, mkv, mg_row):
    out, _ = _tile_attend(_q_tile(q_row, gq), mkv)
    return _head_rows(out * _low_tile(_silu(mg_row)))


def _gather_chunks(ref, n_rows, n_chunks, offset=0):
    total = ref.shape[0] // n_rows
    return jnp.concatenate([ref[pl.ds(offset + c, n_rows, stride=total), :]
                            for c in range(n_chunks)], axis=-1)


def _merge_row(ref, cols, n, row):
    tile = pl.ds(pl.multiple_of((n // SAMPLE_GB) * SAMPLE_GB, SAMPLE_GB), SAMPLE_GB)
    sub = lax.broadcasted_iota(jnp.int32, (SAMPLE_GB, row.shape[1]), 0)
    ref[tile, cols] = jnp.where(sub == n % SAMPLE_GB, row, ref[tile, cols])


def _a_sample_first(x_ref, g_ref, win_ref, cw_ref, st_ref, sto_ref, zm_ref, cat_ref):
    nb = x_ref.shape[0]
    n_st = D_CONV // HEAD_DIM
    xn = _rms(x_ref[...], g_ref[...]).astype(BF16)
    z = _dot(xn, win_ref[...])
    u = z[:, D_CONV:2 * D_CONV] * z[:, 0:D_CONV]
    s0 = _gather_chunks(st_ref, nb, n_st)
    s1 = _gather_chunks(st_ref, nb, n_st, n_st)
    y = cw_ref[0:1, :] * s0 + cw_ref[1:2, :] * s1 + cw_ref[2:3, :] * u
    for c in range(n_st):
        sto_ref[pl.ds(c, nb, stride=2 * n_st), :] = s1[:, _hs(c)]
        sto_ref[pl.ds(n_st + c, nb, stride=2 * n_st), :] = u[:, _hs(c)]
    cat_ref[:, 0:D_CONV] = _silu(z[:, 3 * D_CONV:4 * D_CONV]) * z[:, 2 * D_CONV:3 * D_CONV] * y
    cat_ref[:, D_CONV:] = jnp.zeros((nb, D_ATT), F32)
    zm_ref[...] = z[:, 4 * D_CONV:]


def _kv_sample_rows(x_ref, g_ref, w_ref, kn_ref, o_ref):
    nb = x_ref.shape[0]
    xn = _rms(x_ref[...], g_ref[...]).astype(BF16)
    kv = _dot(xn, w_ref[...])
    n_chunks = N_GROUPS * KV_CHUNKS
    for g in range(N_GROUPS):
        for c in range(KV_CHUNKS):
            chunk = kv[:, _hs(c, g * 2 * D_ATT)]
            if c < N_HEADS:
                chunk = _rms(chunk, kn_ref[g:g + 1, :])
            o_ref[pl.ds(g * KV_CHUNKS + c, nb, stride=n_chunks), :] = chunk


def _b_sample_row(row, kv_new, qn_ref, gq, caches, mkv):
    base = N_GROUPS * D_ATT
    outs, lses = [], []
    for g in range(N_GROUPS):
        qt = _q_tile(row[:, g * D_ATT:(g + 1) * D_ATT], qn_ref[g:g + 1, :])
        out, lse = _tile_attend(qt, jnp.concatenate([caches[g], kv_new[g][None]], axis=0))
        outs.append(out)
        lses.append(lse)
    m = jnp.maximum(jnp.maximum(lses[0], lses[1]), lses[2])
    es = [jnp.exp2(l - m) for l in lses]
    den = es[0] + es[1] + es[2]
    ws = [_swap_halves(jnp.broadcast_to(e / den, (KV_CHUNKS, HEAD_DIM)), 0) for e in es]
    dil = outs[0] * ws[0] + outs[1] * ws[1] + outs[2] * ws[2]
    dil_row = _head_rows(dil * _low_tile(_silu(row[:, base:base + D_ATT])))
    mem_row = _sample_mem_attn(row[:, base + D_ATT:base + 2 * D_ATT], gq, mkv,
                               row[:, base + 2 * D_ATT:])
    return jnp.concatenate([dil_row, mem_row], axis=-1)


def kernel(x_prompt, x_sample, cache_mem_kv, state_conv, cache_dil0_kv, cache_dil1_kv, cache_dil2_kv, mem_prompt, norm_a, w_in_a, conv_w_a, w_out_a, norm_b, w_in_b, q_norm_b, w_out_b, kv_norm, w_kv, k_norm_dil, mem_norm, w_mem_kv, mem_k_norm, mem_q_norm):
    n_a = w_in_a.shape[0]
    n_b = w_in_b.shape[0]
    depth = n_a + n_b
    b, s, _ = x_prompt.shape
    nb = x_sample.shape[0]
    assert x_sample.shape[1] == 1 and s % TM == 0 and nb % SAMPLE_GB == 0
    kv_dims = (2, N_HEADS, HEAD_DIM)

    w_in_b16, w_out_b16 = w_in_b.astype(BF16), w_out_b.astype(BF16)
    norm_a3 = norm_a.reshape(n_a, 1, D_MODEL)
    norm_b3 = norm_b.reshape(n_b, 1, D_MODEL)
    mem_q_norm3 = mem_q_norm.reshape(depth, 1, HEAD_DIM)

    mem_kv_f, mem_kv_b = _memkv_prompt(mem_prompt, mem_norm, w_mem_kv, mem_k_norm)
    mem_kv_b = mem_kv_b.reshape(depth, b, MEM_LEN, 2 * D_ATT)
    mem_cache = cache_mem_kv.reshape(depth, nb, MEM_LEN, KV_CHUNKS, HEAD_DIM)
    caches = []
    for g, cache in enumerate((cache_dil0_kv, cache_dil1_kv, cache_dil2_kv)):
        assert cache.shape[1] == WINDOWS[g]
        caches.append(cache.reshape(nb, BAND, DILATIONS[g], KV_CHUNKS, HEAD_DIM))
    state = state_conv.reshape(n_a, -1, HEAD_DIM)

    xp = x_prompt
    xs = x_sample.reshape(nb, D_MODEL)
    conv_p, conv_s = [], []
    for a in range(n_a):
        xp, cp, xs, cs = _a_layer(xp, xs, norm_a3, w_in_a, conv_w_a, mem_kv_b, state, mem_cache,
                                  a, a, mem_q_norm3, w_out_a)
        conv_p.append(cp)
        conv_s.append(cs.reshape(nb, CONV_W - 1, D_CONV))

    kvb0, kvb1, kvb2, st0, st1, st2, kv_new = _kv_shared(xp, xs, kv_norm, w_kv, k_norm_dil)
    kvbs = (kvb0, kvb1, kvb2)
    kv_new = kv_new.reshape(nb, N_GROUPS, KV_CHUNKS, HEAD_DIM)

    for bl in range(n_b):
        layer = n_a + bl
        xp, xs = _b_layer(xp, xs, norm_b3, w_in_b16, q_norm_b, mem_kv_b, kvbs, kv_new, caches,
                          mem_cache, bl, layer, mem_q_norm3, w_out_b16)

    dil_s = [kv_new[:, g].reshape(nb, 1, *kv_dims) for g in range(N_GROUPS)]
    return (xp, xs.reshape(nb, 1, D_MODEL),
            mem_kv_f.reshape(depth, b, MEM_LEN, *kv_dims),
            jnp.stack(conv_p), jnp.stack(conv_s),
            st0.reshape(b, WINDOWS[0], *kv_dims), st1.reshape(b, WINDOWS[1], *kv_dims),
            st2.reshape(b, s, *kv_dims), *dil_s)
```

```python
import functools

import jax
import jax.numpy as jnp
from jax import lax
from jax.experimental import pallas as pl
from jax.experimental.pallas import tpu as pltpu

F32 = jnp.float32
BF16 = jnp.bfloat16

D_MODEL = 1024
HEAD_DIM = 128
N_HEADS = 4
D_ATT = N_HEADS * HEAD_DIM
KV_CHUNKS = 2 * N_HEADS
D_CONV = D_MODEL
CONV_W = 3
N_GROUPS = 3
DILATIONS = (1, 4, 16)
WINDOWS = (128, 512, 2048)
BAND = 128
SUB_STRIDE = 4
MEM_LEN = 256
EPS = 1e-6
NEG = -1e30
SCALE = HEAD_DIM ** -0.5
QK_SCALE = SCALE * 1.4426950408889634

TM = 512
SAMPLE_GB = 8
CARRY = 8
W_STAGE_ROWS = 256
VMEM_LIMIT = 56 * 1024 * 1024


def _rms(x, g):
    return x * lax.rsqrt(jnp.mean(x * x, axis=-1, keepdims=True) + EPS) * g


def _silu(x):
    return x * (1.0 / (1.0 + jnp.exp(-x)))


def _dot(a, b):
    return jnp.dot(a, b, preferred_element_type=F32)


def _dot_nt(a, b):
    return lax.dot_general(a, b, (((1,), (1,)), ((), ())), preferred_element_type=F32)


def _hs(h, base=0):
    return slice(base + h * HEAD_DIM, base + (h + 1) * HEAD_DIM)


def _params(sem):
    return pltpu.CompilerParams(dimension_semantics=sem, vmem_limit_bytes=VMEM_LIMIT)


def _fixed_spec(block, index):
    return pl.BlockSpec(block, lambda *_: index, pipeline_mode=pl.Buffered(1))


def _const_spec(shape):
    return _fixed_spec(shape, (0,) * len(shape))


def _layer_spec(shape, layer):
    return _fixed_spec((None,) + tuple(shape), (layer,) + (0,) * len(shape))


def _cast_weight(w_hbm, dst_ref, stage_ref, sem_ref):
    rows = stage_ref.shape[1]
    n_chunks = w_hbm.shape[0] // rows

    def copy(c, slot):
        src = w_hbm.at[pl.ds(pl.multiple_of(c * rows, rows), rows)]
        return pltpu.make_async_copy(src, stage_ref.at[slot], sem_ref.at[slot])

    copy(0, 0).start()

    def body(c, carry):
        slot = c % 2

        @pl.when(c + 1 < n_chunks)
        def _():
            copy(c + 1, 1 - slot).start()

        copy(c, slot).wait()
        dst_ref[pl.ds(pl.multiple_of(c * rows, rows), rows), :] = stage_ref[slot].astype(BF16)
        return carry

    lax.fori_loop(0, n_chunks, body, 0)


def _weight_scratch(shape):
    k, n = shape
    assert k % W_STAGE_ROWS == 0
    return [pltpu.VMEM((k, n), BF16), pltpu.VMEM((2, W_STAGE_ROWS, n), F32),
            pltpu.SemaphoreType.DMA((2,))]


def _memkv_kernel(mem_ref, g_ref, w_ref, kg_ref, of_ref, ob_ref):
    rt = mem_ref.shape[0]
    xn = _rms(mem_ref[...], g_ref[...]).astype(BF16)
    kv = _dot(xn, w_ref[...].astype(BF16))
    for c in range(KV_CHUNKS):
        chunk = kv[:, _hs(c)]
        if c < N_HEADS:
            chunk = _rms(chunk, kg_ref[...])
        of_ref[pl.ds(c, rt, stride=KV_CHUNKS), :] = chunk
        ob_ref[:, _hs(c)] = chunk.astype(BF16)


def _memkv_prompt(mem, mem_norm, w_mem_kv, mem_k_norm):
    depth = w_mem_kv.shape[0]
    rows = mem.shape[0] * mem.shape[1]
    rt = 1024
    return pl.pallas_call(
        _memkv_kernel,
        grid=(depth, rows // rt),
        in_specs=[
            pl.BlockSpec((rt, D_MODEL), lambda l, i: (i, 0)),
            pl.BlockSpec((None, 1, D_MODEL), lambda l, i: (l, 0, 0)),
            pl.BlockSpec((None, D_MODEL, 2 * D_ATT), lambda l, i: (l, 0, 0)),
            pl.BlockSpec((None, 1, HEAD_DIM), lambda l, i: (l, 0, 0)),
        ],
        out_specs=[
            pl.BlockSpec((None, rt * KV_CHUNKS, HEAD_DIM), lambda l, i: (l, i, 0)),
            pl.BlockSpec((None, rt, 2 * D_ATT), lambda l, i: (l, i, 0)),
        ],
        out_shape=[jax.ShapeDtypeStruct((depth, rows * KV_CHUNKS, HEAD_DIM), F32),
                   jax.ShapeDtypeStruct((depth, rows, 2 * D_ATT), BF16)],
        compiler_params=_params(("parallel", "parallel")),
        name="memkv_prompt",
    )(mem.reshape(rows, D_MODEL), mem_norm.reshape(depth, 1, D_MODEL), w_mem_kv,
      mem_k_norm.reshape(depth, 1, HEAD_DIM))


def _pv_and_rowsum(p, v):
    ones = jnp.ones((v.shape[0], HEAD_DIM), BF16)
    both = _dot(p, jnp.concatenate([v, ones], axis=1))
    return both[:, :HEAD_DIM], both[:, HEAD_DIM:]


def _mem_attn(mq, mg, mkv_ref, gq, store):
    for h in range(N_HEADS):
        q = _rms(mq[:, _hs(h)], gq).astype(BF16)
        s = _dot_nt(q, mkv_ref[:, _hs(h)])
        m = jnp.max(s, axis=-1, keepdims=True)
        acc, l = _pv_and_rowsum(jnp.exp2(s - m).astype(BF16), mkv_ref[:, _hs(h, D_ATT)])
        store(h, acc / l * _silu(mg[:, _hs(h)]))


def _step_index():
    return pl.program_id(0) * pl.num_programs(1) + pl.program_id(1)


def _is_last_step():
    return _step_index() == pl.num_programs(0) * pl.num_programs(1) - 1


def _a_layer_kernel(x_ref, g_ref, win_hbm, cw_ref, mkv_ref, gq_ref, wout_hbm,
                    xs_ref, st_ref, mc_ref,
                    y_ref, cs_ref, ys_ref, sto_ref,
                    ext_ref, cat_ref, zs_ref, cats_ref,
                    win_ref, win_stage, win_sem, wout_ref, wout_stage, wout_sem, *, a):
    t = pl.program_id(1)
    n = _step_index()

    @pl.when(t == 0)
    def _():
        ext_ref[0:CARRY, :] = jnp.zeros((CARRY, D_CONV), F32)

    @pl.when(n == 0)
    def _():
        _cast_weight(win_hbm.at[a], win_ref, win_stage, win_sem)
        _cast_weight(wout_hbm.at[a], wout_ref, wout_stage, wout_sem)
        _a_sample_first(xs_ref, g_ref, win_ref, cw_ref, st_ref, sto_ref, zs_ref, cats_ref)

    row = zs_ref[pl.ds(n, 1), :]
    _merge_row(cats_ref, slice(D_CONV, D_CONV + D_ATT), n,
               _sample_mem_attn(row[:, :D_ATT], gq_ref[...], mc_ref[0], row[:, D_ATT:]))

    x = x_ref[...]
    xn = _rms(x, g_ref[...]).astype(BF16)

    def proj(lo, hi):
        return _dot(xn, win_ref[:, lo:hi])

    u = proj(D_CONV, 2 * D_CONV) * proj(0, D_CONV)
    ext_ref[CARRY:CARRY + TM, :] = u
    y = (cw_ref[0:1, :] * ext_ref[CARRY - 2:CARRY - 2 + TM, :]
         + cw_ref[1:2, :] * ext_ref[CARRY - 1:CARRY - 1 + TM, :] + cw_ref[2:3, :] * u)
    ext_ref[0:CARRY, :] = ext_ref[TM:TM + CARRY, :]
    cs_ref[...] = ext_ref[CARRY - (CONV_W - 1):CARRY, :]
    bg = proj(2 * D_CONV, 3 * D_CONV)
    g = proj(3 * D_CONV, 4 * D_CONV)
    cat_ref[:, 0:D_CONV] = (_silu(g) * bg * y).astype(BF16)
    mq = proj(4 * D_CONV, 4 * D_CONV + D_ATT)
    mg = proj(4 * D_CONV + D_ATT, 4 * D_CONV + 2 * D_ATT)

    def store(h, val):
        cat_ref[:, _hs(h, D_CONV)] = val.astype(BF16)

    _mem_attn(mq, mg, mkv_ref, gq_ref[...] * QK_SCALE, store)
    y_ref[...] = x + _dot(cat_ref[...], wout_ref[...])

    @pl.when(_is_last_step())
    def _():
        ys_ref[...] = xs_ref[...] + _dot(cats_ref[...].astype(BF16), wout_ref[...])


def _sample_specs(nb, nt, layer):
    full = lambda shape: pl.BlockSpec(shape, lambda i, t: (0,) * len(shape))
    mem_cache = pl.BlockSpec((None, 1, MEM_LEN, KV_CHUNKS, HEAD_DIM),
                             lambda i, t: (layer, i * nt + t, 0, 0, 0))
    return full, mem_cache


def _a_layer(x, xs, norm_a, w_in, conv_w, mkv_b, state, mem_cache, a, layer, mem_q_norm, w_out):
    b, s, _ = x.shape
    nb = xs.shape[0]
    nt = s // TM
    assert nb == b * nt
    st_rows = nb * (CONV_W - 1) * D_CONV // HEAD_DIM
    full, mem_cache_spec = _sample_specs(nb, nt, layer)
    return pl.pallas_call(
        functools.partial(_a_layer_kernel, a=a),
        grid=(b, nt),
        in_specs=[
            pl.BlockSpec((None, TM, D_MODEL), lambda i, t: (i, t, 0)),
            _layer_spec((1, D_MODEL), a),
            pl.BlockSpec(memory_space=pl.ANY),
            _layer_spec((CONV_W, D_CONV), a),
            pl.BlockSpec((None, None, MEM_LEN, 2 * D_ATT), lambda i, t: (layer, i, 0, 0)),
            _layer_spec((1, HEAD_DIM), layer),
            pl.BlockSpec(memory_space=pl.ANY),
            full((nb, D_MODEL)),
            _layer_spec((st_rows, HEAD_DIM), a),
            mem_cache_spec,
        ],
        out_specs=[
            pl.BlockSpec((None, TM, D_MODEL), lambda i, t: (i, t, 0)),
            pl.BlockSpec((None, CONV_W - 1, D_CONV), lambda i, t: (i, 0, 0)),
            full((nb, D_MODEL)),
            full((st_rows, HEAD_DIM)),
        ],
        out_shape=[jax.ShapeDtypeStruct((b, s, D_MODEL), F32),
                   jax.ShapeDtypeStruct((b, CONV_W - 1, D_CONV), F32),
                   jax.ShapeDtypeStruct((nb, D_MODEL), F32),
                   jax.ShapeDtypeStruct((st_rows, HEAD_DIM), F32)],
        scratch_shapes=[pltpu.VMEM((TM + CARRY, D_CONV), F32),
                        pltpu.VMEM((TM, D_CONV + D_ATT), BF16),
                        pltpu.VMEM((nb, 2 * D_ATT), F32),
                        pltpu.VMEM((nb, D_CONV + D_ATT), F32)]
        + _weight_scratch(w_in.shape[1:]) + _weight_scratch(w_out.shape[1:]),
        compiler_params=_params(("arbitrary", "arbitrary")),
        name=f"a_layer_{layer}",
    )(x, norm_a, w_in, conv_w, mkv_b, mem_q_norm, w_out, xs, state, mem_cache)


def _residue_rows(slab_ref, split_ref, c, d):
    rows = slab_ref.shape[1]
    if d == SUB_STRIDE:
        return lambda r: slab_ref[c, pl.ds(r, rows // d, stride=d), :]
    assert d == SUB_STRIDE * SUB_STRIDE
    for r1 in range(SUB_STRIDE):
        split_ref[c, r1] = slab_ref[c, pl.ds(r1, rows // SUB_STRIDE, stride=SUB_STRIDE), :]
    return lambda r: split_ref[c, r % SUB_STRIDE,
                               pl.ds(r // SUB_STRIDE, rows // d, stride=SUB_STRIDE), :]


def _to_residue_major(dst_ref, slab_ref, split_ref, n_chunks, d):
    for c in range(n_chunks):
        get = _residue_rows(slab_ref, split_ref, c, d)
        for r in range(d):
            dst_ref[r, :, _hs(c)] = get(r).astype(dst_ref.dtype)


def _kv_kernel(x_ref, g_ref, w_hbm, kn_ref, xs_ref, kvb0_ref, kvb1_ref, kvb2_ref,
               st0_ref, st1_ref, st2_ref, kvn_ref, slab1_ref, slab2_ref, split_ref,
               w_ref, w_stage, w_sem):
    t = pl.program_id(1)
    last = pl.num_programs(1) - 1

    @pl.when(_step_index() == 0)
    def _():
        _cast_weight(w_hbm, w_ref, w_stage, w_sem)
        _kv_sample_rows(xs_ref, g_ref, w_ref, kn_ref, kvn_ref)

    xn = _rms(x_ref[...], g_ref[...]).astype(BF16)
    kvb_refs = (kvb0_ref, kvb1_ref, kvb2_ref)
    slabs = (None, slab1_ref, slab2_ref)
    for g in reversed(range(N_GROUPS)):
        kv = _dot(xn, w_ref[:, g * 2 * D_ATT:(g + 1) * 2 * D_ATT])
        chunks = [_rms(kv[:, _hs(h)], kn_ref[g:g + 1, :]) for h in range(N_HEADS)]
        chunks += [kv[:, _hs(h, D_ATT)] for h in range(N_HEADS)]
        d = DILATIONS[g]
        for c, chunk in enumerate(chunks):
            if d == 1:
                kvb_refs[g][0, :, _hs(c)] = chunk.astype(BF16)
            else:
                slabs[g][c] = chunk
            if g == 2:
                st2_ref[pl.ds(c, TM, stride=KV_CHUNKS), :] = chunk
        if d > 1:
            _to_residue_major(kvb_refs[g], slabs[g], split_ref, KV_CHUNKS, d)
        if g == 1:
            @pl.when(t == last)
            def _():
                for c, chunk in enumerate(chunks):
                    st1_ref[pl.ds(c, TM, stride=KV_CHUNKS), :] = chunk
        if g == 0:
            @pl.when(t == last)
            def _():
                for c, chunk in enumerate(chunks):
                    st0_ref[pl.ds(c, WINDOWS[0], stride=KV_CHUNKS), :] = chunk[TM - WINDOWS[0]:, :]


def _kv_shared(x, xs, kv_norm, w_kv, k_norm_dil):
    b, s, _ = x.shape
    nb = xs.shape[0]
    kvn_rows = nb * N_GROUPS * KV_CHUNKS
    assert WINDOWS[1] == TM and WINDOWS[2] == s and WINDOWS[0] <= TM
    first = lambda i, t: (i, 0, 0)
    res_spec = lambda d: pl.BlockSpec((None, d, TM // d, 2 * D_ATT), lambda i, t: (i, 0, t, 0))
    res_sds = lambda d: jax.ShapeDtypeStruct((b, d, s // d, 2 * D_ATT), BF16)
    st_sds = lambda rows: jax.ShapeDtypeStruct((b, rows * KV_CHUNKS, HEAD_DIM), F32)
    return pl.pallas_call(
        _kv_kernel,
        grid=(b, s // TM),
        in_specs=[
            pl.BlockSpec((None, TM, D_MODEL), lambda i, t: (i, t, 0)),
            _const_spec((1, D_MODEL)),
            pl.BlockSpec(memory_space=pl.ANY),
            _const_spec((N_GROUPS, HEAD_DIM)),
            _const_spec((nb, D_MODEL)),
        ],
        out_specs=[
            res_spec(DILATIONS[0]), res_spec(DILATIONS[1]), res_spec(DILATIONS[2]),
            pl.BlockSpec((None, WINDOWS[0] * KV_CHUNKS, HEAD_DIM), first),
            pl.BlockSpec((None, WINDOWS[1] * KV_CHUNKS, HEAD_DIM), first),
            pl.BlockSpec((None, TM * KV_CHUNKS, HEAD_DIM), lambda i, t: (i, t, 0)),
            pl.BlockSpec((kvn_rows, HEAD_DIM), lambda i, t: (0, 0)),
        ],
        out_shape=[res_sds(DILATIONS[0]), res_sds(DILATIONS[1]), res_sds(DILATIONS[2]),
                   st_sds(WINDOWS[0]), st_sds(WINDOWS[1]), st_sds(s),
                   jax.ShapeDtypeStruct((kvn_rows, HEAD_DIM), F32)],
        scratch_shapes=[pltpu.VMEM((KV_CHUNKS, TM, HEAD_DIM), F32),
                        pltpu.VMEM((KV_CHUNKS, TM, HEAD_DIM), F32),
                        pltpu.VMEM((KV_CHUNKS, SUB_STRIDE, TM // SUB_STRIDE, HEAD_DIM), F32)]
        + _weight_scratch(w_kv.shape),
        compiler_params=_params(("arbitrary", "arbitrary")),
        name="kv_prompt",
    )(x, kv_norm.reshape(1, D_MODEL), w_kv, k_norm_dil, xs)


def _band_scores(qs, ks, masks):
    return jnp.concatenate([jnp.where(mk, _dot_nt(q, k), NEG) for q, k, mk in zip(qs, ks, masks)],
                           axis=0)


def _band_finish(s, vs, nq):
    m = jnp.max(s, axis=-1, keepdims=True)
    pb = jnp.exp2(s - m).astype(BF16)
    res = []
    for i, v in enumerate(vs):
        rows = slice(i * nq, (i + 1) * nq)
        acc, l = _pv_and_rowsum(pb[rows], v)
        res.append((acc, l, m[rows]))
    return res


def _b_layer_kernel(x_ref, g_ref, win_ref, qn_ref, mkv_ref, gq_ref, wout_ref,
                    kv0_ref, kv0p_ref, kv1_ref, kv1p_ref, kv2_ref,
                    xs_ref, kvn_ref, c0_ref, c1_ref, c2_ref, mc_ref,
                    y_ref, ys_ref,
                    qslab1_ref, qslab2_ref, qsplit_ref, oslab_ref, lslab_ref, mslab_ref, cat_ref,
                    zs_ref, cats_ref):
    t = pl.program_id(1)
    n = _step_index()

    @pl.when(n == 0)
    def _():
        zs_ref[...] = _dot(_rms(xs_ref[...], g_ref[...]).astype(BF16), win_ref[...])
        cats_ref[...] = jnp.zeros(cats_ref.shape, F32)

    _merge_row(cats_ref, slice(0, 2 * D_ATT), n,
               _b_sample_row(zs_ref[pl.ds(n, 1), :], kvn_ref[n], qn_ref, gq_ref[...],
                             (c0_ref[0], c1_ref[0], c2_ref[0]), mc_ref[0]))

    x = x_ref[...]
    xn = _rms(x, g_ref[...]).astype(BF16)

    def proj(lo, hi):
        return _dot(xn, win_ref[:, lo:hi])

    qi = lax.broadcasted_iota(jnp.int32, (BAND, 2 * BAND), 0)
    kj = lax.broadcasted_iota(jnp.int32, (BAND, 2 * BAND), 1)
    mask_rest = (kj >= qi) & (kj <= qi + BAND)
    mask_edge = (kj >= jnp.maximum(qi, jnp.where(t == 0, BAND, 0))) & (kj <= qi + BAND)

    lane = lax.broadcasted_iota(jnp.int32, (1, HEAD_DIM), 1)

    def finish(g, rows, res):
        sums = jnp.ones(res[0][1].shape, F32)
        maxes = jnp.zeros(res[0][1].shape, F32)
        for h, (acc, l, m) in enumerate(res):
            oslab_ref[g, h, rows, :] = acc
            sums = jnp.where(lane == h, l, sums)
            maxes = jnp.where(lane == h, m, maxes)
        lslab_ref[g, rows, :] = sums
        mslab_ref[g, rows, :] = maxes

    def normed_heads(g):
        qd = proj(g * D_ATT, (g + 1) * D_ATT)
        gain = qn_ref[g:g + 1, :] * QK_SCALE
        return [_rms(qd[:, _hs(h)], gain) for h in range(N_HEADS)]

    n_sub = TM // BAND
    qh = [q.astype(BF16) for q in normed_heads(0)]
    units0 = [(i, h) for i in range(n_sub) for h in range(N_HEADS)]

    def keys0(i, cols):
        if i == 0:
            return jnp.concatenate([kv0p_ref[0, :, cols], kv0_ref[0, 0:BAND, cols]], axis=0)
        return kv0_ref[0, (i - 1) * BAND:(i + 1) * BAND, cols]

    d1 = DILATIONS[1]
    for h, q in enumerate(normed_heads(1)):
        qslab1_ref[h] = q
    get_q1 = [_residue_rows(qslab1_ref, None, h, d1) for h in range(N_HEADS)]
    units1 = [(r, h) for r in range(d1) for h in range(N_HEADS)]

    def keys1(r, cols):
        return jnp.concatenate([kv1p_ref[r, :, cols], kv1_ref[r, :, cols]], axis=0)

    d2 = DILATIONS[2]
    nq = TM // d2
    for h, q in enumerate(normed_heads(2)):
        qslab2_ref[h] = q
    get_q2 = [_residue_rows(qslab2_ref, qsplit_ref, h, d2) for h in range(N_HEADS)]
    units2 = [(r, h) for r in range(d2) for h in range(N_HEADS)]
    qi2 = lax.broadcasted_iota(jnp.int32, (nq, BAND), 0)
    kj2 = lax.broadcasted_iota(jnp.int32, (nq, BAND), 1)
    mask2 = kj2 <= qi2 + t * nq

    def attend(g, units, q_of, k_of, v_of, mask_of, nq_, rows_of, per_block):
        half = len(units) // 2
        for lo in (0, half):
            part = units[lo:lo + half]
            sc = _band_scores([q_of(u) for u in part], [k_of(u) for u in part],
                              [mask_of(u) for u in part])
            res = _band_finish(sc, [v_of(u) for u in part], nq_)
            for j in range(0, len(part), N_HEADS):
                finish(g, rows_of(part[j]), res[j:j + N_HEADS])

    attend(0, units0, lambda u: qh[u[1]][u[0] * BAND:(u[0] + 1) * BAND],
           lambda u: keys0(u[0], _hs(u[1])), lambda u: keys0(u[0], _hs(u[1], D_ATT)),
           lambda u: mask_edge if u[0] == 0 else mask_rest, BAND,
           lambda u: pl.ds(u[0] * BAND, BAND), N_HEADS)
    attend(1, units1, lambda u: get_q1[u[1]](u[0]).astype(BF16),
           lambda u: keys1(u[0], _hs(u[1])), lambda u: keys1(u[0], _hs(u[1], D_ATT)),
           lambda u: mask_edge, BAND, lambda u: pl.ds(u[0], BAND, stride=d1), N_HEADS)
    attend(2, units2, lambda u: get_q2[u[1]](u[0]).astype(BF16),
           lambda u: kv2_ref[u[0], :, _hs(u[1])], lambda u: kv2_ref[u[0], :, _hs(u[1], D_ATT)],
           lambda u: mask2, nq, lambda u: pl.ds(u[0], nq, stride=d2), N_HEADS)

    base = N_GROUPS * D_ATT
    gd = proj(base, base + D_ATT)
    m0, m1, m2 = mslab_ref[0], mslab_ref[1], mslab_ref[2]
    m = jnp.maximum(jnp.maximum(m0, m1), m2)
    e0, e1, e2 = jnp.exp2(m0 - m), jnp.exp2(m1 - m), jnp.exp2(m2 - m)
    inv = 1.0 / (e0 * lslab_ref[0] + e1 * lslab_ref[1] + e2 * lslab_ref[2])
    w0, w1, w2 = e0 * inv, e1 * inv, e2 * inv
    for h in range(N_HEADS):
        dil = (oslab_ref[0, h] * w0[:, h:h + 1] + oslab_ref[1, h] * w1[:, h:h + 1]
               + oslab_ref[2, h] * w2[:, h:h + 1])
        cat_ref[:, _hs(h)] = (dil * _silu(gd[:, _hs(h)])).astype(BF16)
    mq = proj(base + D_ATT, base + 2 * D_ATT)
    mg = proj(base + 2 * D_ATT, base + 3 * D_ATT)

    def store(h, val):
        cat_ref[:, _hs(h, D_ATT)] = val.astype(BF16)

    _mem_attn(mq, mg, mkv_ref, gq_ref[...] * QK_SCALE, store)
    y_ref[...] = x + _dot(cat_ref[...], wout_ref[...])

    @pl.when(_is_last_step())
    def _():
        ys_ref[...] = xs_ref[...] + _dot(cats_ref[...].astype(BF16), wout_ref[...])


def _b_layer(x, xs, norm_b, w_in, q_norm, mkv_b, kvbs, kv_new, caches, mem_cache, bl, layer,
             mem_q_norm, w_out):
    b, s, _ = x.shape
    nb = xs.shape[0]
    nt = s // TM
    assert nb == b * nt
    full, mem_cache_spec = _sample_specs(nb, nt, layer)
    cache_spec = pl.BlockSpec((1, BAND, None, KV_CHUNKS, HEAD_DIM),
                              lambda i, t: (i * nt + t, 0, 0, 0, 0))
    assert TM // BAND == DILATIONS[1] and s // DILATIONS[2] == BAND
    tile = lambda i, t: (i, t, 0)
    sub = TM // BAND
    kv_w = 2 * D_ATT
    return pl.pallas_call(
        _b_layer_kernel,
        grid=(b, s // TM),
        in_specs=[
            pl.BlockSpec((None, TM, D_MODEL), tile),
            _layer_spec((1, D_MODEL), bl),
            _layer_spec(w_in.shape[1:], bl),
            _layer_spec((N_GROUPS, HEAD_DIM), bl),
            pl.BlockSpec((None, None, MEM_LEN, kv_w), lambda i, t: (layer, i, 0, 0)),
            _layer_spec((1, HEAD_DIM), layer),
            _layer_spec(w_out.shape[1:], bl),
            pl.BlockSpec((None, 1, TM, kv_w), lambda i, t: (i, 0, t, 0)),
            pl.BlockSpec((None, 1, BAND, kv_w),
                         lambda i, t: (i, 0, jnp.maximum(t * sub - 1, 0), 0)),
            pl.BlockSpec((None, DILATIONS[1], BAND, kv_w), lambda i, t: (i, 0, t, 0)),
            pl.BlockSpec((None, DILATIONS[1], BAND, kv_w),
                         lambda i, t: (i, 0, jnp.maximum(t - 1, 0), 0)),
            pl.BlockSpec((None, DILATIONS[2], BAND, kv_w), lambda i, t: (i, 0, 0, 0)),
            full((nb, D_MODEL)),
            full(kv_new.shape),
            cache_spec, cache_spec, cache_spec,
            mem_cache_spec,
        ],
        out_specs=[pl.BlockSpec((None, TM, D_MODEL), tile), full((nb, D_MODEL))],
        out_shape=[jax.ShapeDtypeStruct((b, s, D_MODEL), F32),
                   jax.ShapeDtypeStruct((nb, D_MODEL), F32)],
        scratch_shapes=[pltpu.VMEM((N_HEADS, TM, HEAD_DIM), F32),
                        pltpu.VMEM((N_HEADS, TM, HEAD_DIM), F32),
                        pltpu.VMEM((N_HEADS, SUB_STRIDE, TM // SUB_STRIDE, HEAD_DIM), F32),
                        pltpu.VMEM((N_GROUPS, N_HEADS, TM, HEAD_DIM), F32),
                        pltpu.VMEM((N_GROUPS, TM, HEAD_DIM), F32),
                        pltpu.VMEM((N_GROUPS, TM, HEAD_DIM), F32),
                        pltpu.VMEM((TM, 2 * D_ATT), BF16),
                        pltpu.VMEM((nb, w_in.shape[2]), F32),
                        pltpu.VMEM((nb, 2 * D_ATT), F32)],
        compiler_params=_params(("arbitrary", "arbitrary")),
        name=f"b_layer_{layer}",
    )(x, norm_b, w_in, q_norm, mkv_b, mem_q_norm, w_out,
      kvbs[0], kvbs[0], kvbs[1], kvbs[1], kvbs[2],
      xs, kv_new, *caches, mem_cache)


def _swap_halves(x, axis):
    return pltpu.roll(x, N_HEADS, axis=axis)


def _head_rows(tile):
    return jnp.concatenate([tile[N_HEADS + h:N_HEADS + h + 1, :] for h in range(N_HEADS)],
                           axis=-1)


def _low_tile(row):
    return jnp.concatenate([jnp.zeros((N_HEADS, HEAD_DIM), F32)]
                           + [row[:, _hs(h)] for h in range(N_HEADS)], axis=0)


def _q_tile(row, g):
    return jnp.concatenate([_rms(row[:, _hs(h)], g) * QK_SCALE for h in range(N_HEADS)]
                           + [jnp.zeros((N_HEADS, HEAD_DIM), F32)], axis=0)


def _tile_attend(qt, kv):
    s = jnp.sum(kv * qt[None], axis=-1, keepdims=True)
    m = jnp.max(s, axis=0)
    p = jnp.exp2(s - m[None])
    l = jnp.sum(p, axis=0)
    acc = jnp.sum(_swap_halves(jnp.broadcast_to(p, kv.shape), 1) * kv, axis=0)
    out = acc / _swap_halves(jnp.broadcast_to(l, acc.shape), 0)
    return out, m + jnp.log2(l)


def _sample_mem_attn(q_row, gq, mkv, mg_row):
    out, _ = _tile_attend(_q_tile(q_row, gq), mkv)
    return _head_rows(out * _low_tile(_silu(mg_row)))


def _gather_chunks(ref, n_rows, n_chunks, offset=0):
    total = ref.shape[0] // n_rows
    return jnp.concatenate([ref[pl.ds(offset + c, n_rows, stride=total), :]
                            for c in range(n_chunks)], axis=-1)


def _merge_row(ref, cols, n, row):
    tile = pl.ds(pl.multiple_of((n // SAMPLE_GB) * SAMPLE_GB, SAMPLE_GB), SAMPLE_GB)
    sub = lax.broadcasted_iota(jnp.int32, (SAMPLE_GB, row.shape[1]), 0)
    ref[tile, cols] = jnp.where(sub == n % SAMPLE_GB, row, ref[tile, cols])


def _a_sample_first(x_ref, g_ref, win_ref, cw_ref, st_ref, sto_ref, zm_ref, cat_ref):
    nb = x_ref.shape[0]
    n_st = D_CONV // HEAD_DIM
    xn = _rms(x_ref[...], g_ref[...]).astype(BF16)
    z = _dot(xn, win_ref[...])
    u = z[:, D_CONV:2 * D_CONV] * z[:, 0:D_CONV]
    s0 = _gather_chunks(st_ref, nb, n_st)
    s1 = _gather_chunks(st_ref, nb, n_st, n_st)
    y = cw_ref[0:1, :] * s0 + cw_ref[1:2, :] * s1 + cw_ref[2:3, :] * u
    for c in range(n_st):
        sto_ref[pl.ds(c, nb, stride=2 * n_st), :] = s1[:, _hs(c)]
        sto_ref[pl.ds(n_st + c, nb, stride=2 * n_st), :] = u[:, _hs(c)]
    cat_ref[:, 0:D_CONV] = _silu(z[:, 3 * D_CONV:4 * D_CONV]) * z[:, 2 * D_CONV:3 * D_CONV] * y
    cat_ref[:, D_CONV:] = jnp.zeros((nb, D_ATT), F32)
    zm_ref[...] = z[:, 4 * D_CONV:]


def _kv_sample_rows(x_ref, g_ref, w_ref, kn_ref, o_ref):
    nb = x_ref.shape[0]
    xn = _rms(x_ref[...], g_ref[...]).astype(BF16)
    kv = _dot(xn, w_ref[...])
    n_chunks = N_GROUPS * KV_CHUNKS
    for g in range(N_GROUPS):
        for c in range(KV_CHUNKS):
            chunk = kv[:, _hs(c, g * 2 * D_ATT)]
            if c < N_HEADS:
                chunk = _rms(chunk, kn_ref[g:g + 1, :])
            o_ref[pl.ds(g * KV_CHUNKS + c, nb, stride=n_chunks), :] = chunk


def _b_sample_row(row, kv_new, qn_ref, gq, caches, mkv):
    base = N_GROUPS * D_ATT
    outs, lses = [], []
    for g in range(N_GROUPS):
        qt = _q_tile(row[:, g * D_ATT:(g + 1) * D_ATT], qn_ref[g:g + 1, :])
        out, lse = _tile_attend(qt, jnp.concatenate([caches[g], kv_new[g][None]], axis=0))
        outs.append(out)
        lses.append(lse)
    m = jnp.maximum(jnp.maximum(lses[0], lses[1]), lses[2])
    es = [jnp.exp2(l - m) for l in lses]
    den = es[0] + es[1] + es[2]
    ws = [_swap_halves(jnp.broadcast_to(e / den, (KV_CHUNKS, HEAD_DIM)), 0) for e in es]
    dil = outs[0] * ws[0] + outs[1] * ws[1] + outs[2] * ws[2]
    dil_row = _head_rows(dil * _low_tile(_silu(row[:, base:base + D_ATT])))
    mem_row = _sample_mem_attn(row[:, base + D_ATT:base + 2 * D_ATT], gq, mkv,
                               row[:, base + 2 * D_ATT:])
    return jnp.concatenate([dil_row, mem_row], axis=-1)


def kernel(x_prompt, x_sample, cache_mem_kv, state_conv, cache_dil0_kv, cache_dil1_kv, cache_dil2_kv, mem_prompt, norm_a, w_in_a, conv_w_a, w_out_a, norm_b, w_in_b, q_norm_b, w_out_b, kv_norm, w_kv, k_norm_dil, mem_norm, w_mem_kv, mem_k_norm, mem_q_norm):
    n_a = w_in_a.shape[0]
    n_b = w_in_b.shape[0]
    depth = n_a + n_b
    b, s, _ = x_prompt.shape
    nb = x_sample.shape[0]
    assert x_sample.shape[1] == 1 and s % TM == 0 and nb % SAMPLE_GB == 0
    kv_dims = (2, N_HEADS, HEAD_DIM)

    w_in_b16, w_out_b16 = w_in_b.astype(BF16), w_out_b.astype(BF16)
    norm_a3 = norm_a.reshape(n_a, 1, D_MODEL)
    norm_b3 = norm_b.reshape(n_b, 1, D_MODEL)
    mem_q_norm3 = mem_q_norm.reshape(depth, 1, HEAD_DIM)

    mem_kv_f, mem_kv_b = _memkv_prompt(mem_prompt, mem_norm, w_mem_kv, mem_k_norm)
    mem_kv_b = mem_kv_b.reshape(depth, b, MEM_LEN, 2 * D_ATT)
    mem_cache = cache_mem_kv.reshape(depth, nb, MEM_LEN, KV_CHUNKS, HEAD_DIM)
    caches = []
    for g, cache in enumerate((cache_dil0_kv, cache_dil1_kv, cache_dil2_kv)):
        assert cache.shape[1] == WINDOWS[g]
        caches.append(cache.reshape(nb, BAND, DILATIONS[g], KV_CHUNKS, HEAD_DIM))
    state = state_conv.reshape(n_a, -1, HEAD_DIM)

    xp = x_prompt
    xs = x_sample.reshape(nb, D_MODEL)
    conv_p, conv_s = [], []
    for a in range(n_a):
        xp, cp, xs, cs = _a_layer(xp, xs, norm_a3, w_in_a, conv_w_a, mem_kv_b, state, mem_cache,
                                  a, a, mem_q_norm3, w_out_a)
        conv_p.append(cp)
        conv_s.append(cs.reshape(nb, CONV_W - 1, D_CONV))

    kvb0, kvb1, kvb2, st0, st1, st2, kv_new = _kv_shared(xp, xs, kv_norm, w_kv, k_norm_dil)
    kvbs = (kvb0, kvb1, kvb2)
    kv_new = kv_new.reshape(nb, N_GROUPS, KV_CHUNKS, HEAD_DIM)

    for bl in range(n_b):
        layer = n_a + bl
        xp, xs = _b_layer(xp, xs, norm_b3, w_in_b16, q_norm_b, mem_kv_b, kvbs, kv_new, caches,
                          mem_cache, bl, layer, mem_q_norm3, w_out_b16)

    dil_s = [kv_new[:, g].reshape(nb, 1, *kv_dims) for g in range(N_GROUPS)]
    return (xp, xs.reshape(nb, 1, D_MODEL),
            mem_kv_f.reshape(depth, b, MEM_LEN, *kv_dims),
            jnp.stack(conv_p), jnp.stack(conv_s),
            st0.reshape(b, WINDOWS[0], *kv_dims), st1.reshape(b, WINDOWS[1], *kv_dims),
            st2.reshape(b, s, *kv_dims), *dil_s)
```

```python
import functools

import jax
import jax.numpy as jnp
from jax import lax
from jax.experimental import pallas as pl
from jax.experimental.pallas import tpu as pltpu

F32 = jnp.float32
BF16 = jnp.bfloat16

D_MODEL = 1024
HEAD_DIM = 128
N_HEADS = 4
D_ATT = N_HEADS * HEAD_DIM
KV_CHUNKS = 2 * N_HEADS
D_CONV = D_MODEL
CONV_W = 3
N_GROUPS = 3
DILATIONS = (1, 4, 16)
WINDOWS = (128, 512, 2048)
BAND = 128
SUB_STRIDE = 4
MEM_LEN = 256
EPS = 1e-6
NEG = -1e30
SCALE = HEAD_DIM ** -0.5
QK_SCALE = SCALE * 1.4426950408889634

TM = 512
SAMPLE_GB = 8
CARRY = 8
W_STAGE_ROWS = 256
VMEM_LIMIT = 56 * 1024 * 1024


def _rms(x, g):
    return x * lax.rsqrt(jnp.mean(x * x, axis=-1, keepdims=True) + EPS) * g


def _silu(x):
    return x * (1.0 / (1.0 + jnp.exp(-x)))


def _dot(a, b):
    return jnp.dot(a, b, preferred_element_type=F32)


def _dot_nt(a, b):
    return lax.dot_general(a, b, (((1,), (1,)), ((), ())), preferred_element_type=F32)


def _hs(h, base=0):
    return slice(base + h * HEAD_DIM, base + (h + 1) * HEAD_DIM)


def _params(sem):
    return pltpu.CompilerParams(dimension_semantics=sem, vmem_limit_bytes=VMEM_LIMIT)


def _fixed_spec(block, index):
    return pl.BlockSpec(block, lambda *_: index, pipeline_mode=pl.Buffered(1))


def _const_spec(shape):
    return _fixed_spec(shape, (0,) * len(shape))


def _layer_spec(shape, layer):
    return _fixed_spec((None,) + tuple(shape), (layer,) + (0,) * len(shape))


def _cast_weight(w_hbm, dst_ref, stage_ref, sem_ref):
    rows = stage_ref.shape[1]
    n_chunks = w_hbm.shape[0] // rows

    def copy(c, slot):
        src = w_hbm.at[pl.ds(pl.multiple_of(c * rows, rows), rows)]
        return pltpu.make_async_copy(src, stage_ref.at[slot], sem_ref.at[slot])

    copy(0, 0).start()

    def body(c, carry):
        slot = c % 2

        @pl.when(c + 1 < n_chunks)
        def _():
            copy(c + 1, 1 - slot).start()

        copy(c, slot).wait()
        dst_ref[pl.ds(pl.multiple_of(c * rows, rows), rows), :] = stage_ref[slot].astype(BF16)
        return carry

    lax.fori_loop(0, n_chunks, body, 0)


def _weight_scratch(shape):
    k, n = shape
    assert k % W_STAGE_ROWS == 0
    return [pltpu.VMEM((k, n), BF16), pltpu.VMEM((2, W_STAGE_ROWS, n), F32),
            pltpu.SemaphoreType.DMA((2,))]


def _memkv_kernel(mem_ref, g_ref, w_ref, kg_ref, of_ref, ob_ref):
    rt = mem_ref.shape[0]
    xn = _rms(mem_ref[...], g_ref[...]).astype(BF16)
    kv = _dot(xn, w_ref[...].astype(BF16))
    for c in range(KV_CHUNKS):
        chunk = kv[:, _hs(c)]
        if c < N_HEADS:
            chunk = _rms(chunk, kg_ref[...])
        of_ref[pl.ds(c, rt, stride=KV_CHUNKS), :] = chunk
        ob_ref[:, _hs(c)] = chunk.astype(BF16)


def _memkv_prompt(mem, mem_norm, w_mem_kv, mem_k_norm):
    depth = w_mem_kv.shape[0]
    rows = mem.shape[0] * mem.shape[1]
    rt = 1024
    return pl.pallas_call(
        _memkv_kernel,
        grid=(depth, rows // rt),
        in_specs=[
            pl.BlockSpec((rt, D_MODEL), lambda l, i: (i, 0)),
            pl.BlockSpec((None, 1, D_MODEL), lambda l, i: (l, 0, 0)),
            pl.BlockSpec((None, D_MODEL, 2 * D_ATT), lambda l, i: (l, 0, 0)),
            pl.BlockSpec((None, 1, HEAD_DIM), lambda l, i: (l, 0, 0)),
        ],
        out_specs=[
            pl.BlockSpec((None, rt * KV_CHUNKS, HEAD_DIM), lambda l, i: (l, i, 0)),
            pl.BlockSpec((None, rt, 2 * D_ATT), lambda l, i: (l, i, 0)),
        ],
        out_shape=[jax.ShapeDtypeStruct((depth, rows * KV_CHUNKS, HEAD_DIM), F32),
                   jax.ShapeDtypeStruct((depth, rows, 2 * D_ATT), BF16)],
        compiler_params=_params(("parallel", "parallel")),
        name="memkv_prompt",
    )(mem.reshape(rows, D_MODEL), mem_norm.reshape(depth, 1, D_MODEL), w_mem_kv,
      mem_k_norm.reshape(depth, 1, HEAD_DIM))


def _pv_and_rowsum(p, v):
    ones = jnp.ones((v.shape[0], HEAD_DIM), BF16)
    both = _dot(p, jnp.concatenate([v, ones], axis=1))
    return both[:, :HEAD_DIM], both[:, HEAD_DIM:]


def _mem_attn(mq, mg, mkv_ref, gq, store):
    for h in range(N_HEADS):
        q = _rms(mq[:, _hs(h)], gq).astype(BF16)
        s = _dot_nt(q, mkv_ref[:, _hs(h)])
        m = jnp.max(s, axis=-1, keepdims=True)
        acc, l = _pv_and_rowsum(jnp.exp2(s - m).astype(BF16), mkv_ref[:, _hs(h, D_ATT)])
        g = mg[:, _hs(h)]
        store(h, acc * g * (1.0 / (l * (1.0 + jnp.exp(-g)))))


def _step_index():
    return pl.program_id(0) * pl.num_programs(1) + pl.program_id(1)


def _is_last_step():
    return _step_index() == pl.num_programs(0) * pl.num_programs(1) - 1


def _a_layer_kernel(x_ref, g_ref, win_hbm, cw_ref, mkv_ref, gq_ref, wout_hbm,
                    xs_ref, st_ref, mc_ref,
                    y_ref, cs_ref, ys_ref, sto_ref,
                    ext_ref, cat_ref, zs_ref, cats_ref,
                    win_ref, win_stage, win_sem, wout_ref, wout_stage, wout_sem, *, a):
    t = pl.program_id(1)
    n = _step_index()

    @pl.when(t == 0)
    def _():
        ext_ref[0:CARRY, :] = jnp.zeros((CARRY, D_CONV), F32)

    @pl.when(n == 0)
    def _():
        _cast_weight(win_hbm.at[a], win_ref, win_stage, win_sem)
        _cast_weight(wout_hbm.at[a], wout_ref, wout_stage, wout_sem)
        _a_sample_first(xs_ref, g_ref, win_ref, cw_ref, st_ref, sto_ref, zs_ref, cats_ref)

    row = zs_ref[pl.ds(n, 1), :]
    _merge_row(cats_ref, slice(D_CONV, D_CONV + D_ATT), n,
               _sample_mem_attn(row[:, :D_ATT], gq_ref[...], mc_ref[0], row[:, D_ATT:]))

    x = x_ref[...]
    xn = _rms(x, g_ref[...]).astype(BF16)

    def proj(lo, hi):
        return _dot(xn, win_ref[:, lo:hi])

    u = proj(D_CONV, 2 * D_CONV) * proj(0, D_CONV)
    ext_ref[CARRY:CARRY + TM, :] = u
    y = (cw_ref[0:1, :] * ext_ref[CARRY - 2:CARRY - 2 + TM, :]
         + cw_ref[1:2, :] * ext_ref[CARRY - 1:CARRY - 1 + TM, :] + cw_ref[2:3, :] * u)
    ext_ref[0:CARRY, :] = ext_ref[TM:TM + CARRY, :]
    cs_ref[...] = ext_ref[CARRY - (CONV_W - 1):CARRY, :]
    bg = proj(2 * D_CONV, 3 * D_CONV)
    g = proj(3 * D_CONV, 4 * D_CONV)
    cat_ref[:, 0:D_CONV] = (_silu(g) * bg * y).astype(BF16)
    mq = proj(4 * D_CONV, 4 * D_CONV + D_ATT)
    mg = proj(4 * D_CONV + D_ATT, 4 * D_CONV + 2 * D_ATT)

    def store(h, val):
        cat_ref[:, _hs(h, D_CONV)] = val.astype(BF16)

    _mem_attn(mq, mg, mkv_ref, gq_ref[...] * QK_SCALE, store)
    y_ref[...] = x + _dot(cat_ref[...], wout_ref[...])

    @pl.when(_is_last_step())
    def _():
        ys_ref[...] = xs_ref[...] + _dot(cats_ref[...].astype(BF16), wout_ref[...])


def _sample_specs(nb, nt, layer):
    full = lambda shape: pl.BlockSpec(shape, lambda i, t: (0,) * len(shape))
    mem_cache = pl.BlockSpec((None, 1, MEM_LEN, KV_CHUNKS, HEAD_DIM),
                             lambda i, t: (layer, i * nt + t, 0, 0, 0))
    return full, mem_cache


def _a_layer(x, xs, norm_a, w_in, conv_w, mkv_b, state, mem_cache, a, layer, mem_q_norm, w_out):
    b, s, _ = x.shape
    nb = xs.shape[0]
    nt = s // TM
    assert nb == b * nt
    st_rows = nb * (CONV_W - 1) * D_CONV // HEAD_DIM
    full, mem_cache_spec = _sample_specs(nb, nt, layer)
    return pl.pallas_call(
        functools.partial(_a_layer_kernel, a=a),
        grid=(b, nt),
        in_specs=[
            pl.BlockSpec((None, TM, D_MODEL), lambda i, t: (i, t, 0)),
            _layer_spec((1, D_MODEL), a),
            pl.BlockSpec(memory_space=pl.ANY),
            _layer_spec((CONV_W, D_CONV), a),
            pl.BlockSpec((None, None, MEM_LEN, 2 * D_ATT), lambda i, t: (layer, i, 0, 0)),
            _layer_spec((1, HEAD_DIM), layer),
            pl.BlockSpec(memory_space=pl.ANY),
            full((nb, D_MODEL)),
            _layer_spec((st_rows, HEAD_DIM), a),
            mem_cache_spec,
        ],
        out_specs=[
            pl.BlockSpec((None, TM, D_MODEL), lambda i, t: (i, t, 0)),
            pl.BlockSpec((None, CONV_W - 1, D_CONV), lambda i, t: (i, 0, 0)),
            full((nb, D_MODEL)),
            full((st_rows, HEAD_DIM)),
        ],
        out_shape=[jax.ShapeDtypeStruct((b, s, D_MODEL), F32),
                   jax.ShapeDtypeStruct((b, CONV_W - 1, D_CONV), F32),
                   jax.ShapeDtypeStruct((nb, D_MODEL), F32),
                   jax.ShapeDtypeStruct((st_rows, HEAD_DIM), F32)],
        scratch_shapes=[pltpu.VMEM((TM + CARRY, D_CONV), F32),
                        pltpu.VMEM((TM, D_CONV + D_ATT), BF16),
                        pltpu.VMEM((nb, 2 * D_ATT), F32),
                        pltpu.VMEM((nb, D_CONV + D_ATT), F32)]
        + _weight_scratch(w_in.shape[1:]) + _weight_scratch(w_out.shape[1:]),
        compiler_params=_params(("arbitrary", "arbitrary")),
        name=f"a_layer_{layer}",
    )(x, norm_a, w_in, conv_w, mkv_b, mem_q_norm, w_out, xs, state, mem_cache)


def _residue_rows(slab_ref, split_ref, c, d):
    rows = slab_ref.shape[1]
    if d == SUB_STRIDE:
        return lambda r: slab_ref[c, pl.ds(r, rows // d, stride=d), :]
    assert d == SUB_STRIDE * SUB_STRIDE
    for r1 in range(SUB_STRIDE):
        split_ref[c, r1] = slab_ref[c, pl.ds(r1, rows // SUB_STRIDE, stride=SUB_STRIDE), :]
    return lambda r: split_ref[c, r % SUB_STRIDE,
                               pl.ds(r // SUB_STRIDE, rows // d, stride=SUB_STRIDE), :]


def _to_residue_major(dst_ref, slab_ref, split_ref, n_chunks, d):
    for c in range(n_chunks):
        get = _residue_rows(slab_ref, split_ref, c, d)
        for r in range(d):
            dst_ref[r, :, _hs(c)] = get(r).astype(dst_ref.dtype)


def _kv_kernel(x_ref, g_ref, w_hbm, kn_ref, xs_ref, kvb0_ref, kvb1_ref, kvb2_ref,
               st0_ref, st1_ref, st2_ref, kvn_ref, slab1_ref, slab2_ref, split_ref,
               w_ref, w_stage, w_sem):
    t = pl.program_id(1)
    last = pl.num_programs(1) - 1

    @pl.when(_step_index() == 0)
    def _():
        _cast_weight(w_hbm, w_ref, w_stage, w_sem)
        _kv_sample_rows(xs_ref, g_ref, w_ref, kn_ref, kvn_ref)

    xn = _rms(x_ref[...], g_ref[...]).astype(BF16)
    kvb_refs = (kvb0_ref, kvb1_ref, kvb2_ref)
    slabs = (None, slab1_ref, slab2_ref)
    for g in reversed(range(N_GROUPS)):
        kv = _dot(xn, w_ref[:, g * 2 * D_ATT:(g + 1) * 2 * D_ATT])
        chunks = [_rms(kv[:, _hs(h)], kn_ref[g:g + 1, :]) for h in range(N_HEADS)]
        chunks += [kv[:, _hs(h, D_ATT)] for h in range(N_HEADS)]
        d = DILATIONS[g]
        for c, chunk in enumerate(chunks):
            if d == 1:
                kvb_refs[g][0, :, _hs(c)] = chunk.astype(BF16)
            else:
                slabs[g][c] = chunk
            if g == 2:
                st2_ref[pl.ds(c, TM, stride=KV_CHUNKS), :] = chunk
        if d > 1:
            _to_residue_major(kvb_refs[g], slabs[g], split_ref, KV_CHUNKS, d)
        if g == 1:
            @pl.when(t == last)
            def _():
                for c, chunk in enumerate(chunks):
                    st1_ref[pl.ds(c, TM, stride=KV_CHUNKS), :] = chunk
        if g == 0:
            @pl.when(t == last)
            def _():
                for c, chunk in enumerate(chunks):
                    st0_ref[pl.ds(c, WINDOWS[0], stride=KV_CHUNKS), :] = chunk[TM - WINDOWS[0]:, :]


def _kv_shared(x, xs, kv_norm, w_kv, k_norm_dil):
    b, s, _ = x.shape
    nb = xs.shape[0]
    kvn_rows = nb * N_GROUPS * KV_CHUNKS
    assert WINDOWS[1] == TM and WINDOWS[2] == s and WINDOWS[0] <= TM
    first = lambda i, t: (i, 0, 0)
    res_spec = lambda d: pl.BlockSpec((None, d, TM // d, 2 * D_ATT), lambda i, t: (i, 0, t, 0))
    res_sds = lambda d: jax.ShapeDtypeStruct((b, d, s // d, 2 * D_ATT), BF16)
    st_sds = lambda rows: jax.ShapeDtypeStruct((b, rows * KV_CHUNKS, HEAD_DIM), F32)
    return pl.pallas_call(
        _kv_kernel,
        grid=(b, s // TM),
        in_specs=[
            pl.BlockSpec((None, TM, D_MODEL), lambda i, t: (i, t, 0)),
            _const_spec((1, D_MODEL)),
            pl.BlockSpec(memory_space=pl.ANY),
            _const_spec((N_GROUPS, HEAD_DIM)),
            _const_spec((nb, D_MODEL)),
        ],
        out_specs=[
            res_spec(DILATIONS[0]), res_spec(DILATIONS[1]), res_spec(DILATIONS[2]),
            pl.BlockSpec((None, WINDOWS[0] * KV_CHUNKS, HEAD_DIM), first),
            pl.BlockSpec((None, WINDOWS[1] * KV_CHUNKS, HEAD_DIM), first),
            pl.BlockSpec((None, TM * KV_CHUNKS, HEAD_DIM), lambda i, t: (i, t, 0)),
            pl.BlockSpec((kvn_rows, HEAD_DIM), lambda i, t: (0, 0)),
        ],
        out_shape=[res_sds(DILATIONS[0]), res_sds(DILATIONS[1]), res_sds(DILATIONS[2]),
                   st_sds(WINDOWS[0]), st_sds(WINDOWS[1]), st_sds(s),
                   jax.ShapeDtypeStruct((kvn_rows, HEAD_DIM), F32)],
        scratch_shapes=[pltpu.VMEM((KV_CHUNKS, TM, HEAD_DIM), F32),
                        pltpu.VMEM((KV_CHUNKS, TM, HEAD_DIM), F32),
                        pltpu.VMEM((KV_CHUNKS, SUB_STRIDE, TM // SUB_STRIDE, HEAD_DIM), F32)]
        + _weight_scratch(w_kv.shape),
        compiler_params=_params(("arbitrary", "arbitrary")),
        name="kv_prompt",
    )(x, kv_norm.reshape(1, D_MODEL), w_kv, k_norm_dil, xs)


def _band_scores(qs, ks, masks):
    return jnp.concatenate([jnp.where(mk, _dot_nt(q, k), NEG) for q, k, mk in zip(qs, ks, masks)],
                           axis=0)


def _band_finish(s, vs, nq):
    m = jnp.max(s, axis=-1, keepdims=True)
    pb = jnp.exp2(s - m).astype(BF16)
    res = []
    for i, v in enumerate(vs):
        rows = slice(i * nq, (i + 1) * nq)
        acc, l = _pv_and_rowsum(pb[rows], v)
        res.append((acc, l, m[rows]))
    return res


def _b_layer_kernel(x_ref, g_ref, win_ref, qn_ref, mkv_ref, gq_ref, wout_ref,
                    kv0_ref, kv0p_ref, kv1_ref, kv1p_ref, kv2_ref,
                    xs_ref, kvn_ref, c0_ref, c1_ref, c2_ref, mc_ref,
                    y_ref, ys_ref,
                    qslab1_ref, qslab2_ref, qsplit_ref, oslab_ref, lslab_ref, mslab_ref, cat_ref,
                    zs_ref, cats_ref):
    t = pl.program_id(1)
    n = _step_index()

    @pl.when(n == 0)
    def _():
        zs_ref[...] = _dot(_rms(xs_ref[...], g_ref[...]).astype(BF16), win_ref[...])
        cats_ref[...] = jnp.zeros(cats_ref.shape, F32)

    _merge_row(cats_ref, slice(0, 2 * D_ATT), n,
               _b_sample_row(zs_ref[pl.ds(n, 1), :], kvn_ref[n], qn_ref, gq_ref[...],
                             (c0_ref[0], c1_ref[0], c2_ref[0]), mc_ref[0]))

    x = x_ref[...]
    xn = _rms(x, g_ref[...]).astype(BF16)

    def proj(lo, hi):
        return _dot(xn, win_ref[:, lo:hi])

    qi = lax.broadcasted_iota(jnp.int32, (BAND, 2 * BAND), 0)
    kj = lax.broadcasted_iota(jnp.int32, (BAND, 2 * BAND), 1)
    mask_rest = (kj >= qi) & (kj <= qi + BAND)
    mask_edge = (kj >= jnp.maximum(qi, jnp.where(t == 0, BAND, 0))) & (kj <= qi + BAND)

    lane = lax.broadcasted_iota(jnp.int32, (1, HEAD_DIM), 1)

    def finish(g, rows, res):
        sums = jnp.ones(res[0][1].shape, F32)
        maxes = jnp.zeros(res[0][1].shape, F32)
        for h, (acc, l, m) in enumerate(res):
            oslab_ref[g, h, rows, :] = acc
            sums = jnp.where(lane == h, l, sums)
            maxes = jnp.where(lane == h, m, maxes)
        lslab_ref[g, rows, :] = sums
        mslab_ref[g, rows, :] = maxes

    def normed_heads(g):
        qd = proj(g * D_ATT, (g + 1) * D_ATT)
        gain = qn_ref[g:g + 1, :] * QK_SCALE
        return [_rms(qd[:, _hs(h)], gain) for h in range(N_HEADS)]

    n_sub = TM // BAND
    qh = [q.astype(BF16) for q in normed_heads(0)]
    units0 = [(i, h) for i in range(n_sub) for h in range(N_HEADS)]

    def keys0(i, cols):
        if i == 0:
            return jnp.concatenate([kv0p_ref[0, :, cols], kv0_ref[0, 0:BAND, cols]], axis=0)
        return kv0_ref[0, (i - 1) * BAND:(i + 1) * BAND, cols]

    d1 = DILATIONS[1]
    for h, q in enumerate(normed_heads(1)):
        qslab1_ref[h] = q
    get_q1 = [_residue_rows(qslab1_ref, None, h, d1) for h in range(N_HEADS)]
    units1 = [(r, h) for r in range(d1) for h in range(N_HEADS)]

    def keys1(r, cols):
        return jnp.concatenate([kv1p_ref[r, :, cols], kv1_ref[r, :, cols]], axis=0)

    d2 = DILATIONS[2]
    nq = TM // d2
    for h, q in enumerate(normed_heads(2)):
        qslab2_ref[h] = q
    get_q2 = [_residue_rows(qslab2_ref, qsplit_ref, h, d2) for h in range(N_HEADS)]
    units2 = [(r, h) for r in range(d2) for h in range(N_HEADS)]
    qi2 = lax.broadcasted_iota(jnp.int32, (nq, BAND), 0)
    kj2 = lax.broadcasted_iota(jnp.int32, (nq, BAND), 1)
    mask2 = kj2 <= qi2 + t * nq

    s0 = _band_scores([qh[h][i * BAND:(i + 1) * BAND] for i, h in units0],
                      [keys0(i, _hs(h)) for i, h in units0],
                      [mask_edge if i == 0 else mask_rest for i, h in units0])
    s1 = _band_scores([get_q1[h](r).astype(BF16) for r, h in units1],
                      [keys1(r, _hs(h)) for r, h in units1], [mask_edge] * len(units1))
    s2 = _band_scores([get_q2[h](r).astype(BF16) for r, h in units2],
                      [kv2_ref[r, :, _hs(h)] for r, h in units2], [mask2] * len(units2))
    res = _band_finish(s0, [keys0(i, _hs(h, D_ATT)) for i, h in units0], BAND)
    for i in range(n_sub):
        finish(0, pl.ds(i * BAND, BAND), res[i * N_HEADS:(i + 1) * N_HEADS])
    res = _band_finish(s1, [keys1(r, _hs(h, D_ATT)) for r, h in units1], BAND)
    for r in range(d1):
        finish(1, pl.ds(r, BAND, stride=d1), res[r * N_HEADS:(r + 1) * N_HEADS])
    res = _band_finish(s2, [kv2_ref[r, :, _hs(h, D_ATT)] for r, h in units2], nq)
    for r in range(d2):
        finish(2, pl.ds(r, nq, stride=d2), res[r * N_HEADS:(r + 1) * N_HEADS])

    base = N_GROUPS * D_ATT
    gd = proj(base, base + D_ATT)
    m0, m1, m2 = mslab_ref[0], mslab_ref[1], mslab_ref[2]
    m = jnp.maximum(jnp.maximum(m0, m1), m2)
    e0, e1, e2 = jnp.exp2(m0 - m), jnp.exp2(m1 - m), jnp.exp2(m2 - m)
    inv = 1.0 / (e0 * lslab_ref[0] + e1 * lslab_ref[1] + e2 * lslab_ref[2])
    w0, w1, w2 = e0 * inv, e1 * inv, e2 * inv
    for h in range(N_HEADS):
        dil = (oslab_ref[0, h] * w0[:, h:h + 1] + oslab_ref[1, h] * w1[:, h:h + 1]
               + oslab_ref[2, h] * w2[:, h:h + 1])
        cat_ref[:, _hs(h)] = (dil * _silu(gd[:, _hs(h)])).astype(BF16)
    mq = proj(base + D_ATT, base + 2 * D_ATT)
    mg = proj(base + 2 * D_ATT, base + 3 * D_ATT)

    def store(h, val):
        cat_ref[:, _hs(h, D_ATT)] = val.astype(BF16)

    _mem_attn(mq, mg, mkv_ref, gq_ref[...] * QK_SCALE, store)
    y_ref[...] = x + _dot(cat_ref[...], wout_ref[...])

    @pl.when(_is_last_step())
    def _():
        ys_ref[...] = xs_ref[...] + _dot(cats_ref[...].astype(BF16), wout_ref[...])


def _b_layer(x, xs, norm_b, w_in, q_norm, mkv_b, kvbs, kv_new, caches, mem_cache, bl, layer,
             mem_q_norm, w_out):
    b, s, _ = x.shape
    nb = xs.shape[0]
    nt = s // TM
    assert nb == b * nt
    full, mem_cache_spec = _sample_specs(nb, nt, layer)
    cache_spec = pl.BlockSpec((1, BAND, None, KV_CHUNKS, HEAD_DIM),
                              lambda i, t: (i * nt + t, 0, 0, 0, 0))
    assert TM // BAND == DILATIONS[1] and s // DILATIONS[2] == BAND
    tile = lambda i, t: (i, t, 0)
    sub = TM // BAND
    kv_w = 2 * D_ATT
    return pl.pallas_call(
        _b_layer_kernel,
        grid=(b, s // TM),
        in_specs=[
            pl.BlockSpec((None, TM, D_MODEL), tile),
            _layer_spec((1, D_MODEL), bl),
            _layer_spec(w_in.shape[1:], bl),
            _layer_spec((N_GROUPS, HEAD_DIM), bl),
            pl.BlockSpec((None, None, MEM_LEN, kv_w), lambda i, t: (layer, i, 0, 0)),
            _layer_spec((1, HEAD_DIM), layer),
            _layer_spec(w_out.shape[1:], bl),
            pl.BlockSpec((None, 1, TM, kv_w), lambda i, t: (i, 0, t, 0)),
            pl.BlockSpec((None, 1, BAND, kv_w),
                         lambda i, t: (i, 0, jnp.maximum(t * sub - 1, 0), 0)),
            pl.BlockSpec((None, DILATIONS[1], BAND, kv_w), lambda i, t: (i, 0, t, 0)),
            pl.BlockSpec((None, DILATIONS[1], BAND, kv_w),
                         lambda i, t: (i, 0, jnp.maximum(t - 1, 0), 0)),
            pl.BlockSpec((None, DILATIONS[2], BAND, kv_w), lambda i, t: (i, 0, 0, 0)),
            full((nb, D_MODEL)),
            full(kv_new.shape),
            cache_spec, cache_spec, cache_spec,
            mem_cache_spec,
        ],
        out_specs=[pl.BlockSpec((None, TM, D_MODEL), tile), full((nb, D_MODEL))],
        out_shape=[jax.ShapeDtypeStruct((b, s, D_MODEL), F32),
                   jax.ShapeDtypeStruct((nb, D_MODEL), F32)],
        scratch_shapes=[pltpu.VMEM((N_HEADS, TM, HEAD_DIM), F32),
                        pltpu.VMEM((N_HEADS, TM, HEAD_DIM), F32),
                        pltpu.VMEM((N_HEADS, SUB_STRIDE, TM // SUB_STRIDE, HEAD_DIM), F32),
                        pltpu.VMEM((N_GROUPS, N_HEADS, TM, HEAD_DIM), F32),
                        pltpu.VMEM((N_GROUPS, TM, HEAD_DIM), F32),
                        pltpu.VMEM((N_GROUPS, TM, HEAD_DIM), F32),
                        pltpu.VMEM((TM, 2 * D_ATT), BF16),
                        pltpu.VMEM((nb, w_in.shape[2]), F32),
                        pltpu.VMEM((nb, 2 * D_ATT), F32)],
        compiler_params=_params(("arbitrary", "arbitrary")),
        name=f"b_layer_{layer}",
    )(x, norm_b, w_in, q_norm, mkv_b, mem_q_norm, w_out,
      kvbs[0], kvbs[0], kvbs[1], kvbs[1], kvbs[2],
      xs, kv_new, *caches, mem_cache)


def _swap_halves(x, axis):
    return pltpu.roll(x, N_HEADS, axis=axis)


def _head_rows(tile):
    return jnp.concatenate([tile[N_HEADS + h:N_HEADS + h + 1, :] for h in range(N_HEADS)],
                           axis=-1)


def _low_tile(row):
    return jnp.concatenate([jnp.zeros((N_HEADS, HEAD_DIM), F32)]
                           + [row[:, _hs(h)] for h in range(N_HEADS)], axis=0)


def _q_tile(row, g):
    return jnp.concatenate([_rms(row[:, _hs(h)], g) * QK_SCALE for h in range(N_HEADS)]
                           + [jnp.zeros((N_HEADS, HEAD_DIM), F32)], axis=0)


def _tile_attend(qt, kv):
    s = jnp.sum(kv * qt[None], axis=-1, keepdims=True)
    m = jnp.max(s, axis=0)
    p = jnp.exp2(s - m[None])
    l = jnp.sum(p, axis=0)
    acc = jnp.sum(_swap_halves(jnp.broadcast_to(p, kv.shape), 1) * kv, axis=0)
    out = acc / _swap_halves(jnp.broadcast_to(l, acc.shape), 0)
    return out, m + jnp.log2(l)


def _sample_mem_attn(q_row, gq, mkv, mg_row):
    out, _ = _tile_attend(_q_tile(q_row, gq), mkv)
    return _head_rows(out * _low_tile(_silu(mg_row)))


def _gather_chunks(ref, n_rows, n_chunks, offset=0):
    total = ref.shape[0] // n_rows
    return jnp.concatenate([ref[pl.ds(offset + c, n_rows, stride=total), :]
                            for c in range(n_chunks)], axis=-1)


def _merge_row(ref, cols, n, row):
    tile = pl.ds(pl.multiple_of((n // SAMPLE_GB) * SAMPLE_GB, SAMPLE_GB), SAMPLE_GB)
    sub = lax.broadcasted_iota(jnp.int32, (SAMPLE_GB, row.shape[1]), 0)
    ref[tile, cols] = jnp.where(sub == n % SAMPLE_GB, row, ref[tile, cols])


def _a_sample_first(x_ref, g_ref, win_ref, cw_ref, st_ref, sto_ref, zm_ref, cat_ref):
    nb = x_ref.shape[0]
    n_st = D_CONV // HEAD_DIM
    xn = _rms(x_ref[...], g_ref[...]).astype(BF16)
    z = _dot(xn, win_ref[...])
    u = z[:, D_CONV:2 * D_CONV] * z[:, 0:D_CONV]
    s0 = _gather_chunks(st_ref, nb, n_st)
    s1 = _gather_chunks(st_ref, nb, n_st, n_st)
    y = cw_ref[0:1, :] * s0 + cw_ref[1:2, :] * s1 + cw_ref[2:3, :] * u
    for c in range(n_st):
        sto_ref[pl.ds(c, nb, stride=2 * n_st), :] = s1[:, _hs(c)]
        sto_ref[pl.ds(n_st + c, nb, stride=2 * n_st), :] = u[:, _hs(c)]
    cat_ref[:, 0:D_CONV] = _silu(z[:, 3 * D_CONV:4 * D_CONV]) * z[:, 2 * D_CONV:3 * D_CONV] * y
    cat_ref[:, D_CONV:] = jnp.zeros((nb, D_ATT), F32)
    zm_ref[...] = z[:, 4 * D_CONV:]


def _kv_sample_rows(x_ref, g_ref, w_ref, kn_ref, o_ref):
    nb = x_ref.shape[0]
    xn = _rms(x_ref[...], g_ref[...]).astype(BF16)
    kv = _dot(xn, w_ref[...])
    n_chunks = N_GROUPS * KV_CHUNKS
    for g in range(N_GROUPS):
        for c in range(KV_CHUNKS):
            chunk = kv[:, _hs(c, g * 2 * D_ATT)]
            if c < N_HEADS:
                chunk = _rms(chunk, kn_ref[g:g + 1, :])
            o_ref[pl.ds(g * KV_CHUNKS + c, nb, stride=n_chunks), :] = chunk


def _b_sample_row(row, kv_new, qn_ref, gq, caches, mkv):
    base = N_GROUPS * D_ATT
    outs, lses = [], []
    for g in range(N_GROUPS):
        qt = _q_tile(row[:, g * D_ATT:(g + 1) * D_ATT], qn_ref[g:g + 1, :])
        out, lse = _tile_attend(qt, jnp.concatenate([caches[g], kv_new[g][None]], axis=0))
        outs.append(out)
        lses.append(lse)
    m = jnp.maximum(jnp.maximum(lses[0], lses[1]), lses[2])
    es = [jnp.exp2(l - m) for l in lses]
    den = es[0] + es[1] + es[2]
    ws = [_swap_halves(jnp.broadcast_to(e / den, (KV_CHUNKS, HEAD_DIM)), 0) for e in es]
    dil = outs[0] * ws[0] + outs[1] * ws[1] + outs[2] * ws[2]
    dil_row = _head_rows(dil * _low_tile(_silu(row[:, base:base + D_ATT])))
    mem_row = _sample_mem_attn(row[:, base + D_ATT:base + 2 * D_ATT], gq, mkv,
                               row[:, base + 2 * D_ATT:])
    return jnp.concatenate([dil_row, mem_row], axis=-1)


def kernel(x_prompt, x_sample, cache_mem_kv, state_conv, cache_dil0_kv, cache_dil1_kv, cache_dil2_kv, mem_prompt, norm_a, w_in_a, conv_w_a, w_out_a, norm_b, w_in_b, q_norm_b, w_out_b, kv_norm, w_kv, k_norm_dil, mem_norm, w_mem_kv, mem_k_norm, mem_q_norm):
    n_a = w_in_a.shape[0]
    n_b = w_in_b.shape[0]
    depth = n_a + n_b
    b, s, _ = x_prompt.shape
    nb = x_sample.shape[0]
    assert x_sample.shape[1] == 1 and s % TM == 0 and nb % SAMPLE_GB == 0
    kv_dims = (2, N_HEADS, HEAD_DIM)

    w_in_b16, w_out_b16 = w_in_b.astype(BF16), w_out_b.astype(BF16)
    norm_a3 = norm_a.reshape(n_a, 1, D_MODEL)
    norm_b3 = norm_b.reshape(n_b, 1, D_MODEL)
    mem_q_norm3 = mem_q_norm.reshape(depth, 1, HEAD_DIM)

    mem_kv_f, mem_kv_b = _memkv_prompt(mem_prompt, mem_norm, w_mem_kv, mem_k_norm)
    mem_kv_b = mem_kv_b.reshape(depth, b, MEM_LEN, 2 * D_ATT)
    mem_cache = cache_mem_kv.reshape(depth, nb, MEM_LEN, KV_CHUNKS, HEAD_DIM)
    caches = []
    for g, cache in enumerate((cache_dil0_kv, cache_dil1_kv, cache_dil2_kv)):
        assert cache.shape[1] == WINDOWS[g]
        caches.append(cache.reshape(nb, BAND, DILATIONS[g], KV_CHUNKS, HEAD_DIM))
    state = state_conv.reshape(n_a, -1, HEAD_DIM)

    xp = x_prompt
    xs = x_sample.reshape(nb, D_MODEL)
    conv_p, conv_s = [], []
    for a in range(n_a):
        xp, cp, xs, cs = _a_layer(xp, xs, norm_a3, w_in_a, conv_w_a, mem_kv_b, state, mem_cache,
                                  a, a, mem_q_norm3, w_out_a)
        conv_p.append(cp)
        conv_s.append(cs.reshape(nb, CONV_W - 1, D_CONV))

    kvb0, kvb1, kvb2, st0, st1, st2, kv_new = _kv_shared(xp, xs, kv_norm, w_kv, k_norm_dil)
    kvbs = (kvb0, kvb1, kvb2)
    kv_new = kv_new.reshape(nb, N_GROUPS, KV_CHUNKS, HEAD_DIM)

    for bl in range(n_b):
        layer = n_a + bl
        xp, xs = _b_layer(xp, xs, norm_b3, w_in_b16, q_norm_b, mem_kv_b, kvbs, kv_new, caches,
                          mem_cache, bl, layer, mem_q_norm3, w_out_b16)

    dil_s = [kv_new[:, g].reshape(nb, 1, *kv_dims) for g in range(N_GROUPS)]
    return (xp, xs.reshape(nb, 1, D_MODEL),
            mem_kv_f.reshape(depth, b, MEM_LEN, *kv_dims),
            jnp.stack(conv_p), jnp.stack(conv_s),
            st0.reshape(b, WINDOWS[0], *kv_dims), st1.reshape(b, WINDOWS[1], *kv_dims),
            st2.reshape(b, s, *kv_dims), *dil_s)
```
